```python
import math
import jax, jax.numpy as jnp
from jax import lax
import numpy as np

D_MODEL = 1024
BATCH = 2
SEQ = 8192
DEPTH = 1

MEM_LEN = 256
EPS = 1e-6
D_MIX = D_MODEL
D_CONV = D_MIX // 2
CONV_HEADS = 8
CONV_WIDTH = 31
D_SSM = D_MIX - D_CONV
SSM_GROUP = 16
N_SSM_GROUPS = D_SSM // SSM_GROUP
SSM_STATE = 64
N_XHEADS = 4
XHEAD_DIM = D_MODEL // N_XHEADS
N_EXPERT_GROUPS = 4
EXPERTS_PER_GROUP = 8
N_EXPERTS = N_EXPERT_GROUPS * EXPERTS_PER_GROUP
TOP_K = 2
D_EXPERT = D_MODEL // 2
MOE_BLOCK = 128

kernel_name = "hymba_conformer_s5_hmoe_block"


def rms_norm(x, w):
    x32 = x.astype(jnp.float32)
    y = x32 * lax.rsqrt(jnp.mean(x32 * x32, axis=-1, keepdims=True) + EPS)
    return (y * w.astype(jnp.float32)).astype(x.dtype)


def conformer_conv(a, g, conv_w, conv_b, ln_w, ln_b):
    v = a * jax.nn.sigmoid(g)
    c = v.shape[-1]
    v = lax.conv_general_dilated(
        v, conv_w[:, None, :].astype(v.dtype), window_strides=(1,),
        padding=[(CONV_WIDTH - 1, 0)], dimension_numbers=('NWC', 'WIO', 'NWC'),
        feature_group_count=c) + conv_b.astype(v.dtype)
    v32 = v.astype(jnp.float32)
    mu = jnp.mean(v32, axis=-1, keepdims=True)
    var = jnp.mean(jnp.square(v32 - mu), axis=-1, keepdims=True)
    v32 = (v32 - mu) * lax.rsqrt(var + EPS) * ln_w.astype(jnp.float32) + ln_b.astype(jnp.float32)
    return jax.nn.silu(v32).astype(a.dtype)


def _diag_combine(e1, e2):
    a1, b1 = e1
    a2, b2 = e2
    return a1 * a2, a2 * b1 + b2


def s5_mixer(u, A_re, A_im, log_dt, B_re, B_im, C_re, C_im, D, glu_w):
    bsz, s, _ = u.shape
    u32 = u.astype(jnp.float32).reshape(bsz, s, N_SSM_GROUPS, SSM_GROUP)
    lam = lax.complex(A_re.astype(jnp.float32), A_im.astype(jnp.float32))
    dt = jnp.exp(log_dt.astype(jnp.float32))[:, None]
    a_bar = jnp.exp(lam * dt)
    b = lax.complex(B_re.astype(jnp.float32), B_im.astype(jnp.float32))
    b_bar = ((a_bar - 1.0) / lam)[..., None] * b
    bu = jnp.einsum('bsgc,gpc->bsgp', u32.astype(jnp.complex64), b_bar)
    a_seq = jnp.broadcast_to(a_bar, bu.shape)
    _, states = lax.associative_scan(_diag_combine, (a_seq, bu), axis=1)
    c = lax.complex(C_re.astype(jnp.float32), C_im.astype(jnp.float32))
    y = jnp.einsum('bsgp,gcp->bsgc', states, c).real + D.astype(jnp.float32) * u32
    y = jax.nn.gelu(y)
    ab = jnp.einsum('bsgc,gce->bsge', y, glu_w.astype(jnp.float32))
    out = ab[..., :SSM_GROUP] * jax.nn.sigmoid(ab[..., SSM_GROUP:])
    return out.reshape(bsz, s, D_SSM).astype(u.dtype)


def memory_cross_attention(h, m, wq, wk, wv, wo):
    bsz, s, _ = h.shape
    mlen = m.shape[1]
    q = (h @ wq).reshape(bsz, s, N_XHEADS, XHEAD_DIM).astype(jnp.float32)
    k = (m @ wk).reshape(bsz, mlen, N_XHEADS, XHEAD_DIM).astype(jnp.float32)
    v = (m @ wv).reshape(bsz, mlen, N_XHEADS, XHEAD_DIM).astype(jnp.float32)
    scores = jnp.einsum('bshd,bmhd->bhsm', q, k) * (XHEAD_DIM ** -0.5)
    p = jax.nn.softmax(scores, axis=-1)
    o = jnp.einsum('bhsm,bmhd->bshd', p, v).reshape(bsz, s, D_MODEL).astype(h.dtype)
    return o @ wo


def hierarchical_moe(h, wg, bg, we, be, w1, w3, w2):
    bsz, s, d = h.shape
    t = bsz * s
    xt = h.reshape(t, d)
    x32 = xt.astype(jnp.float32)
    g_probs = jax.nn.softmax(x32 @ wg.astype(jnp.float32) + bg.astype(jnp.float32), axis=-1)
    g_w, g_idx = lax.top_k(g_probs, 1)
    e_logits = (x32 @ we.astype(jnp.float32)).reshape(t, N_EXPERT_GROUPS, EXPERTS_PER_GROUP)
    e_logits = e_logits + be.astype(jnp.float32)
    e_logits = e_logits[jnp.arange(t), g_idx[:, 0]]
    top_l, top_i = lax.top_k(e_logits, TOP_K)
    gates = g_w * jax.nn.softmax(top_l, axis=-1)
    expert = g_idx * EPG_CONST() + top_i

    n_assign = t * TOP_K
    e_flat = expert.reshape(n_assign).astype(jnp.int32)
    gate_flat = gates.reshape(n_assign)
    tok_flat = jnp.arange(n_assign, dtype=jnp.int32) // TOP_K
    order = jnp.argsort(e_flat)
    e_sorted = e_flat[order]
    counts = jnp.bincount(e_flat, length=N_EXPERTS).astype(jnp.int32)
    padded = (counts + MOE_BLOCK - 1) // MOE_BLOCK * MOE_BLOCK
    start = jnp.cumsum(counts) - counts
    pstart = jnp.cumsum(padded) - padded
    rank = jnp.arange(n_assign, dtype=jnp.int32) - start[e_sorted]
    slot = pstart[e_sorted] + rank
    n_blocks = -(-n_assign // MOE_BLOCK) + N_EXPERTS
    cap = n_blocks * MOE_BLOCK
    slot_tok = jnp.full((cap,), t, jnp.int32).at[slot].set(tok_flat[order])
    slot_gate = jnp.zeros((cap,), jnp.float32).at[slot].set(gate_flat[order])
    block_start = jnp.arange(n_blocks, dtype=jnp.int32) * MOE_BLOCK
    pend = pstart + padded
    block_expert = jnp.minimum(
        jnp.sum((block_start[:, None] >= pend[None, :]).astype(jnp.int32), axis=1), N_EXPERTS - 1)

    x_pad = jnp.concatenate([xt, jnp.zeros((1, d), xt.dtype)], axis=0)
    xb = x_pad[slot_tok].reshape(n_blocks, MOE_BLOCK, d)

    def expert_block(args):
        xblk, e = args
        hid = jax.nn.silu(xblk @ w1[e]) * (xblk @ w3[e])
        return hid @ w2[e]

    yb = lax.map(expert_block, (xb, block_expert))
    y = yb.reshape(cap, d) * slot_gate[:, None].astype(yb.dtype)
    out = jnp.zeros((t + 1, d), y.dtype).at[slot_tok].add(y)[:t]
    return out.reshape(bsz, s, d).astype(h.dtype)


def EPG_CONST():
    return EXPERTS_PER_GROUP


def setup_inputs(seed: int = 0) -> dict:
    key = jax.random.key(seed)
    ks = jax.random.split(key, 40)
    L, D = DEPTH, D_MODEL
    G, P, c = N_SSM_GROUPS, SSM_STATE, SSM_GROUP
    f32 = jnp.float32

    def nrm(k, shape, scale):
        return jax.random.normal(k, shape, f32) * scale

    def gain(k, shape):
        return 1.0 + 0.05 * jax.random.normal(k, shape, f32)

    n_idx = jnp.arange(P, dtype=f32)
    return {
        "x": nrm(ks[0], (BATCH, SEQ, D), 1.0),
        "mem": nrm(ks[1], (BATCH, MEM_LEN, D), 1.0),
        "norm_mix_w": gain(ks[2], (L, D)),
        "w_in": nrm(ks[3], (L, D, 2 * D_CONV + D_SSM), D ** -0.5),
        "conv_w": nrm(ks[4], (L, CONV_WIDTH, D_CONV), CONV_WIDTH ** -0.5),
        "conv_b": nrm(ks[5], (L, D_CONV), 0.02),
        "conv_ln_w": gain(ks[6], (L, D_CONV)),
        "conv_ln_b": nrm(ks[7], (L, D_CONV), 0.02),
        "ssm_A_re": -0.5 + 0.01 * jax.random.normal(ks[8], (L, G, P), f32),
        "ssm_A_im": math.pi * n_idx + 0.01 * jax.random.normal(ks[9], (L, G, P), f32),
        "ssm_log_dt": jax.random.uniform(ks[10], (L, G), f32, math.log(1e-3), math.log(1e-1)),
        "ssm_B_re": nrm(ks[11], (L, G, P, c), (2 * c) ** -0.5),
        "ssm_B_im": nrm(ks[12], (L, G, P, c), (2 * c) ** -0.5),
        "ssm_C_re": nrm(ks[13], (L, G, c, P), (2 * P) ** -0.5),
        "ssm_C_im": nrm(ks[14], (L, G, c, P), (2 * P) ** -0.5),
        "ssm_D": 1.0 + 0.1 * jax.random.normal(ks[15], (L, G, c), f32),
        "ssm_glu_w": nrm(ks[16], (L, G, c, 2 * c), c ** -0.5),
        "w_out": nrm(ks[17], (L, D_MIX, D), D_MIX ** -0.5),
        "norm_x_w": gain(ks[18], (L, D)),
        "norm_mem_w": gain(ks[19], (L, D)),
        "xq_w": nrm(ks[20], (L, D, D), D ** -0.5),
        "xk_w": nrm(ks[21], (L, D, D), D ** -0.5),
        "xv_w": nrm(ks[22], (L, D, D), D ** -0.5),
        "xo_w": nrm(ks[23], (L, D, D), D ** -0.5),
        "norm_ffn_w": gain(ks[24], (L, D)),
        "router_group_w": nrm(ks[25], (L, D, N_EXPERT_GROUPS), D ** -0.5),
        "router_group_b": nrm(ks[26], (L, N_EXPERT_GROUPS), 0.01),
        "router_expert_w": nrm(ks[27], (L, D, N_EXPERTS), D ** -0.5),
        "router_expert_b": nrm(ks[28], (L, N_EXPERT_GROUPS, EXPERTS_PER_GROUP), 0.01),
        "moe_w1": nrm(ks[29], (L, N_EXPERTS, D, D_EXPERT), D ** -0.5),
        "moe_w3": nrm(ks[30], (L, N_EXPERTS, D, D_EXPERT), D ** -0.5),
        "moe_w2": nrm(ks[31], (L, N_EXPERTS, D_EXPERT, D), D_EXPERT ** -0.5),
        "final_norm_w": gain(ks[32], (D,)),
    }


def reference(x, mem, norm_mix_w, w_in, conv_w, conv_b, conv_ln_w, conv_ln_b,
              ssm_A_re, ssm_A_im, ssm_log_dt, ssm_B_re, ssm_B_im, ssm_C_re, ssm_C_im,
              ssm_D, ssm_glu_w, w_out, norm_x_w, norm_mem_w, xq_w, xk_w, xv_w, xo_w,
              norm_ffn_w, router_group_w, router_group_b, router_expert_w, router_expert_b,
              moe_w1, moe_w3, moe_w2, final_norm_w):
    h = x
    for l in range(DEPTH):
        hn = rms_norm(h, norm_mix_w[l])
        proj = hn @ w_in[l]
        conv_a = proj[..., :D_CONV]
        conv_g = proj[..., D_CONV:2 * D_CONV]
        ssm_u = proj[..., 2 * D_CONV:]
        y_conv = conformer_conv(conv_a, conv_g, conv_w[l], conv_b[l], conv_ln_w[l], conv_ln_b[l])
        y_ssm = s5_mixer(ssm_u, ssm_A_re[l], ssm_A_im[l], ssm_log_dt[l], ssm_B_re[l], ssm_B_im[l],
                         ssm_C_re[l], ssm_C_im[l], ssm_D[l], ssm_glu_w[l])
        h = h + jnp.concatenate([y_conv, y_ssm], axis=-1) @ w_out[l]
        h = h + memory_cross_attention(rms_norm(h, norm_x_w[l]), rms_norm(mem, norm_mem_w[l]),
                                       xq_w[l], xk_w[l], xv_w[l], xo_w[l])
        h = h + hierarchical_moe(rms_norm(h, norm_ffn_w[l]), router_group_w[l], router_group_b[l],
                                 router_expert_w[l], router_expert_b[l],
                                 moe_w1[l], moe_w3[l], moe_w2[l])
    return rms_norm(h, final_norm_w)
```

```python
import functools

import jax
import jax.numpy as jnp
from jax import lax
from jax.experimental import pallas as pl
from jax.experimental.pallas import tpu as pltpu

D_MODEL = 1024
D_CONV = 512
CONV_WIDTH = 31
D_SSM = 512
SSM_GROUP = 16
N_SSM_GROUPS = 32
SSM_STATE = 64
N_XHEADS = 4
XHEAD_DIM = 256
N_EXPERT_GROUPS = 4
EXPERTS_PER_GROUP = 8
N_EXPERTS = 32
D_EXPERT = 512
EPS = 1e-6

F32 = jnp.float32
BF16 = jnp.bfloat16

SUBLANES = 8
LANES = 128

TM_PROJ = 512
TM_CONV = 256
CONV_CHUNK = 64
CONV_HALO = 32
S5_STEPS = 64
S5_TILE = SUBLANES * S5_STEPS
S5_LANES = 128
S5_GROUPS = S5_LANES // SSM_GROUP
S5_STATE = S5_GROUPS * SSM_STATE
TM_MIX = 512
ROUTE_LANES = 128
EXPERT_LANE0 = N_EXPERT_GROUPS
MOE_BLOCK = 256
TM_ROWS = 256


def _rms(x, w):
    return x * lax.rsqrt(jnp.mean(x * x, axis=-1, keepdims=True) + EPS) * w


def _dot(a, b):
    return jnp.dot(a, b, preferred_element_type=F32)


def _s5_prep_kernel(are_ref, aim_ref, ldt_ref, btre_ref, btim_ref,
                    a1_ref, atab_ref, pj_ref, aj_ref, bbar_ref):
    lam_re = are_ref[...]
    lam_im = aim_ref[...]
    dt = jnp.exp(ldt_ref[...])
    x = lam_re * dt
    y = lam_im * dt

    def power(k):
        mag = jnp.exp(k * x)
        return mag * jnp.cos(k * y), mag * jnp.sin(k * y)

    n = x.shape[-1]
    ones8 = jnp.ones((SUBLANES, n), F32)
    a_re, a_im = power(ones8)
    a1_ref[0] = a_re
    a1_ref[1] = a_im

    steps = (lax.broadcasted_iota(jnp.int32, (S5_STEPS, n), 0) + 1).astype(F32)
    t_re, t_im = power(steps)
    atab_ref[0] = t_re
    atab_ref[1] = t_im

    row = lax.broadcasted_iota(jnp.int32, (SUBLANES, n), 0)
    for i, d in enumerate((1, 2, 4)):
        p_re, p_im = power(ones8 * float(d * S5_STEPS))
        keep = row >= d
        pj_ref[0, i] = jnp.where(keep, p_re, 0.0)
        pj_ref[1, i] = jnp.where(keep, p_im, 0.0)
    j_re, j_im = power(ones8 * float(S5_STEPS))
    aj_ref[0] = j_re
    aj_ref[1] = j_im

    num_re = a_re[0:1] - 1.0
    num_im = a_im[0:1]
    den = lam_re * lam_re + lam_im * lam_im
    c_re = (num_re * lam_re + num_im * lam_im) / den
    c_im = (num_im * lam_re - num_re * lam_im) / den
    b_re = btre_ref[...]
    b_im = btim_ref[...]
    bbar_ref[0] = c_re * b_re - c_im * b_im
    bbar_ref[1] = c_re * b_im + c_im * b_re


def _s5_prep(a_re, a_im, log_dt, b_re, b_im):
    n = N_SSM_GROUPS * SSM_STATE
    are = a_re.reshape(1, n)
    aim = a_im.reshape(1, n)
    ldt = jnp.repeat(log_dt, SSM_STATE).reshape(1, n)
    btre = jnp.transpose(b_re, (2, 0, 1)).reshape(SSM_GROUP, n)
    btim = jnp.transpose(b_im, (2, 0, 1)).reshape(SSM_GROUP, n)
    return pl.pallas_call(
        _s5_prep_kernel,
        out_shape=(
            jax.ShapeDtypeStruct((2, SUBLANES, n), F32),
            jax.ShapeDtypeStruct((2, S5_STEPS, n), F32),
            jax.ShapeDtypeStruct((2, 3, SUBLANES, n), F32),
            jax.ShapeDtypeStruct((2, SUBLANES, n), F32),
            jax.ShapeDtypeStruct((2, SSM_GROUP, n), F32),
        ),
        name="s5_prep",
    )(are, aim, ldt, btre, btim)


def _block_diag(w):
    nl, g, r, c = w.shape
    eye = jnp.eye(g, dtype=w.dtype)
    return jnp.einsum("lgrc,gh->lgrhc", w, eye).reshape(nl, g * r, g * c)


def _in_proj_kernel(x_ref, nw_ref, w_ref, v_ref, u_ref):
    xn = _rms(x_ref[...], nw_ref[...]).astype(BF16)
    proj = _dot(xn, w_ref[...])
    a = proj[:, :D_CONV]
    g = proj[:, D_CONV:2 * D_CONV]
    v_ref[...] = a * jax.nn.sigmoid(g)
    u_ref[...] = proj[:, 2 * D_CONV:]


def _in_proj(xt, norm_w, w_in):
    t = xt.shape[0]
    n_out = 2 * D_CONV + D_SSM
    return pl.pallas_call(
        _in_proj_kernel,
        grid=(t // TM_PROJ,),
        in_specs=[
            pl.BlockSpec((TM_PROJ, D_MODEL), lambda i: (i, 0)),
            pl.BlockSpec((1, D_MODEL), lambda i: (0, 0)),
            pl.BlockSpec((D_MODEL, n_out), lambda i: (0, 0)),
        ],
        out_specs=(
            pl.BlockSpec((TM_PROJ, D_CONV), lambda i: (i, 0)),
            pl.BlockSpec((TM_PROJ, D_SSM), lambda i: (i, 0)),
        ),
        out_shape=(
            jax.ShapeDtypeStruct((t, D_CONV), F32),
            jax.ShapeDtypeStruct((t, D_SSM), F32),
        ),
        compiler_params=pltpu.CompilerParams(
            dimension_semantics=("arbitrary",), vmem_limit_bytes=40 << 20),
        name="in_proj",
    )(xt, norm_w, w_in)


def _conv_kernel(v_ref, w_ref, b_ref, lnw_ref, lnb_ref, o_ref, ext_ref):
    tt = pl.program_id(1)

    @pl.when(tt == 0)
    def _():
        ext_ref[pl.ds(0, CONV_HALO), :] = jnp.zeros((CONV_HALO, D_CONV), F32)

    @pl.when(tt > 0)
    def _():
        ext_ref[pl.ds(0, CONV_HALO), :] = ext_ref[pl.ds(TM_CONV, CONV_HALO), :]

    ext_ref[pl.ds(CONV_HALO, TM_CONV), :] = v_ref[...]
    bias = b_ref[...]
    lnw = lnw_ref[...]
    lnb = lnb_ref[...]
    tap0 = CONV_HALO - (CONV_WIDTH - 1)

    for ci in range(TM_CONV // CONV_CHUNK):
        base = ci * CONV_CHUNK
        acc = jnp.broadcast_to(bias, (CONV_CHUNK, D_CONV))
        for s in range(SUBLANES):
            taps = [j for j in range(CONV_WIDTH) if (tap0 + j) % SUBLANES == s]
            span = (tap0 + taps[-1]) // SUBLANES * SUBLANES + CONV_CHUNK
            win = ext_ref[pl.ds(base + s, span), :]
            for j in taps:
                a = (tap0 + j) // SUBLANES * SUBLANES
                acc = acc + w_ref[pl.ds(j, 1), :] * win[a:a + CONV_CHUNK, :]
        mu = jnp.mean(acc, axis=-1, keepdims=True)
        cen = acc - mu
        var = jnp.mean(cen * cen, axis=-1, keepdims=True)
        z = cen * lax.rsqrt(var + EPS) * lnw + lnb
        o_ref[pl.ds(base, CONV_CHUNK), :] = z * jax.nn.sigmoid(z)


def _conv(v, conv_w, conv_b, ln_w, ln_b, bsz, seq):
    nt = seq // TM_CONV
    row = lambda b, t: (b * nt + t, 0)
    const = lambda b, t: (0, 0)
    return pl.pallas_call(
        _conv_kernel,
        grid=(bsz, nt),
        in_specs=[
            pl.BlockSpec((TM_CONV, D_CONV), row),
            pl.BlockSpec((CONV_WIDTH, D_CONV), const),
            pl.BlockSpec((1, D_CONV), const),
            pl.BlockSpec((1, D_CONV), const),
            pl.BlockSpec((1, D_CONV), const),
        ],
        out_specs=pl.BlockSpec((TM_CONV, D_CONV), row),
        out_shape=jax.ShapeDtypeStruct(v.shape, F32),
        scratch_shapes=[pltpu.VMEM((CONV_HALO + TM_CONV, D_CONV), F32)],
        compiler_params=pltpu.CompilerParams(
            dimension_semantics=("arbitrary", "arbitrary")),
        name="conv",
    )(v, conv_w, conv_b, ln_w, ln_b)


def _cmul(a_re, a_im, b_re, b_im):
    return a_re * b_re - a_im * b_im, a_re * b_im + a_im * b_re


def _s5_kernel(u_ref, bcat_ref, ccat_ref, d_ref, wab_ref, a1_ref, atab_ref,
               pj_ref, aj_ref, o_ref, up_ref, bu_ref, st_ref, carry_ref):
    tt = pl.program_id(2)
    ns = S5_STATE

    @pl.when(tt == 0)
    def _():
        carry_ref[...] = jnp.zeros(carry_ref.shape, F32)

    for j in range(S5_STEPS):
        up_ref[pl.ds(SUBLANES * j, SUBLANES), :] = u_ref[pl.ds(j, SUBLANES, stride=S5_STEPS), :]
    up = up_ref[...]
    bu_ref[...] = _dot(up.astype(BF16), bcat_ref[0])

    a_re = a1_ref[0]
    a_im = a1_ref[1]

    def local_step(j, s):
        s_re, s_im = s
        rows = pl.ds(pl.multiple_of(j * SUBLANES, SUBLANES), SUBLANES)
        m_re, m_im = _cmul(a_re, a_im, s_re, s_im)
        n_re = m_re + bu_ref[rows, pl.ds(0, ns)]
        n_im = m_im + bu_ref[rows, pl.ds(ns, ns)]
        st_ref[rows, pl.ds(0, ns)] = n_re
        st_ref[rows, pl.ds(ns, ns)] = n_im
        return n_re, n_im

    zero = jnp.zeros((SUBLANES, ns), F32)
    e_re, e_im = lax.fori_loop(0, S5_STEPS, local_step, (zero, zero), unroll=8)

    row = lax.broadcasted_iota(jnp.int32, (SUBLANES, ns), 0)
    first = row == 0
    c_re = jnp.where(first, pltpu.roll(carry_ref[0], 1, 0), pltpu.roll(e_re, 1, 0))
    c_im = jnp.where(first, pltpu.roll(carry_ref[1], 1, 0), pltpu.roll(e_im, 1, 0))
    for i, d in enumerate((1, 2, 4)):
        r_re = pltpu.roll(c_re, d, 0)
        r_im = pltpu.roll(c_im, d, 0)
        m_re, m_im = _cmul(pj_ref[0, i], pj_ref[1, i], r_re, r_im)
        c_re = c_re + m_re
        c_im = c_im + m_im
    f_re, f_im = _cmul(aj_ref[0], aj_ref[1], c_re, c_im)
    carry_ref[0] = f_re + e_re
    carry_ref[1] = f_im + e_im

    def fix_step(j, carry):
        rows = pl.ds(pl.multiple_of(j * SUBLANES, SUBLANES), SUBLANES)
        t_re = atab_ref[0, pl.ds(j, 1), :]
        t_im = atab_ref[1, pl.ds(j, 1), :]
        m_re, m_im = _cmul(t_re, t_im, c_re, c_im)
        st_ref[rows, pl.ds(0, ns)] = st_ref[rows, pl.ds(0, ns)] + m_re
        st_ref[rows, pl.ds(ns, ns)] = st_ref[rows, pl.ds(ns, ns)] + m_im
        return carry

    lax.fori_loop(0, S5_STEPS, fix_step, 0, unroll=8)

    y = _dot(st_ref[...].astype(BF16), ccat_ref[0]) + d_ref[...] * up
    y = jax.nn.gelu(y)
    ab = _dot(y.astype(BF16), wab_ref[0])
    out = ab[:, :S5_LANES] * jax.nn.sigmoid(ab[:, S5_LANES:])
    for j in range(S5_STEPS):
        o_ref[pl.ds(j, SUBLANES, stride=S5_STEPS), :] = out[SUBLANES * j:SUBLANES * (j + 1), :]


def _s5(u, bcat, ccat, d, wab, a1, atab, pj, aj, bsz, seq):
    nt = seq // S5_TILE
    nlb = D_SSM // S5_LANES
    ns = S5_STATE
    row = lambda b, l, t: (b * nt + t, l)
    lane3 = lambda b, l, t: (0, 0, l)
    return pl.pallas_call(
        _s5_kernel,
        grid=(bsz, nlb, nt),
        in_specs=[
            pl.BlockSpec((S5_TILE, S5_LANES), row),
            pl.BlockSpec((1, S5_LANES, 2 * ns), lambda b, l, t: (l, 0, 0)),
            pl.BlockSpec((1, 2 * ns, S5_LANES), lambda b, l, t: (l, 0, 0)),
            pl.BlockSpec((1, S5_LANES), lambda b, l, t: (0, l)),
            pl.BlockSpec((1, S5_LANES, 2 * S5_LANES), lambda b, l, t: (l, 0, 0)),
            pl.BlockSpec((2, SUBLANES, ns), lane3),
            pl.BlockSpec((2, S5_STEPS, ns), lane3),
            pl.BlockSpec((2, 3, SUBLANES, ns), lambda b, l, t: (0, 0, 0, l)),
            pl.BlockSpec((2, SUBLANES, ns), lane3),
        ],
        out_specs=pl.BlockSpec((S5_TILE, S5_LANES), row),
        out_shape=jax.ShapeDtypeStruct(u.shape, F32),
        scratch_shapes=[
            pltpu.VMEM((S5_TILE, S5_LANES), F32),
            pltpu.VMEM((S5_TILE, 2 * ns), F32),
            pltpu.VMEM((S5_TILE, 2 * ns), F32),
            pltpu.VMEM((2, SUBLANES, ns), F32),
        ],
        compiler_params=pltpu.CompilerParams(
            dimension_semantics=("arbitrary", "arbitrary", "arbitrary"),
            vmem_limit_bytes=40 << 20),
        name="s5",
    )(u, bcat, ccat, d, wab, a1, atab, pj, aj)


def _kv_kernel(m_ref, nw_ref, wk_ref, wv_ref, k_ref, v_ref):
    mn = _rms(m_ref[0], nw_ref[...]).astype(BF16)
    k_ref[0] = _dot(mn, wk_ref[...]).astype(BF16)
    v_ref[0] = _dot(mn, wv_ref[...]).astype(BF16)


def _kv(mem, norm_w, wk, wv):
    bsz, mlen, d = mem.shape
    blk = pl.BlockSpec((1, mlen, d), lambda b: (b, 0, 0))
    wspec = pl.BlockSpec((d, d), lambda b: (0, 0))
    return pl.pallas_call(
        _kv_kernel,
        grid=(bsz,),
        in_specs=[blk, pl.BlockSpec((1, d), lambda b: (0, 0)), wspec, wspec],
        out_specs=(blk, blk),
        out_shape=(jax.ShapeDtypeStruct(mem.shape, BF16),) * 2,
        compiler_params=pltpu.CompilerParams(
            dimension_semantics=("arbitrary",), vmem_limit_bytes=40 << 20),
        name="kv",
    )(mem, norm_w, wk, wv)


def _mix_attn_kernel(x_ref, yc_ref, ys_ref, wot_ref, wob_ref, nx_ref, wq_ref,
                     k_ref, v_ref, wo_ref, nf_ref, wr_ref, br_ref,
                     h_ref, hf_ref, ri_ref, rf_ref, cnt_ref, run_ref):
    i = pl.program_id(0)
    tm = x_ref.shape[0]

    @pl.when(i == 0)
    def _():
        run_ref[...] = jnp.zeros(run_ref.shape, F32)

    h1 = (x_ref[...] + _dot(yc_ref[...].astype(BF16), wot_ref[...])
          + _dot(ys_ref[...].astype(BF16), wob_ref[...]))

    hn = _rms(h1, nx_ref[...]).astype(BF16)
    q = _dot(hn, wq_ref[...])
    heads = []
    for hd in range(N_XHEADS):
        sl = slice(hd * XHEAD_DIM, (hd + 1) * XHEAD_DIM)
        qh = q[:, sl].astype(BF16)
        s = lax.dot_general(qh, k_ref[0, :, sl], (((1,), (1,)), ((), ())),
                            preferred_element_type=F32) * (XHEAD_DIM ** -0.5)
        s = s - jnp.max(s, axis=-1, keepdims=True)
        p = jnp.exp(s)
        p = p / jnp.sum(p, axis=-1, keepdims=True)
        heads.append(_dot(p.astype(BF16), v_ref[0, :, sl]).astype(BF16))
    o = jnp.concatenate(heads, axis=-1)
    h2 = h1 + _dot(o, wo_ref[...])
    h_ref[...] = h2

    hf = _rms(h2, nf_ref[...])
    hf_ref[...] = hf
    logits = _dot(hf.astype(BF16), wr_ref[...]) + br_ref[...]

    lane = lax.broadcasted_iota(jnp.int32, (tm, ROUTE_LANES), 1)
    lane_f = lane.astype(F32)
    neg = -jnp.inf
    big = float(ROUTE_LANES)

    def top1(vals):
        m = jnp.max(vals, axis=-1, keepdims=True)
        idx = jnp.min(jnp.where(vals == m, lane_f, big), axis=-1, keepdims=True)
        return m, idx

    gl = jnp.where(lane < N_EXPERT_GROUPS, logits, neg)
    gmax, gidx = top1(gl)
    g_w = 1.0 / jnp.sum(jnp.exp(gl - gmax), axis=-1, keepdims=True)
    lo = EXPERT_LANE0 + EXPERTS_PER_GROUP * gidx
    el = jnp.where((lane_f >= lo) & (lane_f < lo + EXPERTS_PER_GROUP), logits, neg)
    m1, i1 = top1(el)
    m2, i2 = top1(jnp.where(lane_f == i1, neg, el))
    e21 = jnp.exp(m2 - m1)
    gate1 = g_w / (1.0 + e21)
    gate2 = g_w * e21 / (1.0 + e21)

    hot1 = lane_f == i1
    hot2 = lane_f == i2
    hot = jnp.where(hot1 | hot2, 1.0, 0.0)
    r_i = lax.broadcasted_iota(jnp.int32, (tm, tm), 0)
    c_i = lax.broadcasted_iota(jnp.int32, (tm, tm), 1)
    tri = jnp.where(r_i > c_i, 1.0, 0.0).astype(BF16)
    before = _dot(tri, hot.astype(BF16)) + run_ref[0:1, :]
    rank1 = jnp.sum(jnp.where(hot1, before, 0.0), axis=-1, keepdims=True)
    rank2 = jnp.sum(jnp.where(hot2, before, 0.0), axis=-1, keepdims=True)
    total = run_ref[0:1, :] + jnp.sum(hot, axis=0, keepdims=True)
    run_ref[...] = jnp.broadcast_to(total, run_ref.shape)
    cnt_ref[...] = jnp.broadcast_to(total, cnt_ref.shape).astype(jnp.int32)

    e1 = i1 - float(EXPERT_LANE0)
    e2 = i2 - float(EXPERT_LANE0)
    ri = jnp.where(lane == 0, e1, jnp.where(lane == 1, e2,
         jnp.where(lane == 2, rank1, jnp.where(lane == 3, rank2, 0.0))))
    ri_ref[...] = ri.astype(jnp.int32)
    rf_ref[...] = jnp.where(lane == 0, gate1, jnp.where(lane == 1, gate2, 0.0))


def _mix_attn(xt, yc, ys, wot, wob, nx, wq, k, v, wo, nf, wr, br, seq):
    t, d = xt.shape
    tiles_per_batch = seq // TM_MIX
    mlen = k.shape[1]
    row = lambda i: (i, 0)
    const = lambda i: (0, 0)
    kvspec = pl.BlockSpec((1, mlen, d), lambda i: (i // tiles_per_batch, 0, 0))
    return pl.pallas_call(
        _mix_attn_kernel,
        grid=(t // TM_MIX,),
        in_specs=[
            pl.BlockSpec((TM_MIX, d), row),
            pl.BlockSpec((TM_MIX, D_CONV), row),
            pl.BlockSpec((TM_MIX, D_SSM), row),
            pl.BlockSpec((D_CONV, d), const),
            pl.BlockSpec((D_SSM, d), const),
            pl.BlockSpec((1, d), const),
            pl.BlockSpec((d, d), const),
            kvspec, kvspec,
            pl.BlockSpec((d, d), const),
            pl.BlockSpec((1, d), const),
            pl.BlockSpec((d, ROUTE_LANES), const),
            pl.BlockSpec((1, ROUTE_LANES), const),
        ],
        out_specs=(
            pl.BlockSpec((TM_MIX, d), row),
            pl.BlockSpec((TM_MIX, d), row),
            pl.BlockSpec((TM_MIX, ROUTE_LANES), row),
            pl.BlockSpec((TM_MIX, ROUTE_LANES), row),
            pl.BlockSpec((SUBLANES, ROUTE_LANES), const),
        ),
        out_shape=(
            jax.ShapeDtypeStruct((t, d), F32),
            jax.ShapeDtypeStruct((t, d), F32),
            jax.ShapeDtypeStruct((t, ROUTE_LANES), jnp.int32),
            jax.ShapeDtypeStruct((t, ROUTE_LANES), F32),
            jax.ShapeDtypeStruct((SUBLANES, ROUTE_LANES), jnp.int32),
        ),
        scratch_shapes=[pltpu.VMEM((SUBLANES, ROUTE_LANES), F32)],
        compiler_params=pltpu.CompilerParams(
            dimension_semantics=("arbitrary",), vmem_limit_bytes=56 << 20),
        name="mix_attn",
    )(xt, yc, ys, wot, wob, nx, wq, k, v, wo, nf, wr, br)


def _row_copy(src_ref, src_row, dst_ref, dst_row, sem):
    return pltpu.make_async_copy(src_ref.at[pl.ds(src_row, 1), :],
                                 dst_ref.at[pl.ds(dst_row, 1), :], sem)


def _scatter_kernel(slot_ref, hf_ref, xs_in_ref, xs_ref, sem):
    del xs_in_ref
    tm = hf_ref.shape[0]

    def issue(r, c):
        for k in range(2):
            _row_copy(hf_ref, r, xs_ref, slot_ref[0, 0, 2 * r + k], sem).start()
        return c

    lax.fori_loop(0, tm, issue, 0)

    def drain(r, c):
        for k in range(2):
            _row_copy(hf_ref, 0, xs_ref, 0, sem).wait()
        return c

    lax.fori_loop(0, tm, drain, 0)


def _scatter(slots, hf, cap):
    t, d = hf.shape
    nsteps = t // TM_ROWS
    slots3 = slots.reshape(nsteps, 1, 2 * TM_ROWS)
    return pl.pallas_call(
        _scatter_kernel,
        grid=(nsteps,),
        in_specs=[
            pl.BlockSpec((1, 1, 2 * TM_ROWS), lambda i: (i, 0, 0),
                         memory_space=pltpu.SMEM),
            pl.BlockSpec((TM_ROWS, d), lambda i: (i, 0)),
            pl.BlockSpec(memory_space=pl.ANY),
        ],
        out_specs=pl.BlockSpec(memory_space=pl.ANY),
        out_shape=jax.ShapeDtypeStruct((cap, d), F32),
        scratch_shapes=[pltpu.SemaphoreType.DMA(())],
        input_output_aliases={2: 0},
        compiler_params=pltpu.CompilerParams(dimension_semantics=("arbitrary",)),
        name="scatter",
    )(slots3, hf, jnp.zeros((cap, d), F32))


def _experts_kernel(be_ref, na_ref, xs_ref, w1_ref, w3_ref, w2_ref, ys_ref,
                    w1s_ref, w3s_ref, w2s_ref):
    b = pl.program_id(0)
    active = b < na_ref[0]
    prev = be_ref[jnp.maximum(b - 1, 0)]
    fresh = (b == 0) | (be_ref[b] != prev)

    @pl.when(active & fresh)
    def _():
        w1s_ref[...] = w1_ref[0].astype(BF16)
        w3s_ref[...] = w3_ref[0].astype(BF16)
        w2s_ref[...] = w2_ref[0].astype(BF16)

    @pl.when(active)
    def _():
        xb = xs_ref[...].astype(BF16)
        h1 = _dot(xb, w1s_ref[...])
        h3 = _dot(xb, w3s_ref[...])
        hid = (h1 * jax.nn.sigmoid(h1) * h3).astype(BF16)
        ys_ref[...] = _dot(hid, w2s_ref[...])

    @pl.when(jnp.logical_not(active))
    def _():
        ys_ref[...] = jnp.zeros(ys_ref.shape, F32)


def _experts(block_expert, n_active, xs, w1, w3, w2):
    cap, d = xs.shape
    nb = cap // MOE_BLOCK

    def blk(b, be, na):
        return (jnp.minimum(b, na[0] - 1), 0)

    def wsel(b, be, na):
        return (be[jnp.minimum(b, na[0] - 1)], 0, 0)

    grid_spec = pltpu.PrefetchScalarGridSpec(
        num_scalar_prefetch=2,
        grid=(nb,),
        in_specs=[
            pl.BlockSpec((MOE_BLOCK, d), blk),
            pl.BlockSpec((1, d, D_EXPERT), wsel),
            pl.BlockSpec((1, d, D_EXPERT), wsel),
            pl.BlockSpec((1, D_EXPERT, d), wsel),
        ],
        out_specs=pl.BlockSpec((MOE_BLOCK, d), lambda b, be, na: (b, 0)),
        scratch_shapes=[
            pltpu.VMEM((d, D_EXPERT), BF16),
            pltpu.VMEM((d, D_EXPERT), BF16),
            pltpu.VMEM((D_EXPERT, d), BF16),
        ],
    )
    return pl.pallas_call(
        _experts_kernel,
        grid_spec=grid_spec,
        out_shape=jax.ShapeDtypeStruct((cap, d), F32),
        compiler_params=pltpu.CompilerParams(
            dimension_semantics=("arbitrary",), vmem_limit_bytes=48 << 20),
        name="experts",
    )(block_expert, n_active, xs, w1, w3, w2)


def _combine_kernel(slot_ref, h_ref, gate_ref, fw_ref, ys_ref, o_ref,
                    ya_ref, yb_ref, sem):
    tm = h_ref.shape[0]
    bufs = (ya_ref, yb_ref)

    def issue(r, c):
        for k in range(2):
            _row_copy(ys_ref, slot_ref[0, 0, 2 * r + k], bufs[k], r, sem).start()
        return c

    lax.fori_loop(0, tm, issue, 0)

    def drain(r, c):
        for k in range(2):
            _row_copy(ys_ref, 0, bufs[k], 0, sem).wait()
        return c

    lax.fori_loop(0, tm, drain, 0)

    gates = gate_ref[...]
    h3 = h_ref[...] + gates[:, 0:1] * ya_ref[...] + gates[:, 1:2] * yb_ref[...]
    o_ref[...] = _rms(h3, fw_ref[...])


def _combine(slots, h2, gates, final_w, ys):
    t, d = h2.shape
    nsteps = t // TM_ROWS
    slots3 = slots.reshape(nsteps, 1, 2 * TM_ROWS)
    return pl.pallas_call(
        _combine_kernel,
        grid=(nsteps,),
        in_specs=[
            pl.BlockSpec((1, 1, 2 * TM_ROWS), lambda i: (i, 0, 0),
                         memory_space=pltpu.SMEM),
            pl.BlockSpec((TM_ROWS, d), lambda i: (i, 0)),
            pl.BlockSpec((TM_ROWS, ROUTE_LANES), lambda i: (i, 0)),
            pl.BlockSpec((1, d), lambda i: (0, 0)),
            pl.BlockSpec(memory_space=pl.ANY),
        ],
        out_specs=pl.BlockSpec((TM_ROWS, d), lambda i: (i, 0)),
        out_shape=jax.ShapeDtypeStruct((t, d), F32),
        scratch_shapes=[
            pltpu.VMEM((TM_ROWS, d), F32),
            pltpu.VMEM((TM_ROWS, d), F32),
            pltpu.SemaphoreType.DMA(()),
        ],
        compiler_params=pltpu.CompilerParams(dimension_semantics=("arbitrary",)),
        name="combine",
    )(slots3, h2, gates, final_w, ys)


def _layer(h, mem, p, final_w):
    bsz, seq, d = h.shape
    t = bsz * seq
    xt = h.reshape(t, d)

    v, u = _in_proj(xt, p["norm_mix_w"].reshape(1, d), p["w_in"].astype(BF16))
    y_conv = _conv(v, p["conv_w"], p["conv_b"].reshape(1, D_CONV),
                   p["conv_ln_w"].reshape(1, D_CONV), p["conv_ln_b"].reshape(1, D_CONV),
                   bsz, seq)

    a1, atab, pj, aj, bbar = _s5_prep(p["ssm_A_re"], p["ssm_A_im"], p["ssm_log_dt"],
                                      p["ssm_B_re"], p["ssm_B_im"])
    nlb = D_SSM // S5_LANES
    bb = bbar.reshape(2, SSM_GROUP, nlb, S5_GROUPS, SSM_STATE).transpose(0, 2, 3, 1, 4)
    bcat = jnp.concatenate([_block_diag(bb[0]), _block_diag(bb[1])], axis=-1).astype(BF16)
    c_re = p["ssm_C_re"].reshape(nlb, S5_GROUPS, SSM_GROUP, SSM_STATE).transpose(0, 1, 3, 2)
    c_im = p["ssm_C_im"].reshape(nlb, S5_GROUPS, SSM_GROUP, SSM_STATE).transpose(0, 1, 3, 2)
    ccat = jnp.concatenate([_block_diag(c_re), -_block_diag(c_im)], axis=1).astype(BF16)
    glu = p["ssm_glu_w"].reshape(nlb, S5_GROUPS, SSM_GROUP, 2 * SSM_GROUP)
    wab = jnp.concatenate([_block_diag(glu[..., :SSM_GROUP]),
                           _block_diag(glu[..., SSM_GROUP:])], axis=-1).astype(BF16)
    y_ssm = _s5(u, bcat, ccat, p["ssm_D"].reshape(1, D_SSM), wab,
                a1, atab, pj, aj, bsz, seq)

    k, vv = _kv(mem, p["norm_mem_w"].reshape(1, d), p["xk_w"].astype(BF16),
                p["xv_w"].astype(BF16))
    w_out = p["w_out"].astype(BF16)
    wr = jnp.concatenate([p["router_group_w"], p["router_expert_w"]], axis=1)
    wr = jnp.pad(wr, ((0, 0), (0, ROUTE_LANES - wr.shape[1]))).astype(BF16)
    br = jnp.concatenate([p["router_group_b"], p["router_expert_b"].reshape(-1)])
    br = jnp.pad(br, (0, ROUTE_LANES - br.shape[0])).reshape(1, ROUTE_LANES)
    h2, hf, route_i, route_f, cnt = _mix_attn(
        xt, y_conv, y_ssm, w_out[:D_CONV], w_out[D_CONV:], p["norm_x_w"].reshape(1, d),
        p["xq_w"].astype(BF16), k, vv, p["xo_w"].astype(BF16),
        p["norm_ffn_w"].reshape(1, d), wr, br, seq)

    counts = cnt[0, EXPERT_LANE0:EXPERT_LANE0 + N_EXPERTS]
    padded = (counts + MOE_BLOCK - 1) // MOE_BLOCK * MOE_BLOCK
    pend = jnp.cumsum(padded)
    pstart = pend - padded
    experts = route_i[:, 0:2]
    slots = (pstart[experts] + route_i[:, 2:4]).reshape(-1).astype(jnp.int32)
    n_blocks = -(-2 * t // MOE_BLOCK) + N_EXPERTS
    cap = n_blocks * MOE_BLOCK
    block_start = jnp.arange(n_blocks, dtype=jnp.int32) * MOE_BLOCK
    block_expert = jnp.minimum(
        jnp.sum((block_start[:, None] >= pend[None, :]).astype(jnp.int32), axis=1),
        N_EXPERTS - 1).astype(jnp.int32)
    n_active = (pend[-1:] // MOE_BLOCK).astype(jnp.int32)

    xs = _scatter(slots, hf, cap)
    ys = _experts(block_expert, n_active, xs, p["moe_w1"], p["moe_w3"], p["moe_w2"])
    out = _combine(slots, h2, route_f, final_w.reshape(1, d), ys)
    return out.reshape(bsz, seq, d)


def kernel(x, mem, norm_mix_w, w_in, conv_w, conv_b, conv_ln_w, conv_ln_b, ssm_A_re, ssm_A_im, ssm_log_dt, ssm_B_re, ssm_B_im, ssm_C_re, ssm_C_im, ssm_D, ssm_glu_w, w_out, norm_x_w, norm_mem_w, xq_w, xk_w, xv_w, xo_w, norm_ffn_w, router_group_w, router_group_b, router_expert_w, router_expert_b, moe_w1, moe_w3, moe_w2, final_norm_w):
    stacked = dict(
        norm_mix_w=norm_mix_w, w_in=w_in, conv_w=conv_w, conv_b=conv_b,
        conv_ln_w=conv_ln_w, conv_ln_b=conv_ln_b, ssm_A_re=ssm_A_re, ssm_A_im=ssm_A_im,
        ssm_log_dt=ssm_log_dt, ssm_B_re=ssm_B_re, ssm_B_im=ssm_B_im, ssm_C_re=ssm_C_re,
        ssm_C_im=ssm_C_im, ssm_D=ssm_D, ssm_glu_w=ssm_glu_w, w_out=w_out,
        norm_x_w=norm_x_w, norm_mem_w=norm_mem_w, xq_w=xq_w, xk_w=xk_w, xv_w=xv_w,
        xo_w=xo_w, norm_ffn_w=norm_ffn_w, router_group_w=router_group_w,
        router_group_b=router_group_b, router_expert_w=router_expert_w,
        router_expert_b=router_expert_b, moe_w1=moe_w1, moe_w3=moe_w3, moe_w2=moe_w2)
    depth = norm_mix_w.shape[0]
    assert depth == 1, "final norm is fused into the single layer's combine step"
    layer = {name: w[0] for name, w in stacked.items()}
    return _layer(x, mem, layer, final_norm_w)
```

```python
import functools

import jax
import jax.numpy as jnp
from jax import lax
from jax.experimental import pallas as pl
from jax.experimental.pallas import tpu as pltpu

D_MODEL = 1024
D_CONV = 512
CONV_WIDTH = 31
D_SSM = 512
SSM_GROUP = 16
N_SSM_GROUPS = 32
SSM_STATE = 64
N_XHEADS = 4
XHEAD_DIM = 256
N_EXPERT_GROUPS = 4
EXPERTS_PER_GROUP = 8
N_EXPERTS = 32
D_EXPERT = 512
EPS = 1e-6

F32 = jnp.float32
BF16 = jnp.bfloat16

SUBLANES = 8
LANES = 128

TM_PROJ = 512
TM_CONV = 512
CONV_CHUNK = 128
CONV_NORM_ROWS = 128
CONV_HALO = 32
S5_STEPS = 64
S5_TILE = SUBLANES * S5_STEPS
S5_LANES = 128
S5_GROUPS = S5_LANES // SSM_GROUP
S5_STATE = S5_GROUPS * SSM_STATE
TM_MIX = 512
ROUTE_LANES = 128
EXPERT_LANE0 = N_EXPERT_GROUPS
MOE_BLOCK = 256
TM_ROWS = 256


def _rms(x, w):
    return x * lax.rsqrt(jnp.mean(x * x, axis=-1, keepdims=True) + EPS) * w


def _dot(a, b):
    return jnp.dot(a, b, preferred_element_type=F32)


def _s5_prep_kernel(are_ref, aim_ref, ldt_ref, btre_ref, btim_ref,
                    a1_ref, atab_ref, pj_ref, aj_ref, bbar_ref):
    lam_re = are_ref[...]
    lam_im = aim_ref[...]
    dt = jnp.exp(ldt_ref[...])
    x = lam_re * dt
    y = lam_im * dt

    def power(k):
        mag = jnp.exp(k * x)
        return mag * jnp.cos(k * y), mag * jnp.sin(k * y)

    n = x.shape[-1]
    ones8 = jnp.ones((SUBLANES, n), F32)
    a_re, a_im = power(ones8)
    a1_ref[0] = a_re
    a1_ref[1] = a_im

    steps = (lax.broadcasted_iota(jnp.int32, (S5_STEPS, n), 0) + 1).astype(F32)
    t_re, t_im = power(steps)
    atab_ref[0] = t_re
    atab_ref[1] = t_im

    row = lax.broadcasted_iota(jnp.int32, (SUBLANES, n), 0)
    for i, d in enumerate((1, 2, 4)):
        p_re, p_im = power(ones8 * float(d * S5_STEPS))
        keep = row >= d
        pj_ref[0, i] = jnp.where(keep, p_re, 0.0)
        pj_ref[1, i] = jnp.where(keep, p_im, 0.0)
    j_re, j_im = power(ones8 * float(S5_STEPS))
    aj_ref[0] = j_re
    aj_ref[1] = j_im

    num_re = a_re[0:1] - 1.0
    num_im = a_im[0:1]
    den = lam_re * lam_re + lam_im * lam_im
    c_re = (num_re * lam_re + num_im * lam_im) / den
    c_im = (num_im * lam_re - num_re * lam_im) / den
    b_re = btre_ref[...]
    b_im = btim_ref[...]
    bbar_ref[0] = c_re * b_re - c_im * b_im
    bbar_ref[1] = c_re * b_im + c_im * b_re


def _s5_prep(a_re, a_im, log_dt, b_re, b_im):
    n = N_SSM_GROUPS * SSM_STATE
    are = a_re.reshape(1, n)
    aim = a_im.reshape(1, n)
    ldt = jnp.repeat(log_dt, SSM_STATE).reshape(1, n)
    btre = jnp.transpose(b_re, (2, 0, 1)).reshape(SSM_GROUP, n)
    btim = jnp.transpose(b_im, (2, 0, 1)).reshape(SSM_GROUP, n)
    return pl.pallas_call(
        _s5_prep_kernel,
        out_shape=(
            jax.ShapeDtypeStruct((2, SUBLANES, n), F32),
            jax.ShapeDtypeStruct((2, S5_STEPS, n), F32),
            jax.ShapeDtypeStruct((2, 3, SUBLANES, n), F32),
            jax.ShapeDtypeStruct((2, SUBLANES, n), F32),
            jax.ShapeDtypeStruct((2, SSM_GROUP, n), F32),
        ),
        name="s5_prep",
    )(are, aim, ldt, btre, btim)


def _block_diag(w):
    nl, g, r, c = w.shape
    eye = jnp.eye(g, dtype=w.dtype)
    return jnp.einsum("lgrc,gh->lgrhc", w, eye).reshape(nl, g * r, g * c)


def _in_proj_kernel(x_ref, nw_ref, w_ref, v_ref, u_ref):
    xn = _rms(x_ref[...], nw_ref[...]).astype(BF16)
    proj = _dot(xn, w_ref[...])
    a = proj[:, :D_CONV]
    g = proj[:, D_CONV:2 * D_CONV]
    v_ref[...] = a * jax.nn.sigmoid(g)
    u_ref[...] = proj[:, 2 * D_CONV:]


def _in_proj(xt, norm_w, w_in):
    t = xt.shape[0]
    n_out = 2 * D_CONV + D_SSM
    return pl.pallas_call(
        _in_proj_kernel,
        grid=(t // TM_PROJ,),
        in_specs=[
            pl.BlockSpec((TM_PROJ, D_MODEL), lambda i: (i, 0)),
            pl.BlockSpec((1, D_MODEL), lambda i: (0, 0)),
            pl.BlockSpec((D_MODEL, n_out), lambda i: (0, 0)),
        ],
        out_specs=(
            pl.BlockSpec((TM_PROJ, D_CONV), lambda i: (i, 0)),
            pl.BlockSpec((TM_PROJ, D_SSM), lambda i: (i, 0)),
        ),
        out_shape=(
            jax.ShapeDtypeStruct((t, D_CONV), F32),
            jax.ShapeDtypeStruct((t, D_SSM), F32),
        ),
        compiler_params=pltpu.CompilerParams(
            dimension_semantics=("arbitrary",), vmem_limit_bytes=40 << 20),
        name="in_proj",
    )(xt, norm_w, w_in)


def _conv_kernel(v_ref, w_ref, b_ref, lnw_ref, lnb_ref, o_ref, ext_ref, sh_ref):
    tt = pl.program_id(1)
    rows = CONV_HALO + TM_CONV

    @pl.when(tt == 0)
    def _():
        ext_ref[pl.ds(0, CONV_HALO), :] = jnp.zeros((CONV_HALO, D_CONV), F32)

    @pl.when(tt > 0)
    def _():
        ext_ref[pl.ds(0, CONV_HALO), :] = ext_ref[pl.ds(TM_CONV, CONV_HALO), :]

    ext_ref[pl.ds(CONV_HALO, TM_CONV), :] = v_ref[...]
    for s in range(1, SUBLANES):
        sh_ref[s - 1, pl.ds(0, rows - SUBLANES), :] = ext_ref[pl.ds(s, rows - SUBLANES), :]
    bias = b_ref[...]
    lnw = lnw_ref[...]
    lnb = lnb_ref[...]
    tap0 = CONV_HALO - (CONV_WIDTH - 1)

    groups = CONV_CHUNK // SUBLANES

    def chunk(ci, carry):
        base = pl.multiple_of(ci * CONV_CHUNK, CONV_CHUNK)
        for lt in range(D_CONV // LANES):
            lanes = pl.ds(lt * LANES, LANES)
            acc = [jnp.broadcast_to(bias[:, lt * LANES:(lt + 1) * LANES], (SUBLANES, LANES))] * groups
            for s in range(SUBLANES):
                taps = [j for j in range(CONV_WIDTH) if (tap0 + j) % SUBLANES == s]
                src = ext_ref if s == 0 else sh_ref.at[s - 1]
                ngroups = (tap0 + taps[-1] - s) // SUBLANES + groups
                win = [src[pl.ds(base + SUBLANES * g, SUBLANES), lanes] for g in range(ngroups)]
                for j in taps:
                    g0 = (tap0 + j - s) // SUBLANES
                    wj = w_ref[j, :, lanes]
                    acc = [acc[r] + wj * win[g0 + r] for r in range(groups)]
            o_ref[pl.ds(base, CONV_CHUNK), lanes] = jnp.concatenate(acc, axis=0)
        return carry

    lax.fori_loop(0, TM_CONV // CONV_CHUNK, chunk, 0)

    for bi in range(TM_CONV // CONV_NORM_ROWS):
        rows_b = pl.ds(bi * CONV_NORM_ROWS, CONV_NORM_ROWS)
        acc = o_ref[rows_b, :]
        mu = jnp.mean(acc, axis=-1, keepdims=True)
        cen = acc - mu
        var = jnp.mean(cen * cen, axis=-1, keepdims=True)
        z = cen * lax.rsqrt(var + EPS) * lnw + lnb
        o_ref[rows_b, :] = z * jax.nn.sigmoid(z)


def _conv(v, conv_w, conv_b, ln_w, ln_b, bsz, seq):
    nt = seq // TM_CONV
    row = lambda b, t: (b * nt + t, 0)
    const = lambda b, t: (0, 0)
    return pl.pallas_call(
        _conv_kernel,
        grid=(bsz, nt),
        in_specs=[
            pl.BlockSpec((TM_CONV, D_CONV), row),
            pl.BlockSpec((CONV_WIDTH, SUBLANES, D_CONV), lambda b, t: (0, 0, 0)),
            pl.BlockSpec((1, D_CONV), const),
            pl.BlockSpec((1, D_CONV), const),
            pl.BlockSpec((1, D_CONV), const),
        ],
        out_specs=pl.BlockSpec((TM_CONV, D_CONV), row),
        out_shape=jax.ShapeDtypeStruct(v.shape, F32),
        scratch_shapes=[
            pltpu.VMEM((CONV_HALO + TM_CONV, D_CONV), F32),
            pltpu.VMEM((SUBLANES - 1, CONV_HALO + TM_CONV, D_CONV), F32),
        ],
        compiler_params=pltpu.CompilerParams(
            dimension_semantics=("arbitrary", "arbitrary")),
        name="conv",
    )(v, jnp.broadcast_to(conv_w[:, None, :], (CONV_WIDTH, SUBLANES, D_CONV)), conv_b, ln_w, ln_b)


def _cmul(a_re, a_im, b_re, b_im):
    return a_re * b_re - a_im * b_im, a_re * b_im + a_im * b_re


def _s5_kernel(u_ref, bcat_ref, ccat_ref, d_ref, wab_ref, a1_ref, atab_ref,
               pj_ref, aj_ref, o_ref, up_ref, bu_ref, st_ref, carry_ref):
    tt = pl.program_id(2)
    ns = S5_STATE

    @pl.when(tt == 0)
    def _():
        carry_ref[...] = jnp.zeros(carry_ref.shape, F32)

    for j in range(S5_STEPS):
        up_ref[pl.ds(SUBLANES * j, SUBLANES), :] = u_ref[pl.ds(j, SUBLANES, stride=S5_STEPS), :]
    up = up_ref[...]
    bu_ref[...] = _dot(up.astype(BF16), bcat_ref[0])

    a_re = a1_ref[0]
    a_im = a1_ref[1]

    def local_step(j, s):
        s_re, s_im = s
        rows = pl.ds(pl.multiple_of(j * SUBLANES, SUBLANES), SUBLANES)
        m_re, m_im = _cmul(a_re, a_im, s_re, s_im)
        n_re = m_re + bu_ref[rows, pl.ds(0, ns)]
        n_im = m_im + bu_ref[rows, pl.ds(ns, ns)]
        st_ref[rows, pl.ds(0, ns)] = n_re
        st_ref[rows, pl.ds(ns, ns)] = n_im
        return n_re, n_im

    zero = jnp.zeros((SUBLANES, ns), F32)
    e_re, e_im = lax.fori_loop(0, S5_STEPS, local_step, (zero, zero), unroll=8)

    row = lax.broadcasted_iota(jnp.int32, (SUBLANES, ns), 0)
    first = row == 0
    c_re = jnp.where(first, pltpu.roll(carry_ref[0], 1, 0), pltpu.roll(e_re, 1, 0))
    c_im = jnp.where(first, pltpu.roll(carry_ref[1], 1, 0), pltpu.roll(e_im, 1, 0))
    for i, d in enumerate((1, 2, 4)):
        r_re = pltpu.roll(c_re, d, 0)
        r_im = pltpu.roll(c_im, d, 0)
        m_re, m_im = _cmul(pj_ref[0, i], pj_ref[1, i], r_re, r_im)
        c_re = c_re + m_re
        c_im = c_im + m_im
    f_re, f_im = _cmul(aj_ref[0], aj_ref[1], c_re, c_im)
    carry_ref[0] = f_re + e_re
    carry_ref[1] = f_im + e_im

    def fix_step(j, carry):
        rows = pl.ds(pl.multiple_of(j * SUBLANES, SUBLANES), SUBLANES)
        t_re = atab_ref[0, pl.ds(j, 1), :]
        t_im = atab_ref[1, pl.ds(j, 1), :]
        m_re, m_im = _cmul(t_re, t_im, c_re, c_im)
        st_ref[rows, pl.ds(0, ns)] = st_ref[rows, pl.ds(0, ns)] + m_re
        st_ref[rows, pl.ds(ns, ns)] = st_ref[rows, pl.ds(ns, ns)] + m_im
        return carry

    lax.fori_loop(0, S5_STEPS, fix_step, 0, unroll=8)

    y = _dot(st_ref[...].astype(BF16), ccat_ref[0]) + d_ref[...] * up
    y = jax.nn.gelu(y)
    ab = _dot(y.astype(BF16), wab_ref[0])
    out = ab[:, :S5_LANES] * jax.nn.sigmoid(ab[:, S5_LANES:])
    for j in range(S5_STEPS):
        o_ref[pl.ds(j, SUBLANES, stride=S5_STEPS), :] = out[SUBLANES * j:SUBLANES * (j + 1), :]


def _s5(u, bcat, ccat, d, wab, a1, atab, pj, aj, bsz, seq):
    nt = seq // S5_TILE
    nlb = D_SSM // S5_LANES
    ns = S5_STATE
    row = lambda b, l, t: (b * nt + t, l)
    lane3 = lambda b, l, t: (0, 0, l)
    return pl.pallas_call(
        _s5_kernel,
        grid=(bsz, nlb, nt),
        in_specs=[
            pl.BlockSpec((S5_TILE, S5_LANES), row),
            pl.BlockSpec((1, S5_LANES, 2 * ns), lambda b, l, t: (l, 0, 0)),
            pl.BlockSpec((1, 2 * ns, S5_LANES), lambda b, l, t: (l, 0, 0)),
            pl.BlockSpec((1, S5_LANES), lambda b, l, t: (0, l)),
            pl.BlockSpec((1, S5_LANES, 2 * S5_LANES), lambda b, l, t: (l, 0, 0)),
            pl.BlockSpec((2, SUBLANES, ns), lane3),
            pl.BlockSpec((2, S5_STEPS, ns), lane3),
            pl.BlockSpec((2, 3, SUBLANES, ns), lambda b, l, t: (0, 0, 0, l)),
            pl.BlockSpec((2, SUBLANES, ns), lane3),
        ],
        out_specs=pl.BlockSpec((S5_TILE, S5_LANES), row),
        out_shape=jax.ShapeDtypeStruct(u.shape, F32),
        scratch_shapes=[
            pltpu.VMEM((S5_TILE, S5_LANES), F32),
            pltpu.VMEM((S5_TILE, 2 * ns), F32),
            pltpu.VMEM((S5_TILE, 2 * ns), F32),
            pltpu.VMEM((2, SUBLANES, ns), F32),
        ],
        compiler_params=pltpu.CompilerParams(
            dimension_semantics=("arbitrary", "arbitrary", "arbitrary"),
            vmem_limit_bytes=40 << 20),
        name="s5",
    )(u, bcat, ccat, d, wab, a1, atab, pj, aj)


def _kv_kernel(m_ref, nw_ref, wk_ref, wv_ref, k_ref, v_ref):
    mn = _rms(m_ref[0], nw_ref[...]).astype(BF16)
    k_ref[0] = _dot(mn, wk_ref[...]).astype(BF16)
    v_ref[0] = _dot(mn, wv_ref[...]).astype(BF16)


def _kv(mem, norm_w, wk, wv):
    bsz, mlen, d = mem.shape
    blk = pl.BlockSpec((1, mlen, d), lambda b: (b, 0, 0))
    wspec = pl.BlockSpec((d, d), lambda b: (0, 0))
    return pl.pallas_call(
        _kv_kernel,
        grid=(bsz,),
        in_specs=[blk, pl.BlockSpec((1, d), lambda b: (0, 0)), wspec, wspec],
        out_specs=(blk, blk),
        out_shape=(jax.ShapeDtypeStruct(mem.shape, BF16),) * 2,
        compiler_params=pltpu.CompilerParams(
            dimension_semantics=("arbitrary",), vmem_limit_bytes=40 << 20),
        name="kv",
    )(mem, norm_w, wk, wv)


def _mix_attn_kernel(x_ref, yc_ref, ys_ref, wot_ref, wob_ref, nx_ref, wq_ref,
                     k_ref, v_ref, wo_ref, nf_ref, wr_ref, br_ref,
                     h_ref, hf_ref, ri_ref, rf_ref, cnt_ref, run_ref):
    i = pl.program_id(0)
    tm = x_ref.shape[0]

    @pl.when(i == 0)
    def _():
        run_ref[...] = jnp.zeros(run_ref.shape, F32)

    h1 = (x_ref[...] + _dot(yc_ref[...].astype(BF16), wot_ref[...])
          + _dot(ys_ref[...].astype(BF16), wob_ref[...]))

    hn = _rms(h1, nx_ref[...]).astype(BF16)
    q = _dot(hn, wq_ref[...])
    heads = []
    for hd in range(N_XHEADS):
        sl = slice(hd * XHEAD_DIM, (hd + 1) * XHEAD_DIM)
        qh = q[:, sl].astype(BF16)
        s = lax.dot_general(qh, k_ref[0, :, sl], (((1,), (1,)), ((), ())),
                            preferred_element_type=F32) * (XHEAD_DIM ** -0.5)
        s = s - jnp.max(s, axis=-1, keepdims=True)
        p = jnp.exp(s)
        p = p / jnp.sum(p, axis=-1, keepdims=True)
        heads.append(_dot(p.astype(BF16), v_ref[0, :, sl]).astype(BF16))
    o = jnp.concatenate(heads, axis=-1)
    h2 = h1 + _dot(o, wo_ref[...])
    h_ref[...] = h2

    hf = _rms(h2, nf_ref[...])
    hf_ref[...] = hf
    logits = _dot(hf.astype(BF16), wr_ref[...]) + br_ref[...]

    lane = lax.broadcasted_iota(jnp.int32, (tm, ROUTE_LANES), 1)
    lane_f = lane.astype(F32)
    neg = -jnp.inf
    big = float(ROUTE_LANES)

    def top1(vals):
        m = jnp.max(vals, axis=-1, keepdims=True)
        idx = jnp.min(jnp.where(vals == m, lane_f, big), axis=-1, keepdims=True)
        return m, idx

    gl = jnp.where(lane < N_EXPERT_GROUPS, logits, neg)
    gmax, gidx = top1(gl)
    g_w = 1.0 / jnp.sum(jnp.exp(gl - gmax), axis=-1, keepdims=True)
    lo = EXPERT_LANE0 + EXPERTS_PER_GROUP * gidx
    el = jnp.where((lane_f >= lo) & (lane_f < lo + EXPERTS_PER_GROUP), logits, neg)
    m1, i1 = top1(el)
    m2, i2 = top1(jnp.where(lane_f == i1, neg, el))
    e21 = jnp.exp(m2 - m1)
    gate1 = g_w / (1.0 + e21)
    gate2 = g_w * e21 / (1.0 + e21)

    hot1 = lane_f == i1
    hot2 = lane_f == i2
    hot = jnp.where(hot1 | hot2, 1.0, 0.0)
    r_i = lax.broadcasted_iota(jnp.int32, (tm, tm), 0)
    c_i = lax.broadcasted_iota(jnp.int32, (tm, tm), 1)
    tri = jnp.where(r_i > c_i, 1.0, 0.0).astype(BF16)
    before = _dot(tri, hot.astype(BF16)) + run_ref[0:1, :]
    rank1 = jnp.sum(jnp.where(hot1, before, 0.0), axis=-1, keepdims=True)
    rank2 = jnp.sum(jnp.where(hot2, before, 0.0), axis=-1, keepdims=True)
    total = run_ref[0:1, :] + jnp.sum(hot, axis=0, keepdims=True)
    run_ref[...] = jnp.broadcast_to(total, run_ref.shape)
    cnt_ref[...] = jnp.broadcast_to(total, cnt_ref.shape).astype(jnp.int32)

    e1 = i1 - float(EXPERT_LANE0)
    e2 = i2 - float(EXPERT_LANE0)
    ri = jnp.where(lane == 0, e1, jnp.where(lane == 1, e2,
         jnp.where(lane == 2, rank1, jnp.where(lane == 3, rank2, 0.0))))
    ri_ref[...] = ri.astype(jnp.int32)
    rf_ref[...] = jnp.where(lane == 0, gate1, jnp.where(lane == 1, gate2, 0.0))


def _mix_attn(xt, yc, ys, wot, wob, nx, wq, k, v, wo, nf, wr, br, seq):
    t, d = xt.shape
    tiles_per_batch = seq // TM_MIX
    mlen = k.shape[1]
    row = lambda i: (i, 0)
    const = lambda i: (0, 0)
    kvspec = pl.BlockSpec((1, mlen, d), lambda i: (i // tiles_per_batch, 0, 0))
    return pl.pallas_call(
        _mix_attn_kernel,
        grid=(t // TM_MIX,),
        in_specs=[
            pl.BlockSpec((TM_MIX, d), row),
            pl.BlockSpec((TM_MIX, D_CONV), row),
            pl.BlockSpec((TM_MIX, D_SSM), row),
            pl.BlockSpec((D_CONV, d), const),
            pl.BlockSpec((D_SSM, d), const),
            pl.BlockSpec((1, d), const),
            pl.BlockSpec((d, d), const),
            kvspec, kvspec,
            pl.BlockSpec((d, d), const),
            pl.BlockSpec((1, d), const),
            pl.BlockSpec((d, ROUTE_LANES), const),
            pl.BlockSpec((1, ROUTE_LANES), const),
        ],
        out_specs=(
            pl.BlockSpec((TM_MIX, d), row),
            pl.BlockSpec((TM_MIX, d), row),
            pl.BlockSpec((TM_MIX, ROUTE_LANES), row),
            pl.BlockSpec((TM_MIX, ROUTE_LANES), row),
            pl.BlockSpec((SUBLANES, ROUTE_LANES), const),
        ),
        out_shape=(
            jax.ShapeDtypeStruct((t, d), F32),
            jax.ShapeDtypeStruct((t, d), F32),
            jax.ShapeDtypeStruct((t, ROUTE_LANES), jnp.int32),
            jax.ShapeDtypeStruct((t, ROUTE_LANES), F32),
            jax.ShapeDtypeStruct((SUBLANES, ROUTE_LANES), jnp.int32),
        ),
        scratch_shapes=[pltpu.VMEM((SUBLANES, ROUTE_LANES), F32)],
        compiler_params=pltpu.CompilerParams(
            dimension_semantics=("arbitrary",), vmem_limit_bytes=56 << 20),
        name="mix_attn",
    )(xt, yc, ys, wot, wob, nx, wq, k, v, wo, nf, wr, br)


def _row_copy(src_ref, src_row, dst_ref, dst_row, sem):
    return pltpu.make_async_copy(src_ref.at[pl.ds(src_row, 1), :],
                                 dst_ref.at[pl.ds(dst_row, 1), :], sem)


def _scatter_kernel(slot_ref, hf_ref, xs_in_ref, xs_ref, sem):
    del xs_in_ref
    tm = hf_ref.shape[0]

    def issue(r, c):
        for k in range(2):
            _row_copy(hf_ref, r, xs_ref, slot_ref[0, 0, 2 * r + k], sem).start(priority=k)
        return c

    lax.fori_loop(0, tm, issue, 0)

    def drain(r, c):
        for k in range(2):
            _row_copy(hf_ref, 0, xs_ref, 0, sem).wait()
        return c

    lax.fori_loop(0, tm, drain, 0)


def _scatter(slots, hf, cap):
    t, d = hf.shape
    nsteps = t // TM_ROWS
    slots3 = slots.reshape(nsteps, 1, 2 * TM_ROWS)
    return pl.pallas_call(
        _scatter_kernel,
        grid=(nsteps,),
        in_specs=[
            pl.BlockSpec((1, 1, 2 * TM_ROWS), lambda i: (i, 0, 0),
                         memory_space=pltpu.SMEM),
            pl.BlockSpec((TM_ROWS, d), lambda i: (i, 0)),
            pl.BlockSpec(memory_space=pl.ANY),
        ],
        out_specs=pl.BlockSpec(memory_space=pl.ANY),
        out_shape=jax.ShapeDtypeStruct((cap, d), F32),
        scratch_shapes=[pltpu.SemaphoreType.DMA(())],
        input_output_aliases={2: 0},
        compiler_params=pltpu.CompilerParams(dimension_semantics=("arbitrary",)),
        name="scatter",
    )(slots3, hf, jnp.zeros((cap, d), F32))


def _experts_kernel(be_ref, na_ref, xs_ref, w1_ref, w3_ref, w2_ref, ys_ref,
                    w1s_ref, w3s_ref, w2s_ref):
    b = pl.program_id(0)
    active = b < na_ref[0]
    prev = be_ref[jnp.maximum(b - 1, 0)]
    fresh = (b == 0) | (be_ref[b] != prev)

    @pl.when(active & fresh)
    def _():
        w1s_ref[...] = w1_ref[0].astype(BF16)
        w3s_ref[...] = w3_ref[0].astype(BF16)
        w2s_ref[...] = w2_ref[0].astype(BF16)

    @pl.when(active)
    def _():
        xb = xs_ref[...].astype(BF16)
        h1 = _dot(xb, w1s_ref[...])
        h3 = _dot(xb, w3s_ref[...])
        hid = (h1 * jax.nn.sigmoid(h1) * h3).astype(BF16)
        ys_ref[...] = _dot(hid, w2s_ref[...])

    @pl.when(jnp.logical_not(active))
    def _():
        ys_ref[...] = jnp.zeros(ys_ref.shape, F32)


def _experts(block_expert, n_active, xs, w1, w3, w2):
    cap, d = xs.shape
    nb = cap // MOE_BLOCK

    def blk(b, be, na):
        return (jnp.minimum(b, na[0] - 1), 0)

    def wsel(b, be, na):
        return (be[jnp.minimum(b, na[0] - 1)], 0, 0)

    grid_spec = pltpu.PrefetchScalarGridSpec(
        num_scalar_prefetch=2,
        grid=(nb,),
        in_specs=[
            pl.BlockSpec((MOE_BLOCK, d), blk),
            pl.BlockSpec((1, d, D_EXPERT), wsel),
            pl.BlockSpec((1, d, D_EXPERT), wsel),
            pl.BlockSpec((1, D_EXPERT, d), wsel),
        ],
        out_specs=pl.BlockSpec((MOE_BLOCK, d), lambda b, be, na: (b, 0)),
        scratch_shapes=[
            pltpu.VMEM((d, D_EXPERT), BF16),
            pltpu.VMEM((d, D_EXPERT), BF16),
            pltpu.VMEM((D_EXPERT, d), BF16),
        ],
    )
    return pl.pallas_call(
        _experts_kernel,
        grid_spec=grid_spec,
        out_shape=jax.ShapeDtypeStruct((cap, d), F32),
        compiler_params=pltpu.CompilerParams(
            dimension_semantics=("arbitrary",), vmem_limit_bytes=48 << 20),
        name="experts",
    )(block_expert, n_active, xs, w1, w3, w2)


def _combine_kernel(slot_ref, h_ref, gate_ref, fw_ref, ys_ref, o_ref,
                    ya_ref, yb_ref, sem):
    tm = h_ref.shape[0]
    bufs = (ya_ref, yb_ref)

    def issue(r, c):
        for k in range(2):
            _row_copy(ys_ref, slot_ref[0, 0, 2 * r + k], bufs[k], r, sem).start(priority=k)
        return c

    lax.fori_loop(0, tm, issue, 0)

    def drain(r, c):
        for k in range(2):
            _row_copy(ys_ref, 0, bufs[k], 0, sem).wait()
        return c

    lax.fori_loop(0, tm, drain, 0)

    gates = gate_ref[...]
    h3 = h_ref[...] + gates[:, 0:1] * ya_ref[...] + gates[:, 1:2] * yb_ref[...]
    o_ref[...] = _rms(h3, fw_ref[...])


def _combine(slots, h2, gates, final_w, ys):
    t, d = h2.shape
    nsteps = t // TM_ROWS
    slots3 = slots.reshape(nsteps, 1, 2 * TM_ROWS)
    return pl.pallas_call(
        _combine_kernel,
        grid=(nsteps,),
        in_specs=[
            pl.BlockSpec((1, 1, 2 * TM_ROWS), lambda i: (i, 0, 0),
                         memory_space=pltpu.SMEM),
            pl.BlockSpec((TM_ROWS, d), lambda i: (i, 0)),
            pl.BlockSpec((TM_ROWS, ROUTE_LANES), lambda i: (i, 0)),
            pl.BlockSpec((1, d), lambda i: (0, 0)),
            pl.BlockSpec(memory_space=pl.ANY),
        ],
        out_specs=pl.BlockSpec((TM_ROWS, d), lambda i: (i, 0)),
        out_shape=jax.ShapeDtypeStruct((t, d), F32),
        scratch_shapes=[
            pltpu.VMEM((TM_ROWS, d), F32),
            pltpu.VMEM((TM_ROWS, d), F32),
            pltpu.SemaphoreType.DMA(()),
        ],
        compiler_params=pltpu.CompilerParams(dimension_semantics=("arbitrary",)),
        name="combine",
    )(slots3, h2, gates, final_w, ys)


def _layer(h, mem, p, final_w):
    bsz, seq, d = h.shape
    t = bsz * seq
    xt = h.reshape(t, d)

    v, u = _in_proj(xt, p["norm_mix_w"].reshape(1, d), p["w_in"].astype(BF16))
    y_conv = _conv(v, p["conv_w"], p["conv_b"].reshape(1, D_CONV),
                   p["conv_ln_w"].reshape(1, D_CONV), p["conv_ln_b"].reshape(1, D_CONV),
                   bsz, seq)

    a1, atab, pj, aj, bbar = _s5_prep(p["ssm_A_re"], p["ssm_A_im"], p["ssm_log_dt"],
                                      p["ssm_B_re"], p["ssm_B_im"])
    nlb = D_SSM // S5_LANES
    bb = bbar.reshape(2, SSM_GROUP, nlb, S5_GROUPS, SSM_STATE).transpose(0, 2, 3, 1, 4)
    bcat = jnp.concatenate([_block_diag(bb[0]), _block_diag(bb[1])], axis=-1).astype(BF16)
    c_re = p["ssm_C_re"].reshape(nlb, S5_GROUPS, SSM_GROUP, SSM_STATE).transpose(0, 1, 3, 2)
    c_im = p["ssm_C_im"].reshape(nlb, S5_GROUPS, SSM_GROUP, SSM_STATE).transpose(0, 1, 3, 2)
    ccat = jnp.concatenate([_block_diag(c_re), -_block_diag(c_im)], axis=1).astype(BF16)
    glu = p["ssm_glu_w"].reshape(nlb, S5_GROUPS, SSM_GROUP, 2 * SSM_GROUP)
    wab = jnp.concatenate([_block_diag(glu[..., :SSM_GROUP]),
                           _block_diag(glu[..., SSM_GROUP:])], axis=-1).astype(BF16)
    y_ssm = _s5(u, bcat, ccat, p["ssm_D"].reshape(1, D_SSM), wab,
                a1, atab, pj, aj, bsz, seq)

    k, vv = _kv(mem, p["norm_mem_w"].reshape(1, d), p["xk_w"].astype(BF16),
                p["xv_w"].astype(BF16))
    w_out = p["w_out"].astype(BF16)
    wr = jnp.concatenate([p["router_group_w"], p["router_expert_w"]], axis=1)
    wr = jnp.pad(wr, ((0, 0), (0, ROUTE_LANES - wr.shape[1]))).astype(BF16)
    br = jnp.concatenate([p["router_group_b"], p["router_expert_b"].reshape(-1)])
    br = jnp.pad(br, (0, ROUTE_LANES - br.shape[0])).reshape(1, ROUTE_LANES)
    h2, hf, route_i, route_f, cnt = _mix_attn(
        xt, y_conv, y_ssm, w_out[:D_CONV], w_out[D_CONV:], p["norm_x_w"].reshape(1, d),
        p["xq_w"].astype(BF16), k, vv, p["xo_w"].astype(BF16),
        p["norm_ffn_w"].reshape(1, d), wr, br, seq)

    counts = cnt[0, EXPERT_LANE0:EXPERT_LANE0 + N_EXPERTS]
    padded = (counts + MOE_BLOCK - 1) // MOE_BLOCK * MOE_BLOCK
    pend = jnp.cumsum(padded)
    pstart = pend - padded
    experts = route_i[:, 0:2]
    slots = (pstart[experts] + route_i[:, 2:4]).reshape(-1).astype(jnp.int32)
    n_blocks = -(-2 * t // MOE_BLOCK) + N_EXPERTS
    cap = n_blocks * MOE_BLOCK
    block_start = jnp.arange(n_blocks, dtype=jnp.int32) * MOE_BLOCK
    block_expert = jnp.minimum(
        jnp.sum((block_start[:, None] >= pend[None, :]).astype(jnp.int32), axis=1),
        N_EXPERTS - 1).astype(jnp.int32)
    n_active = (pend[-1:] // MOE_BLOCK).astype(jnp.int32)

    xs = _scatter(slots, hf, cap)
    ys = _experts(block_expert, n_active, xs, p["moe_w1"], p["moe_w3"], p["moe_w2"])
    out = _combine(slots, h2, route_f, final_w.reshape(1, d), ys)
    return out.reshape(bsz, seq, d)


def kernel(x, mem, norm_mix_w, w_in, conv_w, conv_b, conv_ln_w, conv_ln_b, ssm_A_re, ssm_A_im, ssm_log_dt, ssm_B_re, ssm_B_im, ssm_C_re, ssm_C_im, ssm_D, ssm_glu_w, w_out, norm_x_w, norm_mem_w, xq_w, xk_w, xv_w, xo_w, norm_ffn_w, router_group_w, router_group_b, router_expert_w, router_expert_b, moe_w1, moe_w3, moe_w2, final_norm_w):
    stacked = dict(
        norm_mix_w=norm_mix_w, w_in=w_in, conv_w=conv_w, conv_b=conv_b,
        conv_ln_w=conv_ln_w, conv_ln_b=conv_ln_b, ssm_A_re=ssm_A_re, ssm_A_im=ssm_A_im,
        ssm_log_dt=ssm_log_dt, ssm_B_re=ssm_B_re, ssm_B_im=ssm_B_im, ssm_C_re=ssm_C_re,
        ssm_C_im=ssm_C_im, ssm_D=ssm_D, ssm_glu_w=ssm_glu_w, w_out=w_out,
        norm_x_w=norm_x_w, norm_mem_w=norm_mem_w, xq_w=xq_w, xk_w=xk_w, xv_w=xv_w,
        xo_w=xo_w, norm_ffn_w=norm_ffn_w, router_group_w=router_group_w,
        router_group_b=router_group_b, router_expert_w=router_expert_w,
        router_expert_b=router_expert_b, moe_w1=moe_w1, moe_w3=moe_w3, moe_w2=moe_w2)
    depth = norm_mix_w.shape[0]
    assert depth == 1, "final norm is fused into the single layer's combine step"
    layer = {name: w[0] for name, w in stacked.items()}
    return _layer(x, mem, layer, final_norm_w)
```

```python
import functools

import jax
import jax.numpy as jnp
from jax import lax
from jax.experimental import pallas as pl
from jax.experimental.pallas import tpu as pltpu

D_MODEL = 1024
D_CONV = 512
CONV_WIDTH = 31
D_SSM = 512
SSM_GROUP = 16
N_SSM_GROUPS = 32
SSM_STATE = 64
N_XHEADS = 4
XHEAD_DIM = 256
N_EXPERT_GROUPS = 4
EXPERTS_PER_GROUP = 8
N_EXPERTS = 32
D_EXPERT = 512
EPS = 1e-6

F32 = jnp.float32
BF16 = jnp.bfloat16

SUBLANES = 8
LANES = 128

TM_PROJ = 512
TM_CONV = 512
CONV_CHUNK = 128
CONV_NORM_ROWS = 128
CONV_HALO = 32
S5_STEPS = 64
S5_TILE = SUBLANES * S5_STEPS
S5_LANES = 128
S5_GROUPS = S5_LANES // SSM_GROUP
S5_STATE = S5_GROUPS * SSM_STATE
TM_MIX = 512
ROUTE_LANES = 128
EXPERT_LANE0 = N_EXPERT_GROUPS
MOE_CHUNK = SUBLANES
MOE_BLOCK = 256
BLOCK_CHUNKS = MOE_BLOCK // MOE_CHUNK
TILE_USED_CHUNKS = 2 * TM_MIX // MOE_CHUNK + N_EXPERTS * (MOE_CHUNK - 1) // MOE_CHUNK
TILE_CHUNKS = 160
TILE_ROWS = TILE_CHUNKS * MOE_CHUNK
D_PACK = D_MODEL // 2
HI_MASK = 0xFFFF0000


def _rms(x, w):
    return x * lax.rsqrt(jnp.mean(x * x, axis=-1, keepdims=True) + EPS) * w


def _dot(a, b):
    return jnp.dot(a, b, preferred_element_type=F32)


def _s5_prep_kernel(are_ref, aim_ref, ldt_ref, btre_ref, btim_ref,
                    a1_ref, atab_ref, pj_ref, aj_ref, bbar_ref):
    lam_re = are_ref[...]
    lam_im = aim_ref[...]
    dt = jnp.exp(ldt_ref[...])
    x = lam_re * dt
    y = lam_im * dt

    def power(k):
        mag = jnp.exp(k * x)
        return mag * jnp.cos(k * y), mag * jnp.sin(k * y)

    n = x.shape[-1]
    ones8 = jnp.ones((SUBLANES, n), F32)
    a_re, a_im = power(ones8)
    a1_ref[0] = a_re
    a1_ref[1] = a_im

    steps = (lax.broadcasted_iota(jnp.int32, (S5_STEPS, n), 0) + 1).astype(F32)
    t_re, t_im = power(steps)
    atab_ref[0] = t_re
    atab_ref[1] = t_im

    row = lax.broadcasted_iota(jnp.int32, (SUBLANES, n), 0)
    for i, d in enumerate((1, 2, 4)):
        p_re, p_im = power(ones8 * float(d * S5_STEPS))
        keep = row >= d
        pj_ref[0, i] = jnp.where(keep, p_re, 0.0)
        pj_ref[1, i] = jnp.where(keep, p_im, 0.0)
    j_re, j_im = power(ones8 * float(S5_STEPS))
    aj_ref[0] = j_re
    aj_ref[1] = j_im

    num_re = a_re[0:1] - 1.0
    num_im = a_im[0:1]
    den = lam_re * lam_re + lam_im * lam_im
    c_re = (num_re * lam_re + num_im * lam_im) / den
    c_im = (num_im * lam_re - num_re * lam_im) / den
    b_re = btre_ref[...]
    b_im = btim_ref[...]
    bbar_ref[0] = c_re * b_re - c_im * b_im
    bbar_ref[1] = c_re * b_im + c_im * b_re


def _s5_prep(a_re, a_im, log_dt, b_re, b_im):
    n = N_SSM_GROUPS * SSM_STATE
    are = a_re.reshape(1, n)
    aim = a_im.reshape(1, n)
    ldt = jnp.repeat(log_dt, SSM_STATE).reshape(1, n)
    btre = jnp.transpose(b_re, (2, 0, 1)).reshape(SSM_GROUP, n)
    btim = jnp.transpose(b_im, (2, 0, 1)).reshape(SSM_GROUP, n)
    return pl.pallas_call(
        _s5_prep_kernel,
        out_shape=(
            jax.ShapeDtypeStruct((2, SUBLANES, n), F32),
            jax.ShapeDtypeStruct((2, S5_STEPS, n), F32),
            jax.ShapeDtypeStruct((2, 3, SUBLANES, n), F32),
            jax.ShapeDtypeStruct((2, SUBLANES, n), F32),
            jax.ShapeDtypeStruct((2, SSM_GROUP, n), F32),
        ),
        name="s5_prep",
    )(are, aim, ldt, btre, btim)


def _block_diag(w):
    nl, g, r, c = w.shape
    eye = jnp.eye(g, dtype=w.dtype)
    return jnp.einsum("lgrc,gh->lgrhc", w, eye).reshape(nl, g * r, g * c)


def _in_proj_kernel(x_ref, nw_ref, w_ref, v_ref, u_ref):
    xn = _rms(x_ref[...], nw_ref[...]).astype(BF16)
    proj = _dot(xn, w_ref[...])
    a = proj[:, :D_CONV]
    g = proj[:, D_CONV:2 * D_CONV]
    v_ref[...] = a * jax.nn.sigmoid(g)
    u_ref[...] = proj[:, 2 * D_CONV:]


def _in_proj(xt, norm_w, w_in):
    t = xt.shape[0]
    n_out = 2 * D_CONV + D_SSM
    return pl.pallas_call(
        _in_proj_kernel,
        grid=(t // TM_PROJ,),
        in_specs=[
            pl.BlockSpec((TM_PROJ, D_MODEL), lambda i: (i, 0)),
            pl.BlockSpec((1, D_MODEL), lambda i: (0, 0)),
            pl.BlockSpec((D_MODEL, n_out), lambda i: (0, 0)),
        ],
        out_specs=(
            pl.BlockSpec((TM_PROJ, D_CONV), lambda i: (i, 0)),
            pl.BlockSpec((TM_PROJ, D_SSM), lambda i: (i, 0)),
        ),
        out_shape=(
            jax.ShapeDtypeStruct((t, D_CONV), F32),
            jax.ShapeDtypeStruct((t, D_SSM), F32),
        ),
        compiler_params=pltpu.CompilerParams(
            dimension_semantics=("arbitrary",), vmem_limit_bytes=40 << 20),
        name="in_proj",
    )(xt, norm_w, w_in)


def _conv_kernel(v_ref, w_ref, b_ref, lnw_ref, lnb_ref, o_ref, ext_ref, sh_ref):
    tt = pl.program_id(1)
    rows = CONV_HALO + TM_CONV

    @pl.when(tt == 0)
    def _():
        ext_ref[pl.ds(0, CONV_HALO), :] = jnp.zeros((CONV_HALO, D_CONV), F32)

    @pl.when(tt > 0)
    def _():
        ext_ref[pl.ds(0, CONV_HALO), :] = ext_ref[pl.ds(TM_CONV, CONV_HALO), :]

    ext_ref[pl.ds(CONV_HALO, TM_CONV), :] = v_ref[...]
    for s in range(1, SUBLANES):
        sh_ref[s - 1, pl.ds(0, rows - SUBLANES), :] = ext_ref[pl.ds(s, rows - SUBLANES), :]
    bias = b_ref[...]
    lnw = lnw_ref[...]
    lnb = lnb_ref[...]
    tap0 = CONV_HALO - (CONV_WIDTH - 1)

    groups = CONV_CHUNK // SUBLANES

    def chunk(ci, carry):
        base = pl.multiple_of(ci * CONV_CHUNK, CONV_CHUNK)
        for lt in range(D_CONV // LANES):
            lanes = pl.ds(lt * LANES, LANES)
            acc = [jnp.broadcast_to(bias[:, lt * LANES:(lt + 1) * LANES], (SUBLANES, LANES))] * groups
            for s in range(SUBLANES):
                taps = [j for j in range(CONV_WIDTH) if (tap0 + j) % SUBLANES == s]
                src = ext_ref if s == 0 else sh_ref.at[s - 1]
                ngroups = (tap0 + taps[-1] - s) // SUBLANES + groups
                win = [src[pl.ds(base + SUBLANES * g, SUBLANES), lanes] for g in range(ngroups)]
                for j in taps:
                    g0 = (tap0 + j - s) // SUBLANES
                    wj = w_ref[j, :, lanes]
                    acc = [acc[r] + wj * win[g0 + r] for r in range(groups)]
            o_ref[pl.ds(base, CONV_CHUNK), lanes] = jnp.concatenate(acc, axis=0)
        return carry

    lax.fori_loop(0, TM_CONV // CONV_CHUNK, chunk, 0)

    for bi in range(TM_CONV // CONV_NORM_ROWS):
        rows_b = pl.ds(bi * CONV_NORM_ROWS, CONV_NORM_ROWS)
        acc = o_ref[rows_b, :]
        mu = jnp.mean(acc, axis=-1, keepdims=True)
        cen = acc - mu
        var = jnp.mean(cen * cen, axis=-1, keepdims=True)
        z = cen * lax.rsqrt(var + EPS) * lnw + lnb
        o_ref[rows_b, :] = z * jax.nn.sigmoid(z)


def _conv(v, conv_w, conv_b, ln_w, ln_b, bsz, seq):
    nt = seq // TM_CONV
    row = lambda b, t: (b * nt + t, 0)
    const = lambda b, t: (0, 0)
    return pl.pallas_call(
        _conv_kernel,
        grid=(bsz, nt),
        in_specs=[
            pl.BlockSpec((TM_CONV, D_CONV), row),
            pl.BlockSpec((CONV_WIDTH, SUBLANES, D_CONV), lambda b, t: (0, 0, 0)),
            pl.BlockSpec((1, D_CONV), const),
            pl.BlockSpec((1, D_CONV), const),
            pl.BlockSpec((1, D_CONV), const),
        ],
        out_specs=pl.BlockSpec((TM_CONV, D_CONV), row),
        out_shape=jax.ShapeDtypeStruct(v.shape, F32),
        scratch_shapes=[
            pltpu.VMEM((CONV_HALO + TM_CONV, D_CONV), F32),
            pltpu.VMEM((SUBLANES - 1, CONV_HALO + TM_CONV, D_CONV), F32),
        ],
        compiler_params=pltpu.CompilerParams(
            dimension_semantics=("arbitrary", "arbitrary")),
        name="conv",
    )(v, jnp.broadcast_to(conv_w[:, None, :], (CONV_WIDTH, SUBLANES, D_CONV)), conv_b, ln_w, ln_b)


def _cmul(a_re, a_im, b_re, b_im):
    return a_re * b_re - a_im * b_im, a_re * b_im + a_im * b_re


def _s5_kernel(u_ref, bcat_ref, ccat_ref, d_ref, wab_ref, a1_ref, atab_ref,
               pj_ref, aj_ref, o_ref, up_ref, bu_ref, st_ref, carry_ref):
    tt = pl.program_id(2)
    ns = S5_STATE

    @pl.when(tt == 0)
    def _():
        carry_ref[...] = jnp.zeros(carry_ref.shape, F32)

    for j in range(S5_STEPS):
        up_ref[pl.ds(SUBLANES * j, SUBLANES), :] = u_ref[pl.ds(j, SUBLANES, stride=S5_STEPS), :]
    up = up_ref[...]
    bu_ref[...] = _dot(up.astype(BF16), bcat_ref[0])

    a_re = a1_ref[0]
    a_im = a1_ref[1]

    def local_step(j, s):
        s_re, s_im = s
        rows = pl.ds(pl.multiple_of(j * SUBLANES, SUBLANES), SUBLANES)
        m_re, m_im = _cmul(a_re, a_im, s_re, s_im)
        n_re = m_re + bu_ref[rows, pl.ds(0, ns)]
        n_im = m_im + bu_ref[rows, pl.ds(ns, ns)]
        st_ref[rows, pl.ds(0, ns)] = n_re
        st_ref[rows, pl.ds(ns, ns)] = n_im
        return n_re, n_im

    zero = jnp.zeros((SUBLANES, ns), F32)
    e_re, e_im = lax.fori_loop(0, S5_STEPS, local_step, (zero, zero), unroll=8)

    row = lax.broadcasted_iota(jnp.int32, (SUBLANES, ns), 0)
    first = row == 0
    c_re = jnp.where(first, pltpu.roll(carry_ref[0], 1, 0), pltpu.roll(e_re, 1, 0))
    c_im = jnp.where(first, pltpu.roll(carry_ref[1], 1, 0), pltpu.roll(e_im, 1, 0))
    for i, d in enumerate((1, 2, 4)):
        r_re = pltpu.roll(c_re, d, 0)
        r_im = pltpu.roll(c_im, d, 0)
        m_re, m_im = _cmul(pj_ref[0, i], pj_ref[1, i], r_re, r_im)
        c_re = c_re + m_re
        c_im = c_im + m_im
    f_re, f_im = _cmul(aj_ref[0], aj_ref[1], c_re, c_im)
    carry_ref[0] = f_re + e_re
    carry_ref[1] = f_im + e_im

    def fix_step(j, carry):
        rows = pl.ds(pl.multiple_of(j * SUBLANES, SUBLANES), SUBLANES)
        t_re = atab_ref[0, pl.ds(j, 1), :]
        t_im = atab_ref[1, pl.ds(j, 1), :]
        m_re, m_im = _cmul(t_re, t_im, c_re, c_im)
        st_ref[rows, pl.ds(0, ns)] = st_ref[rows, pl.ds(0, ns)] + m_re
        st_ref[rows, pl.ds(ns, ns)] = st_ref[rows, pl.ds(ns, ns)] + m_im
        return carry

    lax.fori_loop(0, S5_STEPS, fix_step, 0, unroll=8)

    y = _dot(st_ref[...].astype(BF16), ccat_ref[0]) + d_ref[...] * up
    y = jax.nn.gelu(y)
    ab = _dot(y.astype(BF16), wab_ref[0])
    out = ab[:, :S5_LANES] * jax.nn.sigmoid(ab[:, S5_LANES:])
    for j in range(S5_STEPS):
        o_ref[pl.ds(j, SUBLANES, stride=S5_STEPS), :] = out[SUBLANES * j:SUBLANES * (j + 1), :]


def _s5(u, bcat, ccat, d, wab, a1, atab, pj, aj, bsz, seq):
    nt = seq // S5_TILE
    nlb = D_SSM // S5_LANES
    ns = S5_STATE
    row = lambda b, l, t: (b * nt + t, l)
    lane3 = lambda b, l, t: (0, 0, l)
    return pl.pallas_call(
        _s5_kernel,
        grid=(bsz, nlb, nt),
        in_specs=[
            pl.BlockSpec((S5_TILE, S5_LANES), row),
            pl.BlockSpec((1, S5_LANES, 2 * ns), lambda b, l, t: (l, 0, 0)),
            pl.BlockSpec((1, 2 * ns, S5_LANES), lambda b, l, t: (l, 0, 0)),
            pl.BlockSpec((1, S5_LANES), lambda b, l, t: (0, l)),
            pl.BlockSpec((1, S5_LANES, 2 * S5_LANES), lambda b, l, t: (l, 0, 0)),
            pl.BlockSpec((2, SUBLANES, ns), lane3),
            pl.BlockSpec((2, S5_STEPS, ns), lane3),
            pl.BlockSpec((2, 3, SUBLANES, ns), lambda b, l, t: (0, 0, 0, l)),
            pl.BlockSpec((2, SUBLANES, ns), lane3),
        ],
        out_specs=pl.BlockSpec((S5_TILE, S5_LANES), row),
        out_shape=jax.ShapeDtypeStruct(u.shape, F32),
        scratch_shapes=[
            pltpu.VMEM((S5_TILE, S5_LANES), F32),
            pltpu.VMEM((S5_TILE, 2 * ns), F32),
            pltpu.VMEM((S5_TILE, 2 * ns), F32),
            pltpu.VMEM((2, SUBLANES, ns), F32),
        ],
        compiler_params=pltpu.CompilerParams(
            dimension_semantics=("arbitrary", "arbitrary", "arbitrary"),
            vmem_limit_bytes=40 << 20),
        name="s5",
    )(u, bcat, ccat, d, wab, a1, atab, pj, aj)


def _kv_kernel(m_ref, nw_ref, wk_ref, wv_ref, k_ref, v_ref):
    mn = _rms(m_ref[0], nw_ref[...]).astype(BF16)
    k_ref[0] = _dot(mn, wk_ref[...]).astype(BF16)
    v_ref[0] = _dot(mn, wv_ref[...]).astype(BF16)


def _kv(mem, norm_w, wk, wv):
    bsz, mlen, d = mem.shape
    blk = pl.BlockSpec((1, mlen, d), lambda b: (b, 0, 0))
    wspec = pl.BlockSpec((d, d), lambda b: (0, 0))
    return pl.pallas_call(
        _kv_kernel,
        grid=(bsz,),
        in_specs=[blk, pl.BlockSpec((1, d), lambda b: (0, 0)), wspec, wspec],
        out_specs=(blk, blk),
        out_shape=(jax.ShapeDtypeStruct(mem.shape, BF16),) * 2,
        compiler_params=pltpu.CompilerParams(
            dimension_semantics=("arbitrary",), vmem_limit_bytes=40 << 20),
        name="kv",
    )(mem, norm_w, wk, wv)


def _mix_attn_kernel(x_ref, yc_ref, ys_ref, wot_ref, wob_ref, nx_ref, wq_ref,
                     k_ref, v_ref, wo_ref, nf_ref, wr_ref, br_ref,
                     h_ref, xs_ref, y0_ref, gs_ref, ri_ref, cnt_ref):
    tm = x_ref.shape[0]

    h1 = (x_ref[...] + _dot(yc_ref[...].astype(BF16), wot_ref[...])
          + _dot(ys_ref[...].astype(BF16), wob_ref[...]))

    hn = _rms(h1, nx_ref[...]).astype(BF16)
    q = _dot(hn, wq_ref[...])
    heads = []
    for hd in range(N_XHEADS):
        sl = slice(hd * XHEAD_DIM, (hd + 1) * XHEAD_DIM)
        qh = q[:, sl].astype(BF16)
        s = lax.dot_general(qh, k_ref[0, :, sl], (((1,), (1,)), ((), ())),
                            preferred_element_type=F32) * (XHEAD_DIM ** -0.5)
        s = s - jnp.max(s, axis=-1, keepdims=True)
        p = jnp.exp(s)
        p = p / jnp.sum(p, axis=-1, keepdims=True)
        heads.append(_dot(p.astype(BF16), v_ref[0, :, sl]).astype(BF16))
    o = jnp.concatenate(heads, axis=-1)
    h2 = h1 + _dot(o, wo_ref[...])
    h_ref[...] = h2

    hf = _rms(h2, nf_ref[...])
    logits = _dot(hf.astype(BF16), wr_ref[...]) + br_ref[...]

    lane = lax.broadcasted_iota(jnp.int32, (tm, ROUTE_LANES), 1)
    lane_f = lane.astype(F32)
    neg = -jnp.inf
    big = float(ROUTE_LANES)

    def top1(vals):
        m = jnp.max(vals, axis=-1, keepdims=True)
        idx = jnp.min(jnp.where(vals == m, lane_f, big), axis=-1, keepdims=True)
        return m, idx

    gl = jnp.where(lane < N_EXPERT_GROUPS, logits, neg)
    gmax, gidx = top1(gl)
    g_w = 1.0 / jnp.sum(jnp.exp(gl - gmax), axis=-1, keepdims=True)
    lo = EXPERT_LANE0 + EXPERTS_PER_GROUP * gidx
    el = jnp.where((lane_f >= lo) & (lane_f < lo + EXPERTS_PER_GROUP), logits, neg)
    m1, i1 = top1(el)
    m2, i2 = top1(jnp.where(lane_f == i1, neg, el))
    e21 = jnp.exp(m2 - m1)
    gate1 = g_w / (1.0 + e21)
    gate2 = g_w * e21 / (1.0 + e21)

    hot1 = lane_f == i1
    hot2 = lane_f == i2
    hot = jnp.where(hot1 | hot2, 1.0, 0.0)
    r_i = lax.broadcasted_iota(jnp.int32, (tm, tm), 0)
    c_i = lax.broadcasted_iota(jnp.int32, (tm, tm), 1)
    tri = jnp.where(r_i > c_i, 1.0, 0.0).astype(BF16)
    before = _dot(tri, hot.astype(BF16))
    count = jnp.sum(hot, axis=0, keepdims=True)
    chunks = jnp.floor((count + (MOE_CHUNK - 1.0)) * (1.0 / MOE_CHUNK))
    l_i = lax.broadcasted_iota(jnp.int32, (ROUTE_LANES, ROUTE_LANES), 0)
    l_j = lax.broadcasted_iota(jnp.int32, (ROUTE_LANES, ROUTE_LANES), 1)
    upper = jnp.where(l_i < l_j, 1.0, 0.0).astype(BF16)
    first_chunk = _dot(jnp.broadcast_to(chunks, (SUBLANES, ROUTE_LANES)).astype(BF16), upper)[0:1]
    start = first_chunk * float(MOE_CHUNK) + before
    pos1 = jnp.sum(jnp.where(hot1, start, 0.0), axis=-1, keepdims=True)
    pos2 = jnp.sum(jnp.where(hot2, start, 0.0), axis=-1, keepdims=True)
    cnt_ref[0] = jnp.broadcast_to(count, (SUBLANES, ROUTE_LANES)).astype(jnp.int32)
    ri_ref[...] = jnp.where(lane == 0, pos1, jnp.where(lane == 1, pos2, 0.0)).astype(jnp.int32)

    rho = lax.broadcasted_iota(jnp.int32, (tm, TILE_ROWS), 1).astype(F32)
    is1 = rho == pos1
    is2 = rho == pos2
    pt = jnp.where(is1 | is2, 1.0, 0.0).astype(BF16)
    contract0 = (((0,), (0,)), ((), ()))
    srt = lax.dot_general(pt, hf.astype(BF16), contract0, preferred_element_type=F32)
    xs_ref[...] = _pack_words(srt).reshape(TILE_CHUNKS, MOE_CHUNK, D_PACK)
    y0_ref[...] = jnp.zeros(y0_ref.shape, jnp.uint32)

    def pieces(g):
        hi = g.astype(BF16).astype(F32)
        mid = (g - hi).astype(BF16).astype(F32)
        low = (g - hi - mid).astype(BF16).astype(F32)
        return hi, mid, low

    p1 = pieces(gate1)
    p2 = pieces(gate2)
    g6 = jnp.where(lane == 6, 1.0, 0.0)
    for li, piece in enumerate(p1 + p2):
        g6 = jnp.where(lane == li, piece, g6)
    ptk = jnp.where(is1, 1.0, jnp.where(is2, 2.0, 0.0)).astype(BF16)
    sg = lax.dot_general(ptk, g6.astype(BF16), contract0, preferred_element_type=F32)
    which = sg[:, 6:7]
    first = sg[:, 0:1] + sg[:, 1:2] + sg[:, 2:3]
    second = 0.5 * (sg[:, 3:4] + sg[:, 4:5] + sg[:, 5:6])
    gsort = jnp.where(which == 1.0, first, jnp.where(which == 2.0, second, 0.0))
    gs_ref[...] = jnp.broadcast_to(gsort, (TILE_ROWS, ROUTE_LANES))


def _pack_words(v):
    hi = lax.bitcast_convert_type(v[:, :D_PACK], jnp.uint32) & jnp.uint32(HI_MASK)
    lo = lax.bitcast_convert_type(v[:, D_PACK:], jnp.uint32) >> 16
    return hi | lo


def _unpack_words(w):
    hi = lax.bitcast_convert_type(w & jnp.uint32(HI_MASK), F32)
    lo = lax.bitcast_convert_type(w << 16, F32)
    return hi, lo


def _mix_attn(xt, yc, ys, wot, wob, nx, wq, k, v, wo, nf, wr, br, seq):
    t, d = xt.shape
    nt = t // TM_MIX
    tiles_per_batch = seq // TM_MIX
    mlen = k.shape[1]
    row = lambda i: (i, 0)
    const = lambda i: (0, 0)
    tile3 = lambda i: (i, 0, 0)
    kvspec = pl.BlockSpec((1, mlen, d), lambda i: (i // tiles_per_batch, 0, 0))
    packed = jax.ShapeDtypeStruct((nt * TILE_CHUNKS, MOE_CHUNK, D_PACK), jnp.uint32)
    return pl.pallas_call(
        _mix_attn_kernel,
        grid=(nt,),
        in_specs=[
            pl.BlockSpec((TM_MIX, d), row),
            pl.BlockSpec((TM_MIX, D_CONV), row),
            pl.BlockSpec((TM_MIX, D_SSM), row),
            pl.BlockSpec((D_CONV, d), const),
            pl.BlockSpec((D_SSM, d), const),
            pl.BlockSpec((1, d), const),
            pl.BlockSpec((d, d), const),
            kvspec, kvspec,
            pl.BlockSpec((d, d), const),
            pl.BlockSpec((1, d), const),
            pl.BlockSpec((d, ROUTE_LANES), const),
            pl.BlockSpec((1, ROUTE_LANES), const),
        ],
        out_specs=(
            pl.BlockSpec((TM_MIX, d), row),
            pl.BlockSpec((TILE_CHUNKS, MOE_CHUNK, D_PACK), tile3),
            pl.BlockSpec((TILE_CHUNKS, MOE_CHUNK, D_PACK), tile3),
            pl.BlockSpec((TILE_ROWS, ROUTE_LANES), row),
            pl.BlockSpec((TM_MIX, ROUTE_LANES), row),
            pl.BlockSpec((1, SUBLANES, ROUTE_LANES), tile3),
        ),
        out_shape=(
            jax.ShapeDtypeStruct((t, d), F32),
            packed,
            packed,
            jax.ShapeDtypeStruct((nt * TILE_ROWS, ROUTE_LANES), F32),
            jax.ShapeDtypeStruct((t, ROUTE_LANES), jnp.int32),
            jax.ShapeDtypeStruct((nt, SUBLANES, ROUTE_LANES), jnp.int32),
        ),
        compiler_params=pltpu.CompilerParams(
            dimension_semantics=("arbitrary",), vmem_limit_bytes=60 << 20),
        name="mix_attn",
    )(xt, yc, ys, wot, wob, nx, wq, k, v, wo, nf, wr, br)


ZERO_CHUNK = TILE_CHUNKS - 1
DUMP_TILE0 = 16
SPARE_PER_TILE = TILE_CHUNKS - TILE_USED_CHUNKS


def _dump_chunk(slot, k):
    idx = slot * BLOCK_CHUNKS + k
    return (DUMP_TILE0 + idx // SPARE_PER_TILE) * TILE_CHUNKS + TILE_USED_CHUNKS + idx % SPARE_PER_TILE


def _experts_kernel(src_ref, dst_ref, be_ref, na_ref, xs_hbm, w1_ref, w3_ref, w2_ref,
                    y0_hbm, ys_hbm, xbuf, ybuf, w1s_ref, w3s_ref, w2s_ref, in_sem, out_sem):
    del y0_hbm
    b = pl.program_id(0)
    n_active = na_ref[0]
    slot = b % 2
    other = 1 - slot

    def gather(blk, sl):
        for k in range(BLOCK_CHUNKS):
            pltpu.make_async_copy(xs_hbm.at[src_ref[blk * BLOCK_CHUNKS + k]],
                                  xbuf.at[sl, k], in_sem.at[sl]).start()

    def gather_wait(sl):
        pltpu.make_async_copy(xs_hbm.at[pl.ds(0, BLOCK_CHUNKS)], xbuf.at[sl],
                              in_sem.at[sl]).wait()

    def scatter(blk, sl):
        for k in range(BLOCK_CHUNKS):
            d = dst_ref[blk * BLOCK_CHUNKS + k]
            dump = jnp.where(sl == 0, _dump_chunk(0, k), _dump_chunk(1, k))
            pltpu.make_async_copy(ybuf.at[sl, k], ys_hbm.at[jnp.where(d < 0, dump, d)],
                                  out_sem.at[sl]).start()

    def scatter_wait(sl):
        pltpu.make_async_copy(ybuf.at[sl], ys_hbm.at[pl.ds(0, BLOCK_CHUNKS)],
                              out_sem.at[sl]).wait()

    active = b < n_active

    @pl.when(b == 0)
    def _():
        gather(0, 0)

    @pl.when(b + 1 < n_active)
    def _():
        gather(b + 1, other)

    prev = be_ref[jnp.maximum(b - 1, 0)]
    fresh = (b == 0) | (be_ref[b] != prev)

    @pl.when(active & fresh)
    def _():
        w1s_ref[...] = w1_ref[0].astype(BF16)
        w3s_ref[...] = w3_ref[0].astype(BF16)
        w2s_ref[...] = w2_ref[0].astype(BF16)

    @pl.when(active)
    def _():
        gather_wait(slot)

        @pl.when(b >= 2)
        def _():
            scatter_wait(slot)

        hi, lo = _unpack_words(xbuf[slot].reshape(MOE_BLOCK, D_PACK))
        xb = jnp.concatenate([hi, lo], axis=1).astype(BF16)
        h1 = _dot(xb, w1s_ref[...])
        h3 = _dot(xb, w3s_ref[...])
        hid = (h1 * jax.nn.sigmoid(h1) * h3).astype(BF16)
        y = _dot(hid, w2s_ref[...]).astype(BF16).astype(F32)
        ybuf[slot] = _pack_words(y).reshape(BLOCK_CHUNKS, MOE_CHUNK, D_PACK)
        scatter(b, slot)

        @pl.when(b == n_active - 1)
        def _():
            scatter_wait(slot)

            @pl.when(b >= 1)
            def _():
                scatter_wait(other)


def _experts(src, dst, block_expert, n_active, xs, w1, w3, w2, y0):
    nb = block_expert.shape[0]
    d = D_MODEL

    def wsel(b, src, dst, be, na):
        return (be[jnp.minimum(b, na[0] - 1)], 0, 0)

    grid_spec = pltpu.PrefetchScalarGridSpec(
        num_scalar_prefetch=4,
        grid=(nb,),
        in_specs=[
            pl.BlockSpec(memory_space=pl.ANY),
            pl.BlockSpec((1, d, D_EXPERT), wsel),
            pl.BlockSpec((1, d, D_EXPERT), wsel),
            pl.BlockSpec((1, D_EXPERT, d), wsel),
            pl.BlockSpec(memory_space=pl.ANY),
        ],
        out_specs=pl.BlockSpec(memory_space=pl.ANY),
        scratch_shapes=[
            pltpu.VMEM((2, BLOCK_CHUNKS, MOE_CHUNK, D_PACK), jnp.uint32),
            pltpu.VMEM((2, BLOCK_CHUNKS, MOE_CHUNK, D_PACK), jnp.uint32),
            pltpu.VMEM((d, D_EXPERT), BF16),
            pltpu.VMEM((d, D_EXPERT), BF16),
            pltpu.VMEM((D_EXPERT, d), BF16),
            pltpu.SemaphoreType.DMA((2,)),
            pltpu.SemaphoreType.DMA((2,)),
        ],
    )
    return pl.pallas_call(
        _experts_kernel,
        grid_spec=grid_spec,
        out_shape=jax.ShapeDtypeStruct(y0.shape, jnp.uint32),
        input_output_aliases={8: 0},
        compiler_params=pltpu.CompilerParams(
            dimension_semantics=("arbitrary",), vmem_limit_bytes=48 << 20),
        name="experts",
    )(src, dst, block_expert, n_active, xs, w1, w3, w2, y0)


def _combine_kernel(h_ref, ri_ref, gs_ref, ys_ref, fw_ref, o_ref):
    tm = h_ref.shape[0]
    hi, lo = _unpack_words(ys_ref[...].reshape(TILE_ROWS, D_PACK))
    g = gs_ref[...]
    gw = jnp.concatenate([g] * (D_PACK // ROUTE_LANES), axis=1)
    yg = jnp.concatenate([hi * gw, lo * gw], axis=1).astype(BF16)
    pos = ri_ref[...].astype(F32)
    rho = lax.broadcasted_iota(jnp.int32, (tm, TILE_ROWS), 1).astype(F32)
    q = jnp.where((rho == pos[:, 0:1]) | (rho == pos[:, 1:2]), 1.0, 0.0).astype(BF16)
    o_ref[...] = _rms(h_ref[...] + _dot(q, yg), fw_ref[...])


def _combine(h2, route_i, gsort, ys, final_w):
    t, d = h2.shape
    nt = t // TM_MIX
    row = lambda i: (i, 0)
    return pl.pallas_call(
        _combine_kernel,
        grid=(nt,),
        in_specs=[
            pl.BlockSpec((TM_MIX, d), row),
            pl.BlockSpec((TM_MIX, ROUTE_LANES), row),
            pl.BlockSpec((TILE_ROWS, ROUTE_LANES), row),
            pl.BlockSpec((TILE_CHUNKS, MOE_CHUNK, D_PACK), lambda i: (i, 0, 0)),
            pl.BlockSpec((1, d), lambda i: (0, 0)),
        ],
        out_specs=pl.BlockSpec((TM_MIX, d), row),
        out_shape=jax.ShapeDtypeStruct((t, d), F32),
        compiler_params=pltpu.CompilerParams(
            dimension_semantics=("arbitrary",), vmem_limit_bytes=48 << 20),
        name="combine",
    )(h2, route_i, gsort, ys, final_w)


def _routing_tables(cnt):
    nt = cnt.shape[0]
    n = cnt[:, 0, EXPERT_LANE0:EXPERT_LANE0 + N_EXPERTS]
    c = (n + MOE_CHUNK - 1) // MOE_CHUNK
    local = jnp.cumsum(c, axis=1) - c
    per_expert = jnp.sum(c, axis=0)
    padded = (per_expert + BLOCK_CHUNKS - 1) // BLOCK_CHUNKS * BLOCK_CHUNKS
    gend = jnp.cumsum(padded)
    gstart = gend - padded
    within = jnp.cumsum(c, axis=0) - c
    start = (gstart[None, :] + within).T.reshape(-1)
    length = c.T.reshape(-1)
    base = (jnp.arange(nt, dtype=jnp.int32)[:, None] * TILE_CHUNKS + local).T.reshape(-1)
    g_max = (2 * nt * TM_MIX // MOE_CHUNK + nt * N_EXPERTS
             + N_EXPERTS * (BLOCK_CHUNKS - 1))
    n_blocks = -(-g_max // BLOCK_CHUNKS)
    g = jnp.arange(n_blocks * BLOCK_CHUNKS, dtype=jnp.int32)
    pair = jnp.sum((start[None, :] <= g[:, None]).astype(jnp.int32), axis=1) - 1
    k = g - start[pair]
    valid = k < length[pair]
    chunk = base[pair] + k
    src = jnp.where(valid, chunk, ZERO_CHUNK).astype(jnp.int32)
    dst = jnp.where(valid, chunk, -1).astype(jnp.int32)
    block_start = jnp.arange(n_blocks, dtype=jnp.int32) * BLOCK_CHUNKS
    block_expert = jnp.minimum(
        jnp.sum((block_start[:, None] >= gend[None, :]).astype(jnp.int32), axis=1),
        N_EXPERTS - 1).astype(jnp.int32)
    n_active = (gend[-1:] // BLOCK_CHUNKS).astype(jnp.int32)
    return src, dst, block_expert, n_active


def _layer(h, mem, p, final_w):
    bsz, seq, d = h.shape
    t = bsz * seq
    assert t // TM_MIX >= DUMP_TILE0 + 2 * BLOCK_CHUNKS // SPARE_PER_TILE
    xt = h.reshape(t, d)

    v, u = _in_proj(xt, p["norm_mix_w"].reshape(1, d), p["w_in"].astype(BF16))
    y_conv = _conv(v, p["conv_w"], p["conv_b"].reshape(1, D_CONV),
                   p["conv_ln_w"].reshape(1, D_CONV), p["conv_ln_b"].reshape(1, D_CONV),
                   bsz, seq)

    a1, atab, pj, aj, bbar = _s5_prep(p["ssm_A_re"], p["ssm_A_im"], p["ssm_log_dt"],
                                      p["ssm_B_re"], p["ssm_B_im"])
    nlb = D_SSM // S5_LANES
    bb = bbar.reshape(2, SSM_GROUP, nlb, S5_GROUPS, SSM_STATE).transpose(0, 2, 3, 1, 4)
    bcat = jnp.concatenate([_block_diag(bb[0]), _block_diag(bb[1])], axis=-1).astype(BF16)
    c_re = p["ssm_C_re"].reshape(nlb, S5_GROUPS, SSM_GROUP, SSM_STATE).transpose(0, 1, 3, 2)
    c_im = p["ssm_C_im"].reshape(nlb, S5_GROUPS, SSM_GROUP, SSM_STATE).transpose(0, 1, 3, 2)
    ccat = jnp.concatenate([_block_diag(c_re), -_block_diag(c_im)], axis=1).astype(BF16)
    glu = p["ssm_glu_w"].reshape(nlb, S5_GROUPS, SSM_GROUP, 2 * SSM_GROUP)
    wab = jnp.concatenate([_block_diag(glu[..., :SSM_GROUP]),
                           _block_diag(glu[..., SSM_GROUP:])], axis=-1).astype(BF16)
    y_ssm = _s5(u, bcat, ccat, p["ssm_D"].reshape(1, D_SSM), wab,
                a1, atab, pj, aj, bsz, seq)

    k, vv = _kv(mem, p["norm_mem_w"].reshape(1, d), p["xk_w"].astype(BF16),
                p["xv_w"].astype(BF16))
    w_out = p["w_out"].astype(BF16)
    wr = jnp.concatenate([p["router_group_w"], p["router_expert_w"]], axis=1)
    wr = jnp.pad(wr, ((0, 0), (0, ROUTE_LANES - wr.shape[1]))).astype(BF16)
    br = jnp.concatenate([p["router_group_b"], p["router_expert_b"].reshape(-1)])
    br = jnp.pad(br, (0, ROUTE_LANES - br.shape[0])).reshape(1, ROUTE_LANES)
    h2, xs, y0, gsort, route_i, cnt = _mix_attn(
        xt, y_conv, y_ssm, w_out[:D_CONV], w_out[D_CONV:], p["norm_x_w"].reshape(1, d),
        p["xq_w"].astype(BF16), k, vv, p["xo_w"].astype(BF16),
        p["norm_ffn_w"].reshape(1, d), wr, br, seq)

    src, dst, block_expert, n_active = _routing_tables(cnt)
    ys = _experts(src, dst, block_expert, n_active, xs,
                  p["moe_w1"], p["moe_w3"], p["moe_w2"], y0)
    out = _combine(h2, route_i, gsort, ys, final_w.reshape(1, d))
    return out.reshape(bsz, seq, d)


def kernel(x, mem, norm_mix_w, w_in, conv_w, conv_b, conv_ln_w, conv_ln_b, ssm_A_re, ssm_A_im, ssm_log_dt, ssm_B_re, ssm_B_im, ssm_C_re, ssm_C_im, ssm_D, ssm_glu_w, w_out, norm_x_w, norm_mem_w, xq_w, xk_w, xv_w, xo_w, norm_ffn_w, router_group_w, router_group_b, router_expert_w, router_expert_b, moe_w1, moe_w3, moe_w2, final_norm_w):
    stacked = dict(
        norm_mix_w=norm_mix_w, w_in=w_in, conv_w=conv_w, conv_b=conv_b,
        conv_ln_w=conv_ln_w, conv_ln_b=conv_ln_b, ssm_A_re=ssm_A_re, ssm_A_im=ssm_A_im,
        ssm_log_dt=ssm_log_dt, ssm_B_re=ssm_B_re, ssm_B_im=ssm_B_im, ssm_C_re=ssm_C_re,
        ssm_C_im=ssm_C_im, ssm_D=ssm_D, ssm_glu_w=ssm_glu_w, w_out=w_out,
        norm_x_w=norm_x_w, norm_mem_w=norm_mem_w, xq_w=xq_w, xk_w=xk_w, xv_w=xv_w,
        xo_w=xo_w, norm_ffn_w=norm_ffn_w, router_group_w=router_group_w,
        router_group_b=router_group_b, router_expert_w=router_expert_w,
        router_expert_b=router_expert_b, moe_w1=moe_w1, moe_w3=moe_w3, moe_w2=moe_w2)
    depth = norm_mix_w.shape[0]
    assert depth == 1, "final norm is fused into the single layer's combine step"
    layer = {name: w[0] for name, w in stacked.items()}
    return _layer(x, mem, layer, final_norm_w)
```

```python
import functools

import jax
import jax.numpy as jnp
from jax import lax
from jax.experimental import pallas as pl
from jax.experimental.pallas import tpu as pltpu

D_MODEL = 1024
D_CONV = 512
CONV_WIDTH = 31
D_SSM = 512
SSM_GROUP = 16
N_SSM_GROUPS = 32
SSM_STATE = 64
N_XHEADS = 4
XHEAD_DIM = 256
N_EXPERT_GROUPS = 4
EXPERTS_PER_GROUP = 8
N_EXPERTS = 32
D_EXPERT = 512
EPS = 1e-6

F32 = jnp.float32
BF16 = jnp.bfloat16

SUBLANES = 8
LANES = 128

TM_PROJ = 512
TM_CONV = 512
CONV_CHUNK = 128
CONV_NORM_ROWS = 128
CONV_HALO = 32
S5_STEPS = 64
S5_TILE = SUBLANES * S5_STEPS
S5_GROUP_ROWS = 128
S5_LANES = 128
S5_GROUPS = S5_LANES // SSM_GROUP
S5_STATE = S5_GROUPS * SSM_STATE
TM_MIX = 512
ROUTE_LANES = 128
EXPERT_LANE0 = N_EXPERT_GROUPS
MOE_CHUNK = SUBLANES
MOE_BLOCK = 256
BLOCK_CHUNKS = MOE_BLOCK // MOE_CHUNK
TILE_USED_CHUNKS = 2 * TM_MIX // MOE_CHUNK + N_EXPERTS * (MOE_CHUNK - 1) // MOE_CHUNK
TILE_CHUNKS = 160
TILE_ROWS = TILE_CHUNKS * MOE_CHUNK
D_PACK = D_MODEL // 2
HI_MASK = 0xFFFF0000


def _rms(x, w):
    return x * lax.rsqrt(jnp.mean(x * x, axis=-1, keepdims=True) + EPS) * w


def _dot(a, b):
    return jnp.dot(a, b, preferred_element_type=F32)


def _s5_prep_kernel(are_ref, aim_ref, ldt_ref, btre_ref, btim_ref,
                    a1_ref, pj_ref, aj_ref, bbar_ref):
    lam_re = are_ref[...]
    lam_im = aim_ref[...]
    dt = jnp.exp(ldt_ref[...])
    x = lam_re * dt
    y = lam_im * dt

    def power(k):
        mag = jnp.exp(k * x)
        return mag * jnp.cos(k * y), mag * jnp.sin(k * y)

    n = x.shape[-1]
    ones8 = jnp.ones((SUBLANES, n), F32)
    a_re, a_im = power(ones8)
    a1_ref[0] = a_re
    a1_ref[1] = a_im

    row = lax.broadcasted_iota(jnp.int32, (SUBLANES, n), 0)
    for i, d in enumerate((1, 2, 4)):
        p_re, p_im = power(ones8 * float(d * S5_STEPS))
        keep = row >= d
        pj_ref[0, i] = jnp.where(keep, p_re, 0.0)
        pj_ref[1, i] = jnp.where(keep, p_im, 0.0)
    j_re, j_im = power(ones8 * float(S5_STEPS))
    aj_ref[0] = j_re
    aj_ref[1] = j_im

    num_re = a_re[0:1] - 1.0
    num_im = a_im[0:1]
    den = lam_re * lam_re + lam_im * lam_im
    c_re = (num_re * lam_re + num_im * lam_im) / den
    c_im = (num_im * lam_re - num_re * lam_im) / den
    b_re = btre_ref[...]
    b_im = btim_ref[...]
    bbar_ref[0] = c_re * b_re - c_im * b_im
    bbar_ref[1] = c_re * b_im + c_im * b_re


def _s5_prep(a_re, a_im, log_dt, b_re, b_im):
    n = N_SSM_GROUPS * SSM_STATE
    are = a_re.reshape(1, n)
    aim = a_im.reshape(1, n)
    ldt = jnp.repeat(log_dt, SSM_STATE).reshape(1, n)
    btre = jnp.transpose(b_re, (2, 0, 1)).reshape(SSM_GROUP, n)
    btim = jnp.transpose(b_im, (2, 0, 1)).reshape(SSM_GROUP, n)
    return pl.pallas_call(
        _s5_prep_kernel,
        out_shape=(
            jax.ShapeDtypeStruct((2, SUBLANES, n), F32),
            jax.ShapeDtypeStruct((2, 3, SUBLANES, n), F32),
            jax.ShapeDtypeStruct((2, SUBLANES, n), F32),
            jax.ShapeDtypeStruct((2, SSM_GROUP, n), F32),
        ),
        name="s5_prep",
    )(are, aim, ldt, btre, btim)


def _block_diag(w):
    nl, g, r, c = w.shape
    eye = jnp.eye(g, dtype=w.dtype)
    return jnp.einsum("lgrc,gh->lgrhc", w, eye).reshape(nl, g * r, g * c)


def _in_proj_kernel(x_ref, nw_ref, w_ref, v_ref, u_ref):
    xn = _rms(x_ref[...], nw_ref[...]).astype(BF16)
    proj = _dot(xn, w_ref[...])
    a = proj[:, :D_CONV]
    g = proj[:, D_CONV:2 * D_CONV]
    v_ref[...] = a * jax.nn.sigmoid(g)
    u_ref[...] = proj[:, 2 * D_CONV:]


def _in_proj(xt, norm_w, w_in):
    t = xt.shape[0]
    n_out = 2 * D_CONV + D_SSM
    return pl.pallas_call(
        _in_proj_kernel,
        grid=(t // TM_PROJ,),
        in_specs=[
            pl.BlockSpec((TM_PROJ, D_MODEL), lambda i: (i, 0)),
            pl.BlockSpec((1, D_MODEL), lambda i: (0, 0)),
            pl.BlockSpec((D_MODEL, n_out), lambda i: (0, 0)),
        ],
        out_specs=(
            pl.BlockSpec((TM_PROJ, D_CONV), lambda i: (i, 0)),
            pl.BlockSpec((TM_PROJ, D_SSM), lambda i: (i, 0)),
        ),
        out_shape=(
            jax.ShapeDtypeStruct((t, D_CONV), F32),
            jax.ShapeDtypeStruct((t, D_SSM), F32),
        ),
        compiler_params=pltpu.CompilerParams(
            dimension_semantics=("arbitrary",), vmem_limit_bytes=40 << 20),
        name="in_proj",
    )(xt, norm_w, w_in)


def _conv_kernel(v_ref, w_ref, b_ref, lnw_ref, lnb_ref, o_ref, ext_ref, sh_ref):
    tt = pl.program_id(1)
    rows = CONV_HALO + TM_CONV

    @pl.when(tt == 0)
    def _():
        ext_ref[pl.ds(0, CONV_HALO), :] = jnp.zeros((CONV_HALO, D_CONV), F32)

    @pl.when(tt > 0)
    def _():
        ext_ref[pl.ds(0, CONV_HALO), :] = ext_ref[pl.ds(TM_CONV, CONV_HALO), :]

    ext_ref[pl.ds(CONV_HALO, TM_CONV), :] = v_ref[...]
    for s in range(1, SUBLANES):
        sh_ref[s - 1, pl.ds(0, rows - SUBLANES), :] = ext_ref[pl.ds(s, rows - SUBLANES), :]
    bias = b_ref[...]
    lnw = lnw_ref[...]
    lnb = lnb_ref[...]
    tap0 = CONV_HALO - (CONV_WIDTH - 1)

    groups = CONV_CHUNK // SUBLANES

    def chunk(ci, carry):
        base = pl.multiple_of(ci * CONV_CHUNK, CONV_CHUNK)
        for lt in range(D_CONV // LANES):
            lanes = pl.ds(lt * LANES, LANES)
            acc = [jnp.broadcast_to(bias[:, lt * LANES:(lt + 1) * LANES], (SUBLANES, LANES))] * groups
            for s in range(SUBLANES):
                taps = [j for j in range(CONV_WIDTH) if (tap0 + j) % SUBLANES == s]
                src = ext_ref if s == 0 else sh_ref.at[s - 1]
                ngroups = (tap0 + taps[-1] - s) // SUBLANES + groups
                win = [src[pl.ds(base + SUBLANES * g, SUBLANES), lanes] for g in range(ngroups)]
                for j in taps:
                    g0 = (tap0 + j - s) // SUBLANES
                    wj = w_ref[j, :, lanes]
                    acc = [acc[r] + wj * win[g0 + r] for r in range(groups)]
            o_ref[pl.ds(base, CONV_CHUNK), lanes] = jnp.concatenate(acc, axis=0)
        return carry

    lax.fori_loop(0, TM_CONV // CONV_CHUNK, chunk, 0)

    for bi in range(TM_CONV // CONV_NORM_ROWS):
        rows_b = pl.ds(bi * CONV_NORM_ROWS, CONV_NORM_ROWS)
        acc = o_ref[rows_b, :]
        mu = jnp.mean(acc, axis=-1, keepdims=True)
        cen = acc - mu
        var = jnp.mean(cen * cen, axis=-1, keepdims=True)
        z = cen * lax.rsqrt(var + EPS) * lnw + lnb
        o_ref[rows_b, :] = z * jax.nn.sigmoid(z)


def _conv(v, conv_w, conv_b, ln_w, ln_b, bsz, seq):
    nt = seq // TM_CONV
    row = lambda b, t: (b * nt + t, 0)
    const = lambda b, t: (0, 0)
    return pl.pallas_call(
        _conv_kernel,
        grid=(bsz, nt),
        in_specs=[
            pl.BlockSpec((TM_CONV, D_CONV), row),
            pl.BlockSpec((CONV_WIDTH, SUBLANES, D_CONV), lambda b, t: (0, 0, 0)),
            pl.BlockSpec((1, D_CONV), const),
            pl.BlockSpec((1, D_CONV), const),
            pl.BlockSpec((1, D_CONV), const),
        ],
        out_specs=pl.BlockSpec((TM_CONV, D_CONV), row),
        out_shape=jax.ShapeDtypeStruct(v.shape, F32),
        scratch_shapes=[
            pltpu.VMEM((CONV_HALO + TM_CONV, D_CONV), F32),
            pltpu.VMEM((SUBLANES - 1, CONV_HALO + TM_CONV, D_CONV), F32),
        ],
        compiler_params=pltpu.CompilerParams(
            dimension_semantics=("arbitrary", "arbitrary")),
        name="conv",
    )(v, jnp.broadcast_to(conv_w[:, None, :], (CONV_WIDTH, SUBLANES, D_CONV)), conv_b, ln_w, ln_b)


def _cmul(a_re, a_im, b_re, b_im):
    return a_re * b_re - a_im * b_im, a_re * b_im + a_im * b_re


def _s5_kernel(u_ref, bcat_ref, ccat_ref, d_ref, wab_ref, a1_ref,
               pj_ref, aj_ref, o_ref, up_ref, bu_ref, st_ref, carry_ref):
    tt = pl.program_id(2)
    ns = S5_STATE
    steps_per_group = S5_GROUP_ROWS // SUBLANES

    @pl.when(tt == 0)
    def _():
        carry_ref[...] = jnp.zeros(carry_ref.shape, F32)

    for j in range(S5_STEPS):
        up_ref[pl.ds(SUBLANES * j, SUBLANES), :] = u_ref[pl.ds(j, SUBLANES, stride=S5_STEPS), :]

    a_re = a1_ref[0]
    a_im = a1_ref[1]

    def step(j, s_re, s_im):
        rows = pl.ds(j * SUBLANES, SUBLANES)
        m_re, m_im = _cmul(a_re, a_im, s_re, s_im)
        return m_re + bu_ref[rows, pl.ds(0, ns)], m_im + bu_ref[rows, pl.ds(ns, ns)]

    s_re = jnp.zeros((SUBLANES, ns), F32)
    s_im = s_re
    for g in range(S5_TILE // S5_GROUP_ROWS):
        rows = pl.ds(g * S5_GROUP_ROWS, S5_GROUP_ROWS)
        bu_ref[rows, :] = _dot(up_ref[rows, :].astype(BF16), bcat_ref[0])
        for jj in range(steps_per_group):
            s_re, s_im = step(g * steps_per_group + jj, s_re, s_im)
    e_re, e_im = s_re, s_im

    row = lax.broadcasted_iota(jnp.int32, (SUBLANES, ns), 0)
    first = row == 0
    c_re = jnp.where(first, pltpu.roll(carry_ref[0], 1, 0), pltpu.roll(e_re, 1, 0))
    c_im = jnp.where(first, pltpu.roll(carry_ref[1], 1, 0), pltpu.roll(e_im, 1, 0))
    for i, d in enumerate((1, 2, 4)):
        r_re = pltpu.roll(c_re, d, 0)
        r_im = pltpu.roll(c_im, d, 0)
        m_re, m_im = _cmul(pj_ref[0, i], pj_ref[1, i], r_re, r_im)
        c_re = c_re + m_re
        c_im = c_im + m_im
    f_re, f_im = _cmul(aj_ref[0], aj_ref[1], c_re, c_im)
    carry_ref[0] = f_re + e_re
    carry_ref[1] = f_im + e_im

    s_re, s_im = c_re, c_im
    pack = 2 * SUBLANES
    for g in range(S5_TILE // S5_GROUP_ROWS):
        rows = pl.ds(g * S5_GROUP_ROWS, S5_GROUP_ROWS)
        for jj in range(0, steps_per_group, 2):
            j = g * steps_per_group + jj
            p_re, p_im = step(j, s_re, s_im)
            s_re, s_im = step(j + 1, p_re, p_im)
            st_ref[pl.ds(j * SUBLANES, pack), pl.ds(0, ns)] = (
                jnp.concatenate([p_re, s_re], axis=0).astype(BF16))
            st_ref[pl.ds(j * SUBLANES, pack), pl.ds(ns, ns)] = (
                jnp.concatenate([p_im, s_im], axis=0).astype(BF16))
        y = _dot(st_ref[rows, :], ccat_ref[0]) + d_ref[...] * up_ref[rows, :]
        y = jax.nn.gelu(y)
        ab = _dot(y.astype(BF16), wab_ref[0])
        out = ab[:, :S5_LANES] * jax.nn.sigmoid(ab[:, S5_LANES:])
        for jj in range(steps_per_group):
            j = g * steps_per_group + jj
            o_ref[pl.ds(j, SUBLANES, stride=S5_STEPS), :] = out[jj * SUBLANES:(jj + 1) * SUBLANES, :]


def _s5(u, bcat, ccat, d, wab, a1, pj, aj, bsz, seq):
    nt = seq // S5_TILE
    nlb = D_SSM // S5_LANES
    ns = S5_STATE
    row = lambda b, l, t: (b * nt + t, l)
    lane3 = lambda b, l, t: (0, 0, l)
    lane4 = lambda b, l, t: (0, 0, 0, l)
    return pl.pallas_call(
        _s5_kernel,
        grid=(bsz, nlb, nt),
        in_specs=[
            pl.BlockSpec((S5_TILE, S5_LANES), row),
            pl.BlockSpec((1, S5_LANES, 2 * ns), lambda b, l, t: (l, 0, 0)),
            pl.BlockSpec((1, 2 * ns, S5_LANES), lambda b, l, t: (l, 0, 0)),
            pl.BlockSpec((1, S5_LANES), lambda b, l, t: (0, l)),
            pl.BlockSpec((1, S5_LANES, 2 * S5_LANES), lambda b, l, t: (l, 0, 0)),
            pl.BlockSpec((2, SUBLANES, ns), lane3),
            pl.BlockSpec((2, 3, SUBLANES, ns), lane4),
            pl.BlockSpec((2, SUBLANES, ns), lane3),
        ],
        out_specs=pl.BlockSpec((S5_TILE, S5_LANES), row),
        out_shape=jax.ShapeDtypeStruct(u.shape, F32),
        scratch_shapes=[
            pltpu.VMEM((S5_TILE, S5_LANES), F32),
            pltpu.VMEM((S5_TILE, 2 * ns), F32),
            pltpu.VMEM((S5_TILE, 2 * ns), BF16),
            pltpu.VMEM((2, SUBLANES, ns), F32),
        ],
        compiler_params=pltpu.CompilerParams(
            dimension_semantics=("arbitrary", "arbitrary", "arbitrary"),
            vmem_limit_bytes=40 << 20),
        name="s5",
    )(u, bcat, ccat, d, wab, a1, pj, aj)


def _kv_kernel(m_ref, nw_ref, wk_ref, wv_ref, k_ref, v_ref):
    mn = _rms(m_ref[0], nw_ref[...]).astype(BF16)
    k_ref[0] = _dot(mn, wk_ref[...]).astype(BF16)
    v_ref[0] = _dot(mn, wv_ref[...]).astype(BF16)


def _kv(mem, norm_w, wk, wv):
    bsz, mlen, d = mem.shape
    blk = pl.BlockSpec((1, mlen, d), lambda b: (b, 0, 0))
    wspec = pl.BlockSpec((d, d), lambda b: (0, 0))
    return pl.pallas_call(
        _kv_kernel,
        grid=(bsz,),
        in_specs=[blk, pl.BlockSpec((1, d), lambda b: (0, 0)), wspec, wspec],
        out_specs=(blk, blk),
        out_shape=(jax.ShapeDtypeStruct(mem.shape, BF16),) * 2,
        compiler_params=pltpu.CompilerParams(
            dimension_semantics=("arbitrary",), vmem_limit_bytes=40 << 20),
        name="kv",
    )(mem, norm_w, wk, wv)


def _mix_attn_kernel(x_ref, yc_ref, ys_ref, wot_ref, wob_ref, nx_ref, wq_ref,
                     k_ref, v_ref, wo_ref, nf_ref, wr_ref, br_ref,
                     h_ref, xs_ref, y0_ref, gs_ref, ri_ref, cnt_ref):
    tm = x_ref.shape[0]

    h1 = (x_ref[...] + _dot(yc_ref[...].astype(BF16), wot_ref[...])
          + _dot(ys_ref[...].astype(BF16), wob_ref[...]))

    hn = _rms(h1, nx_ref[...]).astype(BF16)
    q = _dot(hn, wq_ref[...])
    heads = []
    for hd in range(N_XHEADS):
        sl = slice(hd * XHEAD_DIM, (hd + 1) * XHEAD_DIM)
        qh = q[:, sl].astype(BF16)
        s = lax.dot_general(qh, k_ref[0, :, sl], (((1,), (1,)), ((), ())),
                            preferred_element_type=F32) * (XHEAD_DIM ** -0.5)
        s = s - jnp.max(s, axis=-1, keepdims=True)
        p = jnp.exp(s)
        p = p / jnp.sum(p, axis=-1, keepdims=True)
        heads.append(_dot(p.astype(BF16), v_ref[0, :, sl]).astype(BF16))
    o = jnp.concatenate(heads, axis=-1)
    h2 = h1 + _dot(o, wo_ref[...])
    h_ref[...] = h2

    hf = _rms(h2, nf_ref[...])
    logits = _dot(hf.astype(BF16), wr_ref[...]) + br_ref[...]

    lane = lax.broadcasted_iota(jnp.int32, (tm, ROUTE_LANES), 1)
    lane_f = lane.astype(F32)
    neg = -jnp.inf
    big = float(ROUTE_LANES)

    def top1(vals):
        m = jnp.max(vals, axis=-1, keepdims=True)
        idx = jnp.min(jnp.where(vals == m, lane_f, big), axis=-1, keepdims=True)
        return m, idx

    gl = jnp.where(lane < N_EXPERT_GROUPS, logits, neg)
    gmax, gidx = top1(gl)
    g_w = 1.0 / jnp.sum(jnp.exp(gl - gmax), axis=-1, keepdims=True)
    lo = EXPERT_LANE0 + EXPERTS_PER_GROUP * gidx
    el = jnp.where((lane_f >= lo) & (lane_f < lo + EXPERTS_PER_GROUP), logits, neg)
    m1, i1 = top1(el)
    m2, i2 = top1(jnp.where(lane_f == i1, neg, el))
    e21 = jnp.exp(m2 - m1)
    gate1 = g_w / (1.0 + e21)
    gate2 = g_w * e21 / (1.0 + e21)

    hot1 = lane_f == i1
    hot2 = lane_f == i2
    hot = jnp.where(hot1 | hot2, 1.0, 0.0)
    r_i = lax.broadcasted_iota(jnp.int32, (tm, tm), 0)
    c_i = lax.broadcasted_iota(jnp.int32, (tm, tm), 1)
    tri = jnp.where(r_i > c_i, 1.0, 0.0).astype(BF16)
    before = _dot(tri, hot.astype(BF16))
    count = jnp.sum(hot, axis=0, keepdims=True)
    chunks = jnp.floor((count + (MOE_CHUNK - 1.0)) * (1.0 / MOE_CHUNK))
    l_i = lax.broadcasted_iota(jnp.int32, (ROUTE_LANES, ROUTE_LANES), 0)
    l_j = lax.broadcasted_iota(jnp.int32, (ROUTE_LANES, ROUTE_LANES), 1)
    upper = jnp.where(l_i < l_j, 1.0, 0.0).astype(BF16)
    first_chunk = _dot(jnp.broadcast_to(chunks, (SUBLANES, ROUTE_LANES)).astype(BF16), upper)[0:1]
    start = first_chunk * float(MOE_CHUNK) + before
    pos1 = jnp.sum(jnp.where(hot1, start, 0.0), axis=-1, keepdims=True)
    pos2 = jnp.sum(jnp.where(hot2, start, 0.0), axis=-1, keepdims=True)
    cnt_ref[0] = jnp.broadcast_to(count, (SUBLANES, ROUTE_LANES)).astype(jnp.int32)
    ri_ref[...] = jnp.where(lane == 0, pos1, jnp.where(lane == 1, pos2, 0.0)).astype(jnp.int32)

    rho = lax.broadcasted_iota(jnp.int32, (tm, TILE_ROWS), 1).astype(F32)
    is1 = rho == pos1
    is2 = rho == pos2
    pt = jnp.where(is1 | is2, 1.0, 0.0).astype(BF16)
    contract0 = (((0,), (0,)), ((), ()))
    srt = lax.dot_general(pt, hf.astype(BF16), contract0, preferred_element_type=F32)
    xs_ref[...] = _pack_words(srt).reshape(TILE_CHUNKS, MOE_CHUNK, D_PACK)
    y0_ref[...] = jnp.zeros(y0_ref.shape, jnp.uint32)

    def pieces(g):
        hi = g.astype(BF16).astype(F32)
        mid = (g - hi).astype(BF16).astype(F32)
        low = (g - hi - mid).astype(BF16).astype(F32)
        return hi, mid, low

    p1 = pieces(gate1)
    p2 = pieces(gate2)
    g6 = jnp.where(lane == 6, 1.0, 0.0)
    for li, piece in enumerate(p1 + p2):
        g6 = jnp.where(lane == li, piece, g6)
    ptk = jnp.where(is1, 1.0, jnp.where(is2, 2.0, 0.0)).astype(BF16)
    sg = lax.dot_general(ptk, g6.astype(BF16), contract0, preferred_element_type=F32)
    which = sg[:, 6:7]
    first = sg[:, 0:1] + sg[:, 1:2] + sg[:, 2:3]
    second = 0.5 * (sg[:, 3:4] + sg[:, 4:5] + sg[:, 5:6])
    gsort = jnp.where(which == 1.0, first, jnp.where(which == 2.0, second, 0.0))
    gs_ref[...] = jnp.broadcast_to(gsort, (TILE_ROWS, ROUTE_LANES))


def _pack_words(v):
    hi = lax.bitcast_convert_type(v[:, :D_PACK], jnp.uint32) & jnp.uint32(HI_MASK)
    lo = lax.bitcast_convert_type(v[:, D_PACK:], jnp.uint32) >> 16
    return hi | lo


def _unpack_words(w):
    hi = lax.bitcast_convert_type(w & jnp.uint32(HI_MASK), F32)
    lo = lax.bitcast_convert_type(w << 16, F32)
    return hi, lo


def _mix_attn(xt, yc, ys, wot, wob, nx, wq, k, v, wo, nf, wr, br, seq):
    t, d = xt.shape
    nt = t // TM_MIX
    tiles_per_batch = seq // TM_MIX
    mlen = k.shape[1]
    row = lambda i: (i, 0)
    const = lambda i: (0, 0)
    tile3 = lambda i: (i, 0, 0)
    kvspec = pl.BlockSpec((1, mlen, d), lambda i: (i // tiles_per_batch, 0, 0))
    packed = jax.ShapeDtypeStruct((nt * TILE_CHUNKS, MOE_CHUNK, D_PACK), jnp.uint32)
    return pl.pallas_call(
        _mix_attn_kernel,
        grid=(nt,),
        in_specs=[
            pl.BlockSpec((TM_MIX, d), row),
            pl.BlockSpec((TM_MIX, D_CONV), row),
            pl.BlockSpec((TM_MIX, D_SSM), row),
            pl.BlockSpec((D_CONV, d), const),
            pl.BlockSpec((D_SSM, d), const),
            pl.BlockSpec((1, d), const),
            pl.BlockSpec((d, d), const),
            kvspec, kvspec,
            pl.BlockSpec((d, d), const),
            pl.BlockSpec((1, d), const),
            pl.BlockSpec((d, ROUTE_LANES), const),
            pl.BlockSpec((1, ROUTE_LANES), const),
        ],
        out_specs=(
            pl.BlockSpec((TM_MIX, d), row),
            pl.BlockSpec((TILE_CHUNKS, MOE_CHUNK, D_PACK), tile3),
            pl.BlockSpec((TILE_CHUNKS, MOE_CHUNK, D_PACK), tile3),
            pl.BlockSpec((TILE_ROWS, ROUTE_LANES), row),
            pl.BlockSpec((TM_MIX, ROUTE_LANES), row),
            pl.BlockSpec((1, SUBLANES, ROUTE_LANES), tile3),
        ),
        out_shape=(
            jax.ShapeDtypeStruct((t, d), F32),
            packed,
            packed,
            jax.ShapeDtypeStruct((nt * TILE_ROWS, ROUTE_LANES), F32),
            jax.ShapeDtypeStruct((t, ROUTE_LANES), jnp.int32),
            jax.ShapeDtypeStruct((nt, SUBLANES, ROUTE_LANES), jnp.int32),
        ),
        compiler_params=pltpu.CompilerParams(
            dimension_semantics=("arbitrary",), vmem_limit_bytes=60 << 20),
        name="mix_attn",
    )(xt, yc, ys, wot, wob, nx, wq, k, v, wo, nf, wr, br)


ZERO_CHUNK = TILE_CHUNKS - 1
DUMP_TILE0 = 16
SPARE_PER_TILE = TILE_CHUNKS - TILE_USED_CHUNKS


def _dump_chunk(slot, k):
    idx = slot * BLOCK_CHUNKS + k
    return (DUMP_TILE0 + idx // SPARE_PER_TILE) * TILE_CHUNKS + TILE_USED_CHUNKS + idx % SPARE_PER_TILE


def _experts_kernel(src_ref, dst_ref, be_ref, na_ref, xs_hbm, w1_ref, w3_ref, w2_ref,
                    y0_hbm, ys_hbm, xbuf, ybuf, w1s_ref, w3s_ref, w2s_ref, in_sem, out_sem):
    del y0_hbm
    b = pl.program_id(0)
    n_active = na_ref[0]
    slot = b % 2
    other = 1 - slot

    def gather(blk, sl):
        for k in range(BLOCK_CHUNKS):
            pltpu.make_async_copy(xs_hbm.at[src_ref[blk * BLOCK_CHUNKS + k]],
                                  xbuf.at[sl, k], in_sem.at[sl]).start()

    def gather_wait(sl):
        pltpu.make_async_copy(xs_hbm.at[pl.ds(0, BLOCK_CHUNKS)], xbuf.at[sl],
                              in_sem.at[sl]).wait()

    def scatter(blk, sl):
        for k in range(BLOCK_CHUNKS):
            d = dst_ref[blk * BLOCK_CHUNKS + k]
            dump = jnp.where(sl == 0, _dump_chunk(0, k), _dump_chunk(1, k))
            pltpu.make_async_copy(ybuf.at[sl, k], ys_hbm.at[jnp.where(d < 0, dump, d)],
                                  out_sem.at[sl]).start()

    def scatter_wait(sl):
        pltpu.make_async_copy(ybuf.at[sl], ys_hbm.at[pl.ds(0, BLOCK_CHUNKS)],
                              out_sem.at[sl]).wait()

    active = b < n_active

    @pl.when(b == 0)
    def _():
        gather(0, 0)

    @pl.when(b + 1 < n_active)
    def _():
        gather(b + 1, other)

    prev = be_ref[jnp.maximum(b - 1, 0)]
    fresh = (b == 0) | (be_ref[b] != prev)

    @pl.when(active & fresh)
    def _():
        w1s_ref[...] = w1_ref[0].astype(BF16)
        w3s_ref[...] = w3_ref[0].astype(BF16)
        w2s_ref[...] = w2_ref[0].astype(BF16)

    @pl.when(active)
    def _():
        gather_wait(slot)

        @pl.when(b >= 2)
        def _():
            scatter_wait(slot)

        hi, lo = _unpack_words(xbuf[slot].reshape(MOE_BLOCK, D_PACK))
        xb = jnp.concatenate([hi, lo], axis=1).astype(BF16)
        h1 = _dot(xb, w1s_ref[...])
        h3 = _dot(xb, w3s_ref[...])
        hid = (h1 * jax.nn.sigmoid(h1) * h3).astype(BF16)
        y = _dot(hid, w2s_ref[...]).astype(BF16).astype(F32)
        ybuf[slot] = _pack_words(y).reshape(BLOCK_CHUNKS, MOE_CHUNK, D_PACK)
        scatter(b, slot)

        @pl.when(b == n_active - 1)
        def _():
            scatter_wait(slot)

            @pl.when(b >= 1)
            def _():
                scatter_wait(other)


def _experts(src, dst, block_expert, n_active, xs, w1, w3, w2, y0):
    nb = block_expert.shape[0]
    d = D_MODEL

    def wsel(b, src, dst, be, na):
        return (be[jnp.minimum(b, na[0] - 1)], 0, 0)

    grid_spec = pltpu.PrefetchScalarGridSpec(
        num_scalar_prefetch=4,
        grid=(nb,),
        in_specs=[
            pl.BlockSpec(memory_space=pl.ANY),
            pl.BlockSpec((1, d, D_EXPERT), wsel),
            pl.BlockSpec((1, d, D_EXPERT), wsel),
            pl.BlockSpec((1, D_EXPERT, d), wsel),
            pl.BlockSpec(memory_space=pl.ANY),
        ],
        out_specs=pl.BlockSpec(memory_space=pl.ANY),
        scratch_shapes=[
            pltpu.VMEM((2, BLOCK_CHUNKS, MOE_CHUNK, D_PACK), jnp.uint32),
            pltpu.VMEM((2, BLOCK_CHUNKS, MOE_CHUNK, D_PACK), jnp.uint32),
            pltpu.VMEM((d, D_EXPERT), BF16),
            pltpu.VMEM((d, D_EXPERT), BF16),
            pltpu.VMEM((D_EXPERT, d), BF16),
            pltpu.SemaphoreType.DMA((2,)),
            pltpu.SemaphoreType.DMA((2,)),
        ],
    )
    return pl.pallas_call(
        _experts_kernel,
        grid_spec=grid_spec,
        out_shape=jax.ShapeDtypeStruct(y0.shape, jnp.uint32),
        input_output_aliases={8: 0},
        compiler_params=pltpu.CompilerParams(
            dimension_semantics=("arbitrary",), vmem_limit_bytes=48 << 20),
        name="experts",
    )(src, dst, block_expert, n_active, xs, w1, w3, w2, y0)


def _combine_kernel(h_ref, ri_ref, gs_ref, ys_ref, fw_ref, o_ref):
    tm = h_ref.shape[0]
    hi, lo = _unpack_words(ys_ref[...].reshape(TILE_ROWS, D_PACK))
    g = gs_ref[...]
    gw = jnp.concatenate([g] * (D_PACK // ROUTE_LANES), axis=1)
    yg = jnp.concatenate([hi * gw, lo * gw], axis=1).astype(BF16)
    pos = ri_ref[...].astype(F32)
    rho = lax.broadcasted_iota(jnp.int32, (tm, TILE_ROWS), 1).astype(F32)
    q = jnp.where((rho == pos[:, 0:1]) | (rho == pos[:, 1:2]), 1.0, 0.0).astype(BF16)
    o_ref[...] = _rms(h_ref[...] + _dot(q, yg), fw_ref[...])


def _combine(h2, route_i, gsort, ys, final_w):
    t, d = h2.shape
    nt = t // TM_MIX
    row = lambda i: (i, 0)
    return pl.pallas_call(
        _combine_kernel,
        grid=(nt,),
        in_specs=[
            pl.BlockSpec((TM_MIX, d), row),
            pl.BlockSpec((TM_MIX, ROUTE_LANES), row),
            pl.BlockSpec((TILE_ROWS, ROUTE_LANES), row),
            pl.BlockSpec((TILE_CHUNKS, MOE_CHUNK, D_PACK), lambda i: (i, 0, 0)),
            pl.BlockSpec((1, d), lambda i: (0, 0)),
        ],
        out_specs=pl.BlockSpec((TM_MIX, d), row),
        out_shape=jax.ShapeDtypeStruct((t, d), F32),
        compiler_params=pltpu.CompilerParams(
            dimension_semantics=("arbitrary",), vmem_limit_bytes=48 << 20),
        name="combine",
    )(h2, route_i, gsort, ys, final_w)


def _routing_tables(cnt):
    nt = cnt.shape[0]
    n = cnt[:, 0, EXPERT_LANE0:EXPERT_LANE0 + N_EXPERTS]
    c = (n + MOE_CHUNK - 1) // MOE_CHUNK
    local = jnp.cumsum(c, axis=1) - c
    per_expert = jnp.sum(c, axis=0)
    padded = (per_expert + BLOCK_CHUNKS - 1) // BLOCK_CHUNKS * BLOCK_CHUNKS
    gend = jnp.cumsum(padded)
    gstart = gend - padded
    within = jnp.cumsum(c, axis=0) - c
    start = (gstart[None, :] + within).T.reshape(-1)
    length = c.T.reshape(-1)
    base = (jnp.arange(nt, dtype=jnp.int32)[:, None] * TILE_CHUNKS + local).T.reshape(-1)
    g_max = (2 * nt * TM_MIX // MOE_CHUNK + nt * N_EXPERTS
             + N_EXPERTS * (BLOCK_CHUNKS - 1))
    n_blocks = -(-g_max // BLOCK_CHUNKS)
    g = jnp.arange(n_blocks * BLOCK_CHUNKS, dtype=jnp.int32)
    nxt = jnp.concatenate([start[1:], jnp.full((1,), 2 ** 30, start.dtype)])
    owner = ((start[None, :] <= g[:, None]) & (nxt[None, :] > g[:, None])).astype(jnp.int32)
    k = g - jnp.sum(owner * start[None, :], axis=1)
    valid = k < jnp.sum(owner * length[None, :], axis=1)
    chunk = jnp.sum(owner * base[None, :], axis=1) + k
    src = jnp.where(valid, chunk, ZERO_CHUNK).astype(jnp.int32)
    dst = jnp.where(valid, chunk, -1).astype(jnp.int32)
    block_start = jnp.arange(n_blocks, dtype=jnp.int32) * BLOCK_CHUNKS
    block_expert = jnp.minimum(
        jnp.sum((block_start[:, None] >= gend[None, :]).astype(jnp.int32), axis=1),
        N_EXPERTS - 1).astype(jnp.int32)
    n_active = (gend[-1:] // BLOCK_CHUNKS).astype(jnp.int32)
    return src, dst, block_expert, n_active


def _layer(h, mem, p, final_w):
    bsz, seq, d = h.shape
    t = bsz * seq
    assert t // TM_MIX >= DUMP_TILE0 + 2 * BLOCK_CHUNKS // SPARE_PER_TILE
    xt = h.reshape(t, d)

    v, u = _in_proj(xt, p["norm_mix_w"].reshape(1, d), p["w_in"].astype(BF16))
    y_conv = _conv(v, p["conv_w"], p["conv_b"].reshape(1, D_CONV),
                   p["conv_ln_w"].reshape(1, D_CONV), p["conv_ln_b"].reshape(1, D_CONV),
                   bsz, seq)

    a1, pj, aj, bbar = _s5_prep(p["ssm_A_re"], p["ssm_A_im"], p["ssm_log_dt"],
                                      p["ssm_B_re"], p["ssm_B_im"])
    nlb = D_SSM // S5_LANES
    bb = bbar.reshape(2, SSM_GROUP, nlb, S5_GROUPS, SSM_STATE).transpose(0, 2, 3, 1, 4)
    bcat = jnp.concatenate([_block_diag(bb[0]), _block_diag(bb[1])], axis=-1).astype(BF16)
    c_re = p["ssm_C_re"].reshape(nlb, S5_GROUPS, SSM_GROUP, SSM_STATE).transpose(0, 1, 3, 2)
    c_im = p["ssm_C_im"].reshape(nlb, S5_GROUPS, SSM_GROUP, SSM_STATE).transpose(0, 1, 3, 2)
    ccat = jnp.concatenate([_block_diag(c_re), -_block_diag(c_im)], axis=1).astype(BF16)
    glu = p["ssm_glu_w"].reshape(nlb, S5_GROUPS, SSM_GROUP, 2 * SSM_GROUP)
    wab = jnp.concatenate([_block_diag(glu[..., :SSM_GROUP]),
                           _block_diag(glu[..., SSM_GROUP:])], axis=-1).astype(BF16)
    y_ssm = _s5(u, bcat, ccat, p["ssm_D"].reshape(1, D_SSM), wab,
                a1, pj, aj, bsz, seq)

    k, vv = _kv(mem, p["norm_mem_w"].reshape(1, d), p["xk_w"].astype(BF16),
                p["xv_w"].astype(BF16))
    w_out = p["w_out"].astype(BF16)
    wr = jnp.concatenate([p["router_group_w"], p["router_expert_w"]], axis=1)
    wr = jnp.pad(wr, ((0, 0), (0, ROUTE_LANES - wr.shape[1]))).astype(BF16)
    br = jnp.concatenate([p["router_group_b"], p["router_expert_b"].reshape(-1)])
    br = jnp.pad(br, (0, ROUTE_LANES - br.shape[0])).reshape(1, ROUTE_LANES)
    h2, xs, y0, gsort, route_i, cnt = _mix_attn(
        xt, y_conv, y_ssm, w_out[:D_CONV], w_out[D_CONV:], p["norm_x_w"].reshape(1, d),
        p["xq_w"].astype(BF16), k, vv, p["xo_w"].astype(BF16),
        p["norm_ffn_w"].reshape(1, d), wr, br, seq)

    src, dst, block_expert, n_active = _routing_tables(cnt)
    ys = _experts(src, dst, block_expert, n_active, xs,
                  p["moe_w1"], p["moe_w3"], p["moe_w2"], y0)
    out = _combine(h2, route_i, gsort, ys, final_w.reshape(1, d))
    return out.reshape(bsz, seq, d)


def kernel(x, mem, norm_mix_w, w_in, conv_w, conv_b, conv_ln_w, conv_ln_b, ssm_A_re, ssm_A_im, ssm_log_dt, ssm_B_re, ssm_B_im, ssm_C_re, ssm_C_im, ssm_D, ssm_glu_w, w_out, norm_x_w, norm_mem_w, xq_w, xk_w, xv_w, xo_w, norm_ffn_w, router_group_w, router_group_b, router_expert_w, router_expert_b, moe_w1, moe_w3, moe_w2, final_norm_w):
    stacked = dict(
        norm_mix_w=norm_mix_w, w_in=w_in, conv_w=conv_w, conv_b=conv_b,
        conv_ln_w=conv_ln_w, conv_ln_b=conv_ln_b, ssm_A_re=ssm_A_re, ssm_A_im=ssm_A_im,
        ssm_log_dt=ssm_log_dt, ssm_B_re=ssm_B_re, ssm_B_im=ssm_B_im, ssm_C_re=ssm_C_re,
        ssm_C_im=ssm_C_im, ssm_D=ssm_D, ssm_glu_w=ssm_glu_w, w_out=w_out,
        norm_x_w=norm_x_w, norm_mem_w=norm_mem_w, xq_w=xq_w, xk_w=xk_w, xv_w=xv_w,
        xo_w=xo_w, norm_ffn_w=norm_ffn_w, router_group_w=router_group_w,
        router_group_b=router_group_b, router_expert_w=router_expert_w,
        router_expert_b=router_expert_b, moe_w1=moe_w1, moe_w3=moe_w3, moe_w2=moe_w2)
    depth = norm_mix_w.shape[0]
    assert depth == 1, "final norm is fused into the single layer's combine step"
    layer = {name: w[0] for name, w in stacked.items()}
    return _layer(x, mem, layer, final_norm_w)
```

```python
import functools

import jax
import jax.numpy as jnp
from jax import lax
from jax.experimental import pallas as pl
from jax.experimental.pallas import tpu as pltpu

D_MODEL = 1024
D_CONV = 512
CONV_WIDTH = 31
D_SSM = 512
SSM_GROUP = 16
N_SSM_GROUPS = 32
SSM_STATE = 64
N_XHEADS = 4
XHEAD_DIM = 256
N_EXPERT_GROUPS = 4
EXPERTS_PER_GROUP = 8
N_EXPERTS = 32
D_EXPERT = 512
EPS = 1e-6

F32 = jnp.float32
BF16 = jnp.bfloat16

SUBLANES = 8
LANES = 128

TM_PROJ = 512
TM_CONV = 512
CONV_CHUNK = 128
CONV_NORM_ROWS = 128
CONV_HALO = 32
S5_STEPS = 64
S5_TILE = SUBLANES * S5_STEPS
S5_GROUP_ROWS = 128
S5_LANES = 128
S5_GROUPS = S5_LANES // SSM_GROUP
S5_STATE = S5_GROUPS * SSM_STATE
TM_MIX = 512
MIX_SPLIT = 1
ROUTE_LANES = 128
EXPERT_LANE0 = N_EXPERT_GROUPS
MOE_CHUNK = SUBLANES
MOE_BLOCK = 256
BLOCK_CHUNKS = MOE_BLOCK // MOE_CHUNK
TILE_USED_CHUNKS = 2 * TM_MIX // MOE_CHUNK + N_EXPERTS * (MOE_CHUNK - 1) // MOE_CHUNK
TILE_CHUNKS = 160
TILE_ROWS = TILE_CHUNKS * MOE_CHUNK
D_PACK = D_MODEL // 2
HI_MASK = 0xFFFF0000


def _rms(x, w):
    return x * lax.rsqrt(jnp.mean(x * x, axis=-1, keepdims=True) + EPS) * w


def _dot(a, b):
    return jnp.dot(a, b, preferred_element_type=F32)


def _s5_prep_kernel(are_ref, aim_ref, ldt_ref, btre_ref, btim_ref,
                    a1_ref, pj_ref, aj_ref, bbar_ref):
    lam_re = are_ref[...]
    lam_im = aim_ref[...]
    dt = jnp.exp(ldt_ref[...])
    x = lam_re * dt
    y = lam_im * dt

    def power(k):
        mag = jnp.exp(k * x)
        return mag * jnp.cos(k * y), mag * jnp.sin(k * y)

    n = x.shape[-1]
    ones8 = jnp.ones((SUBLANES, n), F32)
    a_re, a_im = power(ones8)
    a1_ref[0] = a_re
    a1_ref[1] = a_im

    row = lax.broadcasted_iota(jnp.int32, (SUBLANES, n), 0)
    for i, d in enumerate((1, 2, 4)):
        p_re, p_im = power(ones8 * float(d * S5_STEPS))
        keep = row >= d
        pj_ref[0, i] = jnp.where(keep, p_re, 0.0)
        pj_ref[1, i] = jnp.where(keep, p_im, 0.0)
    j_re, j_im = power(ones8 * float(S5_STEPS))
    aj_ref[0] = j_re
    aj_ref[1] = j_im

    num_re = a_re[0:1] - 1.0
    num_im = a_im[0:1]
    den = lam_re * lam_re + lam_im * lam_im
    c_re = (num_re * lam_re + num_im * lam_im) / den
    c_im = (num_im * lam_re - num_re * lam_im) / den
    b_re = btre_ref[...]
    b_im = btim_ref[...]
    bbar_ref[0] = c_re * b_re - c_im * b_im
    bbar_ref[1] = c_re * b_im + c_im * b_re


def _s5_prep(a_re, a_im, log_dt, b_re, b_im):
    n = N_SSM_GROUPS * SSM_STATE
    are = a_re.reshape(1, n)
    aim = a_im.reshape(1, n)
    ldt = jnp.repeat(log_dt, SSM_STATE).reshape(1, n)
    btre = jnp.transpose(b_re, (2, 0, 1)).reshape(SSM_GROUP, n)
    btim = jnp.transpose(b_im, (2, 0, 1)).reshape(SSM_GROUP, n)
    return pl.pallas_call(
        _s5_prep_kernel,
        out_shape=(
            jax.ShapeDtypeStruct((2, SUBLANES, n), F32),
            jax.ShapeDtypeStruct((2, 3, SUBLANES, n), F32),
            jax.ShapeDtypeStruct((2, SUBLANES, n), F32),
            jax.ShapeDtypeStruct((2, SSM_GROUP, n), F32),
        ),
        name="s5_prep",
    )(are, aim, ldt, btre, btim)


def _block_diag(w):
    nl, g, r, c = w.shape
    eye = jnp.eye(g, dtype=w.dtype)
    return jnp.einsum("lgrc,gh->lgrhc", w, eye).reshape(nl, g * r, g * c)


def _in_proj_kernel(x_ref, nw_ref, w_ref, v_ref, u_ref):
    xn = _rms(x_ref[...], nw_ref[...]).astype(BF16)
    proj = _dot(xn, w_ref[...])
    a = proj[:, :D_CONV]
    g = proj[:, D_CONV:2 * D_CONV]
    v_ref[...] = a * jax.nn.sigmoid(g)
    u_ref[...] = proj[:, 2 * D_CONV:]


def _in_proj(xt, norm_w, w_in):
    t = xt.shape[0]
    n_out = 2 * D_CONV + D_SSM
    return pl.pallas_call(
        _in_proj_kernel,
        grid=(t // TM_PROJ,),
        in_specs=[
            pl.BlockSpec((TM_PROJ, D_MODEL), lambda i: (i, 0)),
            pl.BlockSpec((1, D_MODEL), lambda i: (0, 0)),
            pl.BlockSpec((D_MODEL, n_out), lambda i: (0, 0)),
        ],
        out_specs=(
            pl.BlockSpec((TM_PROJ, D_CONV), lambda i: (i, 0)),
            pl.BlockSpec((TM_PROJ, D_SSM), lambda i: (i, 0)),
        ),
        out_shape=(
            jax.ShapeDtypeStruct((t, D_CONV), F32),
            jax.ShapeDtypeStruct((t, D_SSM), F32),
        ),
        compiler_params=pltpu.CompilerParams(
            dimension_semantics=("arbitrary",), vmem_limit_bytes=40 << 20),
        name="in_proj",
    )(xt, norm_w, w_in)


def _conv_kernel(v_ref, w_ref, b_ref, lnw_ref, lnb_ref, o_ref, ext_ref, sh_ref):
    tt = pl.program_id(1)
    rows = CONV_HALO + TM_CONV

    @pl.when(tt == 0)
    def _():
        ext_ref[pl.ds(0, CONV_HALO), :] = jnp.zeros((CONV_HALO, D_CONV), F32)

    @pl.when(tt > 0)
    def _():
        ext_ref[pl.ds(0, CONV_HALO), :] = ext_ref[pl.ds(TM_CONV, CONV_HALO), :]

    ext_ref[pl.ds(CONV_HALO, TM_CONV), :] = v_ref[...]
    for s in range(1, SUBLANES):
        sh_ref[s - 1, pl.ds(0, rows - SUBLANES), :] = ext_ref[pl.ds(s, rows - SUBLANES), :]
    bias = b_ref[...]
    lnw = lnw_ref[...]
    lnb = lnb_ref[...]
    tap0 = CONV_HALO - (CONV_WIDTH - 1)

    groups = CONV_CHUNK // SUBLANES

    def chunk(ci, carry):
        base = pl.multiple_of(ci * CONV_CHUNK, CONV_CHUNK)
        for lt in range(D_CONV // LANES):
            lanes = pl.ds(lt * LANES, LANES)
            acc = [jnp.broadcast_to(bias[:, lt * LANES:(lt + 1) * LANES], (SUBLANES, LANES))] * groups
            for s in range(SUBLANES):
                taps = [j for j in range(CONV_WIDTH) if (tap0 + j) % SUBLANES == s]
                src = ext_ref if s == 0 else sh_ref.at[s - 1]
                ngroups = (tap0 + taps[-1] - s) // SUBLANES + groups
                win = [src[pl.ds(base + SUBLANES * g, SUBLANES), lanes] for g in range(ngroups)]
                for j in taps:
                    g0 = (tap0 + j - s) // SUBLANES
                    wj = w_ref[j, :, lanes]
                    acc = [acc[r] + wj * win[g0 + r] for r in range(groups)]
            o_ref[pl.ds(base, CONV_CHUNK), lanes] = jnp.concatenate(acc, axis=0)
        return carry

    lax.fori_loop(0, TM_CONV // CONV_CHUNK, chunk, 0)

    for bi in range(TM_CONV // CONV_NORM_ROWS):
        rows_b = pl.ds(bi * CONV_NORM_ROWS, CONV_NORM_ROWS)
        acc = o_ref[rows_b, :]
        mu = jnp.mean(acc, axis=-1, keepdims=True)
        cen = acc - mu
        var = jnp.mean(cen * cen, axis=-1, keepdims=True)
        z = cen * lax.rsqrt(var + EPS) * lnw + lnb
        o_ref[rows_b, :] = z * jax.nn.sigmoid(z)


def _conv(v, conv_w, conv_b, ln_w, ln_b, bsz, seq):
    nt = seq // TM_CONV
    row = lambda b, t: (b * nt + t, 0)
    const = lambda b, t: (0, 0)
    return pl.pallas_call(
        _conv_kernel,
        grid=(bsz, nt),
        in_specs=[
            pl.BlockSpec((TM_CONV, D_CONV), row),
            pl.BlockSpec((CONV_WIDTH, SUBLANES, D_CONV), lambda b, t: (0, 0, 0)),
            pl.BlockSpec((1, D_CONV), const),
            pl.BlockSpec((1, D_CONV), const),
            pl.BlockSpec((1, D_CONV), const),
        ],
        out_specs=pl.BlockSpec((TM_CONV, D_CONV), row),
        out_shape=jax.ShapeDtypeStruct(v.shape, F32),
        scratch_shapes=[
            pltpu.VMEM((CONV_HALO + TM_CONV, D_CONV), F32),
            pltpu.VMEM((SUBLANES - 1, CONV_HALO + TM_CONV, D_CONV), F32),
        ],
        compiler_params=pltpu.CompilerParams(
            dimension_semantics=("arbitrary", "arbitrary")),
        name="conv",
    )(v, jnp.broadcast_to(conv_w[:, None, :], (CONV_WIDTH, SUBLANES, D_CONV)), conv_b, ln_w, ln_b)


def _cmul(a_re, a_im, b_re, b_im):
    return a_re * b_re - a_im * b_im, a_re * b_im + a_im * b_re


def _s5_kernel(u_ref, bcat_ref, ccat_ref, d_ref, wab_ref, a1_ref,
               pj_ref, aj_ref, o_ref, up_ref, bu_ref, st_ref, carry_ref):
    tt = pl.program_id(2)
    ns = S5_STATE
    steps_per_group = S5_GROUP_ROWS // SUBLANES

    @pl.when(tt == 0)
    def _():
        carry_ref[...] = jnp.zeros(carry_ref.shape, F32)

    for j in range(S5_STEPS):
        up_ref[pl.ds(SUBLANES * j, SUBLANES), :] = u_ref[pl.ds(j, SUBLANES, stride=S5_STEPS), :]

    a_re = a1_ref[0]
    a_im = a1_ref[1]

    def step(j, s_re, s_im):
        rows = pl.ds(j * SUBLANES, SUBLANES)
        m_re, m_im = _cmul(a_re, a_im, s_re, s_im)
        return m_re + bu_ref[rows, pl.ds(0, ns)], m_im + bu_ref[rows, pl.ds(ns, ns)]

    s_re = jnp.zeros((SUBLANES, ns), F32)
    s_im = s_re
    for g in range(S5_TILE // S5_GROUP_ROWS):
        rows = pl.ds(g * S5_GROUP_ROWS, S5_GROUP_ROWS)
        bu_ref[rows, :] = _dot(up_ref[rows, :].astype(BF16), bcat_ref[0])
        for jj in range(steps_per_group):
            s_re, s_im = step(g * steps_per_group + jj, s_re, s_im)
    e_re, e_im = s_re, s_im

    row = lax.broadcasted_iota(jnp.int32, (SUBLANES, ns), 0)
    first = row == 0
    c_re = jnp.where(first, pltpu.roll(carry_ref[0], 1, 0), pltpu.roll(e_re, 1, 0))
    c_im = jnp.where(first, pltpu.roll(carry_ref[1], 1, 0), pltpu.roll(e_im, 1, 0))
    for i, d in enumerate((1, 2, 4)):
        r_re = pltpu.roll(c_re, d, 0)
        r_im = pltpu.roll(c_im, d, 0)
        m_re, m_im = _cmul(pj_ref[0, i], pj_ref[1, i], r_re, r_im)
        c_re = c_re + m_re
        c_im = c_im + m_im
    f_re, f_im = _cmul(aj_ref[0], aj_ref[1], c_re, c_im)
    carry_ref[0] = f_re + e_re
    carry_ref[1] = f_im + e_im

    s_re, s_im = c_re, c_im
    pack = 2 * SUBLANES
    for g in range(S5_TILE // S5_GROUP_ROWS):
        rows = pl.ds(g * S5_GROUP_ROWS, S5_GROUP_ROWS)
        for jj in range(0, steps_per_group, 2):
            j = g * steps_per_group + jj
            p_re, p_im = step(j, s_re, s_im)
            s_re, s_im = step(j + 1, p_re, p_im)
            st_ref[pl.ds(j * SUBLANES, pack), pl.ds(0, ns)] = (
                jnp.concatenate([p_re, s_re], axis=0).astype(BF16))
            st_ref[pl.ds(j * SUBLANES, pack), pl.ds(ns, ns)] = (
                jnp.concatenate([p_im, s_im], axis=0).astype(BF16))
        y = _dot(st_ref[rows, :], ccat_ref[0]) + d_ref[...] * up_ref[rows, :]
        y = jax.nn.gelu(y)
        ab = _dot(y.astype(BF16), wab_ref[0])
        out = ab[:, :S5_LANES] * jax.nn.sigmoid(ab[:, S5_LANES:])
        for jj in range(steps_per_group):
            j = g * steps_per_group + jj
            o_ref[pl.ds(j, SUBLANES, stride=S5_STEPS), :] = out[jj * SUBLANES:(jj + 1) * SUBLANES, :]


def _s5(u, bcat, ccat, d, wab, a1, pj, aj, bsz, seq):
    nt = seq // S5_TILE
    nlb = D_SSM // S5_LANES
    ns = S5_STATE
    row = lambda b, l, t: (b * nt + t, l)
    lane3 = lambda b, l, t: (0, 0, l)
    lane4 = lambda b, l, t: (0, 0, 0, l)
    return pl.pallas_call(
        _s5_kernel,
        grid=(bsz, nlb, nt),
        in_specs=[
            pl.BlockSpec((S5_TILE, S5_LANES), row),
            pl.BlockSpec((1, S5_LANES, 2 * ns), lambda b, l, t: (l, 0, 0)),
            pl.BlockSpec((1, 2 * ns, S5_LANES), lambda b, l, t: (l, 0, 0)),
            pl.BlockSpec((1, S5_LANES), lambda b, l, t: (0, l)),
            pl.BlockSpec((1, S5_LANES, 2 * S5_LANES), lambda b, l, t: (l, 0, 0)),
            pl.BlockSpec((2, SUBLANES, ns), lane3),
            pl.BlockSpec((2, 3, SUBLANES, ns), lane4),
            pl.BlockSpec((2, SUBLANES, ns), lane3),
        ],
        out_specs=pl.BlockSpec((S5_TILE, S5_LANES), row),
        out_shape=jax.ShapeDtypeStruct(u.shape, F32),
        scratch_shapes=[
            pltpu.VMEM((S5_TILE, S5_LANES), F32),
            pltpu.VMEM((S5_TILE, 2 * ns), F32),
            pltpu.VMEM((S5_TILE, 2 * ns), BF16),
            pltpu.VMEM((2, SUBLANES, ns), F32),
        ],
        compiler_params=pltpu.CompilerParams(
            dimension_semantics=("arbitrary", "arbitrary", "arbitrary"),
            vmem_limit_bytes=40 << 20),
        name="s5",
    )(u, bcat, ccat, d, wab, a1, pj, aj)


def _kv_kernel(m_ref, nw_ref, wk_ref, wv_ref, k_ref, v_ref):
    mn = _rms(m_ref[0], nw_ref[...]).astype(BF16)
    k_ref[0] = _dot(mn, wk_ref[...]).astype(BF16)
    v_ref[0] = _dot(mn, wv_ref[...]).astype(BF16)


def _kv(mem, norm_w, wk, wv):
    bsz, mlen, d = mem.shape
    blk = pl.BlockSpec((1, mlen, d), lambda b: (b, 0, 0))
    wspec = pl.BlockSpec((d, d), lambda b: (0, 0))
    return pl.pallas_call(
        _kv_kernel,
        grid=(bsz,),
        in_specs=[blk, pl.BlockSpec((1, d), lambda b: (0, 0)), wspec, wspec],
        out_specs=(blk, blk),
        out_shape=(jax.ShapeDtypeStruct(mem.shape, BF16),) * 2,
        compiler_params=pltpu.CompilerParams(
            dimension_semantics=("arbitrary",), vmem_limit_bytes=40 << 20),
        name="kv",
    )(mem, norm_w, wk, wv)


def _mix_attn_kernel(x_ref, yc_ref, ys_ref, wot_ref, wob_ref, nx_ref, wq_ref,
                     k_ref, v_ref, wo_ref, nf_ref, wr_ref, br_ref,
                     h_ref, xs_ref, y0_ref, gs_ref, ri_ref, cnt_ref):
    tm = x_ref.shape[0]
    neg = -jnp.inf
    big = float(ROUTE_LANES)

    groups = [pl.ds(g * (tm // MIX_SPLIT), tm // MIX_SPLIT) for g in range(MIX_SPLIT)]

    def project(rows):
        h1 = (x_ref[rows, :] + _dot(yc_ref[rows, :].astype(BF16), wot_ref[...])
              + _dot(ys_ref[rows, :].astype(BF16), wob_ref[...]))
        hn = _rms(h1, nx_ref[...]).astype(BF16)
        return h1, _dot(hn, wq_ref[...])

    def attend(q):
        heads = []
        for hd in range(N_XHEADS):
            sl = slice(hd * XHEAD_DIM, (hd + 1) * XHEAD_DIM)
            qh = q[:, sl].astype(BF16)
            s = lax.dot_general(qh, k_ref[0, :, sl], (((1,), (1,)), ((), ())),
                                preferred_element_type=F32) * (XHEAD_DIM ** -0.5)
            s = s - jnp.max(s, axis=-1, keepdims=True)
            p = jnp.exp(s)
            p = p / jnp.sum(p, axis=-1, keepdims=True)
            heads.append(_dot(p.astype(BF16), v_ref[0, :, sl]).astype(BF16))
        return jnp.concatenate(heads, axis=-1)

    def route(rows, h1, o):
        rt = rows.size
        h2 = h1 + _dot(o, wo_ref[...])
        h_ref[rows, :] = h2
        hf = _rms(h2, nf_ref[...]).astype(BF16)
        logits = _dot(hf, wr_ref[...]) + br_ref[...]
        lane_i = lax.broadcasted_iota(jnp.int32, (rt, ROUTE_LANES), 1)
        lane_r = lane_i.astype(F32)

        def top1(vals):
            m = jnp.max(vals, axis=-1, keepdims=True)
            idx = jnp.min(jnp.where(vals == m, lane_r, big), axis=-1, keepdims=True)
            return m, idx

        gl = jnp.where(lane_i < N_EXPERT_GROUPS, logits, neg)
        gmax, gidx = top1(gl)
        g_w = 1.0 / jnp.sum(jnp.exp(gl - gmax), axis=-1, keepdims=True)
        lo = EXPERT_LANE0 + EXPERTS_PER_GROUP * gidx
        el = jnp.where((lane_r >= lo) & (lane_r < lo + EXPERTS_PER_GROUP), logits, neg)
        m1, i1 = top1(el)
        m2, i2 = top1(jnp.where(lane_r == i1, neg, el))
        e21 = jnp.exp(m2 - m1)
        return hf, i1, i2, g_w / (1.0 + e21), g_w * e21 / (1.0 + e21)

    projected = [project(rows) for rows in groups]
    attended = [attend(q) for _, q in projected]
    parts = [route(rows, h1, o) for rows, (h1, _), o in zip(groups, projected, attended)]
    hf, i1, i2, gate1, gate2 = (jnp.concatenate(col, axis=0) for col in zip(*parts))
    lane = lax.broadcasted_iota(jnp.int32, (tm, ROUTE_LANES), 1)
    lane_f = lane.astype(F32)

    hot1 = lane_f == i1
    hot2 = lane_f == i2
    hot = jnp.where(hot1 | hot2, 1.0, 0.0)
    r_i = lax.broadcasted_iota(jnp.int32, (tm, tm), 0)
    c_i = lax.broadcasted_iota(jnp.int32, (tm, tm), 1)
    tri = jnp.where(r_i > c_i, 1.0, 0.0).astype(BF16)
    before = _dot(tri, hot.astype(BF16))
    count = jnp.sum(hot, axis=0, keepdims=True)
    chunks = jnp.floor((count + (MOE_CHUNK - 1.0)) * (1.0 / MOE_CHUNK))
    l_i = lax.broadcasted_iota(jnp.int32, (ROUTE_LANES, ROUTE_LANES), 0)
    l_j = lax.broadcasted_iota(jnp.int32, (ROUTE_LANES, ROUTE_LANES), 1)
    upper = jnp.where(l_i < l_j, 1.0, 0.0).astype(BF16)
    first_chunk = _dot(jnp.broadcast_to(chunks, (SUBLANES, ROUTE_LANES)).astype(BF16), upper)[0:1]
    start = first_chunk * float(MOE_CHUNK) + before
    pos1 = jnp.sum(jnp.where(hot1, start, 0.0), axis=-1, keepdims=True)
    pos2 = jnp.sum(jnp.where(hot2, start, 0.0), axis=-1, keepdims=True)
    cnt_ref[0] = jnp.broadcast_to(count, (SUBLANES, ROUTE_LANES)).astype(jnp.int32)
    ri_ref[...] = jnp.where(lane == 0, pos1, jnp.where(lane == 1, pos2, 0.0)).astype(jnp.int32)

    rho = lax.broadcasted_iota(jnp.int32, (tm, TILE_ROWS), 1).astype(F32)
    is1 = rho == pos1
    is2 = rho == pos2
    pt = jnp.where(is1 | is2, 1.0, 0.0).astype(BF16)
    contract0 = (((0,), (0,)), ((), ()))
    srt = lax.dot_general(pt, hf, contract0, preferred_element_type=F32)
    xs_ref[...] = _pack_words(srt).reshape(TILE_CHUNKS, MOE_CHUNK, D_PACK)
    y0_ref[...] = jnp.zeros(y0_ref.shape, jnp.uint32)

    def pieces(g):
        hi = g.astype(BF16).astype(F32)
        mid = (g - hi).astype(BF16).astype(F32)
        low = (g - hi - mid).astype(BF16).astype(F32)
        return hi, mid, low

    p1 = pieces(gate1)
    p2 = pieces(gate2)
    g6 = jnp.where(lane == 6, 1.0, 0.0)
    for li, piece in enumerate(p1 + p2):
        g6 = jnp.where(lane == li, piece, g6)
    ptk = jnp.where(is1, 1.0, jnp.where(is2, 2.0, 0.0)).astype(BF16)
    sg = lax.dot_general(ptk, g6.astype(BF16), contract0, preferred_element_type=F32)
    which = sg[:, 6:7]
    first = sg[:, 0:1] + sg[:, 1:2] + sg[:, 2:3]
    second = 0.5 * (sg[:, 3:4] + sg[:, 4:5] + sg[:, 5:6])
    gsort = jnp.where(which == 1.0, first, jnp.where(which == 2.0, second, 0.0))
    gs_ref[...] = jnp.broadcast_to(gsort, (TILE_ROWS, ROUTE_LANES))


def _pack_words(v):
    hi = lax.bitcast_convert_type(v[:, :D_PACK], jnp.uint32) & jnp.uint32(HI_MASK)
    lo = lax.bitcast_convert_type(v[:, D_PACK:], jnp.uint32) >> 16
    return hi | lo


def _unpack_words(w):
    hi = lax.bitcast_convert_type(w & jnp.uint32(HI_MASK), F32)
    lo = lax.bitcast_convert_type(w << 16, F32)
    return hi, lo


def _mix_attn(xt, yc, ys, wot, wob, nx, wq, k, v, wo, nf, wr, br, seq):
    t, d = xt.shape
    nt = t // TM_MIX
    tiles_per_batch = seq // TM_MIX
    mlen = k.shape[1]
    row = lambda i: (i, 0)
    const = lambda i: (0, 0)
    tile3 = lambda i: (i, 0, 0)
    kvspec = pl.BlockSpec((1, mlen, d), lambda i: (i // tiles_per_batch, 0, 0))
    packed = jax.ShapeDtypeStruct((nt * TILE_CHUNKS, MOE_CHUNK, D_PACK), jnp.uint32)
    return pl.pallas_call(
        _mix_attn_kernel,
        grid=(nt,),
        in_specs=[
            pl.BlockSpec((TM_MIX, d), row),
            pl.BlockSpec((TM_MIX, D_CONV), row),
            pl.BlockSpec((TM_MIX, D_SSM), row),
            pl.BlockSpec((D_CONV, d), const),
            pl.BlockSpec((D_SSM, d), const),
            pl.BlockSpec((1, d), const),
            pl.BlockSpec((d, d), const),
            kvspec, kvspec,
            pl.BlockSpec((d, d), const),
            pl.BlockSpec((1, d), const),
            pl.BlockSpec((d, ROUTE_LANES), const),
            pl.BlockSpec((1, ROUTE_LANES), const),
        ],
        out_specs=(
            pl.BlockSpec((TM_MIX, d), row),
            pl.BlockSpec((TILE_CHUNKS, MOE_CHUNK, D_PACK), tile3),
            pl.BlockSpec((TILE_CHUNKS, MOE_CHUNK, D_PACK), tile3),
            pl.BlockSpec((TILE_ROWS, ROUTE_LANES), row),
            pl.BlockSpec((TM_MIX, ROUTE_LANES), row),
            pl.BlockSpec((1, SUBLANES, ROUTE_LANES), tile3),
        ),
        out_shape=(
            jax.ShapeDtypeStruct((t, d), F32),
            packed,
            packed,
            jax.ShapeDtypeStruct((nt * TILE_ROWS, ROUTE_LANES), F32),
            jax.ShapeDtypeStruct((t, ROUTE_LANES), jnp.int32),
            jax.ShapeDtypeStruct((nt, SUBLANES, ROUTE_LANES), jnp.int32),
        ),
        compiler_params=pltpu.CompilerParams(
            dimension_semantics=("arbitrary",), vmem_limit_bytes=60 << 20),
        name="mix_attn",
    )(xt, yc, ys, wot, wob, nx, wq, k, v, wo, nf, wr, br)


ZERO_CHUNK = TILE_CHUNKS - 1
DUMP_TILE0 = 16
SPARE_PER_TILE = TILE_CHUNKS - TILE_USED_CHUNKS


def _dump_chunk(slot, k):
    idx = slot * BLOCK_CHUNKS + k
    return (DUMP_TILE0 + idx // SPARE_PER_TILE) * TILE_CHUNKS + TILE_USED_CHUNKS + idx % SPARE_PER_TILE


def _experts_kernel(src_ref, dst_ref, be_ref, na_ref, xs_hbm, w1_ref, w3_ref, w2_ref,
                    y0_hbm, ys_hbm, xbuf, ybuf, w1s_ref, w3s_ref, w2s_ref, in_sem, out_sem):
    del y0_hbm
    b = pl.program_id(0)
    n_active = na_ref[0]
    slot = b % 2
    other = 1 - slot

    def gather(blk, sl):
        for k in range(BLOCK_CHUNKS):
            pltpu.make_async_copy(xs_hbm.at[src_ref[blk * BLOCK_CHUNKS + k]],
                                  xbuf.at[sl, k], in_sem.at[sl]).start()

    def gather_wait(sl):
        pltpu.make_async_copy(xs_hbm.at[pl.ds(0, BLOCK_CHUNKS)], xbuf.at[sl],
                              in_sem.at[sl]).wait()

    def scatter(blk, sl):
        for k in range(BLOCK_CHUNKS):
            d = dst_ref[blk * BLOCK_CHUNKS + k]
            dump = jnp.where(sl == 0, _dump_chunk(0, k), _dump_chunk(1, k))
            pltpu.make_async_copy(ybuf.at[sl, k], ys_hbm.at[jnp.where(d < 0, dump, d)],
                                  out_sem.at[sl]).start()

    def scatter_wait(sl):
        pltpu.make_async_copy(ybuf.at[sl], ys_hbm.at[pl.ds(0, BLOCK_CHUNKS)],
                              out_sem.at[sl]).wait()

    active = b < n_active

    @pl.when(b == 0)
    def _():
        gather(0, 0)
        ybuf[...] = jnp.zeros(ybuf.shape, jnp.uint32)

    prev = be_ref[jnp.maximum(b - 1, 0)]
    fresh = (b == 0) | (be_ref[b] != prev)

    @pl.when(active & fresh)
    def _():
        w1s_ref[...] = w1_ref[0].astype(BF16)
        w3s_ref[...] = w3_ref[0].astype(BF16)
        w2s_ref[...] = w2_ref[0].astype(BF16)

    @pl.when(active & (b >= 1))
    def _():
        scatter_wait(slot)

    @pl.when(active)
    def _():
        gather_wait(slot)
        gather(b + 1, other)
        scatter(b, other)

        hi, lo = _unpack_words(xbuf[slot].reshape(MOE_BLOCK, D_PACK))
        xb = jnp.concatenate([hi, lo], axis=1).astype(BF16)
        h1 = _dot(xb, w1s_ref[...])
        h3 = _dot(xb, w3s_ref[...])
        hid = (h1 * jax.nn.sigmoid(h1) * h3).astype(BF16)
        y = _dot(hid, w2s_ref[...]).astype(BF16).astype(F32)
        ybuf[slot] = _pack_words(y).reshape(BLOCK_CHUNKS, MOE_CHUNK, D_PACK)

        @pl.when(b == n_active - 1)
        def _():
            scatter(b + 1, slot)
            gather_wait(other)
            scatter_wait(other)
            scatter_wait(slot)


def _experts(src, dst, block_expert, n_active, xs, w1, w3, w2, y0):
    nb = block_expert.shape[0]
    d = D_MODEL

    def wsel(b, src, dst, be, na):
        return (be[jnp.minimum(b, na[0] - 1)], 0, 0)

    grid_spec = pltpu.PrefetchScalarGridSpec(
        num_scalar_prefetch=4,
        grid=(nb,),
        in_specs=[
            pl.BlockSpec(memory_space=pl.ANY),
            pl.BlockSpec((1, d, D_EXPERT), wsel),
            pl.BlockSpec((1, d, D_EXPERT), wsel),
            pl.BlockSpec((1, D_EXPERT, d), wsel),
            pl.BlockSpec(memory_space=pl.ANY),
        ],
        out_specs=pl.BlockSpec(memory_space=pl.ANY),
        scratch_shapes=[
            pltpu.VMEM((2, BLOCK_CHUNKS, MOE_CHUNK, D_PACK), jnp.uint32),
            pltpu.VMEM((2, BLOCK_CHUNKS, MOE_CHUNK, D_PACK), jnp.uint32),
            pltpu.VMEM((d, D_EXPERT), BF16),
            pltpu.VMEM((d, D_EXPERT), BF16),
            pltpu.VMEM((D_EXPERT, d), BF16),
            pltpu.SemaphoreType.DMA((2,)),
            pltpu.SemaphoreType.DMA((2,)),
        ],
    )
    return pl.pallas_call(
        _experts_kernel,
        grid_spec=grid_spec,
        out_shape=jax.ShapeDtypeStruct(y0.shape, jnp.uint32),
        input_output_aliases={8: 0},
        compiler_params=pltpu.CompilerParams(
            dimension_semantics=("arbitrary",), vmem_limit_bytes=48 << 20),
        name="experts",
    )(src, dst, block_expert, n_active, xs, w1, w3, w2, y0)


def _combine_kernel(h_ref, ri_ref, gs_ref, ys_ref, fw_ref, o_ref):
    tm = h_ref.shape[0]
    hi, lo = _unpack_words(ys_ref[...].reshape(TILE_ROWS, D_PACK))
    g = gs_ref[...]
    gw = jnp.concatenate([g] * (D_PACK // ROUTE_LANES), axis=1)
    yg = jnp.concatenate([hi * gw, lo * gw], axis=1).astype(BF16)
    pos = ri_ref[...].astype(F32)
    rho = lax.broadcasted_iota(jnp.int32, (tm, TILE_ROWS), 1).astype(F32)
    q = jnp.where((rho == pos[:, 0:1]) | (rho == pos[:, 1:2]), 1.0, 0.0).astype(BF16)
    o_ref[...] = _rms(h_ref[...] + _dot(q, yg), fw_ref[...])


def _combine(h2, route_i, gsort, ys, final_w):
    t, d = h2.shape
    nt = t // TM_MIX
    row = lambda i: (i, 0)
    return pl.pallas_call(
        _combine_kernel,
        grid=(nt,),
        in_specs=[
            pl.BlockSpec((TM_MIX, d), row),
            pl.BlockSpec((TM_MIX, ROUTE_LANES), row),
            pl.BlockSpec((TILE_ROWS, ROUTE_LANES), row),
            pl.BlockSpec((TILE_CHUNKS, MOE_CHUNK, D_PACK), lambda i: (i, 0, 0)),
            pl.BlockSpec((1, d), lambda i: (0, 0)),
        ],
        out_specs=pl.BlockSpec((TM_MIX, d), row),
        out_shape=jax.ShapeDtypeStruct((t, d), F32),
        compiler_params=pltpu.CompilerParams(
            dimension_semantics=("arbitrary",), vmem_limit_bytes=48 << 20),
        name="combine",
    )(h2, route_i, gsort, ys, final_w)


def _routing_tables(cnt):
    nt = cnt.shape[0]
    n = cnt[:, 0, EXPERT_LANE0:EXPERT_LANE0 + N_EXPERTS]
    c = (n + MOE_CHUNK - 1) // MOE_CHUNK
    local = jnp.cumsum(c, axis=1) - c
    per_expert = jnp.sum(c, axis=0)
    padded = (per_expert + BLOCK_CHUNKS - 1) // BLOCK_CHUNKS * BLOCK_CHUNKS
    gend = jnp.cumsum(padded)
    gstart = gend - padded
    within = jnp.cumsum(c, axis=0) - c
    start = (gstart[None, :] + within).T.reshape(-1)
    length = c.T.reshape(-1)
    base = (jnp.arange(nt, dtype=jnp.int32)[:, None] * TILE_CHUNKS + local).T.reshape(-1)
    g_max = (2 * nt * TM_MIX // MOE_CHUNK + nt * N_EXPERTS
             + N_EXPERTS * (BLOCK_CHUNKS - 1))
    n_blocks = -(-g_max // BLOCK_CHUNKS)
    g = jnp.arange(n_blocks * BLOCK_CHUNKS, dtype=jnp.int32)
    nxt = jnp.concatenate([start[1:], jnp.full((1,), 2 ** 30, start.dtype)])
    owner = ((start[None, :] <= g[:, None]) & (nxt[None, :] > g[:, None])).astype(jnp.int32)
    k = g - jnp.sum(owner * start[None, :], axis=1)
    valid = k < jnp.sum(owner * length[None, :], axis=1)
    chunk = jnp.sum(owner * base[None, :], axis=1) + k
    pad_src = jnp.full((BLOCK_CHUNKS,), ZERO_CHUNK, jnp.int32)
    pad_dst = jnp.full((BLOCK_CHUNKS,), -1, jnp.int32)
    src = jnp.concatenate([jnp.where(valid, chunk, ZERO_CHUNK).astype(jnp.int32), pad_src])
    dst = jnp.concatenate([pad_dst, jnp.where(valid, chunk, -1).astype(jnp.int32)])
    block_start = jnp.arange(n_blocks, dtype=jnp.int32) * BLOCK_CHUNKS
    block_expert = jnp.minimum(
        jnp.sum((block_start[:, None] >= gend[None, :]).astype(jnp.int32), axis=1),
        N_EXPERTS - 1).astype(jnp.int32)
    n_active = (gend[-1:] // BLOCK_CHUNKS).astype(jnp.int32)
    return src, dst, block_expert, n_active


def _layer(h, mem, p, final_w):
    bsz, seq, d = h.shape
    t = bsz * seq
    assert t // TM_MIX >= DUMP_TILE0 + 2 * BLOCK_CHUNKS // SPARE_PER_TILE
    xt = h.reshape(t, d)

    v, u = _in_proj(xt, p["norm_mix_w"].reshape(1, d), p["w_in"].astype(BF16))
    y_conv = _conv(v, p["conv_w"], p["conv_b"].reshape(1, D_CONV),
                   p["conv_ln_w"].reshape(1, D_CONV), p["conv_ln_b"].reshape(1, D_CONV),
                   bsz, seq)

    a1, pj, aj, bbar = _s5_prep(p["ssm_A_re"], p["ssm_A_im"], p["ssm_log_dt"],
                                      p["ssm_B_re"], p["ssm_B_im"])
    nlb = D_SSM // S5_LANES
    bb = bbar.reshape(2, SSM_GROUP, nlb, S5_GROUPS, SSM_STATE).transpose(0, 2, 3, 1, 4)
    bcat = jnp.concatenate([_block_diag(bb[0]), _block_diag(bb[1])], axis=-1).astype(BF16)
    c_re = p["ssm_C_re"].reshape(nlb, S5_GROUPS, SSM_GROUP, SSM_STATE).transpose(0, 1, 3, 2)
    c_im = p["ssm_C_im"].reshape(nlb, S5_GROUPS, SSM_GROUP, SSM_STATE).transpose(0, 1, 3, 2)
    ccat = jnp.concatenate([_block_diag(c_re), -_block_diag(c_im)], axis=1).astype(BF16)
    glu = p["ssm_glu_w"].reshape(nlb, S5_GROUPS, SSM_GROUP, 2 * SSM_GROUP)
    wab = jnp.concatenate([_block_diag(glu[..., :SSM_GROUP]),
                           _block_diag(glu[..., SSM_GROUP:])], axis=-1).astype(BF16)
    y_ssm = _s5(u, bcat, ccat, p["ssm_D"].reshape(1, D_SSM), wab,
                a1, pj, aj, bsz, seq)

    k, vv = _kv(mem, p["norm_mem_w"].reshape(1, d), p["xk_w"].astype(BF16),
                p["xv_w"].astype(BF16))
    w_out = p["w_out"].astype(BF16)
    wr = jnp.concatenate([p["router_group_w"], p["router_expert_w"]], axis=1)
    wr = jnp.pad(wr, ((0, 0), (0, ROUTE_LANES - wr.shape[1]))).astype(BF16)
    br = jnp.concatenate([p["router_group_b"], p["router_expert_b"].reshape(-1)])
    br = jnp.pad(br, (0, ROUTE_LANES - br.shape[0])).reshape(1, ROUTE_LANES)
    h2, xs, y0, gsort, route_i, cnt = _mix_attn(
        xt, y_conv, y_ssm, w_out[:D_CONV], w_out[D_CONV:], p["norm_x_w"].reshape(1, d),
        p["xq_w"].astype(BF16), k, vv, p["xo_w"].astype(BF16),
        p["norm_ffn_w"].reshape(1, d), wr, br, seq)

    src, dst, block_expert, n_active = _routing_tables(cnt)
    ys = _experts(src, dst, block_expert, n_active, xs,
                  p["moe_w1"], p["moe_w3"], p["moe_w2"], y0)
    out = _combine(h2, route_i, gsort, ys, final_w.reshape(1, d))
    return out.reshape(bsz, seq, d)


def kernel(x, mem, norm_mix_w, w_in, conv_w, conv_b, conv_ln_w, conv_ln_b, ssm_A_re, ssm_A_im, ssm_log_dt, ssm_B_re, ssm_B_im, ssm_C_re, ssm_C_im, ssm_D, ssm_glu_w, w_out, norm_x_w, norm_mem_w, xq_w, xk_w, xv_w, xo_w, norm_ffn_w, router_group_w, router_group_b, router_expert_w, router_expert_b, moe_w1, moe_w3, moe_w2, final_norm_w):
    stacked = dict(
        norm_mix_w=norm_mix_w, w_in=w_in, conv_w=conv_w, conv_b=conv_b,
        conv_ln_w=conv_ln_w, conv_ln_b=conv_ln_b, ssm_A_re=ssm_A_re, ssm_A_im=ssm_A_im,
        ssm_log_dt=ssm_log_dt, ssm_B_re=ssm_B_re, ssm_B_im=ssm_B_im, ssm_C_re=ssm_C_re,
        ssm_C_im=ssm_C_im, ssm_D=ssm_D, ssm_glu_w=ssm_glu_w, w_out=w_out,
        norm_x_w=norm_x_w, norm_mem_w=norm_mem_w, xq_w=xq_w, xk_w=xk_w, xv_w=xv_w,
        xo_w=xo_w, norm_ffn_w=norm_ffn_w, router_group_w=router_group_w,
        router_group_b=router_group_b, router_expert_w=router_expert_w,
        router_expert_b=router_expert_b, moe_w1=moe_w1, moe_w3=moe_w3, moe_w2=moe_w2)
    depth = norm_mix_w.shape[0]
    assert depth == 1, "final norm is fused into the single layer's combine step"
    layer = {name: w[0] for name, w in stacked.items()}
    return _layer(x, mem, layer, final_norm_w)
```

```python
import functools

import jax
import jax.numpy as jnp
from jax import lax
from jax.experimental import pallas as pl
from jax.experimental.pallas import tpu as pltpu

D_MODEL = 1024
D_CONV = 512
CONV_WIDTH = 31
D_SSM = 512
SSM_GROUP = 16
N_SSM_GROUPS = 32
SSM_STATE = 64
N_XHEADS = 4
XHEAD_DIM = 256
N_EXPERT_GROUPS = 4
EXPERTS_PER_GROUP = 8
N_EXPERTS = 32
D_EXPERT = 512
EPS = 1e-6

F32 = jnp.float32
BF16 = jnp.bfloat16

SUBLANES = 8
LANES = 128

TM_PROJ = 512
TM_CONV = 512
CONV_CHUNK = 128
CONV_NORM_ROWS = 128
CONV_HALO = 32
S5_STEPS = 64
S5_TILE = SUBLANES * S5_STEPS
S5_GROUP_ROWS = 128
S5_LANES = 128
S5_GROUPS = S5_LANES // SSM_GROUP
S5_STATE = S5_GROUPS * SSM_STATE
TM_MIX = 512
MIX_SPLIT = 1
ROUTE_LANES = 128
EXPERT_LANE0 = N_EXPERT_GROUPS
MOE_CHUNK = SUBLANES
MOE_BLOCK = 256
BLOCK_CHUNKS = MOE_BLOCK // MOE_CHUNK
TILE_USED_CHUNKS = 2 * TM_MIX // MOE_CHUNK + N_EXPERTS * (MOE_CHUNK - 1) // MOE_CHUNK
TILE_CHUNKS = 160
TILE_ROWS = TILE_CHUNKS * MOE_CHUNK
D_PACK = D_MODEL // 2
HI_MASK = 0xFFFF0000


def _rms(x, w):
    return x * lax.rsqrt(jnp.mean(x * x, axis=-1, keepdims=True) + EPS) * w


def _dot(a, b):
    return jnp.dot(a, b, preferred_element_type=F32)


def _s5_prep_kernel(are_ref, aim_ref, ldt_ref, btre_ref, btim_ref,
                    a1_ref, pj_ref, aj_ref, bbar_ref):
    lam_re = are_ref[...]
    lam_im = aim_ref[...]
    dt = jnp.exp(ldt_ref[...])
    x = lam_re * dt
    y = lam_im * dt

    def power(k):
        mag = jnp.exp(k * x)
        return mag * jnp.cos(k * y), mag * jnp.sin(k * y)

    n = x.shape[-1]
    ones8 = jnp.ones((SUBLANES, n), F32)
    a_re, a_im = power(ones8)
    a1_ref[0] = a_re
    a1_ref[1] = a_im

    row = lax.broadcasted_iota(jnp.int32, (SUBLANES, n), 0)
    for i, d in enumerate((1, 2, 4)):
        p_re, p_im = power(ones8 * float(d * S5_STEPS))
        keep = row >= d
        pj_ref[0, i] = jnp.where(keep, p_re, 0.0)
        pj_ref[1, i] = jnp.where(keep, p_im, 0.0)
    j_re, j_im = power(ones8 * float(S5_STEPS))
    aj_ref[0] = j_re
    aj_ref[1] = j_im

    num_re = a_re[0:1] - 1.0
    num_im = a_im[0:1]
    den = lam_re * lam_re + lam_im * lam_im
    c_re = (num_re * lam_re + num_im * lam_im) / den
    c_im = (num_im * lam_re - num_re * lam_im) / den
    b_re = btre_ref[...]
    b_im = btim_ref[...]
    bbar_ref[0] = c_re * b_re - c_im * b_im
    bbar_ref[1] = c_re * b_im + c_im * b_re


def _s5_prep(a_re, a_im, log_dt, b_re, b_im):
    n = N_SSM_GROUPS * SSM_STATE
    are = a_re.reshape(1, n)
    aim = a_im.reshape(1, n)
    ldt = jnp.repeat(log_dt, SSM_STATE).reshape(1, n)
    btre = jnp.transpose(b_re, (2, 0, 1)).reshape(SSM_GROUP, n)
    btim = jnp.transpose(b_im, (2, 0, 1)).reshape(SSM_GROUP, n)
    return pl.pallas_call(
        _s5_prep_kernel,
        out_shape=(
            jax.ShapeDtypeStruct((2, SUBLANES, n), F32),
            jax.ShapeDtypeStruct((2, 3, SUBLANES, n), F32),
            jax.ShapeDtypeStruct((2, SUBLANES, n), F32),
            jax.ShapeDtypeStruct((2, SSM_GROUP, n), F32),
        ),
        name="s5_prep",
    )(are, aim, ldt, btre, btim)


def _block_diag(w):
    nl, g, r, c = w.shape
    eye = jnp.eye(g, dtype=w.dtype)
    return jnp.einsum("lgrc,gh->lgrhc", w, eye).reshape(nl, g * r, g * c)


def _in_proj_kernel(x_ref, nw_ref, w_ref, v_ref, u_ref):
    xn = _rms(x_ref[...], nw_ref[...]).astype(BF16)
    proj = _dot(xn, w_ref[...])
    a = proj[:, :D_CONV]
    g = proj[:, D_CONV:2 * D_CONV]
    v_ref[...] = a * jax.nn.sigmoid(g)
    u_ref[...] = proj[:, 2 * D_CONV:]


def _in_proj(xt, norm_w, w_in):
    t = xt.shape[0]
    n_out = 2 * D_CONV + D_SSM
    return pl.pallas_call(
        _in_proj_kernel,
        grid=(t // TM_PROJ,),
        in_specs=[
            pl.BlockSpec((TM_PROJ, D_MODEL), lambda i: (i, 0)),
            pl.BlockSpec((1, D_MODEL), lambda i: (0, 0)),
            pl.BlockSpec((D_MODEL, n_out), lambda i: (0, 0)),
        ],
        out_specs=(
            pl.BlockSpec((TM_PROJ, D_CONV), lambda i: (i, 0)),
            pl.BlockSpec((TM_PROJ, D_SSM), lambda i: (i, 0)),
        ),
        out_shape=(
            jax.ShapeDtypeStruct((t, D_CONV), F32),
            jax.ShapeDtypeStruct((t, D_SSM), F32),
        ),
        compiler_params=pltpu.CompilerParams(
            dimension_semantics=("arbitrary",), vmem_limit_bytes=40 << 20),
        name="in_proj",
    )(xt, norm_w, w_in)


def _conv_kernel(v_ref, w_ref, b_ref, lnw_ref, lnb_ref, o_ref, ext_ref, sh_ref):
    tt = pl.program_id(1)
    rows = CONV_HALO + TM_CONV

    @pl.when(tt == 0)
    def _():
        ext_ref[pl.ds(0, CONV_HALO), :] = jnp.zeros((CONV_HALO, D_CONV), F32)

    @pl.when(tt > 0)
    def _():
        ext_ref[pl.ds(0, CONV_HALO), :] = ext_ref[pl.ds(TM_CONV, CONV_HALO), :]

    ext_ref[pl.ds(CONV_HALO, TM_CONV), :] = v_ref[...]
    for s in range(1, SUBLANES):
        sh_ref[s - 1, pl.ds(0, rows - SUBLANES), :] = ext_ref[pl.ds(s, rows - SUBLANES), :]
    bias = b_ref[...]
    lnw = lnw_ref[...]
    lnb = lnb_ref[...]
    tap0 = CONV_HALO - (CONV_WIDTH - 1)

    groups = CONV_CHUNK // SUBLANES

    def chunk(ci, carry):
        base = pl.multiple_of(ci * CONV_CHUNK, CONV_CHUNK)
        for lt in range(D_CONV // LANES):
            lanes = pl.ds(lt * LANES, LANES)
            acc = [jnp.broadcast_to(bias[:, lt * LANES:(lt + 1) * LANES], (SUBLANES, LANES))] * groups
            for s in range(SUBLANES):
                taps = [j for j in range(CONV_WIDTH) if (tap0 + j) % SUBLANES == s]
                src = ext_ref if s == 0 else sh_ref.at[s - 1]
                ngroups = (tap0 + taps[-1] - s) // SUBLANES + groups
                win = [src[pl.ds(base + SUBLANES * g, SUBLANES), lanes] for g in range(ngroups)]
                for j in taps:
                    g0 = (tap0 + j - s) // SUBLANES
                    wj = w_ref[j, :, lanes]
                    acc = [acc[r] + wj * win[g0 + r] for r in range(groups)]
            o_ref[pl.ds(base, CONV_CHUNK), lanes] = jnp.concatenate(acc, axis=0)
        return carry

    lax.fori_loop(0, TM_CONV // CONV_CHUNK, chunk, 0)

    for bi in range(TM_CONV // CONV_NORM_ROWS):
        rows_b = pl.ds(bi * CONV_NORM_ROWS, CONV_NORM_ROWS)
        acc = o_ref[rows_b, :]
        mu = jnp.mean(acc, axis=-1, keepdims=True)
        cen = acc - mu
        var = jnp.mean(cen * cen, axis=-1, keepdims=True)
        z = cen * lax.rsqrt(var + EPS) * lnw + lnb
        o_ref[rows_b, :] = z * jax.nn.sigmoid(z)


def _conv(v, conv_w, conv_b, ln_w, ln_b, bsz, seq):
    nt = seq // TM_CONV
    row = lambda b, t: (b * nt + t, 0)
    const = lambda b, t: (0, 0)
    return pl.pallas_call(
        _conv_kernel,
        grid=(bsz, nt),
        in_specs=[
            pl.BlockSpec((TM_CONV, D_CONV), row),
            pl.BlockSpec((CONV_WIDTH, SUBLANES, D_CONV), lambda b, t: (0, 0, 0)),
            pl.BlockSpec((1, D_CONV), const),
            pl.BlockSpec((1, D_CONV), const),
            pl.BlockSpec((1, D_CONV), const),
        ],
        out_specs=pl.BlockSpec((TM_CONV, D_CONV), row),
        out_shape=jax.ShapeDtypeStruct(v.shape, F32),
        scratch_shapes=[
            pltpu.VMEM((CONV_HALO + TM_CONV, D_CONV), F32),
            pltpu.VMEM((SUBLANES - 1, CONV_HALO + TM_CONV, D_CONV), F32),
        ],
        compiler_params=pltpu.CompilerParams(
            dimension_semantics=("arbitrary", "arbitrary")),
        name="conv",
    )(v, jnp.broadcast_to(conv_w[:, None, :], (CONV_WIDTH, SUBLANES, D_CONV)), conv_b, ln_w, ln_b)


def _cmul(a_re, a_im, b_re, b_im):
    return a_re * b_re - a_im * b_im, a_re * b_im + a_im * b_re


def _s5_kernel(u_ref, bcat_ref, ccat_ref, d_ref, wab_ref, a1_ref,
               pj_ref, aj_ref, o_ref, up_ref, bu_ref, st_ref, carry_ref):
    tt = pl.program_id(1)
    ns = S5_STATE
    nseq = u_ref.shape[0]
    steps_per_group = S5_GROUP_ROWS // SUBLANES
    n_groups = S5_TILE // S5_GROUP_ROWS
    cur = tt % 2
    prv = 1 - cur

    @pl.when((pl.program_id(0) == 0) & (tt == 0))
    def _():
        up_ref[...] = jnp.zeros(up_ref.shape, F32)
        st_ref[...] = jnp.zeros(st_ref.shape, BF16)

    @pl.when(tt == 0)
    def _():
        carry_ref[...] = jnp.zeros(carry_ref.shape, F32)

    for q in range(nseq):
        for j in range(S5_STEPS):
            up_ref[cur, q, pl.ds(SUBLANES * j, SUBLANES), :] = (
                u_ref[q, pl.ds(j, SUBLANES, stride=S5_STEPS), :])

    a_re = a1_ref[0]
    a_im = a1_ref[1]

    def step(q, j, s):
        rows = pl.ds(j * SUBLANES, SUBLANES)
        m_re, m_im = _cmul(a_re, a_im, s[0], s[1])
        return m_re + bu_ref[q, rows, pl.ds(0, ns)], m_im + bu_ref[q, rows, pl.ds(ns, ns)]

    def project_out(q, g):
        rows = pl.ds(g * S5_GROUP_ROWS, S5_GROUP_ROWS)
        y = _dot(st_ref[prv, q, rows, :], ccat_ref[0]) + d_ref[...] * up_ref[prv, q, rows, :]
        y = jax.nn.gelu(y)
        ab = _dot(y.astype(BF16), wab_ref[0])
        out = ab[:, :S5_LANES] * jax.nn.sigmoid(ab[:, S5_LANES:])
        for jj in range(steps_per_group):
            j = g * steps_per_group + jj
            o_ref[q, pl.ds(j, SUBLANES, stride=S5_STEPS), :] = out[jj * SUBLANES:(jj + 1) * SUBLANES, :]

    zero = jnp.zeros((SUBLANES, ns), F32)
    state = [(zero, zero)] * nseq
    for g in range(n_groups):
        rows = pl.ds(g * S5_GROUP_ROWS, S5_GROUP_ROWS)
        for q in range(nseq):
            bu_ref[q, rows, :] = _dot(up_ref[cur, q, rows, :].astype(BF16), bcat_ref[0])
        for jj in range(steps_per_group):
            state = [step(q, g * steps_per_group + jj, state[q]) for q in range(nseq)]
        for q in range(nseq):
            project_out(q, g)

    row = lax.broadcasted_iota(jnp.int32, (SUBLANES, ns), 0)
    first = row == 0
    entry = []
    for q in range(nseq):
        e_re, e_im = state[q]
        c_re = jnp.where(first, pltpu.roll(carry_ref[q, 0], 1, 0), pltpu.roll(e_re, 1, 0))
        c_im = jnp.where(first, pltpu.roll(carry_ref[q, 1], 1, 0), pltpu.roll(e_im, 1, 0))
        for i, d in enumerate((1, 2, 4)):
            r_re = pltpu.roll(c_re, d, 0)
            r_im = pltpu.roll(c_im, d, 0)
            m_re, m_im = _cmul(pj_ref[0, i], pj_ref[1, i], r_re, r_im)
            c_re = c_re + m_re
            c_im = c_im + m_im
        f_re, f_im = _cmul(aj_ref[0], aj_ref[1], c_re, c_im)
        carry_ref[q, 0] = f_re + e_re
        carry_ref[q, 1] = f_im + e_im
        entry.append((c_re, c_im))

    state = entry
    pack = 2 * SUBLANES
    for j in range(0, S5_STEPS, 2):
        mid = [step(q, j, state[q]) for q in range(nseq)]
        state = [step(q, j + 1, mid[q]) for q in range(nseq)]
        for q in range(nseq):
            st_ref[cur, q, pl.ds(j * SUBLANES, pack), pl.ds(0, ns)] = (
                jnp.concatenate([mid[q][0], state[q][0]], axis=0).astype(BF16))
            st_ref[cur, q, pl.ds(j * SUBLANES, pack), pl.ds(ns, ns)] = (
                jnp.concatenate([mid[q][1], state[q][1]], axis=0).astype(BF16))


def _s5(u, bcat, ccat, d, wab, a1, pj, aj, bsz, seq):
    nt = seq // S5_TILE
    nlb = D_SSM // S5_LANES
    ns = S5_STATE
    u3 = u.reshape(bsz, seq, D_SSM)
    row_in = lambda l, t: (0, jnp.minimum(t, nt - 1), l)
    row_out = lambda l, t: (0, jnp.maximum(t - 1, 0), l)
    lane3 = lambda l, t: (0, 0, l)
    lane4 = lambda l, t: (0, 0, 0, l)
    out = pl.pallas_call(
        _s5_kernel,
        grid=(nlb, nt + 1),
        in_specs=[
            pl.BlockSpec((bsz, S5_TILE, S5_LANES), row_in),
            pl.BlockSpec((1, S5_LANES, 2 * ns), lambda l, t: (l, 0, 0)),
            pl.BlockSpec((1, 2 * ns, S5_LANES), lambda l, t: (l, 0, 0)),
            pl.BlockSpec((1, S5_LANES), lambda l, t: (0, l)),
            pl.BlockSpec((1, S5_LANES, 2 * S5_LANES), lambda l, t: (l, 0, 0)),
            pl.BlockSpec((2, SUBLANES, ns), lane3),
            pl.BlockSpec((2, 3, SUBLANES, ns), lane4),
            pl.BlockSpec((2, SUBLANES, ns), lane3),
        ],
        out_specs=pl.BlockSpec((bsz, S5_TILE, S5_LANES), row_out),
        out_shape=jax.ShapeDtypeStruct(u3.shape, F32),
        scratch_shapes=[
            pltpu.VMEM((2, bsz, S5_TILE, S5_LANES), F32),
            pltpu.VMEM((bsz, S5_TILE, 2 * ns), F32),
            pltpu.VMEM((2, bsz, S5_TILE, 2 * ns), BF16),
            pltpu.VMEM((bsz, 2, SUBLANES, ns), F32),
        ],
        compiler_params=pltpu.CompilerParams(
            dimension_semantics=("arbitrary", "arbitrary"),
            vmem_limit_bytes=40 << 20),
        name="s5",
    )(u3, bcat, ccat, d, wab, a1, pj, aj)
    return out.reshape(u.shape)


def _kv_kernel(m_ref, nw_ref, wk_ref, wv_ref, k_ref, v_ref):
    mn = _rms(m_ref[0], nw_ref[...]).astype(BF16)
    k_ref[0] = _dot(mn, wk_ref[...]).astype(BF16)
    v_ref[0] = _dot(mn, wv_ref[...]).astype(BF16)


def _kv(mem, norm_w, wk, wv):
    bsz, mlen, d = mem.shape
    blk = pl.BlockSpec((1, mlen, d), lambda b: (b, 0, 0))
    wspec = pl.BlockSpec((d, d), lambda b: (0, 0))
    return pl.pallas_call(
        _kv_kernel,
        grid=(bsz,),
        in_specs=[blk, pl.BlockSpec((1, d), lambda b: (0, 0)), wspec, wspec],
        out_specs=(blk, blk),
        out_shape=(jax.ShapeDtypeStruct(mem.shape, BF16),) * 2,
        compiler_params=pltpu.CompilerParams(
            dimension_semantics=("arbitrary",), vmem_limit_bytes=40 << 20),
        name="kv",
    )(mem, norm_w, wk, wv)


def _mix_attn_kernel(x_ref, yc_ref, ys_ref, wot_ref, wob_ref, nx_ref, wq_ref,
                     k_ref, v_ref, wo_ref, nf_ref, wr_ref, br_ref,
                     h_ref, xs_ref, y0_ref, gs_ref, ri_ref, cnt_ref):
    tm = x_ref.shape[0]
    neg = -jnp.inf
    big = float(ROUTE_LANES)

    groups = [pl.ds(g * (tm // MIX_SPLIT), tm // MIX_SPLIT) for g in range(MIX_SPLIT)]

    def project(rows):
        h1 = (x_ref[rows, :] + _dot(yc_ref[rows, :].astype(BF16), wot_ref[...])
              + _dot(ys_ref[rows, :].astype(BF16), wob_ref[...]))
        hn = _rms(h1, nx_ref[...]).astype(BF16)
        return h1, _dot(hn, wq_ref[...])

    def attend(q):
        heads = []
        for hd in range(N_XHEADS):
            sl = slice(hd * XHEAD_DIM, (hd + 1) * XHEAD_DIM)
            qh = q[:, sl].astype(BF16)
            s = lax.dot_general(qh, k_ref[0, :, sl], (((1,), (1,)), ((), ())),
                                preferred_element_type=F32) * (XHEAD_DIM ** -0.5)
            s = s - jnp.max(s, axis=-1, keepdims=True)
            p = jnp.exp(s)
            p = p / jnp.sum(p, axis=-1, keepdims=True)
            heads.append(_dot(p.astype(BF16), v_ref[0, :, sl]).astype(BF16))
        return jnp.concatenate(heads, axis=-1)

    def route(rows, h1, o):
        rt = rows.size
        h2 = h1 + _dot(o, wo_ref[...])
        h_ref[rows, :] = h2
        hf = _rms(h2, nf_ref[...]).astype(BF16)
        logits = _dot(hf, wr_ref[...]) + br_ref[...]
        lane_i = lax.broadcasted_iota(jnp.int32, (rt, ROUTE_LANES), 1)
        lane_r = lane_i.astype(F32)

        def top1(vals):
            m = jnp.max(vals, axis=-1, keepdims=True)
            idx = jnp.min(jnp.where(vals == m, lane_r, big), axis=-1, keepdims=True)
            return m, idx

        gl = jnp.where(lane_i < N_EXPERT_GROUPS, logits, neg)
        gmax, gidx = top1(gl)
        g_w = 1.0 / jnp.sum(jnp.exp(gl - gmax), axis=-1, keepdims=True)
        lo = EXPERT_LANE0 + EXPERTS_PER_GROUP * gidx
        el = jnp.where((lane_r >= lo) & (lane_r < lo + EXPERTS_PER_GROUP), logits, neg)
        m1, i1 = top1(el)
        m2, i2 = top1(jnp.where(lane_r == i1, neg, el))
        e21 = jnp.exp(m2 - m1)
        return hf, i1, i2, g_w / (1.0 + e21), g_w * e21 / (1.0 + e21)

    projected = [project(rows) for rows in groups]
    attended = [attend(q) for _, q in projected]
    parts = [route(rows, h1, o) for rows, (h1, _), o in zip(groups, projected, attended)]
    hf, i1, i2, gate1, gate2 = (jnp.concatenate(col, axis=0) for col in zip(*parts))
    lane = lax.broadcasted_iota(jnp.int32, (tm, ROUTE_LANES), 1)
    lane_f = lane.astype(F32)

    hot1 = lane_f == i1
    hot2 = lane_f == i2
    hot = jnp.where(hot1 | hot2, 1.0, 0.0)
    r_i = lax.broadcasted_iota(jnp.int32, (tm, tm), 0)
    c_i = lax.broadcasted_iota(jnp.int32, (tm, tm), 1)
    tri = jnp.where(r_i > c_i, 1.0, 0.0).astype(BF16)
    before = _dot(tri, hot.astype(BF16))
    count = jnp.sum(hot, axis=0, keepdims=True)
    chunks = jnp.floor((count + (MOE_CHUNK - 1.0)) * (1.0 / MOE_CHUNK))
    l_i = lax.broadcasted_iota(jnp.int32, (ROUTE_LANES, ROUTE_LANES), 0)
    l_j = lax.broadcasted_iota(jnp.int32, (ROUTE_LANES, ROUTE_LANES), 1)
    upper = jnp.where(l_i < l_j, 1.0, 0.0).astype(BF16)
    first_chunk = _dot(jnp.broadcast_to(chunks, (SUBLANES, ROUTE_LANES)).astype(BF16), upper)[0:1]
    start = first_chunk * float(MOE_CHUNK) + before
    pos1 = jnp.sum(jnp.where(hot1, start, 0.0), axis=-1, keepdims=True)
    pos2 = jnp.sum(jnp.where(hot2, start, 0.0), axis=-1, keepdims=True)
    cnt_ref[0] = jnp.broadcast_to(count, (SUBLANES, ROUTE_LANES)).astype(jnp.int32)
    ri_ref[...] = jnp.where(lane == 0, pos1, jnp.where(lane == 1, pos2, 0.0)).astype(jnp.int32)

    def pieces(g):
        hi = g.astype(BF16).astype(F32)
        mid = (g - hi).astype(BF16).astype(F32)
        low = (g - hi - mid).astype(BF16).astype(F32)
        return hi, mid, low

    g6 = jnp.where(lane == 6, 1.0, 0.0)
    for li, piece in enumerate(pieces(gate1) + pieces(gate2)):
        g6 = jnp.where(lane == li, piece, g6)
    rho = lax.broadcasted_iota(jnp.int32, (tm, TILE_ROWS), 1).astype(F32)
    ptk = jnp.where(rho == pos1, 1.0, jnp.where(rho == pos2, 2.0, 0.0)).astype(BF16)
    rhs = jnp.concatenate([hf, g6.astype(BF16)], axis=1)
    res = lax.dot_general(ptk, rhs, (((0,), (0,)), ((), ())), preferred_element_type=F32)
    sg = res[:, D_MODEL:]
    which = sg[:, 6:7]
    srt = res[:, :D_MODEL] * jnp.where(which == 2.0, 0.5, 1.0)
    xs_ref[...] = _pack_words(srt).reshape(TILE_CHUNKS, MOE_CHUNK, D_PACK)
    y0_ref[...] = jnp.zeros(y0_ref.shape, jnp.uint32)
    first = sg[:, 0:1] + sg[:, 1:2] + sg[:, 2:3]
    second = 0.5 * (sg[:, 3:4] + sg[:, 4:5] + sg[:, 5:6])
    gsort = jnp.where(which == 1.0, first, jnp.where(which == 2.0, second, 0.0))
    gs_ref[...] = jnp.broadcast_to(gsort, (TILE_ROWS, ROUTE_LANES))


def _pack_words(v):
    hi = lax.bitcast_convert_type(v[:, :D_PACK], jnp.uint32) & jnp.uint32(HI_MASK)
    lo = lax.bitcast_convert_type(v[:, D_PACK:], jnp.uint32) >> 16
    return hi | lo


def _unpack_words(w):
    hi = lax.bitcast_convert_type(w & jnp.uint32(HI_MASK), F32)
    lo = lax.bitcast_convert_type(w << 16, F32)
    return hi, lo


def _mix_attn(xt, yc, ys, wot, wob, nx, wq, k, v, wo, nf, wr, br, seq):
    t, d = xt.shape
    nt = t // TM_MIX
    tiles_per_batch = seq // TM_MIX
    mlen = k.shape[1]
    row = lambda i: (i, 0)
    const = lambda i: (0, 0)
    tile3 = lambda i: (i, 0, 0)
    kvspec = pl.BlockSpec((1, mlen, d), lambda i: (i // tiles_per_batch, 0, 0))
    packed = jax.ShapeDtypeStruct((nt * TILE_CHUNKS, MOE_CHUNK, D_PACK), jnp.uint32)
    return pl.pallas_call(
        _mix_attn_kernel,
        grid=(nt,),
        in_specs=[
            pl.BlockSpec((TM_MIX, d), row),
            pl.BlockSpec((TM_MIX, D_CONV), row),
            pl.BlockSpec((TM_MIX, D_SSM), row),
            pl.BlockSpec((D_CONV, d), const),
            pl.BlockSpec((D_SSM, d), const),
            pl.BlockSpec((1, d), const),
            pl.BlockSpec((d, d), const),
            kvspec, kvspec,
            pl.BlockSpec((d, d), const),
            pl.BlockSpec((1, d), const),
            pl.BlockSpec((d, ROUTE_LANES), const),
            pl.BlockSpec((1, ROUTE_LANES), const),
        ],
        out_specs=(
            pl.BlockSpec((TM_MIX, d), row),
            pl.BlockSpec((TILE_CHUNKS, MOE_CHUNK, D_PACK), tile3),
            pl.BlockSpec((TILE_CHUNKS, MOE_CHUNK, D_PACK), tile3),
            pl.BlockSpec((TILE_ROWS, ROUTE_LANES), row),
            pl.BlockSpec((TM_MIX, ROUTE_LANES), row),
            pl.BlockSpec((1, SUBLANES, ROUTE_LANES), tile3),
        ),
        out_shape=(
            jax.ShapeDtypeStruct((t, d), F32),
            packed,
            packed,
            jax.ShapeDtypeStruct((nt * TILE_ROWS, ROUTE_LANES), F32),
            jax.ShapeDtypeStruct((t, ROUTE_LANES), jnp.int32),
            jax.ShapeDtypeStruct((nt, SUBLANES, ROUTE_LANES), jnp.int32),
        ),
        compiler_params=pltpu.CompilerParams(
            dimension_semantics=("arbitrary",), vmem_limit_bytes=60 << 20),
        name="mix_attn",
    )(xt, yc, ys, wot, wob, nx, wq, k, v, wo, nf, wr, br)


ZERO_CHUNK = TILE_CHUNKS - 1
DUMP_TILE0 = 16
SPARE_PER_TILE = TILE_CHUNKS - TILE_USED_CHUNKS


def _dump_chunk(slot, k):
    idx = slot * BLOCK_CHUNKS + k
    return (DUMP_TILE0 + idx // SPARE_PER_TILE) * TILE_CHUNKS + TILE_USED_CHUNKS + idx % SPARE_PER_TILE


def _experts_kernel(src_ref, dst_ref, be_ref, seq_ref, nxt_ref, na_ref, xs_hbm, w1_hbm, w3_hbm,
                    w2_hbm, y0_hbm, ys_hbm, xbuf, ybuf, w1f_ref, w3f_ref, w2f_ref,
                    w1s_ref, w3s_ref, w2s_ref, in_sem, out_sem, w_sem):
    del y0_hbm
    b = pl.program_id(0)
    n_active = na_ref[0]
    slot = b % 2
    other = 1 - slot

    def gather(blk, sl):
        for k in range(BLOCK_CHUNKS):
            pltpu.make_async_copy(xs_hbm.at[src_ref[blk * BLOCK_CHUNKS + k]],
                                  xbuf.at[sl, k], in_sem.at[sl]).start()

    def gather_wait(sl):
        pltpu.make_async_copy(xs_hbm.at[pl.ds(0, BLOCK_CHUNKS)], xbuf.at[sl],
                              in_sem.at[sl]).wait()

    def scatter(blk, sl):
        for k in range(BLOCK_CHUNKS):
            d = dst_ref[blk * BLOCK_CHUNKS + k]
            dump = jnp.where(sl == 0, _dump_chunk(0, k), _dump_chunk(1, k))
            pltpu.make_async_copy(ybuf.at[sl, k], ys_hbm.at[jnp.where(d < 0, dump, d)],
                                  out_sem.at[sl]).start()

    def scatter_wait(sl):
        pltpu.make_async_copy(ybuf.at[sl], ys_hbm.at[pl.ds(0, BLOCK_CHUNKS)],
                              out_sem.at[sl]).wait()

    active = b < n_active

    @pl.when(b == 0)
    def _():
        gather(0, 0)
        ybuf[...] = jnp.zeros(ybuf.shape, jnp.uint32)

    prev = be_ref[jnp.maximum(b - 1, 0)]
    fresh = (b == 0) | (be_ref[b] != prev)

    def weight_copies(e, ws):
        return [pltpu.make_async_copy(hbm.at[e], buf.at[ws], w_sem.at[ws])
                for hbm, buf in ((w1_hbm, w1f_ref), (w3_hbm, w3f_ref), (w2_hbm, w2f_ref))]

    @pl.when(active & fresh)
    def _():
        ws = seq_ref[b] % 2

        @pl.when(b == 0)
        def _():
            for cp in weight_copies(be_ref[b], ws):
                cp.start()

        for cp in weight_copies(be_ref[b], ws):
            cp.wait()

        @pl.when(nxt_ref[b] >= 0)
        def _():
            for cp in weight_copies(nxt_ref[b], 1 - ws):
                cp.start()

        w1s_ref[...] = w1f_ref[ws].astype(BF16)
        w3s_ref[...] = w3f_ref[ws].astype(BF16)
        w2s_ref[...] = w2f_ref[ws].astype(BF16)

    @pl.when(active & (b >= 1))
    def _():
        scatter_wait(slot)

    @pl.when(active)
    def _():
        gather_wait(slot)
        gather(b + 1, other)
        scatter(b, other)

        hi, lo = _unpack_words(xbuf[slot].reshape(MOE_BLOCK, D_PACK))
        xb = jnp.concatenate([hi, lo], axis=1).astype(BF16)
        h1 = _dot(xb, w1s_ref[...])
        h3 = _dot(xb, w3s_ref[...])
        hid = (h1 * jax.nn.sigmoid(h1) * h3).astype(BF16)
        y = _dot(hid, w2s_ref[...]).astype(BF16).astype(F32)
        ybuf[slot] = _pack_words(y).reshape(BLOCK_CHUNKS, MOE_CHUNK, D_PACK)

        @pl.when(b == n_active - 1)
        def _():
            scatter(b + 1, slot)
            gather_wait(other)
            scatter_wait(other)
            scatter_wait(slot)


def _experts(src, dst, block_expert, expert_seq, next_expert, n_active, xs, w1, w3, w2, y0):
    nb = block_expert.shape[0]
    d = D_MODEL
    hbm = pl.BlockSpec(memory_space=pl.ANY)
    grid_spec = pltpu.PrefetchScalarGridSpec(
        num_scalar_prefetch=6,
        grid=(nb,),
        in_specs=[hbm, hbm, hbm, hbm, hbm],
        out_specs=hbm,
        scratch_shapes=[
            pltpu.VMEM((2, BLOCK_CHUNKS, MOE_CHUNK, D_PACK), jnp.uint32),
            pltpu.VMEM((2, BLOCK_CHUNKS, MOE_CHUNK, D_PACK), jnp.uint32),
            pltpu.VMEM((2, d, D_EXPERT), F32),
            pltpu.VMEM((2, d, D_EXPERT), F32),
            pltpu.VMEM((2, D_EXPERT, d), F32),
            pltpu.VMEM((d, D_EXPERT), BF16),
            pltpu.VMEM((d, D_EXPERT), BF16),
            pltpu.VMEM((D_EXPERT, d), BF16),
            pltpu.SemaphoreType.DMA((2,)),
            pltpu.SemaphoreType.DMA((2,)),
            pltpu.SemaphoreType.DMA((2,)),
        ],
    )
    return pl.pallas_call(
        _experts_kernel,
        grid_spec=grid_spec,
        out_shape=jax.ShapeDtypeStruct(y0.shape, jnp.uint32),
        input_output_aliases={10: 0},
        compiler_params=pltpu.CompilerParams(
            dimension_semantics=("arbitrary",), vmem_limit_bytes=48 << 20),
        name="experts",
    )(src, dst, block_expert, expert_seq, next_expert, n_active, xs, w1, w3, w2, y0)


def _combine_kernel(h_ref, ri_ref, gs_ref, ys_ref, fw_ref, o_ref):
    tm = h_ref.shape[0]
    hi, lo = _unpack_words(ys_ref[...].reshape(TILE_ROWS, D_PACK))
    g = gs_ref[...]
    gw = jnp.concatenate([g] * (D_PACK // ROUTE_LANES), axis=1)
    yg = jnp.concatenate([hi * gw, lo * gw], axis=1).astype(BF16)
    pos = ri_ref[...].astype(F32)
    rho = lax.broadcasted_iota(jnp.int32, (tm, TILE_ROWS), 1).astype(F32)
    q = jnp.where((rho == pos[:, 0:1]) | (rho == pos[:, 1:2]), 1.0, 0.0).astype(BF16)
    o_ref[...] = _rms(h_ref[...] + _dot(q, yg), fw_ref[...])


def _combine(h2, route_i, gsort, ys, final_w):
    t, d = h2.shape
    nt = t // TM_MIX
    row = lambda i: (i, 0)
    return pl.pallas_call(
        _combine_kernel,
        grid=(nt,),
        in_specs=[
            pl.BlockSpec((TM_MIX, d), row),
            pl.BlockSpec((TM_MIX, ROUTE_LANES), row),
            pl.BlockSpec((TILE_ROWS, ROUTE_LANES), row),
            pl.BlockSpec((TILE_CHUNKS, MOE_CHUNK, D_PACK), lambda i: (i, 0, 0)),
            pl.BlockSpec((1, d), lambda i: (0, 0)),
        ],
        out_specs=pl.BlockSpec((TM_MIX, d), row),
        out_shape=jax.ShapeDtypeStruct((t, d), F32),
        compiler_params=pltpu.CompilerParams(
            dimension_semantics=("arbitrary",), vmem_limit_bytes=48 << 20),
        name="combine",
    )(h2, route_i, gsort, ys, final_w)


def _routing_tables(cnt):
    nt = cnt.shape[0]
    n = cnt[:, 0, EXPERT_LANE0:EXPERT_LANE0 + N_EXPERTS]
    c = (n + MOE_CHUNK - 1) // MOE_CHUNK
    local = jnp.cumsum(c, axis=1) - c
    per_expert = jnp.sum(c, axis=0)
    padded = (per_expert + BLOCK_CHUNKS - 1) // BLOCK_CHUNKS * BLOCK_CHUNKS
    gend = jnp.cumsum(padded)
    gstart = gend - padded
    within = jnp.cumsum(c, axis=0) - c
    start = (gstart[None, :] + within).T.reshape(-1)
    length = c.T.reshape(-1)
    base = (jnp.arange(nt, dtype=jnp.int32)[:, None] * TILE_CHUNKS + local).T.reshape(-1)
    g_max = (2 * nt * TM_MIX // MOE_CHUNK + nt * N_EXPERTS
             + N_EXPERTS * (BLOCK_CHUNKS - 1))
    n_blocks = -(-g_max // BLOCK_CHUNKS)
    g = jnp.arange(n_blocks * BLOCK_CHUNKS, dtype=jnp.int32)
    nxt = jnp.concatenate([start[1:], jnp.full((1,), 2 ** 30, start.dtype)])
    owner = ((start[None, :] <= g[:, None]) & (nxt[None, :] > g[:, None])).astype(jnp.int32)
    k = g - jnp.sum(owner * start[None, :], axis=1)
    valid = k < jnp.sum(owner * length[None, :], axis=1)
    chunk = jnp.sum(owner * base[None, :], axis=1) + k
    pad_src = jnp.full((BLOCK_CHUNKS,), ZERO_CHUNK, jnp.int32)
    pad_dst = jnp.full((BLOCK_CHUNKS,), -1, jnp.int32)
    src = jnp.concatenate([jnp.where(valid, chunk, ZERO_CHUNK).astype(jnp.int32), pad_src])
    dst = jnp.concatenate([pad_dst, jnp.where(valid, chunk, -1).astype(jnp.int32)])
    block_start = jnp.arange(n_blocks, dtype=jnp.int32) * BLOCK_CHUNKS
    block_expert = jnp.minimum(
        jnp.sum((block_start[:, None] >= gend[None, :]).astype(jnp.int32), axis=1),
        N_EXPERTS - 1).astype(jnp.int32)
    n_active = (gend[-1:] // BLOCK_CHUNKS).astype(jnp.int32)
    blk = jnp.arange(n_blocks, dtype=jnp.int32)
    is_first = (blk < n_active[0]) & ((blk == 0) | (block_expert != jnp.roll(block_expert, 1)))
    expert_seq = (jnp.cumsum(is_first.astype(jnp.int32)) - 1).astype(jnp.int32)
    later_first = is_first[None, :] & (blk[None, :] > blk[:, None])
    next_blk = jnp.min(jnp.where(later_first, blk[None, :], n_blocks), axis=1)
    next_expert = jnp.where(
        next_blk < n_blocks,
        jnp.sum(jnp.where(blk[None, :] == next_blk[:, None], block_expert[None, :], 0), axis=1),
        -1).astype(jnp.int32)
    return src, dst, block_expert, expert_seq, next_expert, n_active


def _layer(h, mem, p, final_w):
    bsz, seq, d = h.shape
    t = bsz * seq
    assert t // TM_MIX >= DUMP_TILE0 + 2 * BLOCK_CHUNKS // SPARE_PER_TILE
    xt = h.reshape(t, d)

    v, u = _in_proj(xt, p["norm_mix_w"].reshape(1, d), p["w_in"].astype(BF16))
    y_conv = _conv(v, p["conv_w"], p["conv_b"].reshape(1, D_CONV),
                   p["conv_ln_w"].reshape(1, D_CONV), p["conv_ln_b"].reshape(1, D_CONV),
                   bsz, seq)

    a1, pj, aj, bbar = _s5_prep(p["ssm_A_re"], p["ssm_A_im"], p["ssm_log_dt"],
                                      p["ssm_B_re"], p["ssm_B_im"])
    nlb = D_SSM // S5_LANES
    bb = bbar.reshape(2, SSM_GROUP, nlb, S5_GROUPS, SSM_STATE).transpose(0, 2, 3, 1, 4)
    bcat = jnp.concatenate([_block_diag(bb[0]), _block_diag(bb[1])], axis=-1).astype(BF16)
    c_re = p["ssm_C_re"].reshape(nlb, S5_GROUPS, SSM_GROUP, SSM_STATE).transpose(0, 1, 3, 2)
    c_im = p["ssm_C_im"].reshape(nlb, S5_GROUPS, SSM_GROUP, SSM_STATE).transpose(0, 1, 3, 2)
    ccat = jnp.concatenate([_block_diag(c_re), -_block_diag(c_im)], axis=1).astype(BF16)
    glu = p["ssm_glu_w"].reshape(nlb, S5_GROUPS, SSM_GROUP, 2 * SSM_GROUP)
    wab = jnp.concatenate([_block_diag(glu[..., :SSM_GROUP]),
                           _block_diag(glu[..., SSM_GROUP:])], axis=-1).astype(BF16)
    y_ssm = _s5(u, bcat, ccat, p["ssm_D"].reshape(1, D_SSM), wab,
                a1, pj, aj, bsz, seq)

    k, vv = _kv(mem, p["norm_mem_w"].reshape(1, d), p["xk_w"].astype(BF16),
                p["xv_w"].astype(BF16))
    w_out = p["w_out"].astype(BF16)
    wr = jnp.concatenate([p["router_group_w"], p["router_expert_w"]], axis=1)
    wr = jnp.pad(wr, ((0, 0), (0, ROUTE_LANES - wr.shape[1]))).astype(BF16)
    br = jnp.concatenate([p["router_group_b"], p["router_expert_b"].reshape(-1)])
    br = jnp.pad(br, (0, ROUTE_LANES - br.shape[0])).reshape(1, ROUTE_LANES)
    h2, xs, y0, gsort, route_i, cnt = _mix_attn(
        xt, y_conv, y_ssm, w_out[:D_CONV], w_out[D_CONV:], p["norm_x_w"].reshape(1, d),
        p["xq_w"].astype(BF16), k, vv, p["xo_w"].astype(BF16),
        p["norm_ffn_w"].reshape(1, d), wr, br, seq)

    src, dst, block_expert, expert_seq, next_expert, n_active = _routing_tables(cnt)
    ys = _experts(src, dst, block_expert, expert_seq, next_expert, n_active, xs,
                  p["moe_w1"], p["moe_w3"], p["moe_w2"], y0)
    out = _combine(h2, route_i, gsort, ys, final_w.reshape(1, d))
    return out.reshape(bsz, seq, d)


def kernel(x, mem, norm_mix_w, w_in, conv_w, conv_b, conv_ln_w, conv_ln_b, ssm_A_re, ssm_A_im, ssm_log_dt, ssm_B_re, ssm_B_im, ssm_C_re, ssm_C_im, ssm_D, ssm_glu_w, w_out, norm_x_w, norm_mem_w, xq_w, xk_w, xv_w, xo_w, norm_ffn_w, router_group_w, router_group_b, router_expert_w, router_expert_b, moe_w1, moe_w3, moe_w2, final_norm_w):
    stacked = dict(
        norm_mix_w=norm_mix_w, w_in=w_in, conv_w=conv_w, conv_b=conv_b,
        conv_ln_w=conv_ln_w, conv_ln_b=conv_ln_b, ssm_A_re=ssm_A_re, ssm_A_im=ssm_A_im,
        ssm_log_dt=ssm_log_dt, ssm_B_re=ssm_B_re, ssm_B_im=ssm_B_im, ssm_C_re=ssm_C_re,
        ssm_C_im=ssm_C_im, ssm_D=ssm_D, ssm_glu_w=ssm_glu_w, w_out=w_out,
        norm_x_w=norm_x_w, norm_mem_w=norm_mem_w, xq_w=xq_w, xk_w=xk_w, xv_w=xv_w,
        xo_w=xo_w, norm_ffn_w=norm_ffn_w, router_group_w=router_group_w,
        router_group_b=router_group_b, router_expert_w=router_expert_w,
        router_expert_b=router_expert_b, moe_w1=moe_w1, moe_w3=moe_w3, moe_w2=moe_w2)
    depth = norm_mix_w.shape[0]
    assert depth == 1, "final norm is fused into the single layer's combine step"
    layer = {name: w[0] for name, w in stacked.items()}
    return _layer(x, mem, layer, final_norm_w)
```

```python
import functools

import jax
import jax.numpy as jnp
from jax import lax
from jax.experimental import pallas as pl
from jax.experimental.pallas import tpu as pltpu

D_MODEL = 1024
D_CONV = 512
CONV_WIDTH = 31
D_SSM = 512
SSM_GROUP = 16
N_SSM_GROUPS = 32
SSM_STATE = 64
N_XHEADS = 4
XHEAD_DIM = 256
N_EXPERT_GROUPS = 4
EXPERTS_PER_GROUP = 8
N_EXPERTS = 32
D_EXPERT = 512
EPS = 1e-6

F32 = jnp.float32
BF16 = jnp.bfloat16

SUBLANES = 8
LANES = 128

TM_PROJ = 512
TM_CONV = 512
CONV_CHUNK = 128
CONV_NORM_ROWS = 128
CONV_HALO = 32
S5_STEPS = 64
S5_TILE = SUBLANES * S5_STEPS
S5_GROUP_ROWS = 128
S5_LANES = 128
S5_GROUPS = S5_LANES // SSM_GROUP
S5_STATE = S5_GROUPS * SSM_STATE
TM_MIX = 512
ROUTE_LANES = 128
EXPERT_LANE0 = N_EXPERT_GROUPS
MOE_CHUNK = SUBLANES
MOE_BLOCK = 256
BLOCK_CHUNKS = MOE_BLOCK // MOE_CHUNK
TILE_USED_CHUNKS = 2 * TM_MIX // MOE_CHUNK + N_EXPERTS * (MOE_CHUNK - 1) // MOE_CHUNK
TILE_CHUNKS = 160
TILE_ROWS = TILE_CHUNKS * MOE_CHUNK
D_PACK = D_MODEL // 2
HI_MASK = 0xFFFF0000


def _rms(x, w):
    return x * lax.rsqrt(jnp.mean(x * x, axis=-1, keepdims=True) + EPS) * w


def _dot(a, b):
    return jnp.dot(a, b, preferred_element_type=F32)


def _s5_prep_kernel(are_ref, aim_ref, ldt_ref, btre_ref, btim_ref,
                    a1_ref, pj_ref, aj_ref, bbar_ref):
    lam_re = are_ref[...]
    lam_im = aim_ref[...]
    dt = jnp.exp(ldt_ref[...])
    x = lam_re * dt
    y = lam_im * dt

    def power(k):
        mag = jnp.exp(k * x)
        return mag * jnp.cos(k * y), mag * jnp.sin(k * y)

    n = x.shape[-1]
    ones8 = jnp.ones((SUBLANES, n), F32)
    a_re, a_im = power(ones8)
    a1_ref[0] = a_re
    a1_ref[1] = a_im

    row = lax.broadcasted_iota(jnp.int32, (SUBLANES, n), 0)
    for i, d in enumerate((1, 2, 4)):
        p_re, p_im = power(ones8 * float(d * S5_STEPS))
        keep = row >= d
        pj_ref[0, i] = jnp.where(keep, p_re, 0.0)
        pj_ref[1, i] = jnp.where(keep, p_im, 0.0)
    j_re, j_im = power(ones8 * float(S5_STEPS))
    aj_ref[0] = j_re
    aj_ref[1] = j_im

    num_re = a_re[0:1] - 1.0
    num_im = a_im[0:1]
    den = lam_re * lam_re + lam_im * lam_im
    c_re = (num_re * lam_re + num_im * lam_im) / den
    c_im = (num_im * lam_re - num_re * lam_im) / den
    b_re = btre_ref[...]
    b_im = btim_ref[...]
    bbar_ref[0] = c_re * b_re - c_im * b_im
    bbar_ref[1] = c_re * b_im + c_im * b_re


def _s5_prep(a_re, a_im, log_dt, b_re, b_im):
    n = N_SSM_GROUPS * SSM_STATE
    are = a_re.reshape(1, n)
    aim = a_im.reshape(1, n)
    ldt = jnp.repeat(log_dt, SSM_STATE).reshape(1, n)
    btre = jnp.transpose(b_re, (2, 0, 1)).reshape(SSM_GROUP, n)
    btim = jnp.transpose(b_im, (2, 0, 1)).reshape(SSM_GROUP, n)
    return pl.pallas_call(
        _s5_prep_kernel,
        out_shape=(
            jax.ShapeDtypeStruct((2, SUBLANES, n), F32),
            jax.ShapeDtypeStruct((2, 3, SUBLANES, n), F32),
            jax.ShapeDtypeStruct((2, SUBLANES, n), F32),
            jax.ShapeDtypeStruct((2, SSM_GROUP, n), F32),
        ),
        name="s5_prep",
    )(are, aim, ldt, btre, btim)


def _block_diag(w):
    nl, g, r, c = w.shape
    eye = jnp.eye(g, dtype=w.dtype)
    return jnp.einsum("lgrc,gh->lgrhc", w, eye).reshape(nl, g * r, g * c)


def _in_proj_kernel(x_ref, nw_ref, w_ref, v_ref, u_ref):
    xn = _rms(x_ref[...], nw_ref[...]).astype(BF16)
    proj = _dot(xn, w_ref[...])
    a = proj[:, :D_CONV]
    g = proj[:, D_CONV:2 * D_CONV]
    v_ref[...] = a * jax.nn.sigmoid(g)
    u_ref[...] = proj[:, 2 * D_CONV:]


def _in_proj(xt, norm_w, w_in):
    t = xt.shape[0]
    n_out = 2 * D_CONV + D_SSM
    return pl.pallas_call(
        _in_proj_kernel,
        grid=(t // TM_PROJ,),
        in_specs=[
            pl.BlockSpec((TM_PROJ, D_MODEL), lambda i: (i, 0)),
            pl.BlockSpec((1, D_MODEL), lambda i: (0, 0)),
            pl.BlockSpec((D_MODEL, n_out), lambda i: (0, 0)),
        ],
        out_specs=(
            pl.BlockSpec((TM_PROJ, D_CONV), lambda i: (i, 0)),
            pl.BlockSpec((TM_PROJ, D_SSM), lambda i: (i, 0)),
        ),
        out_shape=(
            jax.ShapeDtypeStruct((t, D_CONV), F32),
            jax.ShapeDtypeStruct((t, D_SSM), F32),
        ),
        compiler_params=pltpu.CompilerParams(
            dimension_semantics=("arbitrary",), vmem_limit_bytes=40 << 20),
        name="in_proj",
    )(xt, norm_w, w_in)


def _conv_kernel(v_ref, w_ref, b_ref, lnw_ref, lnb_ref, o_ref, ext_ref, sh_ref):
    tt = pl.program_id(1)
    rows = CONV_HALO + TM_CONV

    @pl.when(tt == 0)
    def _():
        ext_ref[pl.ds(0, CONV_HALO), :] = jnp.zeros((CONV_HALO, D_CONV), F32)

    @pl.when(tt > 0)
    def _():
        ext_ref[pl.ds(0, CONV_HALO), :] = ext_ref[pl.ds(TM_CONV, CONV_HALO), :]

    ext_ref[pl.ds(CONV_HALO, TM_CONV), :] = v_ref[...]
    for s in range(1, SUBLANES):
        sh_ref[s - 1, pl.ds(0, rows - SUBLANES), :] = ext_ref[pl.ds(s, rows - SUBLANES), :]
    bias = b_ref[...]
    lnw = lnw_ref[...]
    lnb = lnb_ref[...]
    tap0 = CONV_HALO - (CONV_WIDTH - 1)

    groups = CONV_CHUNK // SUBLANES

    def chunk(ci, carry):
        base = pl.multiple_of(ci * CONV_CHUNK, CONV_CHUNK)
        for lt in range(D_CONV // LANES):
            lanes = pl.ds(lt * LANES, LANES)
            acc = [jnp.broadcast_to(bias[:, lt * LANES:(lt + 1) * LANES], (SUBLANES, LANES))] * groups
            for s in range(SUBLANES):
                taps = [j for j in range(CONV_WIDTH) if (tap0 + j) % SUBLANES == s]
                src = ext_ref if s == 0 else sh_ref.at[s - 1]
                ngroups = (tap0 + taps[-1] - s) // SUBLANES + groups
                win = [src[pl.ds(base + SUBLANES * g, SUBLANES), lanes] for g in range(ngroups)]
                for j in taps:
                    g0 = (tap0 + j - s) // SUBLANES
                    wj = w_ref[j, :, lanes]
                    acc = [acc[r] + wj * win[g0 + r] for r in range(groups)]
            o_ref[pl.ds(base, CONV_CHUNK), lanes] = jnp.concatenate(acc, axis=0)
        return carry

    lax.fori_loop(0, TM_CONV // CONV_CHUNK, chunk, 0)

    for bi in range(TM_CONV // CONV_NORM_ROWS):
        rows_b = pl.ds(bi * CONV_NORM_ROWS, CONV_NORM_ROWS)
        acc = o_ref[rows_b, :]
        mu = jnp.mean(acc, axis=-1, keepdims=True)
        cen = acc - mu
        var = jnp.mean(cen * cen, axis=-1, keepdims=True)
        z = cen * lax.rsqrt(var + EPS) * lnw + lnb
        o_ref[rows_b, :] = z * jax.nn.sigmoid(z)


def _conv(v, conv_w, conv_b, ln_w, ln_b, bsz, seq):
    nt = seq // TM_CONV
    row = lambda b, t: (b * nt + t, 0)
    const = lambda b, t: (0, 0)
    return pl.pallas_call(
        _conv_kernel,
        grid=(bsz, nt),
        in_specs=[
            pl.BlockSpec((TM_CONV, D_CONV), row),
            pl.BlockSpec((CONV_WIDTH, SUBLANES, D_CONV), lambda b, t: (0, 0, 0)),
            pl.BlockSpec((1, D_CONV), const),
            pl.BlockSpec((1, D_CONV), const),
            pl.BlockSpec((1, D_CONV), const),
        ],
        out_specs=pl.BlockSpec((TM_CONV, D_CONV), row),
        out_shape=jax.ShapeDtypeStruct(v.shape, F32),
        scratch_shapes=[
            pltpu.VMEM((CONV_HALO + TM_CONV, D_CONV), F32),
            pltpu.VMEM((SUBLANES - 1, CONV_HALO + TM_CONV, D_CONV), F32),
        ],
        compiler_params=pltpu.CompilerParams(
            dimension_semantics=("arbitrary", "arbitrary")),
        name="conv",
    )(v, jnp.broadcast_to(conv_w[:, None, :], (CONV_WIDTH, SUBLANES, D_CONV)), conv_b, ln_w, ln_b)


def _cmul(a_re, a_im, b_re, b_im):
    return a_re * b_re - a_im * b_im, a_re * b_im + a_im * b_re


def _s5_kernel(u_ref, bcat_ref, ccat_ref, d_ref, wab_ref, a1_ref,
               pj_ref, aj_ref, o_ref, up_ref, bu_ref, st_ref, carry_ref):
    tt = pl.program_id(1)
    ns = S5_STATE
    nseq = u_ref.shape[0]
    steps_per_group = S5_GROUP_ROWS // SUBLANES
    n_groups = S5_TILE // S5_GROUP_ROWS
    cur = tt % 2
    prv = 1 - cur

    @pl.when((pl.program_id(0) == 0) & (tt == 0))
    def _():
        up_ref[...] = jnp.zeros(up_ref.shape, F32)
        st_ref[...] = jnp.zeros(st_ref.shape, BF16)

    @pl.when(tt == 0)
    def _():
        carry_ref[...] = jnp.zeros(carry_ref.shape, F32)

    for q in range(nseq):
        for j in range(S5_STEPS):
            up_ref[cur, q, pl.ds(SUBLANES * j, SUBLANES), :] = (
                u_ref[q, pl.ds(j, SUBLANES, stride=S5_STEPS), :])

    a_re = a1_ref[0]
    a_im = a1_ref[1]

    def step(q, j, s):
        rows = pl.ds(j * SUBLANES, SUBLANES)
        m_re, m_im = _cmul(a_re, a_im, s[0], s[1])
        return m_re + bu_ref[q, rows, pl.ds(0, ns)], m_im + bu_ref[q, rows, pl.ds(ns, ns)]

    def project_out(q, g):
        rows = pl.ds(g * S5_GROUP_ROWS, S5_GROUP_ROWS)
        y = _dot(st_ref[prv, q, rows, :], ccat_ref[0]) + d_ref[...] * up_ref[prv, q, rows, :]
        y = jax.nn.gelu(y)
        ab = _dot(y.astype(BF16), wab_ref[0])
        out = ab[:, :S5_LANES] * jax.nn.sigmoid(ab[:, S5_LANES:])
        for jj in range(steps_per_group):
            j = g * steps_per_group + jj
            o_ref[q, pl.ds(j, SUBLANES, stride=S5_STEPS), :] = out[jj * SUBLANES:(jj + 1) * SUBLANES, :]

    zero = jnp.zeros((SUBLANES, ns), F32)
    state = [(zero, zero)] * nseq
    for g in range(n_groups):
        rows = pl.ds(g * S5_GROUP_ROWS, S5_GROUP_ROWS)
        for q in range(nseq):
            bu_ref[q, rows, :] = _dot(up_ref[cur, q, rows, :].astype(BF16), bcat_ref[0])
        for jj in range(steps_per_group):
            state = [step(q, g * steps_per_group + jj, state[q]) for q in range(nseq)]
        for q in range(nseq):
            project_out(q, g)

    row = lax.broadcasted_iota(jnp.int32, (SUBLANES, ns), 0)
    first = row == 0
    entry = []
    for q in range(nseq):
        e_re, e_im = state[q]
        c_re = jnp.where(first, pltpu.roll(carry_ref[q, 0], 1, 0), pltpu.roll(e_re, 1, 0))
        c_im = jnp.where(first, pltpu.roll(carry_ref[q, 1], 1, 0), pltpu.roll(e_im, 1, 0))
        for i, d in enumerate((1, 2, 4)):
            r_re = pltpu.roll(c_re, d, 0)
            r_im = pltpu.roll(c_im, d, 0)
            m_re, m_im = _cmul(pj_ref[0, i], pj_ref[1, i], r_re, r_im)
            c_re = c_re + m_re
            c_im = c_im + m_im
        f_re, f_im = _cmul(aj_ref[0], aj_ref[1], c_re, c_im)
        carry_ref[q, 0] = f_re + e_re
        carry_ref[q, 1] = f_im + e_im
        entry.append((c_re, c_im))

    state = entry
    pack = 2 * SUBLANES
    for j in range(0, S5_STEPS, 2):
        mid = [step(q, j, state[q]) for q in range(nseq)]
        state = [step(q, j + 1, mid[q]) for q in range(nseq)]
        for q in range(nseq):
            st_ref[cur, q, pl.ds(j * SUBLANES, pack), pl.ds(0, ns)] = (
                jnp.concatenate([mid[q][0], state[q][0]], axis=0).astype(BF16))
            st_ref[cur, q, pl.ds(j * SUBLANES, pack), pl.ds(ns, ns)] = (
                jnp.concatenate([mid[q][1], state[q][1]], axis=0).astype(BF16))


def _s5(u, bcat, ccat, d, wab, a1, pj, aj, bsz, seq):
    nt = seq // S5_TILE
    nlb = D_SSM // S5_LANES
    ns = S5_STATE
    u3 = u.reshape(bsz, seq, D_SSM)
    row_in = lambda l, t: (0, jnp.minimum(t, nt - 1), l)
    row_out = lambda l, t: (0, jnp.maximum(t - 1, 0), l)
    lane3 = lambda l, t: (0, 0, l)
    lane4 = lambda l, t: (0, 0, 0, l)
    out = pl.pallas_call(
        _s5_kernel,
        grid=(nlb, nt + 1),
        in_specs=[
            pl.BlockSpec((bsz, S5_TILE, S5_LANES), row_in),
            pl.BlockSpec((1, S5_LANES, 2 * ns), lambda l, t: (l, 0, 0)),
            pl.BlockSpec((1, 2 * ns, S5_LANES), lambda l, t: (l, 0, 0)),
            pl.BlockSpec((1, S5_LANES), lambda l, t: (0, l)),
            pl.BlockSpec((1, S5_LANES, 2 * S5_LANES), lambda l, t: (l, 0, 0)),
            pl.BlockSpec((2, SUBLANES, ns), lane3),
            pl.BlockSpec((2, 3, SUBLANES, ns), lane4),
            pl.BlockSpec((2, SUBLANES, ns), lane3),
        ],
        out_specs=pl.BlockSpec((bsz, S5_TILE, S5_LANES), row_out),
        out_shape=jax.ShapeDtypeStruct(u3.shape, F32),
        scratch_shapes=[
            pltpu.VMEM((2, bsz, S5_TILE, S5_LANES), F32),
            pltpu.VMEM((bsz, S5_TILE, 2 * ns), F32),
            pltpu.VMEM((2, bsz, S5_TILE, 2 * ns), BF16),
            pltpu.VMEM((bsz, 2, SUBLANES, ns), F32),
        ],
        compiler_params=pltpu.CompilerParams(
            dimension_semantics=("arbitrary", "arbitrary"),
            vmem_limit_bytes=40 << 20),
        name="s5",
    )(u3, bcat, ccat, d, wab, a1, pj, aj)
    return out.reshape(u.shape)


def _kv_kernel(m_ref, nw_ref, wk_ref, wv_ref, k_ref, v_ref):
    mn = _rms(m_ref[0], nw_ref[...]).astype(BF16)
    k_ref[0] = _dot(mn, wk_ref[...]).astype(BF16)
    v_ref[0] = _dot(mn, wv_ref[...]).astype(BF16)


def _kv(mem, norm_w, wk, wv):
    bsz, mlen, d = mem.shape
    blk = pl.BlockSpec((1, mlen, d), lambda b: (b, 0, 0))
    wspec = pl.BlockSpec((d, d), lambda b: (0, 0))
    return pl.pallas_call(
        _kv_kernel,
        grid=(bsz,),
        in_specs=[blk, pl.BlockSpec((1, d), lambda b: (0, 0)), wspec, wspec],
        out_specs=(blk, blk),
        out_shape=(jax.ShapeDtypeStruct(mem.shape, BF16),) * 2,
        compiler_params=pltpu.CompilerParams(
            dimension_semantics=("arbitrary",), vmem_limit_bytes=40 << 20),
        name="kv",
    )(mem, norm_w, wk, wv)


def _mix_attn_kernel(x_ref, yc_ref, ys_ref, wot_ref, wob_ref, nx_ref, wq_ref,
                     k_ref, v_ref, wo_ref, nf_ref, wr_ref, br_ref,
                     h_ref, xs_ref, y0_ref, gs_ref, ri_ref, cnt_ref,
                     h2s_ref, tri_ref, rho_ref):
    i = pl.program_id(0)
    tm = x_ref.shape[0]
    cur = i % 2
    prv = 1 - cur
    neg = -jnp.inf
    big = float(ROUTE_LANES)

    @pl.when(i == 0)
    def _():
        h2s_ref[...] = jnp.zeros(h2s_ref.shape, F32)
        r_i = lax.broadcasted_iota(jnp.int32, (tm, tm), 0)
        c_i = lax.broadcasted_iota(jnp.int32, (tm, tm), 1)
        tri_ref[...] = jnp.where(r_i > c_i, 1.0, 0.0).astype(BF16)
        rho_ref[...] = lax.broadcasted_iota(jnp.int32, (tm, TILE_ROWS), 1).astype(F32)

    h1 = (x_ref[...] + _dot(yc_ref[...].astype(BF16), wot_ref[...])
          + _dot(ys_ref[...].astype(BF16), wob_ref[...]))
    hn = _rms(h1, nx_ref[...]).astype(BF16)
    q = _dot(hn, wq_ref[...])

    hf = _rms(h2s_ref[prv], nf_ref[...]).astype(BF16)
    logits = _dot(hf, wr_ref[...]) + br_ref[...]
    lane = lax.broadcasted_iota(jnp.int32, (tm, ROUTE_LANES), 1)
    lane_f = lane.astype(F32)

    def top1(vals):
        m = jnp.max(vals, axis=-1, keepdims=True)
        idx = jnp.min(jnp.where(vals == m, lane_f, big), axis=-1, keepdims=True)
        return m, idx

    gl = jnp.where(lane < N_EXPERT_GROUPS, logits, neg)
    gmax, gidx = top1(gl)
    g_w = 1.0 / jnp.sum(jnp.exp(gl - gmax), axis=-1, keepdims=True)
    lo = EXPERT_LANE0 + EXPERTS_PER_GROUP * gidx
    el = jnp.where((lane_f >= lo) & (lane_f < lo + EXPERTS_PER_GROUP), logits, neg)
    m1, i1 = top1(el)
    m2, i2 = top1(jnp.where(lane_f == i1, neg, el))
    e21 = jnp.exp(m2 - m1)
    gate1 = g_w / (1.0 + e21)
    gate2 = g_w * e21 / (1.0 + e21)

    heads = []
    for hd in range(N_XHEADS):
        sl = slice(hd * XHEAD_DIM, (hd + 1) * XHEAD_DIM)
        qh = q[:, sl].astype(BF16)
        s = lax.dot_general(qh, k_ref[0, :, sl], (((1,), (1,)), ((), ())),
                            preferred_element_type=F32)
        s = s - jnp.max(s, axis=-1, keepdims=True)
        p = jnp.exp(s)
        p = p / jnp.sum(p, axis=-1, keepdims=True)
        heads.append(_dot(p.astype(BF16), v_ref[0, :, sl]).astype(BF16))
    o = jnp.concatenate(heads, axis=-1)

    hot1 = lane_f == i1
    hot2 = lane_f == i2
    hot = jnp.where(hot1 | hot2, 1.0, 0.0)
    before = _dot(tri_ref[...], hot.astype(BF16))
    count = jnp.sum(hot, axis=0, keepdims=True)
    chunks = jnp.floor((count + (MOE_CHUNK - 1.0)) * (1.0 / MOE_CHUNK))
    l_i = lax.broadcasted_iota(jnp.int32, (ROUTE_LANES, ROUTE_LANES), 0)
    l_j = lax.broadcasted_iota(jnp.int32, (ROUTE_LANES, ROUTE_LANES), 1)
    upper = jnp.where(l_i < l_j, 1.0, 0.0).astype(BF16)
    first_chunk = _dot(jnp.broadcast_to(chunks, (SUBLANES, ROUTE_LANES)).astype(BF16), upper)[0:1]
    start = first_chunk * float(MOE_CHUNK) + before
    pos1 = jnp.sum(jnp.where(hot1, start, 0.0), axis=-1, keepdims=True)
    pos2 = jnp.sum(jnp.where(hot2, start, 0.0), axis=-1, keepdims=True)
    cnt_ref[0] = jnp.broadcast_to(count, (SUBLANES, ROUTE_LANES)).astype(jnp.int32)
    ri_ref[...] = jnp.where(lane == 0, pos1, jnp.where(lane == 1, pos2, 0.0)).astype(jnp.int32)

    h2 = h1 + _dot(o, wo_ref[...])
    h_ref[...] = h2
    h2s_ref[cur] = h2

    def pieces(g):
        hi = g.astype(BF16).astype(F32)
        mid = (g - hi).astype(BF16).astype(F32)
        low = (g - hi - mid).astype(BF16).astype(F32)
        return hi, mid, low

    g6 = jnp.where(lane == 6, 1.0, 0.0)
    for li, piece in enumerate(pieces(gate1) + pieces(gate2)):
        g6 = jnp.where(lane == li, piece, g6)
    rho = rho_ref[...]
    ptk = jnp.where(rho == pos1, 1.0, jnp.where(rho == pos2, 2.0, 0.0)).astype(BF16)
    rhs = jnp.concatenate([hf, g6.astype(BF16)], axis=1)
    res = lax.dot_general(ptk, rhs, (((0,), (0,)), ((), ())), preferred_element_type=F32)
    sg = res[:, D_MODEL:]
    which = sg[:, 6:7]
    srt = res[:, :D_MODEL] * jnp.where(which == 2.0, 0.5, 1.0)
    xs_ref[...] = _pack_words(srt).reshape(TILE_CHUNKS, MOE_CHUNK, D_PACK)
    y0_ref[...] = jnp.zeros(y0_ref.shape, jnp.uint32)
    first = sg[:, 0:1] + sg[:, 1:2] + sg[:, 2:3]
    second = 0.5 * (sg[:, 3:4] + sg[:, 4:5] + sg[:, 5:6])
    gsort = jnp.where(which == 1.0, first, jnp.where(which == 2.0, second, 0.0))
    gs_ref[...] = jnp.broadcast_to(gsort, (TILE_ROWS, ROUTE_LANES))


def _pack_words(v):
    hi = lax.bitcast_convert_type(v[:, :D_PACK], jnp.uint32) & jnp.uint32(HI_MASK)
    lo = lax.bitcast_convert_type(v[:, D_PACK:], jnp.uint32) >> 16
    return hi | lo


def _unpack_words(w):
    hi = lax.bitcast_convert_type(w & jnp.uint32(HI_MASK), F32)
    lo = lax.bitcast_convert_type(w << 16, F32)
    return hi, lo


def _mix_attn(xt, yc, ys, wot, wob, nx, wq, k, v, wo, nf, wr, br, seq):
    t, d = xt.shape
    nt = t // TM_MIX
    tiles_per_batch = seq // TM_MIX
    mlen = k.shape[1]
    att = lambda i: jnp.minimum(i, nt - 1)
    rte = lambda i: jnp.maximum(i - 1, 0)
    row_a = lambda i: (att(i), 0)
    row_r = lambda i: (rte(i), 0)
    const = lambda i: (0, 0)
    tile3 = lambda i: (rte(i), 0, 0)
    kvspec = pl.BlockSpec((1, mlen, d), lambda i: (att(i) // tiles_per_batch, 0, 0))
    packed = jax.ShapeDtypeStruct((nt * TILE_CHUNKS, MOE_CHUNK, D_PACK), jnp.uint32)
    return pl.pallas_call(
        _mix_attn_kernel,
        grid=(nt + 1,),
        in_specs=[
            pl.BlockSpec((TM_MIX, d), row_a),
            pl.BlockSpec((TM_MIX, D_CONV), row_a),
            pl.BlockSpec((TM_MIX, D_SSM), row_a),
            pl.BlockSpec((D_CONV, d), const),
            pl.BlockSpec((D_SSM, d), const),
            pl.BlockSpec((1, d), const),
            pl.BlockSpec((d, d), const),
            kvspec, kvspec,
            pl.BlockSpec((d, d), const),
            pl.BlockSpec((1, d), const),
            pl.BlockSpec((d, ROUTE_LANES), const),
            pl.BlockSpec((1, ROUTE_LANES), const),
        ],
        out_specs=(
            pl.BlockSpec((TM_MIX, d), row_a),
            pl.BlockSpec((TILE_CHUNKS, MOE_CHUNK, D_PACK), tile3),
            pl.BlockSpec((TILE_CHUNKS, MOE_CHUNK, D_PACK), tile3),
            pl.BlockSpec((TILE_ROWS, ROUTE_LANES), row_r),
            pl.BlockSpec((TM_MIX, ROUTE_LANES), row_r),
            pl.BlockSpec((1, SUBLANES, ROUTE_LANES), tile3),
        ),
        out_shape=(
            jax.ShapeDtypeStruct((t, d), F32),
            packed,
            packed,
            jax.ShapeDtypeStruct((nt * TILE_ROWS, ROUTE_LANES), F32),
            jax.ShapeDtypeStruct((t, ROUTE_LANES), jnp.int32),
            jax.ShapeDtypeStruct((nt, SUBLANES, ROUTE_LANES), jnp.int32),
        ),
        scratch_shapes=[
            pltpu.VMEM((2, TM_MIX, d), F32),
            pltpu.VMEM((TM_MIX, TM_MIX), BF16),
            pltpu.VMEM((TM_MIX, TILE_ROWS), F32),
        ],
        compiler_params=pltpu.CompilerParams(
            dimension_semantics=("arbitrary",), vmem_limit_bytes=62 << 20),
        name="mix_attn",
    )(xt, yc, ys, wot, wob, nx, wq, k, v, wo, nf, wr, br)


ZERO_CHUNK = TILE_CHUNKS - 1
DUMP_TILE0 = 16
SPARE_PER_TILE = TILE_CHUNKS - TILE_USED_CHUNKS


def _dump_chunk(slot, k):
    idx = slot * BLOCK_CHUNKS + k
    return (DUMP_TILE0 + idx // SPARE_PER_TILE) * TILE_CHUNKS + TILE_USED_CHUNKS + idx % SPARE_PER_TILE


def _experts_kernel(src_ref, dst_ref, be_ref, seq_ref, nxt_ref, na_ref, xs_hbm, w1_hbm, w3_hbm,
                    w2_hbm, y0_hbm, ys_hbm, xbuf, ybuf, w1f_ref, w3f_ref, w2f_ref,
                    w1s_ref, w3s_ref, w2s_ref, in_sem, out_sem, w_sem):
    del y0_hbm
    b = pl.program_id(0)
    n_active = na_ref[0]
    slot = b % 2
    other = 1 - slot

    def gather(blk, sl):
        for k in range(BLOCK_CHUNKS):
            pltpu.make_async_copy(xs_hbm.at[src_ref[blk * BLOCK_CHUNKS + k]],
                                  xbuf.at[sl, k], in_sem.at[sl]).start()

    def gather_wait(sl):
        pltpu.make_async_copy(xs_hbm.at[pl.ds(0, BLOCK_CHUNKS)], xbuf.at[sl],
                              in_sem.at[sl]).wait()

    def scatter(blk, sl):
        for k in range(BLOCK_CHUNKS):
            d = dst_ref[blk * BLOCK_CHUNKS + k]
            dump = jnp.where(sl == 0, _dump_chunk(0, k), _dump_chunk(1, k))
            pltpu.make_async_copy(ybuf.at[sl, k], ys_hbm.at[jnp.where(d < 0, dump, d)],
                                  out_sem.at[sl]).start()

    def scatter_wait(sl):
        pltpu.make_async_copy(ybuf.at[sl], ys_hbm.at[pl.ds(0, BLOCK_CHUNKS)],
                              out_sem.at[sl]).wait()

    active = b < n_active

    @pl.when(b == 0)
    def _():
        gather(0, 0)
        ybuf[...] = jnp.zeros(ybuf.shape, jnp.uint32)

    prev = be_ref[jnp.maximum(b - 1, 0)]
    fresh = (b == 0) | (be_ref[b] != prev)

    def weight_copies(e, ws):
        return [pltpu.make_async_copy(hbm.at[e], buf.at[ws], w_sem.at[ws])
                for hbm, buf in ((w1_hbm, w1f_ref), (w3_hbm, w3f_ref), (w2_hbm, w2f_ref))]

    @pl.when(active & fresh)
    def _():
        ws = seq_ref[b] % 2

        @pl.when(b == 0)
        def _():
            for cp in weight_copies(be_ref[b], ws):
                cp.start()

        for cp in weight_copies(be_ref[b], ws):
            cp.wait()

        @pl.when(nxt_ref[b] >= 0)
        def _():
            for cp in weight_copies(nxt_ref[b], 1 - ws):
                cp.start()

        w1s_ref[...] = w1f_ref[ws].astype(BF16)
        w3s_ref[...] = w3f_ref[ws].astype(BF16)
        w2s_ref[...] = w2f_ref[ws].astype(BF16)

    @pl.when(active & (b >= 1))
    def _():
        scatter_wait(slot)

    @pl.when(active)
    def _():
        gather_wait(slot)
        gather(b + 1, other)
        scatter(b, other)

        hi, lo = _unpack_words(xbuf[slot].reshape(MOE_BLOCK, D_PACK))
        xb = jnp.concatenate([hi, lo], axis=1).astype(BF16)
        h1 = _dot(xb, w1s_ref[...])
        h3 = _dot(xb, w3s_ref[...])
        hid = (h1 * jax.nn.sigmoid(h1) * h3).astype(BF16)
        y = _dot(hid, w2s_ref[...]).astype(BF16).astype(F32)
        ybuf[slot] = _pack_words(y).reshape(BLOCK_CHUNKS, MOE_CHUNK, D_PACK)

        @pl.when(b == n_active - 1)
        def _():
            scatter(b + 1, slot)
            gather_wait(other)
            scatter_wait(other)
            scatter_wait(slot)


def _experts(src, dst, block_expert, expert_seq, next_expert, n_active, xs, w1, w3, w2, y0):
    nb = block_expert.shape[0]
    d = D_MODEL
    hbm = pl.BlockSpec(memory_space=pl.ANY)
    grid_spec = pltpu.PrefetchScalarGridSpec(
        num_scalar_prefetch=6,
        grid=(nb,),
        in_specs=[hbm, hbm, hbm, hbm, hbm],
        out_specs=hbm,
        scratch_shapes=[
            pltpu.VMEM((2, BLOCK_CHUNKS, MOE_CHUNK, D_PACK), jnp.uint32),
            pltpu.VMEM((2, BLOCK_CHUNKS, MOE_CHUNK, D_PACK), jnp.uint32),
            pltpu.VMEM((2, d, D_EXPERT), F32),
            pltpu.VMEM((2, d, D_EXPERT), F32),
            pltpu.VMEM((2, D_EXPERT, d), F32),
            pltpu.VMEM((d, D_EXPERT), BF16),
            pltpu.VMEM((d, D_EXPERT), BF16),
            pltpu.VMEM((D_EXPERT, d), BF16),
            pltpu.SemaphoreType.DMA((2,)),
            pltpu.SemaphoreType.DMA((2,)),
            pltpu.SemaphoreType.DMA((2,)),
        ],
    )
    return pl.pallas_call(
        _experts_kernel,
        grid_spec=grid_spec,
        out_shape=jax.ShapeDtypeStruct(y0.shape, jnp.uint32),
        input_output_aliases={10: 0},
        compiler_params=pltpu.CompilerParams(
            dimension_semantics=("arbitrary",), vmem_limit_bytes=48 << 20),
        name="experts",
    )(src, dst, block_expert, expert_seq, next_expert, n_active, xs, w1, w3, w2, y0)


def _combine_kernel(h_ref, ri_ref, gs_ref, ys_ref, fw_ref, o_ref):
    tm = h_ref.shape[0]
    hi, lo = _unpack_words(ys_ref[...].reshape(TILE_ROWS, D_PACK))
    g = gs_ref[...]
    gw = jnp.concatenate([g] * (D_PACK // ROUTE_LANES), axis=1)
    yg = jnp.concatenate([hi * gw, lo * gw], axis=1).astype(BF16)
    pos = ri_ref[...].astype(F32)
    rho = lax.broadcasted_iota(jnp.int32, (tm, TILE_ROWS), 1).astype(F32)
    q = jnp.where((rho == pos[:, 0:1]) | (rho == pos[:, 1:2]), 1.0, 0.0).astype(BF16)
    o_ref[...] = _rms(h_ref[...] + _dot(q, yg), fw_ref[...])


def _combine(h2, route_i, gsort, ys, final_w):
    t, d = h2.shape
    nt = t // TM_MIX
    row = lambda i: (i, 0)
    return pl.pallas_call(
        _combine_kernel,
        grid=(nt,),
        in_specs=[
            pl.BlockSpec((TM_MIX, d), row),
            pl.BlockSpec((TM_MIX, ROUTE_LANES), row),
            pl.BlockSpec((TILE_ROWS, ROUTE_LANES), row),
            pl.BlockSpec((TILE_CHUNKS, MOE_CHUNK, D_PACK), lambda i: (i, 0, 0)),
            pl.BlockSpec((1, d), lambda i: (0, 0)),
        ],
        out_specs=pl.BlockSpec((TM_MIX, d), row),
        out_shape=jax.ShapeDtypeStruct((t, d), F32),
        compiler_params=pltpu.CompilerParams(
            dimension_semantics=("arbitrary",), vmem_limit_bytes=48 << 20),
        name="combine",
    )(h2, route_i, gsort, ys, final_w)


def _routing_tables(cnt):
    nt = cnt.shape[0]
    n = cnt[:, 0, EXPERT_LANE0:EXPERT_LANE0 + N_EXPERTS]
    c = (n + MOE_CHUNK - 1) // MOE_CHUNK
    local = jnp.cumsum(c, axis=1) - c
    per_expert = jnp.sum(c, axis=0)
    padded = (per_expert + BLOCK_CHUNKS - 1) // BLOCK_CHUNKS * BLOCK_CHUNKS
    gend = jnp.cumsum(padded)
    gstart = gend - padded
    within = jnp.cumsum(c, axis=0) - c
    start = (gstart[None, :] + within).T.reshape(-1)
    length = c.T.reshape(-1)
    base = (jnp.arange(nt, dtype=jnp.int32)[:, None] * TILE_CHUNKS + local).T.reshape(-1)
    g_max = (2 * nt * TM_MIX // MOE_CHUNK + nt * N_EXPERTS
             + N_EXPERTS * (BLOCK_CHUNKS - 1))
    n_blocks = -(-g_max // BLOCK_CHUNKS)
    g = jnp.arange(n_blocks * BLOCK_CHUNKS, dtype=jnp.int32)
    nxt = jnp.concatenate([start[1:], jnp.full((1,), 2 ** 30, start.dtype)])
    owner = ((start[None, :] <= g[:, None]) & (nxt[None, :] > g[:, None])).astype(jnp.int32)
    k = g - jnp.sum(owner * start[None, :], axis=1)
    valid = k < jnp.sum(owner * length[None, :], axis=1)
    chunk = jnp.sum(owner * base[None, :], axis=1) + k
    pad_src = jnp.full((BLOCK_CHUNKS,), ZERO_CHUNK, jnp.int32)
    pad_dst = jnp.full((BLOCK_CHUNKS,), -1, jnp.int32)
    src = jnp.concatenate([jnp.where(valid, chunk, ZERO_CHUNK).astype(jnp.int32), pad_src])
    dst = jnp.concatenate([pad_dst, jnp.where(valid, chunk, -1).astype(jnp.int32)])
    block_start = jnp.arange(n_blocks, dtype=jnp.int32) * BLOCK_CHUNKS
    block_expert = jnp.minimum(
        jnp.sum((block_start[:, None] >= gend[None, :]).astype(jnp.int32), axis=1),
        N_EXPERTS - 1).astype(jnp.int32)
    n_active = (gend[-1:] // BLOCK_CHUNKS).astype(jnp.int32)
    blk = jnp.arange(n_blocks, dtype=jnp.int32)
    is_first = (blk < n_active[0]) & ((blk == 0) | (block_expert != jnp.roll(block_expert, 1)))
    expert_seq = (jnp.cumsum(is_first.astype(jnp.int32)) - 1).astype(jnp.int32)
    later_first = is_first[None, :] & (blk[None, :] > blk[:, None])
    next_blk = jnp.min(jnp.where(later_first, blk[None, :], n_blocks), axis=1)
    next_expert = jnp.where(
        next_blk < n_blocks,
        jnp.sum(jnp.where(blk[None, :] == next_blk[:, None], block_expert[None, :], 0), axis=1),
        -1).astype(jnp.int32)
    return src, dst, block_expert, expert_seq, next_expert, n_active


def _layer(h, mem, p, final_w):
    bsz, seq, d = h.shape
    t = bsz * seq
    assert t // TM_MIX >= DUMP_TILE0 + 2 * BLOCK_CHUNKS // SPARE_PER_TILE
    xt = h.reshape(t, d)

    v, u = _in_proj(xt, p["norm_mix_w"].reshape(1, d), p["w_in"].astype(BF16))
    y_conv = _conv(v, p["conv_w"], p["conv_b"].reshape(1, D_CONV),
                   p["conv_ln_w"].reshape(1, D_CONV), p["conv_ln_b"].reshape(1, D_CONV),
                   bsz, seq)

    a1, pj, aj, bbar = _s5_prep(p["ssm_A_re"], p["ssm_A_im"], p["ssm_log_dt"],
                                      p["ssm_B_re"], p["ssm_B_im"])
    nlb = D_SSM // S5_LANES
    bb = bbar.reshape(2, SSM_GROUP, nlb, S5_GROUPS, SSM_STATE).transpose(0, 2, 3, 1, 4)
    bcat = jnp.concatenate([_block_diag(bb[0]), _block_diag(bb[1])], axis=-1).astype(BF16)
    c_re = p["ssm_C_re"].reshape(nlb, S5_GROUPS, SSM_GROUP, SSM_STATE).transpose(0, 1, 3, 2)
    c_im = p["ssm_C_im"].reshape(nlb, S5_GROUPS, SSM_GROUP, SSM_STATE).transpose(0, 1, 3, 2)
    ccat = jnp.concatenate([_block_diag(c_re), -_block_diag(c_im)], axis=1).astype(BF16)
    glu = p["ssm_glu_w"].reshape(nlb, S5_GROUPS, SSM_GROUP, 2 * SSM_GROUP)
    wab = jnp.concatenate([_block_diag(glu[..., :SSM_GROUP]),
                           _block_diag(glu[..., SSM_GROUP:])], axis=-1).astype(BF16)
    y_ssm = _s5(u, bcat, ccat, p["ssm_D"].reshape(1, D_SSM), wab,
                a1, pj, aj, bsz, seq)

    k, vv = _kv(mem, p["norm_mem_w"].reshape(1, d), p["xk_w"].astype(BF16),
                p["xv_w"].astype(BF16))
    w_out = p["w_out"].astype(BF16)
    wr = jnp.concatenate([p["router_group_w"], p["router_expert_w"]], axis=1)
    wr = jnp.pad(wr, ((0, 0), (0, ROUTE_LANES - wr.shape[1]))).astype(BF16)
    br = jnp.concatenate([p["router_group_b"], p["router_expert_b"].reshape(-1)])
    br = jnp.pad(br, (0, ROUTE_LANES - br.shape[0])).reshape(1, ROUTE_LANES)
    h2, xs, y0, gsort, route_i, cnt = _mix_attn(
        xt, y_conv, y_ssm, w_out[:D_CONV], w_out[D_CONV:], p["norm_x_w"].reshape(1, d),
        (p["xq_w"] * (XHEAD_DIM ** -0.5)).astype(BF16), k, vv, p["xo_w"].astype(BF16),
        p["norm_ffn_w"].reshape(1, d), wr, br, seq)

    src, dst, block_expert, expert_seq, next_expert, n_active = _routing_tables(cnt)
    ys = _experts(src, dst, block_expert, expert_seq, next_expert, n_active, xs,
                  p["moe_w1"], p["moe_w3"], p["moe_w2"], y0)
    out = _combine(h2, route_i, gsort, ys, final_w.reshape(1, d))
    return out.reshape(bsz, seq, d)


def kernel(x, mem, norm_mix_w, w_in, conv_w, conv_b, conv_ln_w, conv_ln_b, ssm_A_re, ssm_A_im, ssm_log_dt, ssm_B_re, ssm_B_im, ssm_C_re, ssm_C_im, ssm_D, ssm_glu_w, w_out, norm_x_w, norm_mem_w, xq_w, xk_w, xv_w, xo_w, norm_ffn_w, router_group_w, router_group_b, router_expert_w, router_expert_b, moe_w1, moe_w3, moe_w2, final_norm_w):
    stacked = dict(
        norm_mix_w=norm_mix_w, w_in=w_in, conv_w=conv_w, conv_b=conv_b,
        conv_ln_w=conv_ln_w, conv_ln_b=conv_ln_b, ssm_A_re=ssm_A_re, ssm_A_im=ssm_A_im,
        ssm_log_dt=ssm_log_dt, ssm_B_re=ssm_B_re, ssm_B_im=ssm_B_im, ssm_C_re=ssm_C_re,
        ssm_C_im=ssm_C_im, ssm_D=ssm_D, ssm_glu_w=ssm_glu_w, w_out=w_out,
        norm_x_w=norm_x_w, norm_mem_w=norm_mem_w, xq_w=xq_w, xk_w=xk_w, xv_w=xv_w,
        xo_w=xo_w, norm_ffn_w=norm_ffn_w, router_group_w=router_group_w,
        router_group_b=router_group_b, router_expert_w=router_expert_w,
        router_expert_b=router_expert_b, moe_w1=moe_w1, moe_w3=moe_w3, moe_w2=moe_w2)
    depth = norm_mix_w.shape[0]
    assert depth == 1, "final norm is fused into the single layer's combine step"
    layer = {name: w[0] for name, w in stacked.items()}
    return _layer(x, mem, layer, final_norm_w)
```

```python
import functools

import jax
import jax.numpy as jnp
from jax import lax
from jax.experimental import pallas as pl
from jax.experimental.pallas import tpu as pltpu

D_MODEL = 1024
D_CONV = 512
CONV_WIDTH = 31
D_SSM = 512
SSM_GROUP = 16
N_SSM_GROUPS = 32
SSM_STATE = 64
N_XHEADS = 4
XHEAD_DIM = 256
N_EXPERT_GROUPS = 4
EXPERTS_PER_GROUP = 8
N_EXPERTS = 32
D_EXPERT = 512
EPS = 1e-6

F32 = jnp.float32
BF16 = jnp.bfloat16

SUBLANES = 8
LANES = 128

TM_PROJ = 512
TM_CONV = 512
CONV_CHUNK = 128
CONV_NORM_ROWS = 128
CONV_HALO = 32
S5_STEPS = 64
S5_TILE = SUBLANES * S5_STEPS
S5_GROUP_ROWS = 128
S5_LANES = 128
S5_GROUPS = S5_LANES // SSM_GROUP
S5_STATE = S5_GROUPS * SSM_STATE
TM_MIX = 512
ROUTE_LANES = 128
EXPERT_LANE0 = N_EXPERT_GROUPS
MOE_CHUNK = SUBLANES
MOE_BLOCK = 256
BLOCK_CHUNKS = MOE_BLOCK // MOE_CHUNK
TILE_USED_CHUNKS = 2 * TM_MIX // MOE_CHUNK + N_EXPERTS * (MOE_CHUNK - 1) // MOE_CHUNK
TILE_CHUNKS = 160
TILE_ROWS = TILE_CHUNKS * MOE_CHUNK
D_PACK = D_MODEL // 2
HI_MASK = 0xFFFF0000


def _rms(x, w):
    return x * lax.rsqrt(jnp.mean(x * x, axis=-1, keepdims=True) + EPS) * w


def _dot(a, b):
    return jnp.dot(a, b, preferred_element_type=F32)


def _s5_prep_kernel(are_ref, aim_ref, ldt_ref, btre_ref, btim_ref,
                    a1_ref, pj_ref, aj_ref, bbar_ref):
    lam_re = are_ref[...]
    lam_im = aim_ref[...]
    dt = jnp.exp(ldt_ref[...])
    x = lam_re * dt
    y = lam_im * dt

    def power(k):
        mag = jnp.exp(k * x)
        return mag * jnp.cos(k * y), mag * jnp.sin(k * y)

    n = x.shape[-1]
    ones8 = jnp.ones((SUBLANES, n), F32)
    a_re, a_im = power(ones8)
    a1_ref[0] = a_re
    a1_ref[1] = a_im

    row = lax.broadcasted_iota(jnp.int32, (SUBLANES, n), 0)
    for i, d in enumerate((1, 2, 4)):
        p_re, p_im = power(ones8 * float(d * S5_STEPS))
        keep = row >= d
        pj_ref[0, i] = jnp.where(keep, p_re, 0.0)
        pj_ref[1, i] = jnp.where(keep, p_im, 0.0)
    j_re, j_im = power(ones8 * float(S5_STEPS))
    aj_ref[0] = j_re
    aj_ref[1] = j_im

    num_re = a_re[0:1] - 1.0
    num_im = a_im[0:1]
    den = lam_re * lam_re + lam_im * lam_im
    c_re = (num_re * lam_re + num_im * lam_im) / den
    c_im = (num_im * lam_re - num_re * lam_im) / den
    b_re = btre_ref[...]
    b_im = btim_ref[...]
    bbar_ref[0] = c_re * b_re - c_im * b_im
    bbar_ref[1] = c_re * b_im + c_im * b_re


def _s5_prep(a_re, a_im, log_dt, b_re, b_im):
    n = N_SSM_GROUPS * SSM_STATE
    are = a_re.reshape(1, n)
    aim = a_im.reshape(1, n)
    ldt = jnp.repeat(log_dt, SSM_STATE).reshape(1, n)
    btre = jnp.transpose(b_re, (2, 0, 1)).reshape(SSM_GROUP, n)
    btim = jnp.transpose(b_im, (2, 0, 1)).reshape(SSM_GROUP, n)
    return pl.pallas_call(
        _s5_prep_kernel,
        out_shape=(
            jax.ShapeDtypeStruct((2, SUBLANES, n), F32),
            jax.ShapeDtypeStruct((2, 3, SUBLANES, n), F32),
            jax.ShapeDtypeStruct((2, SUBLANES, n), F32),
            jax.ShapeDtypeStruct((2, SSM_GROUP, n), F32),
        ),
        name="s5_prep",
    )(are, aim, ldt, btre, btim)


def _block_diag(w):
    nl, g, r, c = w.shape
    eye = jnp.eye(g, dtype=w.dtype)
    return jnp.einsum("lgrc,gh->lgrhc", w, eye).reshape(nl, g * r, g * c)


def _in_proj_kernel(x_ref, nw_ref, w_ref, v_ref, u_ref):
    xn = _rms(x_ref[...], nw_ref[...]).astype(BF16)
    proj = _dot(xn, w_ref[...])
    a = proj[:, :D_CONV]
    g = proj[:, D_CONV:2 * D_CONV]
    v_ref[...] = a * jax.nn.sigmoid(g)
    u_ref[...] = proj[:, 2 * D_CONV:]


def _in_proj(xt, norm_w, w_in):
    t = xt.shape[0]
    n_out = 2 * D_CONV + D_SSM
    return pl.pallas_call(
        _in_proj_kernel,
        grid=(t // TM_PROJ,),
        in_specs=[
            pl.BlockSpec((TM_PROJ, D_MODEL), lambda i: (i, 0)),
            pl.BlockSpec((1, D_MODEL), lambda i: (0, 0)),
            pl.BlockSpec((D_MODEL, n_out), lambda i: (0, 0)),
        ],
        out_specs=(
            pl.BlockSpec((TM_PROJ, D_CONV), lambda i: (i, 0)),
            pl.BlockSpec((TM_PROJ, D_SSM), lambda i: (i, 0)),
        ),
        out_shape=(
            jax.ShapeDtypeStruct((t, D_CONV), F32),
            jax.ShapeDtypeStruct((t, D_SSM), F32),
        ),
        compiler_params=pltpu.CompilerParams(
            dimension_semantics=("arbitrary",), vmem_limit_bytes=40 << 20),
        name="in_proj",
    )(xt, norm_w, w_in)


def _conv_kernel(v_ref, w_ref, b_ref, lnw_ref, lnb_ref, o_ref, ext_ref, sh_ref):
    tt = pl.program_id(1)
    rows = CONV_HALO + TM_CONV

    @pl.when(tt == 0)
    def _():
        ext_ref[pl.ds(0, CONV_HALO), :] = jnp.zeros((CONV_HALO, D_CONV), F32)

    @pl.when(tt > 0)
    def _():
        ext_ref[pl.ds(0, CONV_HALO), :] = ext_ref[pl.ds(TM_CONV, CONV_HALO), :]

    ext_ref[pl.ds(CONV_HALO, TM_CONV), :] = v_ref[...]
    for s in range(1, SUBLANES):
        sh_ref[s - 1, pl.ds(0, rows - SUBLANES), :] = ext_ref[pl.ds(s, rows - SUBLANES), :]
    bias = b_ref[...]
    lnw = lnw_ref[...]
    lnb = lnb_ref[...]
    tap0 = CONV_HALO - (CONV_WIDTH - 1)

    groups = CONV_CHUNK // SUBLANES

    def chunk(ci, carry):
        base = pl.multiple_of(ci * CONV_CHUNK, CONV_CHUNK)
        for lt in range(D_CONV // LANES):
            lanes = pl.ds(lt * LANES, LANES)
            acc = [jnp.broadcast_to(bias[:, lt * LANES:(lt + 1) * LANES], (SUBLANES, LANES))] * groups
            for s in range(SUBLANES):
                taps = [j for j in range(CONV_WIDTH) if (tap0 + j) % SUBLANES == s]
                src = ext_ref if s == 0 else sh_ref.at[s - 1]
                ngroups = (tap0 + taps[-1] - s) // SUBLANES + groups
                win = [src[pl.ds(base + SUBLANES * g, SUBLANES), lanes] for g in range(ngroups)]
                for j in taps:
                    g0 = (tap0 + j - s) // SUBLANES
                    wj = w_ref[j, :, lanes]
                    acc = [acc[r] + wj * win[g0 + r] for r in range(groups)]
            o_ref[pl.ds(base, CONV_CHUNK), lanes] = jnp.concatenate(acc, axis=0)
        return carry

    lax.fori_loop(0, TM_CONV // CONV_CHUNK, chunk, 0)

    for bi in range(TM_CONV // CONV_NORM_ROWS):
        rows_b = pl.ds(bi * CONV_NORM_ROWS, CONV_NORM_ROWS)
        acc = o_ref[rows_b, :]
        mu = jnp.mean(acc, axis=-1, keepdims=True)
        cen = acc - mu
        var = jnp.mean(cen * cen, axis=-1, keepdims=True)
        z = cen * lax.rsqrt(var + EPS) * lnw + lnb
        o_ref[rows_b, :] = z * jax.nn.sigmoid(z)


def _conv(v, conv_w, conv_b, ln_w, ln_b, bsz, seq):
    nt = seq // TM_CONV
    row = lambda b, t: (b * nt + t, 0)
    const = lambda b, t: (0, 0)
    return pl.pallas_call(
        _conv_kernel,
        grid=(bsz, nt),
        in_specs=[
            pl.BlockSpec((TM_CONV, D_CONV), row),
            pl.BlockSpec((CONV_WIDTH, SUBLANES, D_CONV), lambda b, t: (0, 0, 0)),
            pl.BlockSpec((1, D_CONV), const),
            pl.BlockSpec((1, D_CONV), const),
            pl.BlockSpec((1, D_CONV), const),
        ],
        out_specs=pl.BlockSpec((TM_CONV, D_CONV), row),
        out_shape=jax.ShapeDtypeStruct(v.shape, F32),
        scratch_shapes=[
            pltpu.VMEM((CONV_HALO + TM_CONV, D_CONV), F32),
            pltpu.VMEM((SUBLANES - 1, CONV_HALO + TM_CONV, D_CONV), F32),
        ],
        compiler_params=pltpu.CompilerParams(
            dimension_semantics=("arbitrary", "arbitrary")),
        name="conv",
    )(v, jnp.broadcast_to(conv_w[:, None, :], (CONV_WIDTH, SUBLANES, D_CONV)), conv_b, ln_w, ln_b)


def _cmul(a_re, a_im, b_re, b_im):
    return a_re * b_re - a_im * b_im, a_re * b_im + a_im * b_re


def _s5_kernel(u_ref, bcat_ref, ccat_ref, d_ref, wab_ref, a1_ref,
               pj_ref, aj_ref, o_ref, up_ref, bu_ref, st_ref, carry_ref):
    tt = pl.program_id(1)
    ns = S5_STATE
    nseq = u_ref.shape[0]
    steps_per_group = S5_GROUP_ROWS // SUBLANES
    n_groups = S5_TILE // S5_GROUP_ROWS
    cur = tt % 2
    prv = 1 - cur

    @pl.when((pl.program_id(0) == 0) & (tt == 0))
    def _():
        up_ref[...] = jnp.zeros(up_ref.shape, F32)
        st_ref[...] = jnp.zeros(st_ref.shape, BF16)

    @pl.when(tt == 0)
    def _():
        carry_ref[...] = jnp.zeros(carry_ref.shape, F32)

    for q in range(nseq):
        for j in range(S5_STEPS):
            up_ref[cur, q, pl.ds(SUBLANES * j, SUBLANES), :] = (
                u_ref[q, pl.ds(j, SUBLANES, stride=S5_STEPS), :])

    a_re = a1_ref[0]
    a_im = a1_ref[1]

    def step(q, j, s):
        rows = pl.ds(j * SUBLANES, SUBLANES)
        m_re, m_im = _cmul(a_re, a_im, s[0], s[1])
        return m_re + bu_ref[q, rows, pl.ds(0, ns)], m_im + bu_ref[q, rows, pl.ds(ns, ns)]

    def project_out(q, g):
        rows = pl.ds(g * S5_GROUP_ROWS, S5_GROUP_ROWS)
        y = _dot(st_ref[prv, q, rows, :], ccat_ref[0]) + d_ref[...] * up_ref[prv, q, rows, :]
        y = jax.nn.gelu(y)
        ab = _dot(y.astype(BF16), wab_ref[0])
        out = ab[:, :S5_LANES] * jax.nn.sigmoid(ab[:, S5_LANES:])
        for jj in range(steps_per_group):
            j = g * steps_per_group + jj
            o_ref[q, pl.ds(j, SUBLANES, stride=S5_STEPS), :] = out[jj * SUBLANES:(jj + 1) * SUBLANES, :]

    zero = jnp.zeros((SUBLANES, ns), F32)
    state = [(zero, zero)] * nseq
    for g in range(n_groups):
        rows = pl.ds(g * S5_GROUP_ROWS, S5_GROUP_ROWS)
        for q in range(nseq):
            bu_ref[q, rows, :] = _dot(up_ref[cur, q, rows, :].astype(BF16), bcat_ref[0])
        for jj in range(steps_per_group):
            state = [step(q, g * steps_per_group + jj, state[q]) for q in range(nseq)]
        for q in range(nseq):
            project_out(q, g)

    row = lax.broadcasted_iota(jnp.int32, (SUBLANES, ns), 0)
    first = row == 0
    entry = []
    for q in range(nseq):
        e_re, e_im = state[q]
        c_re = jnp.where(first, pltpu.roll(carry_ref[q, 0], 1, 0), pltpu.roll(e_re, 1, 0))
        c_im = jnp.where(first, pltpu.roll(carry_ref[q, 1], 1, 0), pltpu.roll(e_im, 1, 0))
        for i, d in enumerate((1, 2, 4)):
            r_re = pltpu.roll(c_re, d, 0)
            r_im = pltpu.roll(c_im, d, 0)
            m_re, m_im = _cmul(pj_ref[0, i], pj_ref[1, i], r_re, r_im)
            c_re = c_re + m_re
            c_im = c_im + m_im
        f_re, f_im = _cmul(aj_ref[0], aj_ref[1], c_re, c_im)
        carry_ref[q, 0] = f_re + e_re
        carry_ref[q, 1] = f_im + e_im
        entry.append((c_re, c_im))

    state = entry
    pack = 2 * SUBLANES
    for j in range(0, S5_STEPS, 2):
        mid = [step(q, j, state[q]) for q in range(nseq)]
        state = [step(q, j + 1, mid[q]) for q in range(nseq)]
        for q in range(nseq):
            st_ref[cur, q, pl.ds(j * SUBLANES, pack), pl.ds(0, ns)] = (
                jnp.concatenate([mid[q][0], state[q][0]], axis=0).astype(BF16))
            st_ref[cur, q, pl.ds(j * SUBLANES, pack), pl.ds(ns, ns)] = (
                jnp.concatenate([mid[q][1], state[q][1]], axis=0).astype(BF16))


def _s5(u, bcat, ccat, d, wab, a1, pj, aj, bsz, seq):
    nt = seq // S5_TILE
    nlb = D_SSM // S5_LANES
    ns = S5_STATE
    u3 = u.reshape(bsz, seq, D_SSM)
    row_in = lambda l, t: (0, jnp.minimum(t, nt - 1), l)
    row_out = lambda l, t: (0, jnp.maximum(t - 1, 0), l)
    lane3 = lambda l, t: (0, 0, l)
    lane4 = lambda l, t: (0, 0, 0, l)
    out = pl.pallas_call(
        _s5_kernel,
        grid=(nlb, nt + 1),
        in_specs=[
            pl.BlockSpec((bsz, S5_TILE, S5_LANES), row_in),
            pl.BlockSpec((1, S5_LANES, 2 * ns), lambda l, t: (l, 0, 0)),
            pl.BlockSpec((1, 2 * ns, S5_LANES), lambda l, t: (l, 0, 0)),
            pl.BlockSpec((1, S5_LANES), lambda l, t: (0, l)),
            pl.BlockSpec((1, S5_LANES, 2 * S5_LANES), lambda l, t: (l, 0, 0)),
            pl.BlockSpec((2, SUBLANES, ns), lane3),
            pl.BlockSpec((2, 3, SUBLANES, ns), lane4),
            pl.BlockSpec((2, SUBLANES, ns), lane3),
        ],
        out_specs=pl.BlockSpec((bsz, S5_TILE, S5_LANES), row_out),
        out_shape=jax.ShapeDtypeStruct(u3.shape, F32),
        scratch_shapes=[
            pltpu.VMEM((2, bsz, S5_TILE, S5_LANES), F32),
            pltpu.VMEM((bsz, S5_TILE, 2 * ns), F32),
            pltpu.VMEM((2, bsz, S5_TILE, 2 * ns), BF16),
            pltpu.VMEM((bsz, 2, SUBLANES, ns), F32),
        ],
        compiler_params=pltpu.CompilerParams(
            dimension_semantics=("arbitrary", "arbitrary"),
            vmem_limit_bytes=40 << 20),
        name="s5",
    )(u3, bcat, ccat, d, wab, a1, pj, aj)
    return out.reshape(u.shape)


def _kv_kernel(m_ref, nw_ref, wk_ref, wv_ref, k_ref, v_ref):
    mn = _rms(m_ref[0], nw_ref[...]).astype(BF16)
    k_ref[0] = _dot(mn, wk_ref[...]).astype(BF16)
    v_ref[0] = _dot(mn, wv_ref[...]).astype(BF16)


def _kv(mem, norm_w, wk, wv):
    bsz, mlen, d = mem.shape
    blk = pl.BlockSpec((1, mlen, d), lambda b: (b, 0, 0))
    wspec = pl.BlockSpec((d, d), lambda b: (0, 0))
    return pl.pallas_call(
        _kv_kernel,
        grid=(bsz,),
        in_specs=[blk, pl.BlockSpec((1, d), lambda b: (0, 0)), wspec, wspec],
        out_specs=(blk, blk),
        out_shape=(jax.ShapeDtypeStruct(mem.shape, BF16),) * 2,
        compiler_params=pltpu.CompilerParams(
            dimension_semantics=("arbitrary",), vmem_limit_bytes=40 << 20),
        name="kv",
    )(mem, norm_w, wk, wv)


def _mix_attn_kernel(x_ref, yc_ref, ys_ref, wot_ref, wob_ref, nx_ref, wq_ref,
                     k_ref, v_ref, wo_ref, nf_ref, wr_ref, br_ref,
                     h_ref, xs_ref, y0_ref, gs_ref, ri_ref, cnt_ref,
                     h2s_ref, tri_ref, rho_ref):
    i = pl.program_id(0)
    tm = x_ref.shape[0]
    cur = i % 2
    prv = 1 - cur
    neg = -jnp.inf
    big = float(ROUTE_LANES)

    @pl.when(i == 0)
    def _():
        h2s_ref[...] = jnp.zeros(h2s_ref.shape, F32)
        r_i = lax.broadcasted_iota(jnp.int32, (tm, tm), 0)
        c_i = lax.broadcasted_iota(jnp.int32, (tm, tm), 1)
        tri_ref[...] = jnp.where(r_i > c_i, 1.0, 0.0).astype(BF16)
        rho_ref[...] = lax.broadcasted_iota(jnp.int32, (tm, TILE_ROWS), 1).astype(F32)

    h1 = (x_ref[...] + _dot(yc_ref[...].astype(BF16), wot_ref[...])
          + _dot(ys_ref[...].astype(BF16), wob_ref[...]))
    hn = _rms(h1, nx_ref[...]).astype(BF16)
    q = _dot(hn, wq_ref[...])

    hf = _rms(h2s_ref[prv], nf_ref[...]).astype(BF16)
    logits = _dot(hf, wr_ref[...]) + br_ref[...]
    lane = lax.broadcasted_iota(jnp.int32, (tm, ROUTE_LANES), 1)
    lane_f = lane.astype(F32)

    def top1(vals):
        m = jnp.max(vals, axis=-1, keepdims=True)
        idx = jnp.min(jnp.where(vals == m, lane_f, big), axis=-1, keepdims=True)
        return m, idx

    gl = jnp.where(lane < N_EXPERT_GROUPS, logits, neg)
    gmax, gidx = top1(gl)
    g_w = 1.0 / jnp.sum(jnp.exp(gl - gmax), axis=-1, keepdims=True)
    lo = EXPERT_LANE0 + EXPERTS_PER_GROUP * gidx
    el = jnp.where((lane_f >= lo) & (lane_f < lo + EXPERTS_PER_GROUP), logits, neg)
    m1, i1 = top1(el)
    m2, i2 = top1(jnp.where(lane_f == i1, neg, el))
    e21 = jnp.exp(m2 - m1)
    gate1 = g_w / (1.0 + e21)
    gate2 = g_w * e21 / (1.0 + e21)

    heads = []
    for hd in range(N_XHEADS):
        sl = slice(hd * XHEAD_DIM, (hd + 1) * XHEAD_DIM)
        qh = q[:, sl].astype(BF16)
        s = lax.dot_general(qh, k_ref[0, :, sl], (((1,), (1,)), ((), ())),
                            preferred_element_type=F32)
        s = s - jnp.max(s, axis=-1, keepdims=True)
        p = jnp.exp(s)
        p = p / jnp.sum(p, axis=-1, keepdims=True)
        heads.append(_dot(p.astype(BF16), v_ref[0, :, sl]).astype(BF16))
    o = jnp.concatenate(heads, axis=-1)

    hot1 = lane_f == i1
    hot2 = lane_f == i2
    hot = jnp.where(hot1 | hot2, 1.0, 0.0)
    before = _dot(tri_ref[...], hot.astype(BF16))
    count = jnp.sum(hot, axis=0, keepdims=True)
    chunks = jnp.floor((count + (MOE_CHUNK - 1.0)) * (1.0 / MOE_CHUNK))
    l_i = lax.broadcasted_iota(jnp.int32, (ROUTE_LANES, ROUTE_LANES), 0)
    l_j = lax.broadcasted_iota(jnp.int32, (ROUTE_LANES, ROUTE_LANES), 1)
    upper = jnp.where(l_i < l_j, 1.0, 0.0).astype(BF16)
    first_chunk = _dot(jnp.broadcast_to(chunks, (SUBLANES, ROUTE_LANES)).astype(BF16), upper)[0:1]
    start = first_chunk * float(MOE_CHUNK) + before
    pos1 = jnp.sum(jnp.where(hot1, start, 0.0), axis=-1, keepdims=True)
    pos2 = jnp.sum(jnp.where(hot2, start, 0.0), axis=-1, keepdims=True)
    cnt_ref[0] = jnp.broadcast_to(count, (SUBLANES, ROUTE_LANES)).astype(jnp.int32)
    ri_ref[...] = jnp.where(lane == 0, pos1, jnp.where(lane == 1, pos2, 0.0)).astype(jnp.int32)

    h2 = h1 + _dot(o, wo_ref[...])
    h_ref[...] = h2
    h2s_ref[cur] = h2

    def pieces(g):
        hi = g.astype(BF16).astype(F32)
        mid = (g - hi).astype(BF16).astype(F32)
        low = (g - hi - mid).astype(BF16).astype(F32)
        return hi, mid, low

    g6 = jnp.where(lane == 6, 1.0, 0.0)
    for li, piece in enumerate(pieces(gate1) + pieces(gate2)):
        g6 = jnp.where(lane == li, piece, g6)
    rho = rho_ref[...]
    ptk = jnp.where(rho == pos1, 1.0, jnp.where(rho == pos2, 2.0, 0.0)).astype(BF16)
    rhs = jnp.concatenate([hf, g6.astype(BF16)], axis=1)
    res = lax.dot_general(ptk, rhs, (((0,), (0,)), ((), ())), preferred_element_type=F32)
    sg = res[:, D_MODEL:]
    which = sg[:, 6:7]
    srt = res[:, :D_MODEL] * jnp.where(which == 2.0, 0.5, 1.0)
    xs_ref[...] = _pack_words(srt).reshape(TILE_CHUNKS, MOE_CHUNK, D_PACK)
    y0_ref[...] = jnp.zeros(y0_ref.shape, jnp.uint32)
    first = sg[:, 0:1] + sg[:, 1:2] + sg[:, 2:3]
    second = 0.5 * (sg[:, 3:4] + sg[:, 4:5] + sg[:, 5:6])
    gsort = jnp.where(which == 1.0, first, jnp.where(which == 2.0, second, 0.0))
    gs_ref[...] = jnp.broadcast_to(gsort, (TILE_ROWS, ROUTE_LANES))


def _pack_words(v):
    hi = lax.bitcast_convert_type(v[:, :D_PACK], jnp.uint32) & jnp.uint32(HI_MASK)
    lo = lax.bitcast_convert_type(v[:, D_PACK:], jnp.uint32) >> 16
    return hi | lo


def _unpack_words(w):
    hi = lax.bitcast_convert_type(w & jnp.uint32(HI_MASK), F32)
    lo = lax.bitcast_convert_type(w << 16, F32)
    return hi, lo


def _mix_attn(xt, yc, ys, wot, wob, nx, wq, k, v, wo, nf, wr, br, seq):
    t, d = xt.shape
    nt = t // TM_MIX
    tiles_per_batch = seq // TM_MIX
    mlen = k.shape[1]
    att = lambda i: jnp.minimum(i, nt - 1)
    rte = lambda i: jnp.maximum(i - 1, 0)
    row_a = lambda i: (att(i), 0)
    row_r = lambda i: (rte(i), 0)
    const = lambda i: (0, 0)
    tile3 = lambda i: (rte(i), 0, 0)
    kvspec = pl.BlockSpec((1, mlen, d), lambda i: (att(i) // tiles_per_batch, 0, 0))
    packed = jax.ShapeDtypeStruct((nt * TILE_CHUNKS, MOE_CHUNK, D_PACK), jnp.uint32)
    return pl.pallas_call(
        _mix_attn_kernel,
        grid=(nt + 1,),
        in_specs=[
            pl.BlockSpec((TM_MIX, d), row_a),
            pl.BlockSpec((TM_MIX, D_CONV), row_a),
            pl.BlockSpec((TM_MIX, D_SSM), row_a),
            pl.BlockSpec((D_CONV, d), const),
            pl.BlockSpec((D_SSM, d), const),
            pl.BlockSpec((1, d), const),
            pl.BlockSpec((d, d), const),
            kvspec, kvspec,
            pl.BlockSpec((d, d), const),
            pl.BlockSpec((1, d), const),
            pl.BlockSpec((d, ROUTE_LANES), const),
            pl.BlockSpec((1, ROUTE_LANES), const),
        ],
        out_specs=(
            pl.BlockSpec((TM_MIX, d), row_a),
            pl.BlockSpec((TILE_CHUNKS, MOE_CHUNK, D_PACK), tile3),
            pl.BlockSpec((TILE_CHUNKS, MOE_CHUNK, D_PACK), tile3),
            pl.BlockSpec((TILE_ROWS, ROUTE_LANES), row_r),
            pl.BlockSpec((TM_MIX, ROUTE_LANES), row_r),
            pl.BlockSpec((1, SUBLANES, ROUTE_LANES), tile3),
        ),
        out_shape=(
            jax.ShapeDtypeStruct((t, d), F32),
            packed,
            packed,
            jax.ShapeDtypeStruct((nt * TILE_ROWS, ROUTE_LANES), F32),
            jax.ShapeDtypeStruct((t, ROUTE_LANES), jnp.int32),
            jax.ShapeDtypeStruct((nt, SUBLANES, ROUTE_LANES), jnp.int32),
        ),
        scratch_shapes=[
            pltpu.VMEM((2, TM_MIX, d), F32),
            pltpu.VMEM((TM_MIX, TM_MIX), BF16),
            pltpu.VMEM((TM_MIX, TILE_ROWS), F32),
        ],
        compiler_params=pltpu.CompilerParams(
            dimension_semantics=("arbitrary",), vmem_limit_bytes=62 << 20),
        name="mix_attn",
    )(xt, yc, ys, wot, wob, nx, wq, k, v, wo, nf, wr, br)


ZERO_CHUNK = TILE_CHUNKS - 1
DUMP_TILE0 = 16
SPARE_PER_TILE = TILE_CHUNKS - TILE_USED_CHUNKS


def _dump_chunk(slot, k):
    idx = slot * BLOCK_CHUNKS + k
    return (DUMP_TILE0 + idx // SPARE_PER_TILE) * TILE_CHUNKS + TILE_USED_CHUNKS + idx % SPARE_PER_TILE


def _experts_kernel(src_ref, dst_ref, be_ref, seq_ref, nxt_ref, na_ref, xs_hbm, w1_hbm, w3_hbm,
                    w2_hbm, y0_hbm, ys_hbm, xbuf, ybuf, w1f_ref, w3f_ref, w2f_ref,
                    w1s_ref, w3s_ref, w2s_ref, in_sem, out_sem, w_sem):
    del y0_hbm
    b = pl.program_id(0)
    n_active = na_ref[0]
    slot = b % 2
    other = 1 - slot

    def gather(blk, sl):
        for k in range(BLOCK_CHUNKS):
            pltpu.make_async_copy(xs_hbm.at[src_ref[blk * BLOCK_CHUNKS + k]],
                                  xbuf.at[sl, k], in_sem.at[sl]).start()

    def gather_wait(sl):
        pltpu.make_async_copy(xs_hbm.at[pl.ds(0, BLOCK_CHUNKS)], xbuf.at[sl],
                              in_sem.at[sl]).wait()

    def scatter(blk, sl):
        for k in range(BLOCK_CHUNKS):
            d = dst_ref[blk * BLOCK_CHUNKS + k]
            dump = jnp.where(sl == 0, _dump_chunk(0, k), _dump_chunk(1, k))
            pltpu.make_async_copy(ybuf.at[sl, k], ys_hbm.at[jnp.where(d < 0, dump, d)],
                                  out_sem.at[sl]).start()

    def scatter_wait(sl):
        pltpu.make_async_copy(ybuf.at[sl], ys_hbm.at[pl.ds(0, BLOCK_CHUNKS)],
                              out_sem.at[sl]).wait()

    active = b < n_active

    @pl.when(b == 0)
    def _():
        gather(0, 0)
        ybuf[...] = jnp.zeros(ybuf.shape, jnp.uint32)

    prev = be_ref[jnp.maximum(b - 1, 0)]
    fresh = (b == 0) | (be_ref[b] != prev)

    def weight_copies(e, ws):
        return [pltpu.make_async_copy(hbm.at[e], buf.at[ws], w_sem.at[ws])
                for hbm, buf in ((w1_hbm, w1f_ref), (w3_hbm, w3f_ref), (w2_hbm, w2f_ref))]

    @pl.when(active & fresh)
    def _():
        ws = seq_ref[b] % 2

        @pl.when(b == 0)
        def _():
            for cp in weight_copies(be_ref[b], ws):
                cp.start(priority=1)

        for cp in weight_copies(be_ref[b], ws):
            cp.wait()

        @pl.when(nxt_ref[b] >= 0)
        def _():
            for cp in weight_copies(nxt_ref[b], 1 - ws):
                cp.start(priority=1)

        w1s_ref[...] = w1f_ref[ws].astype(BF16)
        w3s_ref[...] = w3f_ref[ws].astype(BF16)
        w2s_ref[...] = w2f_ref[ws].astype(BF16)

    @pl.when(active & (b >= 1))
    def _():
        scatter_wait(slot)

    @pl.when(active)
    def _():
        gather_wait(slot)
        gather(b + 1, other)
        scatter(b, other)

        hi, lo = _unpack_words(xbuf[slot].reshape(MOE_BLOCK, D_PACK))
        xb = jnp.concatenate([hi, lo], axis=1).astype(BF16)
        h1 = _dot(xb, w1s_ref[...])
        h3 = _dot(xb, w3s_ref[...])
        hid = (h1 * jax.nn.sigmoid(h1) * h3).astype(BF16)
        y = _dot(hid, w2s_ref[...]).astype(BF16).astype(F32)
        ybuf[slot] = _pack_words(y).reshape(BLOCK_CHUNKS, MOE_CHUNK, D_PACK)

        @pl.when(b == n_active - 1)
        def _():
            scatter(b + 1, slot)
            gather_wait(other)
            scatter_wait(other)
            scatter_wait(slot)


def _experts(src, dst, block_expert, expert_seq, next_expert, n_active, xs, w1, w3, w2, y0):
    nb = block_expert.shape[0]
    d = D_MODEL
    hbm = pl.BlockSpec(memory_space=pl.ANY)
    grid_spec = pltpu.PrefetchScalarGridSpec(
        num_scalar_prefetch=6,
        grid=(nb,),
        in_specs=[hbm, hbm, hbm, hbm, hbm],
        out_specs=hbm,
        scratch_shapes=[
            pltpu.VMEM((2, BLOCK_CHUNKS, MOE_CHUNK, D_PACK), jnp.uint32),
            pltpu.VMEM((2, BLOCK_CHUNKS, MOE_CHUNK, D_PACK), jnp.uint32),
            pltpu.VMEM((2, d, D_EXPERT), F32),
            pltpu.VMEM((2, d, D_EXPERT), F32),
            pltpu.VMEM((2, D_EXPERT, d), F32),
            pltpu.VMEM((d, D_EXPERT), BF16),
            pltpu.VMEM((d, D_EXPERT), BF16),
            pltpu.VMEM((D_EXPERT, d), BF16),
            pltpu.SemaphoreType.DMA((2,)),
            pltpu.SemaphoreType.DMA((2,)),
            pltpu.SemaphoreType.DMA((2,)),
        ],
    )
    return pl.pallas_call(
        _experts_kernel,
        grid_spec=grid_spec,
        out_shape=jax.ShapeDtypeStruct(y0.shape, jnp.uint32),
        input_output_aliases={10: 0},
        compiler_params=pltpu.CompilerParams(
            dimension_semantics=("arbitrary",), vmem_limit_bytes=48 << 20),
        name="experts",
    )(src, dst, block_expert, expert_seq, next_expert, n_active, xs, w1, w3, w2, y0)


def _combine_kernel(h_ref, ri_ref, gs_ref, ys_ref, fw_ref, o_ref):
    tm = h_ref.shape[0]
    hi, lo = _unpack_words(ys_ref[...].reshape(TILE_ROWS, D_PACK))
    g = gs_ref[...]
    gw = jnp.concatenate([g] * (D_PACK // ROUTE_LANES), axis=1)
    yg = jnp.concatenate([hi * gw, lo * gw], axis=1).astype(BF16)
    pos = ri_ref[...].astype(F32)
    rho = lax.broadcasted_iota(jnp.int32, (tm, TILE_ROWS), 1).astype(F32)
    q = jnp.where((rho == pos[:, 0:1]) | (rho == pos[:, 1:2]), 1.0, 0.0).astype(BF16)
    o_ref[...] = _rms(h_ref[...] + _dot(q, yg), fw_ref[...])


def _combine(h2, route_i, gsort, ys, final_w):
    t, d = h2.shape
    nt = t // TM_MIX
    row = lambda i: (i, 0)
    return pl.pallas_call(
        _combine_kernel,
        grid=(nt,),
        in_specs=[
            pl.BlockSpec((TM_MIX, d), row),
            pl.BlockSpec((TM_MIX, ROUTE_LANES), row),
            pl.BlockSpec((TILE_ROWS, ROUTE_LANES), row),
            pl.BlockSpec((TILE_CHUNKS, MOE_CHUNK, D_PACK), lambda i: (i, 0, 0)),
            pl.BlockSpec((1, d), lambda i: (0, 0)),
        ],
        out_specs=pl.BlockSpec((TM_MIX, d), row),
        out_shape=jax.ShapeDtypeStruct((t, d), F32),
        compiler_params=pltpu.CompilerParams(
            dimension_semantics=("arbitrary",), vmem_limit_bytes=48 << 20),
        name="combine",
    )(h2, route_i, gsort, ys, final_w)


def _routing_tables(cnt):
    nt = cnt.shape[0]
    n = cnt[:, 0, EXPERT_LANE0:EXPERT_LANE0 + N_EXPERTS]
    c = (n + MOE_CHUNK - 1) // MOE_CHUNK
    local = jnp.cumsum(c, axis=1) - c
    per_expert = jnp.sum(c, axis=0)
    padded = (per_expert + BLOCK_CHUNKS - 1) // BLOCK_CHUNKS * BLOCK_CHUNKS
    gend = jnp.cumsum(padded)
    gstart = gend - padded
    within = jnp.cumsum(c, axis=0) - c
    start = (gstart[None, :] + within).T.reshape(-1)
    length = c.T.reshape(-1)
    base = (jnp.arange(nt, dtype=jnp.int32)[:, None] * TILE_CHUNKS + local).T.reshape(-1)
    g_max = (2 * nt * TM_MIX // MOE_CHUNK + nt * N_EXPERTS
             + N_EXPERTS * (BLOCK_CHUNKS - 1))
    n_blocks = -(-g_max // BLOCK_CHUNKS)
    g = jnp.arange(n_blocks * BLOCK_CHUNKS, dtype=jnp.int32)
    nxt = jnp.concatenate([start[1:], jnp.full((1,), 2 ** 30, start.dtype)])
    owner = ((start[None, :] <= g[:, None]) & (nxt[None, :] > g[:, None])).astype(jnp.int32)
    k = g - jnp.sum(owner * start[None, :], axis=1)
    valid = k < jnp.sum(owner * length[None, :], axis=1)
    chunk = jnp.sum(owner * base[None, :], axis=1) + k
    pad_src = jnp.full((BLOCK_CHUNKS,), ZERO_CHUNK, jnp.int32)
    pad_dst = jnp.full((BLOCK_CHUNKS,), -1, jnp.int32)
    src = jnp.concatenate([jnp.where(valid, chunk, ZERO_CHUNK).astype(jnp.int32), pad_src])
    dst = jnp.concatenate([pad_dst, jnp.where(valid, chunk, -1).astype(jnp.int32)])
    block_start = jnp.arange(n_blocks, dtype=jnp.int32) * BLOCK_CHUNKS
    block_expert = jnp.minimum(
        jnp.sum((block_start[:, None] >= gend[None, :]).astype(jnp.int32), axis=1),
        N_EXPERTS - 1).astype(jnp.int32)
    n_active = (gend[-1:] // BLOCK_CHUNKS).astype(jnp.int32)
    blk = jnp.arange(n_blocks, dtype=jnp.int32)
    is_first = (blk < n_active[0]) & ((blk == 0) | (block_expert != jnp.roll(block_expert, 1)))
    expert_seq = (jnp.cumsum(is_first.astype(jnp.int32)) - 1).astype(jnp.int32)
    later_first = is_first[None, :] & (blk[None, :] > blk[:, None])
    next_blk = jnp.min(jnp.where(later_first, blk[None, :], n_blocks), axis=1)
    next_expert = jnp.where(
        next_blk < n_blocks,
        jnp.sum(jnp.where(blk[None, :] == next_blk[:, None], block_expert[None, :], 0), axis=1),
        -1).astype(jnp.int32)
    return src, dst, block_expert, expert_seq, next_expert, n_active


def _layer(h, mem, p, final_w):
    bsz, seq, d = h.shape
    t = bsz * seq
    assert t // TM_MIX >= DUMP_TILE0 + 2 * BLOCK_CHUNKS // SPARE_PER_TILE
    xt = h.reshape(t, d)

    v, u = _in_proj(xt, p["norm_mix_w"].reshape(1, d), p["w_in"].astype(BF16))
    y_conv = _conv(v, p["conv_w"], p["conv_b"].reshape(1, D_CONV),
                   p["conv_ln_w"].reshape(1, D_CONV), p["conv_ln_b"].reshape(1, D_CONV),
                   bsz, seq)

    a1, pj, aj, bbar = _s5_prep(p["ssm_A_re"], p["ssm_A_im"], p["ssm_log_dt"],
                                      p["ssm_B_re"], p["ssm_B_im"])
    nlb = D_SSM // S5_LANES
    bb = bbar.reshape(2, SSM_GROUP, nlb, S5_GROUPS, SSM_STATE).transpose(0, 2, 3, 1, 4)
    bcat = jnp.concatenate([_block_diag(bb[0]), _block_diag(bb[1])], axis=-1).astype(BF16)
    c_re = p["ssm_C_re"].reshape(nlb, S5_GROUPS, SSM_GROUP, SSM_STATE).transpose(0, 1, 3, 2)
    c_im = p["ssm_C_im"].reshape(nlb, S5_GROUPS, SSM_GROUP, SSM_STATE).transpose(0, 1, 3, 2)
    ccat = jnp.concatenate([_block_diag(c_re), -_block_diag(c_im)], axis=1).astype(BF16)
    glu = p["ssm_glu_w"].reshape(nlb, S5_GROUPS, SSM_GROUP, 2 * SSM_GROUP)
    wab = jnp.concatenate([_block_diag(glu[..., :SSM_GROUP]),
                           _block_diag(glu[..., SSM_GROUP:])], axis=-1).astype(BF16)
    y_ssm = _s5(u, bcat, ccat, p["ssm_D"].reshape(1, D_SSM), wab,
                a1, pj, aj, bsz, seq)

    k, vv = _kv(mem, p["norm_mem_w"].reshape(1, d), p["xk_w"].astype(BF16),
                p["xv_w"].astype(BF16))
    w_out = p["w_out"].astype(BF16)
    wr = jnp.concatenate([p["router_group_w"], p["router_expert_w"]], axis=1)
    wr = jnp.pad(wr, ((0, 0), (0, ROUTE_LANES - wr.shape[1]))).astype(BF16)
    br = jnp.concatenate([p["router_group_b"], p["router_expert_b"].reshape(-1)])
    br = jnp.pad(br, (0, ROUTE_LANES - br.shape[0])).reshape(1, ROUTE_LANES)
    h2, xs, y0, gsort, route_i, cnt = _mix_attn(
        xt, y_conv, y_ssm, w_out[:D_CONV], w_out[D_CONV:], p["norm_x_w"].reshape(1, d),
        (p["xq_w"] * (XHEAD_DIM ** -0.5)).astype(BF16), k, vv, p["xo_w"].astype(BF16),
        p["norm_ffn_w"].reshape(1, d), wr, br, seq)

    src, dst, block_expert, expert_seq, next_expert, n_active = _routing_tables(cnt)
    ys = _experts(src, dst, block_expert, expert_seq, next_expert, n_active, xs,
                  p["moe_w1"], p["moe_w3"], p["moe_w2"], y0)
    out = _combine(h2, route_i, gsort, ys, final_w.reshape(1, d))
    return out.reshape(bsz, seq, d)


def kernel(x, mem, norm_mix_w, w_in, conv_w, conv_b, conv_ln_w, conv_ln_b, ssm_A_re, ssm_A_im, ssm_log_dt, ssm_B_re, ssm_B_im, ssm_C_re, ssm_C_im, ssm_D, ssm_glu_w, w_out, norm_x_w, norm_mem_w, xq_w, xk_w, xv_w, xo_w, norm_ffn_w, router_group_w, router_group_b, router_expert_w, router_expert_b, moe_w1, moe_w3, moe_w2, final_norm_w):
    stacked = dict(
        norm_mix_w=norm_mix_w, w_in=w_in, conv_w=conv_w, conv_b=conv_b,
        conv_ln_w=conv_ln_w, conv_ln_b=conv_ln_b, ssm_A_re=ssm_A_re, ssm_A_im=ssm_A_im,
        ssm_log_dt=ssm_log_dt, ssm_B_re=ssm_B_re, ssm_B_im=ssm_B_im, ssm_C_re=ssm_C_re,
        ssm_C_im=ssm_C_im, ssm_D=ssm_D, ssm_glu_w=ssm_glu_w, w_out=w_out,
        norm_x_w=norm_x_w, norm_mem_w=norm_mem_w, xq_w=xq_w, xk_w=xk_w, xv_w=xv_w,
        xo_w=xo_w, norm_ffn_w=norm_ffn_w, router_group_w=router_group_w,
        router_group_b=router_group_b, router_expert_w=router_expert_w,
        router_expert_b=router_expert_b, moe_w1=moe_w1, moe_w3=moe_w3, moe_w2=moe_w2)
    depth = norm_mix_w.shape[0]
    assert depth == 1, "final norm is fused into the single layer's combine step"
    layer = {name: w[0] for name, w in stacked.items()}
    return _layer(x, mem, layer, final_norm_w)
```

```python
import functools

import jax
import jax.numpy as jnp
from jax import lax
from jax.experimental import pallas as pl
from jax.experimental.pallas import tpu as pltpu

D_MODEL = 1024
D_CONV = 512
CONV_WIDTH = 31
D_SSM = 512
SSM_GROUP = 16
N_SSM_GROUPS = 32
SSM_STATE = 64
N_XHEADS = 4
XHEAD_DIM = 256
N_EXPERT_GROUPS = 4
EXPERTS_PER_GROUP = 8
N_EXPERTS = 32
D_EXPERT = 512
EPS = 1e-6

F32 = jnp.float32
BF16 = jnp.bfloat16

SUBLANES = 8
LANES = 128

TM_PROJ = 512
TM_CONV = 512
CONV_CHUNK = 128
CONV_NORM_ROWS = 128
CONV_HALO = 32
S5_STEPS = 64
S5_TILE = SUBLANES * S5_STEPS
S5_GROUP_ROWS = 256
S5_LANES = 128
S5_GROUPS = S5_LANES // SSM_GROUP
S5_STATE = S5_GROUPS * SSM_STATE
TM_MIX = 512
ROUTE_LANES = 128
EXPERT_LANE0 = N_EXPERT_GROUPS
MOE_CHUNK = SUBLANES
MOE_BLOCK = 256
BLOCK_CHUNKS = MOE_BLOCK // MOE_CHUNK
TILE_USED_CHUNKS = 2 * TM_MIX // MOE_CHUNK + N_EXPERTS * (MOE_CHUNK - 1) // MOE_CHUNK
TILE_CHUNKS = 160
TILE_ROWS = TILE_CHUNKS * MOE_CHUNK
D_PACK = D_MODEL // 2
HI_MASK = 0xFFFF0000


def _rms(x, w):
    return x * lax.rsqrt(jnp.mean(x * x, axis=-1, keepdims=True) + EPS) * w


def _dot(a, b):
    return jnp.dot(a, b, preferred_element_type=F32)


def _s5_prep_kernel(are_ref, aim_ref, ldt_ref, btre_ref, btim_ref,
                    a1_ref, pj_ref, aj_ref, bbar_ref):
    lam_re = are_ref[...]
    lam_im = aim_ref[...]
    dt = jnp.exp(ldt_ref[...])
    x = lam_re * dt
    y = lam_im * dt

    def power(k):
        mag = jnp.exp(k * x)
        return mag * jnp.cos(k * y), mag * jnp.sin(k * y)

    n = x.shape[-1]
    ones8 = jnp.ones((SUBLANES, n), F32)
    a_re, a_im = power(ones8)
    a1_ref[0] = a_re
    a1_ref[1] = a_im

    row = lax.broadcasted_iota(jnp.int32, (SUBLANES, n), 0)
    for i, d in enumerate((1, 2, 4)):
        p_re, p_im = power(ones8 * float(d * S5_STEPS))
        keep = row >= d
        pj_ref[0, i] = jnp.where(keep, p_re, 0.0)
        pj_ref[1, i] = jnp.where(keep, p_im, 0.0)
    j_re, j_im = power(ones8 * float(S5_STEPS))
    aj_ref[0] = j_re
    aj_ref[1] = j_im

    num_re = a_re[0:1] - 1.0
    num_im = a_im[0:1]
    den = lam_re * lam_re + lam_im * lam_im
    c_re = (num_re * lam_re + num_im * lam_im) / den
    c_im = (num_im * lam_re - num_re * lam_im) / den
    b_re = btre_ref[...]
    b_im = btim_ref[...]
    bbar_ref[0] = c_re * b_re - c_im * b_im
    bbar_ref[1] = c_re * b_im + c_im * b_re


def _s5_prep(a_re, a_im, log_dt, b_re, b_im):
    n = N_SSM_GROUPS * SSM_STATE
    are = a_re.reshape(1, n)
    aim = a_im.reshape(1, n)
    ldt = jnp.repeat(log_dt, SSM_STATE).reshape(1, n)
    btre = jnp.transpose(b_re, (2, 0, 1)).reshape(SSM_GROUP, n)
    btim = jnp.transpose(b_im, (2, 0, 1)).reshape(SSM_GROUP, n)
    return pl.pallas_call(
        _s5_prep_kernel,
        out_shape=(
            jax.ShapeDtypeStruct((2, SUBLANES, n), F32),
            jax.ShapeDtypeStruct((2, 3, SUBLANES, n), F32),
            jax.ShapeDtypeStruct((2, SUBLANES, n), F32),
            jax.ShapeDtypeStruct((2, SSM_GROUP, n), F32),
        ),
        name="s5_prep",
    )(are, aim, ldt, btre, btim)


def _block_diag(w):
    nl, g, r, c = w.shape
    eye = jnp.eye(g, dtype=w.dtype)
    return jnp.einsum("lgrc,gh->lgrhc", w, eye).reshape(nl, g * r, g * c)


def _in_proj_kernel(x_ref, nw_ref, w_ref, v_ref, u_ref):
    xn = _rms(x_ref[...], nw_ref[...]).astype(BF16)
    proj = _dot(xn, w_ref[...])
    a = proj[:, :D_CONV]
    g = proj[:, D_CONV:2 * D_CONV]
    v_ref[...] = a * jax.nn.sigmoid(g)
    for lt in range(D_SSM // LANES):
        c0 = 2 * D_CONV + lt * LANES
        for r in range(SUBLANES):
            u_ref[lt, pl.ds(r, S5_STEPS, stride=SUBLANES), :] = (
                proj[r * S5_STEPS:(r + 1) * S5_STEPS, c0:c0 + LANES])


def _in_proj(xt, norm_w, w_in):
    t = xt.shape[0]
    n_out = 2 * D_CONV + D_SSM
    return pl.pallas_call(
        _in_proj_kernel,
        grid=(t // TM_PROJ,),
        in_specs=[
            pl.BlockSpec((TM_PROJ, D_MODEL), lambda i: (i, 0)),
            pl.BlockSpec((1, D_MODEL), lambda i: (0, 0)),
            pl.BlockSpec((D_MODEL, n_out), lambda i: (0, 0)),
        ],
        out_specs=(
            pl.BlockSpec((TM_PROJ, D_CONV), lambda i: (i, 0)),
            pl.BlockSpec((D_SSM // LANES, TM_PROJ, LANES), lambda i: (0, i, 0)),
        ),
        out_shape=(
            jax.ShapeDtypeStruct((t, D_CONV), F32),
            jax.ShapeDtypeStruct((D_SSM // LANES, t, LANES), F32),
        ),
        compiler_params=pltpu.CompilerParams(
            dimension_semantics=("arbitrary",), vmem_limit_bytes=40 << 20),
        name="in_proj",
    )(xt, norm_w, w_in)


def _conv_kernel(v_ref, w_ref, b_ref, lnw_ref, lnb_ref, o_ref, ext_ref, sh_ref):
    tt = pl.program_id(1)
    rows = CONV_HALO + TM_CONV

    @pl.when(tt == 0)
    def _():
        ext_ref[pl.ds(0, CONV_HALO), :] = jnp.zeros((CONV_HALO, D_CONV), F32)

    @pl.when(tt > 0)
    def _():
        ext_ref[pl.ds(0, CONV_HALO), :] = ext_ref[pl.ds(TM_CONV, CONV_HALO), :]

    ext_ref[pl.ds(CONV_HALO, TM_CONV), :] = v_ref[...]
    for s in range(1, SUBLANES):
        sh_ref[s - 1, pl.ds(0, rows - SUBLANES), :] = ext_ref[pl.ds(s, rows - SUBLANES), :]
    bias = b_ref[...]
    lnw = lnw_ref[...]
    lnb = lnb_ref[...]
    tap0 = CONV_HALO - (CONV_WIDTH - 1)

    groups = CONV_CHUNK // SUBLANES

    def chunk(ci, carry):
        base = pl.multiple_of(ci * CONV_CHUNK, CONV_CHUNK)
        for lt in range(D_CONV // LANES):
            lanes = pl.ds(lt * LANES, LANES)
            acc = [jnp.broadcast_to(bias[:, lt * LANES:(lt + 1) * LANES], (SUBLANES, LANES))] * groups
            for s in range(SUBLANES):
                taps = [j for j in range(CONV_WIDTH) if (tap0 + j) % SUBLANES == s]
                src = ext_ref if s == 0 else sh_ref.at[s - 1]
                ngroups = (tap0 + taps[-1] - s) // SUBLANES + groups
                win = [src[pl.ds(base + SUBLANES * g, SUBLANES), lanes] for g in range(ngroups)]
                for j in taps:
                    g0 = (tap0 + j - s) // SUBLANES
                    wj = w_ref[j, :, lanes]
                    acc = [acc[r] + wj * win[g0 + r] for r in range(groups)]
            o_ref[pl.ds(base, CONV_CHUNK), lanes] = jnp.concatenate(acc, axis=0)
        return carry

    lax.fori_loop(0, TM_CONV // CONV_CHUNK, chunk, 0)

    for bi in range(TM_CONV // CONV_NORM_ROWS):
        rows_b = pl.ds(bi * CONV_NORM_ROWS, CONV_NORM_ROWS)
        acc = o_ref[rows_b, :]
        mu = jnp.mean(acc, axis=-1, keepdims=True)
        cen = acc - mu
        var = jnp.mean(cen * cen, axis=-1, keepdims=True)
        z = cen * lax.rsqrt(var + EPS) * lnw + lnb
        o_ref[rows_b, :] = z * jax.nn.sigmoid(z)


def _conv(v, conv_w, conv_b, ln_w, ln_b, bsz, seq):
    nt = seq // TM_CONV
    row = lambda b, t: (b * nt + t, 0)
    const = lambda b, t: (0, 0)
    return pl.pallas_call(
        _conv_kernel,
        grid=(bsz, nt),
        in_specs=[
            pl.BlockSpec((TM_CONV, D_CONV), row),
            pl.BlockSpec((CONV_WIDTH, SUBLANES, D_CONV), lambda b, t: (0, 0, 0)),
            pl.BlockSpec((1, D_CONV), const),
            pl.BlockSpec((1, D_CONV), const),
            pl.BlockSpec((1, D_CONV), const),
        ],
        out_specs=pl.BlockSpec((TM_CONV, D_CONV), row),
        out_shape=jax.ShapeDtypeStruct(v.shape, F32),
        scratch_shapes=[
            pltpu.VMEM((CONV_HALO + TM_CONV, D_CONV), F32),
            pltpu.VMEM((SUBLANES - 1, CONV_HALO + TM_CONV, D_CONV), F32),
        ],
        compiler_params=pltpu.CompilerParams(
            dimension_semantics=("arbitrary", "arbitrary")),
        name="conv",
    )(v, jnp.broadcast_to(conv_w[:, None, :], (CONV_WIDTH, SUBLANES, D_CONV)), conv_b, ln_w, ln_b)


def _cmul(a_re, a_im, b_re, b_im):
    return a_re * b_re - a_im * b_im, a_re * b_im + a_im * b_re


def _s5_kernel(u_ref, bcat_ref, ccat_ref, d_ref, wab_ref, a1_ref,
               pj_ref, aj_ref, o_ref, up_ref, bu_ref, st_ref, carry_ref):
    tt = pl.program_id(1)
    ns = S5_STATE
    nseq = u_ref.shape[0]
    steps_per_group = S5_GROUP_ROWS // SUBLANES
    n_groups = S5_TILE // S5_GROUP_ROWS
    cur = tt % 2
    prv = 1 - cur

    @pl.when((pl.program_id(0) == 0) & (tt == 0))
    def _():
        up_ref[...] = jnp.zeros(up_ref.shape, F32)
        st_ref[...] = jnp.zeros(st_ref.shape, BF16)

    @pl.when(tt == 0)
    def _():
        carry_ref[...] = jnp.zeros(carry_ref.shape, F32)

    up_ref[cur] = u_ref[...]

    a_re = a1_ref[0]
    a_im = a1_ref[1]

    def step(q, j, s):
        rows = pl.ds(j * SUBLANES, SUBLANES)
        m_re, m_im = _cmul(a_re, a_im, s[0], s[1])
        return m_re + bu_ref[q, rows, pl.ds(0, ns)], m_im + bu_ref[q, rows, pl.ds(ns, ns)]

    def project_out(q, g):
        rows = pl.ds(g * S5_GROUP_ROWS, S5_GROUP_ROWS)
        y = _dot(st_ref[prv, q, rows, :], ccat_ref[0]) + d_ref[...] * up_ref[prv, q, rows, :]
        y = jax.nn.gelu(y)
        ab = _dot(y.astype(BF16), wab_ref[0])
        o_ref[q, rows, :] = ab[:, :S5_LANES] * jax.nn.sigmoid(ab[:, S5_LANES:])

    zero = jnp.zeros((SUBLANES, ns), F32)
    state = [(zero, zero)] * nseq
    for g in range(n_groups):
        rows = pl.ds(g * S5_GROUP_ROWS, S5_GROUP_ROWS)
        for q in range(nseq):
            bu_ref[q, rows, :] = _dot(up_ref[cur, q, rows, :].astype(BF16), bcat_ref[0])
        for jj in range(steps_per_group):
            state = [step(q, g * steps_per_group + jj, state[q]) for q in range(nseq)]
        for q in range(nseq):
            project_out(q, g)

    row = lax.broadcasted_iota(jnp.int32, (SUBLANES, ns), 0)
    first = row == 0
    entry = []
    for q in range(nseq):
        e_re, e_im = state[q]
        c_re = jnp.where(first, pltpu.roll(carry_ref[q, 0], 1, 0), pltpu.roll(e_re, 1, 0))
        c_im = jnp.where(first, pltpu.roll(carry_ref[q, 1], 1, 0), pltpu.roll(e_im, 1, 0))
        for i, d in enumerate((1, 2, 4)):
            r_re = pltpu.roll(c_re, d, 0)
            r_im = pltpu.roll(c_im, d, 0)
            m_re, m_im = _cmul(pj_ref[0, i], pj_ref[1, i], r_re, r_im)
            c_re = c_re + m_re
            c_im = c_im + m_im
        f_re, f_im = _cmul(aj_ref[0], aj_ref[1], c_re, c_im)
        carry_ref[q, 0] = f_re + e_re
        carry_ref[q, 1] = f_im + e_im
        entry.append((c_re, c_im))

    state = entry
    pack = 2 * SUBLANES
    for j in range(0, S5_STEPS, 2):
        mid = [step(q, j, state[q]) for q in range(nseq)]
        state = [step(q, j + 1, mid[q]) for q in range(nseq)]
        for q in range(nseq):
            st_ref[cur, q, pl.ds(j * SUBLANES, pack), pl.ds(0, ns)] = (
                jnp.concatenate([mid[q][0], state[q][0]], axis=0).astype(BF16))
            st_ref[cur, q, pl.ds(j * SUBLANES, pack), pl.ds(ns, ns)] = (
                jnp.concatenate([mid[q][1], state[q][1]], axis=0).astype(BF16))


def _s5(u, bcat, ccat, d, wab, a1, pj, aj, bsz, seq):
    nt = seq // S5_TILE
    nlb = D_SSM // S5_LANES
    ns = S5_STATE
    assert S5_LANES == LANES and S5_TILE == TM_PROJ == TM_MIX
    u4 = u.reshape(nlb, bsz, seq, S5_LANES)
    row_in = lambda l, t: (l, 0, jnp.minimum(t, nt - 1), 0)
    row_out = lambda l, t: (l, 0, jnp.maximum(t - 1, 0), 0)
    lane3 = lambda l, t: (0, 0, l)
    lane4 = lambda l, t: (0, 0, 0, l)
    out = pl.pallas_call(
        _s5_kernel,
        grid=(nlb, nt + 1),
        in_specs=[
            pl.BlockSpec((None, bsz, S5_TILE, S5_LANES), row_in),
            pl.BlockSpec((1, S5_LANES, 2 * ns), lambda l, t: (l, 0, 0)),
            pl.BlockSpec((1, 2 * ns, S5_LANES), lambda l, t: (l, 0, 0)),
            pl.BlockSpec((1, S5_LANES), lambda l, t: (0, l)),
            pl.BlockSpec((1, S5_LANES, 2 * S5_LANES), lambda l, t: (l, 0, 0)),
            pl.BlockSpec((2, SUBLANES, ns), lane3),
            pl.BlockSpec((2, 3, SUBLANES, ns), lane4),
            pl.BlockSpec((2, SUBLANES, ns), lane3),
        ],
        out_specs=pl.BlockSpec((None, bsz, S5_TILE, S5_LANES), row_out),
        out_shape=jax.ShapeDtypeStruct(u4.shape, F32),
        scratch_shapes=[
            pltpu.VMEM((2, bsz, S5_TILE, S5_LANES), F32),
            pltpu.VMEM((bsz, S5_TILE, 2 * ns), F32),
            pltpu.VMEM((2, bsz, S5_TILE, 2 * ns), BF16),
            pltpu.VMEM((bsz, 2, SUBLANES, ns), F32),
        ],
        compiler_params=pltpu.CompilerParams(
            dimension_semantics=("arbitrary", "arbitrary"),
            vmem_limit_bytes=40 << 20),
        name="s5",
    )(u4, bcat, ccat, d, wab, a1, pj, aj)
    return out.reshape(nlb, bsz * seq, S5_LANES)


def _kv_kernel(m_ref, nw_ref, wk_ref, wv_ref, k_ref, v_ref):
    mn = _rms(m_ref[0], nw_ref[...]).astype(BF16)
    k_ref[0] = _dot(mn, wk_ref[...]).astype(BF16)
    v_ref[0] = _dot(mn, wv_ref[...]).astype(BF16)


def _kv(mem, norm_w, wk, wv):
    bsz, mlen, d = mem.shape
    blk = pl.BlockSpec((1, mlen, d), lambda b: (b, 0, 0))
    wspec = pl.BlockSpec((d, d), lambda b: (0, 0))
    return pl.pallas_call(
        _kv_kernel,
        grid=(bsz,),
        in_specs=[blk, pl.BlockSpec((1, d), lambda b: (0, 0)), wspec, wspec],
        out_specs=(blk, blk),
        out_shape=(jax.ShapeDtypeStruct(mem.shape, BF16),) * 2,
        compiler_params=pltpu.CompilerParams(
            dimension_semantics=("arbitrary",), vmem_limit_bytes=40 << 20),
        name="kv",
    )(mem, norm_w, wk, wv)


def _mix_attn_kernel(x_ref, yc_ref, ys_ref, wot_ref, wob_ref, nx_ref, wq_ref,
                     k_ref, v_ref, wo_ref, nf_ref, wr_ref, br_ref,
                     h_ref, xs_ref, y0_ref, gs_ref, ri_ref, cnt_ref,
                     h2s_ref, tri_ref, rho_ref):
    i = pl.program_id(0)
    tm = x_ref.shape[0]
    cur = i % 2
    prv = 1 - cur
    neg = -jnp.inf
    big = float(ROUTE_LANES)

    @pl.when(i == 0)
    def _():
        h2s_ref[...] = jnp.zeros(h2s_ref.shape, F32)
        r_i = lax.broadcasted_iota(jnp.int32, (tm, tm), 0)
        c_i = lax.broadcasted_iota(jnp.int32, (tm, tm), 1)
        tri_ref[...] = jnp.where(r_i > c_i, 1.0, 0.0).astype(BF16)
        rho_ref[...] = lax.broadcasted_iota(jnp.int32, (tm, TILE_ROWS), 1).astype(F32)

    ys_time = jnp.concatenate(
        [jnp.concatenate([ys_ref[lt, pl.ds(r, S5_STEPS, stride=SUBLANES), :]
                          for r in range(SUBLANES)], axis=0)
         for lt in range(D_SSM // LANES)], axis=1)
    h1 = (x_ref[...] + _dot(yc_ref[...].astype(BF16), wot_ref[...])
          + _dot(ys_time.astype(BF16), wob_ref[...]))
    hn = _rms(h1, nx_ref[...]).astype(BF16)
    q = _dot(hn, wq_ref[...])

    hf = _rms(h2s_ref[prv], nf_ref[...]).astype(BF16)
    logits = _dot(hf, wr_ref[...]) + br_ref[...]
    lane = lax.broadcasted_iota(jnp.int32, (tm, ROUTE_LANES), 1)
    lane_f = lane.astype(F32)

    def top1(vals):
        m = jnp.max(vals, axis=-1, keepdims=True)
        idx = jnp.min(jnp.where(vals == m, lane_f, big), axis=-1, keepdims=True)
        return m, idx

    gl = jnp.where(lane < N_EXPERT_GROUPS, logits, neg)
    gmax, gidx = top1(gl)
    g_w = 1.0 / jnp.sum(jnp.exp(gl - gmax), axis=-1, keepdims=True)
    lo = EXPERT_LANE0 + EXPERTS_PER_GROUP * gidx
    el = jnp.where((lane_f >= lo) & (lane_f < lo + EXPERTS_PER_GROUP), logits, neg)
    m1, i1 = top1(el)
    m2, i2 = top1(jnp.where(lane_f == i1, neg, el))
    e21 = jnp.exp(m2 - m1)
    gate1 = g_w / (1.0 + e21)
    gate2 = g_w * e21 / (1.0 + e21)

    heads = []
    for hd in range(N_XHEADS):
        sl = slice(hd * XHEAD_DIM, (hd + 1) * XHEAD_DIM)
        qh = q[:, sl].astype(BF16)
        s = lax.dot_general(qh, k_ref[0, :, sl], (((1,), (1,)), ((), ())),
                            preferred_element_type=F32)
        s = s - jnp.max(s, axis=-1, keepdims=True)
        p = jnp.exp(s)
        p = p / jnp.sum(p, axis=-1, keepdims=True)
        heads.append(_dot(p.astype(BF16), v_ref[0, :, sl]).astype(BF16))
    o = jnp.concatenate(heads, axis=-1)

    hot1 = lane_f == i1
    hot2 = lane_f == i2
    hot = jnp.where(hot1 | hot2, 1.0, 0.0)
    before = _dot(tri_ref[...], hot.astype(BF16))
    count = jnp.sum(hot, axis=0, keepdims=True)
    chunks = jnp.floor((count + (MOE_CHUNK - 1.0)) * (1.0 / MOE_CHUNK))
    l_i = lax.broadcasted_iota(jnp.int32, (ROUTE_LANES, ROUTE_LANES), 0)
    l_j = lax.broadcasted_iota(jnp.int32, (ROUTE_LANES, ROUTE_LANES), 1)
    upper = jnp.where(l_i < l_j, 1.0, 0.0).astype(BF16)
    first_chunk = _dot(jnp.broadcast_to(chunks, (SUBLANES, ROUTE_LANES)).astype(BF16), upper)[0:1]
    start = first_chunk * float(MOE_CHUNK) + before
    pos1 = jnp.sum(jnp.where(hot1, start, 0.0), axis=-1, keepdims=True)
    pos2 = jnp.sum(jnp.where(hot2, start, 0.0), axis=-1, keepdims=True)
    cnt_ref[0] = jnp.broadcast_to(count, (SUBLANES, ROUTE_LANES)).astype(jnp.int32)
    ri_ref[...] = jnp.where(lane == 0, pos1, jnp.where(lane == 1, pos2, 0.0)).astype(jnp.int32)

    h2 = h1 + _dot(o, wo_ref[...])
    h_ref[...] = h2
    h2s_ref[cur] = h2

    def pieces(g):
        hi = g.astype(BF16).astype(F32)
        mid = (g - hi).astype(BF16).astype(F32)
        low = (g - hi - mid).astype(BF16).astype(F32)
        return hi, mid, low

    g6 = jnp.where(lane == 6, 1.0, 0.0)
    for li, piece in enumerate(pieces(gate1) + pieces(gate2)):
        g6 = jnp.where(lane == li, piece, g6)
    rho = rho_ref[...]
    ptk = jnp.where(rho == pos1, 1.0, jnp.where(rho == pos2, 2.0, 0.0)).astype(BF16)
    rhs = jnp.concatenate([hf, g6.astype(BF16)], axis=1)
    res = lax.dot_general(ptk, rhs, (((0,), (0,)), ((), ())), preferred_element_type=F32)
    sg = res[:, D_MODEL:]
    which = sg[:, 6:7]
    srt = res[:, :D_MODEL] * jnp.where(which == 2.0, 0.5, 1.0)
    xs_ref[...] = _pack_words(srt).reshape(TILE_CHUNKS, MOE_CHUNK, D_PACK)
    y0_ref[...] = jnp.zeros(y0_ref.shape, jnp.uint32)
    first = sg[:, 0:1] + sg[:, 1:2] + sg[:, 2:3]
    second = 0.5 * (sg[:, 3:4] + sg[:, 4:5] + sg[:, 5:6])
    gsort = jnp.where(which == 1.0, first, jnp.where(which == 2.0, second, 0.0))
    gs_ref[...] = jnp.broadcast_to(gsort, (TILE_ROWS, ROUTE_LANES))


def _pack_words(v):
    hi = lax.bitcast_convert_type(v[:, :D_PACK], jnp.uint32) & jnp.uint32(HI_MASK)
    lo = lax.bitcast_convert_type(v[:, D_PACK:], jnp.uint32) >> 16
    return hi | lo


def _unpack_words(w):
    hi = lax.bitcast_convert_type(w & jnp.uint32(HI_MASK), F32)
    lo = lax.bitcast_convert_type(w << 16, F32)
    return hi, lo


def _mix_attn(xt, yc, ys, wot, wob, nx, wq, k, v, wo, nf, wr, br, seq):
    t, d = xt.shape
    nt = t // TM_MIX
    tiles_per_batch = seq // TM_MIX
    mlen = k.shape[1]
    att = lambda i: jnp.minimum(i, nt - 1)
    rte = lambda i: jnp.maximum(i - 1, 0)
    row_a = lambda i: (att(i), 0)
    row_r = lambda i: (rte(i), 0)
    const = lambda i: (0, 0)
    tile3 = lambda i: (rte(i), 0, 0)
    kvspec = pl.BlockSpec((1, mlen, d), lambda i: (att(i) // tiles_per_batch, 0, 0))
    packed = jax.ShapeDtypeStruct((nt * TILE_CHUNKS, MOE_CHUNK, D_PACK), jnp.uint32)
    return pl.pallas_call(
        _mix_attn_kernel,
        grid=(nt + 1,),
        in_specs=[
            pl.BlockSpec((TM_MIX, d), row_a),
            pl.BlockSpec((TM_MIX, D_CONV), row_a),
            pl.BlockSpec((D_SSM // LANES, TM_MIX, LANES), lambda i: (0, att(i), 0)),
            pl.BlockSpec((D_CONV, d), const),
            pl.BlockSpec((D_SSM, d), const),
            pl.BlockSpec((1, d), const),
            pl.BlockSpec((d, d), const),
            kvspec, kvspec,
            pl.BlockSpec((d, d), const),
            pl.BlockSpec((1, d), const),
            pl.BlockSpec((d, ROUTE_LANES), const),
            pl.BlockSpec((1, ROUTE_LANES), const),
        ],
        out_specs=(
            pl.BlockSpec((TM_MIX, d), row_a),
            pl.BlockSpec((TILE_CHUNKS, MOE_CHUNK, D_PACK), tile3),
            pl.BlockSpec((TILE_CHUNKS, MOE_CHUNK, D_PACK), tile3),
            pl.BlockSpec((TILE_ROWS, ROUTE_LANES), row_r),
            pl.BlockSpec((TM_MIX, ROUTE_LANES), row_r),
            pl.BlockSpec((1, SUBLANES, ROUTE_LANES), tile3),
        ),
        out_shape=(
            jax.ShapeDtypeStruct((t, d), F32),
            packed,
            packed,
            jax.ShapeDtypeStruct((nt * TILE_ROWS, ROUTE_LANES), F32),
            jax.ShapeDtypeStruct((t, ROUTE_LANES), jnp.int32),
            jax.ShapeDtypeStruct((nt, SUBLANES, ROUTE_LANES), jnp.int32),
        ),
        scratch_shapes=[
            pltpu.VMEM((2, TM_MIX, d), F32),
            pltpu.VMEM((TM_MIX, TM_MIX), BF16),
            pltpu.VMEM((TM_MIX, TILE_ROWS), F32),
        ],
        compiler_params=pltpu.CompilerParams(
            dimension_semantics=("arbitrary",), vmem_limit_bytes=62 << 20),
        name="mix_attn",
    )(xt, yc, ys, wot, wob, nx, wq, k, v, wo, nf, wr, br)


ZERO_CHUNK = TILE_CHUNKS - 1
DUMP_TILE0 = 16
SPARE_PER_TILE = TILE_CHUNKS - TILE_USED_CHUNKS


def _dump_chunk(slot, k):
    idx = slot * BLOCK_CHUNKS + k
    return (DUMP_TILE0 + idx // SPARE_PER_TILE) * TILE_CHUNKS + TILE_USED_CHUNKS + idx % SPARE_PER_TILE


def _experts_kernel(src_ref, dst_ref, be_ref, seq_ref, nxt_ref, na_ref, xs_hbm, w1_hbm, w3_hbm,
                    w2_hbm, y0_hbm, ys_hbm, xbuf, ybuf, w1f_ref, w3f_ref, w2f_ref,
                    w1s_ref, w3s_ref, w2s_ref, in_sem, out_sem, w_sem):
    del y0_hbm
    b = pl.program_id(0)
    n_active = na_ref[0]
    slot = b % 2
    other = 1 - slot

    def gather(blk, sl):
        for k in range(BLOCK_CHUNKS):
            pltpu.make_async_copy(xs_hbm.at[src_ref[blk * BLOCK_CHUNKS + k]],
                                  xbuf.at[sl, k], in_sem.at[sl]).start()

    def gather_wait(sl):
        pltpu.make_async_copy(xs_hbm.at[pl.ds(0, BLOCK_CHUNKS)], xbuf.at[sl],
                              in_sem.at[sl]).wait()

    def scatter(blk, sl):
        for k in range(BLOCK_CHUNKS):
            d = dst_ref[blk * BLOCK_CHUNKS + k]
            dump = jnp.where(sl == 0, _dump_chunk(0, k), _dump_chunk(1, k))
            pltpu.make_async_copy(ybuf.at[sl, k], ys_hbm.at[jnp.where(d < 0, dump, d)],
                                  out_sem.at[sl]).start()

    def scatter_wait(sl):
        pltpu.make_async_copy(ybuf.at[sl], ys_hbm.at[pl.ds(0, BLOCK_CHUNKS)],
                              out_sem.at[sl]).wait()

    active = b < n_active

    @pl.when(b == 0)
    def _():
        gather(0, 0)
        ybuf[...] = jnp.zeros(ybuf.shape, jnp.uint32)

    prev = be_ref[jnp.maximum(b - 1, 0)]
    fresh = (b == 0) | (be_ref[b] != prev)

    def weight_copies(e, ws):
        return [pltpu.make_async_copy(hbm.at[e], buf.at[ws], w_sem.at[ws])
                for hbm, buf in ((w1_hbm, w1f_ref), (w3_hbm, w3f_ref), (w2_hbm, w2f_ref))]

    @pl.when(active & fresh)
    def _():
        ws = seq_ref[b] % 2

        @pl.when(b == 0)
        def _():
            for cp in weight_copies(be_ref[b], ws):
                cp.start(priority=1)

        for cp in weight_copies(be_ref[b], ws):
            cp.wait()

        @pl.when(nxt_ref[b] >= 0)
        def _():
            for cp in weight_copies(nxt_ref[b], 1 - ws):
                cp.start(priority=1)

        w1s_ref[...] = w1f_ref[ws].astype(BF16)
        w3s_ref[...] = w3f_ref[ws].astype(BF16)
        w2s_ref[...] = w2f_ref[ws].astype(BF16)

    @pl.when(active & (b >= 1))
    def _():
        scatter_wait(slot)

    @pl.when(active)
    def _():
        gather_wait(slot)
        gather(b + 1, other)
        scatter(b, other)

        hi, lo = _unpack_words(xbuf[slot].reshape(MOE_BLOCK, D_PACK))
        xb = jnp.concatenate([hi, lo], axis=1).astype(BF16)
        h1 = _dot(xb, w1s_ref[...])
        h3 = _dot(xb, w3s_ref[...])
        hid = (h1 * jax.nn.sigmoid(h1) * h3).astype(BF16)
        y = _dot(hid, w2s_ref[...]).astype(BF16).astype(F32)
        ybuf[slot] = _pack_words(y).reshape(BLOCK_CHUNKS, MOE_CHUNK, D_PACK)

        @pl.when(b == n_active - 1)
        def _():
            scatter(b + 1, slot)
            gather_wait(other)
            scatter_wait(other)
            scatter_wait(slot)


def _experts(src, dst, block_expert, expert_seq, next_expert, n_active, xs, w1, w3, w2, y0):
    nb = block_expert.shape[0]
    d = D_MODEL
    hbm = pl.BlockSpec(memory_space=pl.ANY)
    grid_spec = pltpu.PrefetchScalarGridSpec(
        num_scalar_prefetch=6,
        grid=(nb,),
        in_specs=[hbm, hbm, hbm, hbm, hbm],
        out_specs=hbm,
        scratch_shapes=[
            pltpu.VMEM((2, BLOCK_CHUNKS, MOE_CHUNK, D_PACK), jnp.uint32),
            pltpu.VMEM((2, BLOCK_CHUNKS, MOE_CHUNK, D_PACK), jnp.uint32),
            pltpu.VMEM((2, d, D_EXPERT), F32),
            pltpu.VMEM((2, d, D_EXPERT), F32),
            pltpu.VMEM((2, D_EXPERT, d), F32),
            pltpu.VMEM((d, D_EXPERT), BF16),
            pltpu.VMEM((d, D_EXPERT), BF16),
            pltpu.VMEM((D_EXPERT, d), BF16),
            pltpu.SemaphoreType.DMA((2,)),
            pltpu.SemaphoreType.DMA((2,)),
            pltpu.SemaphoreType.DMA((2,)),
        ],
    )
    return pl.pallas_call(
        _experts_kernel,
        grid_spec=grid_spec,
        out_shape=jax.ShapeDtypeStruct(y0.shape, jnp.uint32),
        input_output_aliases={10: 0},
        compiler_params=pltpu.CompilerParams(
            dimension_semantics=("arbitrary",), vmem_limit_bytes=48 << 20),
        name="experts",
    )(src, dst, block_expert, expert_seq, next_expert, n_active, xs, w1, w3, w2, y0)


def _combine_kernel(h_ref, ri_ref, gs_ref, ys_ref, fw_ref, o_ref):
    tm = h_ref.shape[0]
    hi, lo = _unpack_words(ys_ref[...].reshape(TILE_ROWS, D_PACK))
    g = gs_ref[...]
    gw = jnp.concatenate([g] * (D_PACK // ROUTE_LANES), axis=1)
    yg = jnp.concatenate([hi * gw, lo * gw], axis=1).astype(BF16)
    pos = ri_ref[...].astype(F32)
    rho = lax.broadcasted_iota(jnp.int32, (tm, TILE_ROWS), 1).astype(F32)
    q = jnp.where((rho == pos[:, 0:1]) | (rho == pos[:, 1:2]), 1.0, 0.0).astype(BF16)
    o_ref[...] = _rms(h_ref[...] + _dot(q, yg), fw_ref[...])


def _combine(h2, route_i, gsort, ys, final_w):
    t, d = h2.shape
    nt = t // TM_MIX
    row = lambda i: (i, 0)
    return pl.pallas_call(
        _combine_kernel,
        grid=(nt,),
        in_specs=[
            pl.BlockSpec((TM_MIX, d), row),
            pl.BlockSpec((TM_MIX, ROUTE_LANES), row),
            pl.BlockSpec((TILE_ROWS, ROUTE_LANES), row),
            pl.BlockSpec((TILE_CHUNKS, MOE_CHUNK, D_PACK), lambda i: (i, 0, 0)),
            pl.BlockSpec((1, d), lambda i: (0, 0)),
        ],
        out_specs=pl.BlockSpec((TM_MIX, d), row),
        out_shape=jax.ShapeDtypeStruct((t, d), F32),
        compiler_params=pltpu.CompilerParams(
            dimension_semantics=("arbitrary",), vmem_limit_bytes=48 << 20),
        name="combine",
    )(h2, route_i, gsort, ys, final_w)


def _routing_tables(cnt):
    nt = cnt.shape[0]
    n = cnt[:, 0, EXPERT_LANE0:EXPERT_LANE0 + N_EXPERTS]
    c = (n + MOE_CHUNK - 1) // MOE_CHUNK
    local = jnp.cumsum(c, axis=1) - c
    per_expert = jnp.sum(c, axis=0)
    padded = (per_expert + BLOCK_CHUNKS - 1) // BLOCK_CHUNKS * BLOCK_CHUNKS
    gend = jnp.cumsum(padded)
    gstart = gend - padded
    within = jnp.cumsum(c, axis=0) - c
    g_max = (2 * nt * TM_MIX // MOE_CHUNK + nt * N_EXPERTS
             + N_EXPERTS * (BLOCK_CHUNKS - 1))
    n_blocks = -(-g_max // BLOCK_CHUNKS)
    g = jnp.arange(n_blocks * BLOCK_CHUNKS, dtype=jnp.int32)
    e_of = jnp.minimum(jnp.sum((gend[None, :] <= g[:, None]).astype(jnp.int32), axis=1),
                       N_EXPERTS - 1)
    hot_e = (e_of[:, None] == jnp.arange(N_EXPERTS, dtype=jnp.int32)[None, :]).astype(F32)
    pick = lambda table: jnp.dot(hot_e, table.T.astype(F32), precision=lax.Precision.HIGHEST)
    rel = g.astype(F32) - pick(gstart[None, :])[:, 0]
    w_rows = pick(within)
    tile = jnp.sum((w_rows <= rel[:, None]).astype(jnp.int32), axis=1) - 1
    hot_t = tile[:, None] == jnp.arange(nt, dtype=jnp.int32)[None, :]
    at_tile = lambda rows: jnp.sum(jnp.where(hot_t, rows, 0.0), axis=1)
    k = rel - at_tile(w_rows)
    valid = k < at_tile(pick(c))
    chunk = (tile * TILE_CHUNKS + (at_tile(pick(local)) + k).astype(jnp.int32))
    pad_src = jnp.full((BLOCK_CHUNKS,), ZERO_CHUNK, jnp.int32)
    pad_dst = jnp.full((BLOCK_CHUNKS,), -1, jnp.int32)
    src = jnp.concatenate([jnp.where(valid, chunk, ZERO_CHUNK).astype(jnp.int32), pad_src])
    dst = jnp.concatenate([pad_dst, jnp.where(valid, chunk, -1).astype(jnp.int32)])
    block_start = jnp.arange(n_blocks, dtype=jnp.int32) * BLOCK_CHUNKS
    block_expert = jnp.minimum(
        jnp.sum((block_start[:, None] >= gend[None, :]).astype(jnp.int32), axis=1),
        N_EXPERTS - 1).astype(jnp.int32)
    n_active = (gend[-1:] // BLOCK_CHUNKS).astype(jnp.int32)
    blk = jnp.arange(n_blocks, dtype=jnp.int32)
    is_first = (blk < n_active[0]) & ((blk == 0) | (block_expert != jnp.roll(block_expert, 1)))
    expert_seq = (jnp.cumsum(is_first.astype(jnp.int32)) - 1).astype(jnp.int32)
    later_first = is_first[None, :] & (blk[None, :] > blk[:, None])
    next_blk = jnp.min(jnp.where(later_first, blk[None, :], n_blocks), axis=1)
    next_expert = jnp.where(
        next_blk < n_blocks,
        jnp.sum(jnp.where(blk[None, :] == next_blk[:, None], block_expert[None, :], 0), axis=1),
        -1).astype(jnp.int32)
    return src, dst, block_expert, expert_seq, next_expert, n_active


def _layer(h, mem, p, final_w):
    bsz, seq, d = h.shape
    t = bsz * seq
    assert t // TM_MIX >= DUMP_TILE0 + 2 * BLOCK_CHUNKS // SPARE_PER_TILE
    xt = h.reshape(t, d)

    v, u = _in_proj(xt, p["norm_mix_w"].reshape(1, d), p["w_in"].astype(BF16))
    y_conv = _conv(v, p["conv_w"], p["conv_b"].reshape(1, D_CONV),
                   p["conv_ln_w"].reshape(1, D_CONV), p["conv_ln_b"].reshape(1, D_CONV),
                   bsz, seq)

    a1, pj, aj, bbar = _s5_prep(p["ssm_A_re"], p["ssm_A_im"], p["ssm_log_dt"],
                                      p["ssm_B_re"], p["ssm_B_im"])
    nlb = D_SSM // S5_LANES
    bb = bbar.reshape(2, SSM_GROUP, nlb, S5_GROUPS, SSM_STATE).transpose(0, 2, 3, 1, 4)
    bcat = jnp.concatenate([_block_diag(bb[0]), _block_diag(bb[1])], axis=-1).astype(BF16)
    c_re = p["ssm_C_re"].reshape(nlb, S5_GROUPS, SSM_GROUP, SSM_STATE).transpose(0, 1, 3, 2)
    c_im = p["ssm_C_im"].reshape(nlb, S5_GROUPS, SSM_GROUP, SSM_STATE).transpose(0, 1, 3, 2)
    ccat = jnp.concatenate([_block_diag(c_re), -_block_diag(c_im)], axis=1).astype(BF16)
    glu = p["ssm_glu_w"].reshape(nlb, S5_GROUPS, SSM_GROUP, 2 * SSM_GROUP)
    wab = jnp.concatenate([_block_diag(glu[..., :SSM_GROUP]),
                           _block_diag(glu[..., SSM_GROUP:])], axis=-1).astype(BF16)
    y_ssm = _s5(u, bcat, ccat, p["ssm_D"].reshape(1, D_SSM), wab,
                a1, pj, aj, bsz, seq)

    k, vv = _kv(mem, p["norm_mem_w"].reshape(1, d), p["xk_w"].astype(BF16),
                p["xv_w"].astype(BF16))
    w_out = p["w_out"].astype(BF16)
    wr = jnp.concatenate([p["router_group_w"], p["router_expert_w"]], axis=1)
    wr = jnp.pad(wr, ((0, 0), (0, ROUTE_LANES - wr.shape[1]))).astype(BF16)
    br = jnp.concatenate([p["router_group_b"], p["router_expert_b"].reshape(-1)])
    br = jnp.pad(br, (0, ROUTE_LANES - br.shape[0])).reshape(1, ROUTE_LANES)
    h2, xs, y0, gsort, route_i, cnt = _mix_attn(
        xt, y_conv, y_ssm, w_out[:D_CONV], w_out[D_CONV:], p["norm_x_w"].reshape(1, d),
        (p["xq_w"] * (XHEAD_DIM ** -0.5)).astype(BF16), k, vv, p["xo_w"].astype(BF16),
        p["norm_ffn_w"].reshape(1, d), wr, br, seq)

    src, dst, block_expert, expert_seq, next_expert, n_active = _routing_tables(cnt)
    ys = _experts(src, dst, block_expert, expert_seq, next_expert, n_active, xs,
                  p["moe_w1"], p["moe_w3"], p["moe_w2"], y0)
    out = _combine(h2, route_i, gsort, ys, final_w.reshape(1, d))
    return out.reshape(bsz, seq, d)


def kernel(x, mem, norm_mix_w, w_in, conv_w, conv_b, conv_ln_w, conv_ln_b, ssm_A_re, ssm_A_im, ssm_log_dt, ssm_B_re, ssm_B_im, ssm_C_re, ssm_C_im, ssm_D, ssm_glu_w, w_out, norm_x_w, norm_mem_w, xq_w, xk_w, xv_w, xo_w, norm_ffn_w, router_group_w, router_group_b, router_expert_w, router_expert_b, moe_w1, moe_w3, moe_w2, final_norm_w):
    stacked = dict(
        norm_mix_w=norm_mix_w, w_in=w_in, conv_w=conv_w, conv_b=conv_b,
        conv_ln_w=conv_ln_w, conv_ln_b=conv_ln_b, ssm_A_re=ssm_A_re, ssm_A_im=ssm_A_im,
        ssm_log_dt=ssm_log_dt, ssm_B_re=ssm_B_re, ssm_B_im=ssm_B_im, ssm_C_re=ssm_C_re,
        ssm_C_im=ssm_C_im, ssm_D=ssm_D, ssm_glu_w=ssm_glu_w, w_out=w_out,
        norm_x_w=norm_x_w, norm_mem_w=norm_mem_w, xq_w=xq_w, xk_w=xk_w, xv_w=xv_w,
        xo_w=xo_w, norm_ffn_w=norm_ffn_w, router_group_w=router_group_w,
        router_group_b=router_group_b, router_expert_w=router_expert_w,
        router_expert_b=router_expert_b, moe_w1=moe_w1, moe_w3=moe_w3, moe_w2=moe_w2)
    depth = norm_mix_w.shape[0]
    assert depth == 1, "final norm is fused into the single layer's combine step"
    layer = {name: w[0] for name, w in stacked.items()}
    return _layer(x, mem, layer, final_norm_w)
```

```python
import jax
import jax.numpy as jnp
from jax import lax
from jax.experimental import pallas as pl
from jax.experimental.pallas import tpu as pltpu

D_MODEL = 1024
D_CONV = 512
CONV_WIDTH = 31
D_SSM = 512
SSM_GROUP = 16
N_SSM_GROUPS = 32
SSM_STATE = 64
N_XHEADS = 4
XHEAD_DIM = 256
N_EXPERT_GROUPS = 4
EXPERTS_PER_GROUP = 8
N_EXPERTS = 32
D_EXPERT = 512
EPS = 1e-6

F32 = jnp.float32
BF16 = jnp.bfloat16

SUBLANES = 8
LANES = 128

TM_PROJ = 512
TM_CONV = 512
CONV_CHUNK = 128
CONV_NORM_ROWS = 128
CONV_HALO = 32
S5_STEPS = 64
S5_TILE = SUBLANES * S5_STEPS
S5_GROUP_ROWS = 256
S5_LANES = 128
S5_GROUPS = S5_LANES // SSM_GROUP
S5_STATE = S5_GROUPS * SSM_STATE
TM_MIX = 512
ROUTE_LANES = 128
EXPERT_LANE0 = N_EXPERT_GROUPS
MOE_CHUNK = SUBLANES
MOE_BLOCK = 256
BLOCK_CHUNKS = MOE_BLOCK // MOE_CHUNK
TILE_USED_CHUNKS = 2 * TM_MIX // MOE_CHUNK + N_EXPERTS * (MOE_CHUNK - 1) // MOE_CHUNK
TILE_CHUNKS = 160
TILE_ROWS = TILE_CHUNKS * MOE_CHUNK
D_PACK = D_MODEL // 2
HI_MASK = 0xFFFF0000

VMEM_SMALL = 40 << 20
VMEM_EXPERTS = 48 << 20
VMEM_MIX = 62 << 20


def _rms(x, w):
    return x * lax.rsqrt(jnp.mean(x * x, axis=-1, keepdims=True) + EPS) * w


def _dot(a, b):
    return jnp.dot(a, b, preferred_element_type=F32)


def _s5_prep_kernel(are_ref, aim_ref, ldt_ref, btre_ref, btim_ref,
                    a1_ref, pj_ref, aj_ref, bbar_ref):
    lam_re = are_ref[...]
    lam_im = aim_ref[...]
    dt = jnp.exp(ldt_ref[...])
    x = lam_re * dt
    y = lam_im * dt

    def power(k):
        mag = jnp.exp(k * x)
        return mag * jnp.cos(k * y), mag * jnp.sin(k * y)

    n = x.shape[-1]
    ones8 = jnp.ones((SUBLANES, n), F32)
    a_re, a_im = power(ones8)
    a1_ref[0] = a_re
    a1_ref[1] = a_im

    row = lax.broadcasted_iota(jnp.int32, (SUBLANES, n), 0)
    for i, d in enumerate((1, 2, 4)):
        p_re, p_im = power(ones8 * float(d * S5_STEPS))
        keep = row >= d
        pj_ref[0, i] = jnp.where(keep, p_re, 0.0)
        pj_ref[1, i] = jnp.where(keep, p_im, 0.0)
    j_re, j_im = power(ones8 * float(S5_STEPS))
    aj_ref[0] = j_re
    aj_ref[1] = j_im

    num_re = a_re[0:1] - 1.0
    num_im = a_im[0:1]
    den = lam_re * lam_re + lam_im * lam_im
    c_re = (num_re * lam_re + num_im * lam_im) / den
    c_im = (num_im * lam_re - num_re * lam_im) / den
    b_re = btre_ref[...]
    b_im = btim_ref[...]
    bbar_ref[0] = c_re * b_re - c_im * b_im
    bbar_ref[1] = c_re * b_im + c_im * b_re


def _s5_prep(a_re, a_im, log_dt, b_re, b_im):
    n = N_SSM_GROUPS * SSM_STATE
    are = a_re.reshape(1, n)
    aim = a_im.reshape(1, n)
    ldt = jnp.repeat(log_dt, SSM_STATE).reshape(1, n)
    btre = jnp.transpose(b_re, (2, 0, 1)).reshape(SSM_GROUP, n)
    btim = jnp.transpose(b_im, (2, 0, 1)).reshape(SSM_GROUP, n)
    return pl.pallas_call(
        _s5_prep_kernel,
        out_shape=(
            jax.ShapeDtypeStruct((2, SUBLANES, n), F32),
            jax.ShapeDtypeStruct((2, 3, SUBLANES, n), F32),
            jax.ShapeDtypeStruct((2, SUBLANES, n), F32),
            jax.ShapeDtypeStruct((2, SSM_GROUP, n), F32),
        ),
        name="s5_prep",
    )(are, aim, ldt, btre, btim)


def _block_diag(w):
    nl, g, r, c = w.shape
    eye = jnp.eye(g, dtype=w.dtype)
    return jnp.einsum("lgrc,gh->lgrhc", w, eye).reshape(nl, g * r, g * c)


def _in_proj_kernel(x_ref, nw_ref, w_ref, v_ref, u_ref):
    xn = _rms(x_ref[...], nw_ref[...]).astype(BF16)
    proj = _dot(xn, w_ref[...])
    a = proj[:, :D_CONV]
    g = proj[:, D_CONV:2 * D_CONV]
    v_ref[...] = a * jax.nn.sigmoid(g)
    for lt in range(D_SSM // LANES):
        c0 = 2 * D_CONV + lt * LANES
        for r in range(SUBLANES):
            u_ref[lt, pl.ds(r, S5_STEPS, stride=SUBLANES), :] = (
                proj[r * S5_STEPS:(r + 1) * S5_STEPS, c0:c0 + LANES])


def _in_proj(xt, norm_w, w_in):
    t = xt.shape[0]
    n_out = 2 * D_CONV + D_SSM
    return pl.pallas_call(
        _in_proj_kernel,
        grid=(t // TM_PROJ,),
        in_specs=[
            pl.BlockSpec((TM_PROJ, D_MODEL), lambda i: (i, 0)),
            pl.BlockSpec((1, D_MODEL), lambda i: (0, 0)),
            pl.BlockSpec((D_MODEL, n_out), lambda i: (0, 0)),
        ],
        out_specs=(
            pl.BlockSpec((TM_PROJ, D_CONV), lambda i: (i, 0)),
            pl.BlockSpec((D_SSM // LANES, TM_PROJ, LANES), lambda i: (0, i, 0)),
        ),
        out_shape=(
            jax.ShapeDtypeStruct((t, D_CONV), F32),
            jax.ShapeDtypeStruct((D_SSM // LANES, t, LANES), F32),
        ),
        compiler_params=pltpu.CompilerParams(
            dimension_semantics=("arbitrary",), vmem_limit_bytes=VMEM_SMALL),
        name="in_proj",
    )(xt, norm_w, w_in)


def _conv_kernel(v_ref, w_ref, b_ref, lnw_ref, lnb_ref, o_ref, ext_ref, sh_ref):
    tt = pl.program_id(1)
    rows = CONV_HALO + TM_CONV

    @pl.when(tt == 0)
    def _():
        ext_ref[pl.ds(0, CONV_HALO), :] = jnp.zeros((CONV_HALO, D_CONV), F32)

    @pl.when(tt > 0)
    def _():
        ext_ref[pl.ds(0, CONV_HALO), :] = ext_ref[pl.ds(TM_CONV, CONV_HALO), :]

    ext_ref[pl.ds(CONV_HALO, TM_CONV), :] = v_ref[...]
    for s in range(1, SUBLANES):
        sh_ref[s - 1, pl.ds(0, rows - SUBLANES), :] = ext_ref[pl.ds(s, rows - SUBLANES), :]
    bias = b_ref[...]
    lnw = lnw_ref[...]
    lnb = lnb_ref[...]
    tap0 = CONV_HALO - (CONV_WIDTH - 1)

    groups = CONV_CHUNK // SUBLANES

    def chunk(ci, carry):
        base = pl.multiple_of(ci * CONV_CHUNK, CONV_CHUNK)
        for lt in range(D_CONV // LANES):
            lanes = pl.ds(lt * LANES, LANES)
            acc = [jnp.broadcast_to(bias[:, lt * LANES:(lt + 1) * LANES], (SUBLANES, LANES))] * groups
            for s in range(SUBLANES):
                taps = [j for j in range(CONV_WIDTH) if (tap0 + j) % SUBLANES == s]
                src = ext_ref if s == 0 else sh_ref.at[s - 1]
                ngroups = (tap0 + taps[-1] - s) // SUBLANES + groups
                win = [src[pl.ds(base + SUBLANES * g, SUBLANES), lanes] for g in range(ngroups)]
                for j in taps:
                    g0 = (tap0 + j - s) // SUBLANES
                    wj = w_ref[j, :, lanes]
                    acc = [acc[r] + wj * win[g0 + r] for r in range(groups)]
            o_ref[pl.ds(base, CONV_CHUNK), lanes] = jnp.concatenate(acc, axis=0)
        return carry

    lax.fori_loop(0, TM_CONV // CONV_CHUNK, chunk, 0)

    for bi in range(TM_CONV // CONV_NORM_ROWS):
        rows_b = pl.ds(bi * CONV_NORM_ROWS, CONV_NORM_ROWS)
        acc = o_ref[rows_b, :]
        mu = jnp.mean(acc, axis=-1, keepdims=True)
        cen = acc - mu
        var = jnp.mean(cen * cen, axis=-1, keepdims=True)
        z = cen * lax.rsqrt(var + EPS) * lnw + lnb
        o_ref[rows_b, :] = z * jax.nn.sigmoid(z)


def _conv(v, conv_w, conv_b, ln_w, ln_b, bsz, seq):
    nt = seq // TM_CONV
    row = lambda b, t: (b * nt + t, 0)
    const = lambda b, t: (0, 0)
    return pl.pallas_call(
        _conv_kernel,
        grid=(bsz, nt),
        in_specs=[
            pl.BlockSpec((TM_CONV, D_CONV), row),
            pl.BlockSpec((CONV_WIDTH, SUBLANES, D_CONV), lambda b, t: (0, 0, 0)),
            pl.BlockSpec((1, D_CONV), const),
            pl.BlockSpec((1, D_CONV), const),
            pl.BlockSpec((1, D_CONV), const),
        ],
        out_specs=pl.BlockSpec((TM_CONV, D_CONV), row),
        out_shape=jax.ShapeDtypeStruct(v.shape, F32),
        scratch_shapes=[
            pltpu.VMEM((CONV_HALO + TM_CONV, D_CONV), F32),
            pltpu.VMEM((SUBLANES - 1, CONV_HALO + TM_CONV, D_CONV), F32),
        ],
        compiler_params=pltpu.CompilerParams(
            dimension_semantics=("arbitrary", "arbitrary")),
        name="conv",
    )(v, jnp.broadcast_to(conv_w[:, None, :], (CONV_WIDTH, SUBLANES, D_CONV)), conv_b, ln_w, ln_b)


def _cmul(a_re, a_im, b_re, b_im):
    return a_re * b_re - a_im * b_im, a_re * b_im + a_im * b_re


def _s5_kernel(u_ref, bcat_ref, ccat_ref, d_ref, wab_ref, a1_ref,
               pj_ref, aj_ref, o_ref, up_ref, bu_ref, st_ref, carry_ref):
    tt = pl.program_id(1)
    ns = S5_STATE
    nseq = u_ref.shape[0]
    steps_per_group = S5_GROUP_ROWS // SUBLANES
    n_groups = S5_TILE // S5_GROUP_ROWS
    cur = tt % 2
    prv = 1 - cur

    @pl.when((pl.program_id(0) == 0) & (tt == 0))
    def _():
        up_ref[...] = jnp.zeros(up_ref.shape, F32)
        st_ref[...] = jnp.zeros(st_ref.shape, BF16)

    @pl.when(tt == 0)
    def _():
        carry_ref[...] = jnp.zeros(carry_ref.shape, F32)

    up_ref[cur] = u_ref[...]

    a_re = a1_ref[0]
    a_im = a1_ref[1]

    def step(q, j, s):
        rows = pl.ds(j * SUBLANES, SUBLANES)
        m_re, m_im = _cmul(a_re, a_im, s[0], s[1])
        return m_re + bu_ref[q, rows, pl.ds(0, ns)], m_im + bu_ref[q, rows, pl.ds(ns, ns)]

    def project_out(q, g):
        rows = pl.ds(g * S5_GROUP_ROWS, S5_GROUP_ROWS)
        y = _dot(st_ref[prv, q, rows, :], ccat_ref[0]) + d_ref[...] * up_ref[prv, q, rows, :]
        y = jax.nn.gelu(y)
        ab = _dot(y.astype(BF16), wab_ref[0])
        o_ref[q, rows, :] = ab[:, :S5_LANES] * jax.nn.sigmoid(ab[:, S5_LANES:])

    zero = jnp.zeros((SUBLANES, ns), F32)
    state = [(zero, zero)] * nseq
    for g in range(n_groups):
        rows = pl.ds(g * S5_GROUP_ROWS, S5_GROUP_ROWS)
        for q in range(nseq):
            bu_ref[q, rows, :] = _dot(up_ref[cur, q, rows, :].astype(BF16), bcat_ref[0])
        for jj in range(steps_per_group):
            state = [step(q, g * steps_per_group + jj, state[q]) for q in range(nseq)]
        for q in range(nseq):
            project_out(q, g)

    row = lax.broadcasted_iota(jnp.int32, (SUBLANES, ns), 0)
    first = row == 0
    entry = []
    for q in range(nseq):
        e_re, e_im = state[q]
        c_re = jnp.where(first, pltpu.roll(carry_ref[q, 0], 1, 0), pltpu.roll(e_re, 1, 0))
        c_im = jnp.where(first, pltpu.roll(carry_ref[q, 1], 1, 0), pltpu.roll(e_im, 1, 0))
        for i, d in enumerate((1, 2, 4)):
            r_re = pltpu.roll(c_re, d, 0)
            r_im = pltpu.roll(c_im, d, 0)
            m_re, m_im = _cmul(pj_ref[0, i], pj_ref[1, i], r_re, r_im)
            c_re = c_re + m_re
            c_im = c_im + m_im
        f_re, f_im = _cmul(aj_ref[0], aj_ref[1], c_re, c_im)
        carry_ref[q, 0] = f_re + e_re
        carry_ref[q, 1] = f_im + e_im
        entry.append((c_re, c_im))

    state = entry
    pack = 2 * SUBLANES
    for j in range(0, S5_STEPS, 2):
        mid = [step(q, j, state[q]) for q in range(nseq)]
        state = [step(q, j + 1, mid[q]) for q in range(nseq)]
        for q in range(nseq):
            st_ref[cur, q, pl.ds(j * SUBLANES, pack), pl.ds(0, ns)] = (
                jnp.concatenate([mid[q][0], state[q][0]], axis=0).astype(BF16))
            st_ref[cur, q, pl.ds(j * SUBLANES, pack), pl.ds(ns, ns)] = (
                jnp.concatenate([mid[q][1], state[q][1]], axis=0).astype(BF16))


def _s5(u, bcat, ccat, d, wab, a1, pj, aj, bsz, seq):
    nt = seq // S5_TILE
    nlb = D_SSM // S5_LANES
    ns = S5_STATE
    assert S5_LANES == LANES and S5_TILE == TM_PROJ == TM_MIX
    u4 = u.reshape(nlb, bsz, seq, S5_LANES)
    row_in = lambda l, t: (l, 0, jnp.minimum(t, nt - 1), 0)
    row_out = lambda l, t: (l, 0, jnp.maximum(t - 1, 0), 0)
    lane3 = lambda l, t: (0, 0, l)
    lane4 = lambda l, t: (0, 0, 0, l)
    out = pl.pallas_call(
        _s5_kernel,
        grid=(nlb, nt + 1),
        in_specs=[
            pl.BlockSpec((None, bsz, S5_TILE, S5_LANES), row_in),
            pl.BlockSpec((1, S5_LANES, 2 * ns), lambda l, t: (l, 0, 0)),
            pl.BlockSpec((1, 2 * ns, S5_LANES), lambda l, t: (l, 0, 0)),
            pl.BlockSpec((1, S5_LANES), lambda l, t: (0, l)),
            pl.BlockSpec((1, S5_LANES, 2 * S5_LANES), lambda l, t: (l, 0, 0)),
            pl.BlockSpec((2, SUBLANES, ns), lane3),
            pl.BlockSpec((2, 3, SUBLANES, ns), lane4),
            pl.BlockSpec((2, SUBLANES, ns), lane3),
        ],
        out_specs=pl.BlockSpec((None, bsz, S5_TILE, S5_LANES), row_out),
        out_shape=jax.ShapeDtypeStruct(u4.shape, F32),
        scratch_shapes=[
            pltpu.VMEM((2, bsz, S5_TILE, S5_LANES), F32),
            pltpu.VMEM((bsz, S5_TILE, 2 * ns), F32),
            pltpu.VMEM((2, bsz, S5_TILE, 2 * ns), BF16),
            pltpu.VMEM((bsz, 2, SUBLANES, ns), F32),
        ],
        compiler_params=pltpu.CompilerParams(
            dimension_semantics=("arbitrary", "arbitrary"),
            vmem_limit_bytes=VMEM_SMALL),
        name="s5",
    )(u4, bcat, ccat, d, wab, a1, pj, aj)
    return out.reshape(nlb, bsz * seq, S5_LANES)


def _kv_kernel(m_ref, nw_ref, wk_ref, wv_ref, k_ref, v_ref):
    mn = _rms(m_ref[0], nw_ref[...]).astype(BF16)
    k_ref[0] = _dot(mn, wk_ref[...]).astype(BF16)
    v_ref[0] = _dot(mn, wv_ref[...]).astype(BF16)


def _kv(mem, norm_w, wk, wv):
    bsz, mlen, d = mem.shape
    blk = pl.BlockSpec((1, mlen, d), lambda b: (b, 0, 0))
    wspec = pl.BlockSpec((d, d), lambda b: (0, 0))
    return pl.pallas_call(
        _kv_kernel,
        grid=(bsz,),
        in_specs=[blk, pl.BlockSpec((1, d), lambda b: (0, 0)), wspec, wspec],
        out_specs=(blk, blk),
        out_shape=(jax.ShapeDtypeStruct(mem.shape, BF16),) * 2,
        compiler_params=pltpu.CompilerParams(
            dimension_semantics=("arbitrary",), vmem_limit_bytes=VMEM_SMALL),
        name="kv",
    )(mem, norm_w, wk, wv)


def _mix_attn_kernel(x_ref, yc_ref, ys_ref, wot_ref, wob_ref, nx_ref, wq_ref,
                     k_ref, v_ref, wo_ref, nf_ref, wr_ref, br_ref,
                     h_ref, xs_ref, y0_ref, gs_ref, ri_ref, cnt_ref,
                     h2s_ref, tri_ref, rho_ref):
    i = pl.program_id(0)
    tm = x_ref.shape[0]
    cur = i % 2
    prv = 1 - cur
    neg = -jnp.inf
    big = float(ROUTE_LANES)

    @pl.when(i == 0)
    def _():
        h2s_ref[...] = jnp.zeros(h2s_ref.shape, F32)
        r_i = lax.broadcasted_iota(jnp.int32, (tm, tm), 0)
        c_i = lax.broadcasted_iota(jnp.int32, (tm, tm), 1)
        tri_ref[...] = jnp.where(r_i > c_i, 1.0, 0.0).astype(BF16)
        rho_ref[...] = lax.broadcasted_iota(jnp.int32, (tm, TILE_ROWS), 1).astype(F32)

    ys_time = jnp.concatenate(
        [jnp.concatenate([ys_ref[lt, pl.ds(r, S5_STEPS, stride=SUBLANES), :]
                          for r in range(SUBLANES)], axis=0)
         for lt in range(D_SSM // LANES)], axis=1)
    h1 = (x_ref[...] + _dot(yc_ref[...].astype(BF16), wot_ref[...])
          + _dot(ys_time.astype(BF16), wob_ref[...]))
    hn = _rms(h1, nx_ref[...]).astype(BF16)
    q = _dot(hn, wq_ref[...])

    hf = _rms(h2s_ref[prv], nf_ref[...]).astype(BF16)
    logits = _dot(hf, wr_ref[...]) + br_ref[...]
    lane = lax.broadcasted_iota(jnp.int32, (tm, ROUTE_LANES), 1)
    lane_f = lane.astype(F32)

    def top1(vals):
        m = jnp.max(vals, axis=-1, keepdims=True)
        idx = jnp.min(jnp.where(vals == m, lane_f, big), axis=-1, keepdims=True)
        return m, idx

    gl = jnp.where(lane < N_EXPERT_GROUPS, logits, neg)
    gmax, gidx = top1(gl)
    g_w = 1.0 / jnp.sum(jnp.exp(gl - gmax), axis=-1, keepdims=True)
    lo = EXPERT_LANE0 + EXPERTS_PER_GROUP * gidx
    el = jnp.where((lane_f >= lo) & (lane_f < lo + EXPERTS_PER_GROUP), logits, neg)
    m1, i1 = top1(el)
    m2, i2 = top1(jnp.where(lane_f == i1, neg, el))
    e21 = jnp.exp(m2 - m1)
    gate1 = g_w / (1.0 + e21)
    gate2 = g_w * e21 / (1.0 + e21)

    heads = []
    for hd in range(N_XHEADS):
        sl = slice(hd * XHEAD_DIM, (hd + 1) * XHEAD_DIM)
        qh = q[:, sl].astype(BF16)
        s = lax.dot_general(qh, k_ref[0, :, sl], (((1,), (1,)), ((), ())),
                            preferred_element_type=F32)
        s = s - jnp.max(s, axis=-1, keepdims=True)
        p = jnp.exp(s)
        p = p / jnp.sum(p, axis=-1, keepdims=True)
        heads.append(_dot(p.astype(BF16), v_ref[0, :, sl]).astype(BF16))
    o = jnp.concatenate(heads, axis=-1)

    hot1 = lane_f == i1
    hot2 = lane_f == i2
    hot = jnp.where(hot1 | hot2, 1.0, 0.0)
    before = _dot(tri_ref[...], hot.astype(BF16))
    count = jnp.sum(hot, axis=0, keepdims=True)
    chunks = jnp.floor((count + (MOE_CHUNK - 1.0)) * (1.0 / MOE_CHUNK))
    l_i = lax.broadcasted_iota(jnp.int32, (ROUTE_LANES, ROUTE_LANES), 0)
    l_j = lax.broadcasted_iota(jnp.int32, (ROUTE_LANES, ROUTE_LANES), 1)
    upper = jnp.where(l_i < l_j, 1.0, 0.0).astype(BF16)
    first_chunk = _dot(jnp.broadcast_to(chunks, (SUBLANES, ROUTE_LANES)).astype(BF16), upper)[0:1]
    start = first_chunk * float(MOE_CHUNK) + before
    pos1 = jnp.sum(jnp.where(hot1, start, 0.0), axis=-1, keepdims=True)
    pos2 = jnp.sum(jnp.where(hot2, start, 0.0), axis=-1, keepdims=True)
    cnt_ref[0] = jnp.broadcast_to(count, (SUBLANES, ROUTE_LANES)).astype(jnp.int32)
    ri_ref[...] = jnp.where(lane == 0, pos1, jnp.where(lane == 1, pos2, 0.0)).astype(jnp.int32)

    h2 = h1 + _dot(o, wo_ref[...])
    h_ref[...] = h2
    h2s_ref[cur] = h2

    def pieces(g):
        hi = g.astype(BF16).astype(F32)
        mid = (g - hi).astype(BF16).astype(F32)
        low = (g - hi - mid).astype(BF16).astype(F32)
        return hi, mid, low

    g6 = jnp.where(lane == 6, 1.0, 0.0)
    for li, piece in enumerate(pieces(gate1) + pieces(gate2)):
        g6 = jnp.where(lane == li, piece, g6)
    rho = rho_ref[...]
    ptk = jnp.where(rho == pos1, 1.0, jnp.where(rho == pos2, 2.0, 0.0)).astype(BF16)
    rhs = jnp.concatenate([hf, g6.astype(BF16)], axis=1)
    res = lax.dot_general(ptk, rhs, (((0,), (0,)), ((), ())), preferred_element_type=F32)
    sg = res[:, D_MODEL:]
    which = sg[:, 6:7]
    srt = res[:, :D_MODEL] * jnp.where(which == 2.0, 0.5, 1.0)
    xs_ref[...] = _pack_words(srt).reshape(TILE_CHUNKS, MOE_CHUNK, D_PACK)
    y0_ref[...] = jnp.zeros(y0_ref.shape, jnp.uint32)
    first = sg[:, 0:1] + sg[:, 1:2] + sg[:, 2:3]
    second = 0.5 * (sg[:, 3:4] + sg[:, 4:5] + sg[:, 5:6])
    gsort = jnp.where(which == 1.0, first, jnp.where(which == 2.0, second, 0.0))
    gs_ref[...] = jnp.broadcast_to(gsort, (TILE_ROWS, ROUTE_LANES))


def _pack_words(v):
    hi = lax.bitcast_convert_type(v[:, :D_PACK], jnp.uint32) & jnp.uint32(HI_MASK)
    lo = lax.bitcast_convert_type(v[:, D_PACK:], jnp.uint32) >> 16
    return hi | lo


def _unpack_words(w):
    hi = lax.bitcast_convert_type(w & jnp.uint32(HI_MASK), F32)
    lo = lax.bitcast_convert_type(w << 16, F32)
    return hi, lo


def _mix_attn(xt, yc, ys, wot, wob, nx, wq, k, v, wo, nf, wr, br, seq):
    t, d = xt.shape
    nt = t // TM_MIX
    tiles_per_batch = seq // TM_MIX
    mlen = k.shape[1]
    att = lambda i: jnp.minimum(i, nt - 1)
    rte = lambda i: jnp.maximum(i - 1, 0)
    row_a = lambda i: (att(i), 0)
    row_r = lambda i: (rte(i), 0)
    const = lambda i: (0, 0)
    tile3 = lambda i: (rte(i), 0, 0)
    kvspec = pl.BlockSpec((1, mlen, d), lambda i: (att(i) // tiles_per_batch, 0, 0))
    packed = jax.ShapeDtypeStruct((nt * TILE_CHUNKS, MOE_CHUNK, D_PACK), jnp.uint32)
    return pl.pallas_call(
        _mix_attn_kernel,
        grid=(nt + 1,),
        in_specs=[
            pl.BlockSpec((TM_MIX, d), row_a),
            pl.BlockSpec((TM_MIX, D_CONV), row_a),
            pl.BlockSpec((D_SSM // LANES, TM_MIX, LANES), lambda i: (0, att(i), 0)),
            pl.BlockSpec((D_CONV, d), const),
            pl.BlockSpec((D_SSM, d), const),
            pl.BlockSpec((1, d), const),
            pl.BlockSpec((d, d), const),
            kvspec, kvspec,
            pl.BlockSpec((d, d), const),
            pl.BlockSpec((1, d), const),
            pl.BlockSpec((d, ROUTE_LANES), const),
            pl.BlockSpec((1, ROUTE_LANES), const),
        ],
        out_specs=(
            pl.BlockSpec((TM_MIX, d), row_a),
            pl.BlockSpec((TILE_CHUNKS, MOE_CHUNK, D_PACK), tile3),
            pl.BlockSpec((TILE_CHUNKS, MOE_CHUNK, D_PACK), tile3),
            pl.BlockSpec((TILE_ROWS, ROUTE_LANES), row_r),
            pl.BlockSpec((TM_MIX, ROUTE_LANES), row_r),
            pl.BlockSpec((1, SUBLANES, ROUTE_LANES), tile3),
        ),
        out_shape=(
            jax.ShapeDtypeStruct((t, d), F32),
            packed,
            packed,
            jax.ShapeDtypeStruct((nt * TILE_ROWS, ROUTE_LANES), F32),
            jax.ShapeDtypeStruct((t, ROUTE_LANES), jnp.int32),
            jax.ShapeDtypeStruct((nt, SUBLANES, ROUTE_LANES), jnp.int32),
        ),
        scratch_shapes=[
            pltpu.VMEM((2, TM_MIX, d), F32),
            pltpu.VMEM((TM_MIX, TM_MIX), BF16),
            pltpu.VMEM((TM_MIX, TILE_ROWS), F32),
        ],
        compiler_params=pltpu.CompilerParams(
            dimension_semantics=("arbitrary",), vmem_limit_bytes=VMEM_MIX),
        name="mix_attn",
    )(xt, yc, ys, wot, wob, nx, wq, k, v, wo, nf, wr, br)


ZERO_CHUNK = TILE_CHUNKS - 1
DUMP_TILE0 = 16
SPARE_PER_TILE = TILE_CHUNKS - TILE_USED_CHUNKS


def _dump_chunk(slot, k):
    idx = slot * BLOCK_CHUNKS + k
    return (DUMP_TILE0 + idx // SPARE_PER_TILE) * TILE_CHUNKS + TILE_USED_CHUNKS + idx % SPARE_PER_TILE


def _experts_kernel(src_ref, dst_ref, be_ref, seq_ref, nxt_ref, na_ref, xs_hbm, w1_hbm, w3_hbm,
                    w2_hbm, y0_hbm, ys_hbm, xbuf, ybuf, w1f_ref, w3f_ref, w2f_ref,
                    w1s_ref, w3s_ref, w2s_ref, in_sem, out_sem, w_sem):
    del y0_hbm
    b = pl.program_id(0)
    n_active = na_ref[0]
    slot = b % 2
    other = 1 - slot

    def gather(blk, sl):
        for k in range(BLOCK_CHUNKS):
            pltpu.make_async_copy(xs_hbm.at[src_ref[blk * BLOCK_CHUNKS + k]],
                                  xbuf.at[sl, k], in_sem.at[sl]).start()

    def gather_wait(sl):
        pltpu.make_async_copy(xs_hbm.at[pl.ds(0, BLOCK_CHUNKS)], xbuf.at[sl],
                              in_sem.at[sl]).wait()

    def scatter(blk, sl):
        for k in range(BLOCK_CHUNKS):
            d = dst_ref[blk * BLOCK_CHUNKS + k]
            dump = jnp.where(sl == 0, _dump_chunk(0, k), _dump_chunk(1, k))
            pltpu.make_async_copy(ybuf.at[sl, k], ys_hbm.at[jnp.where(d < 0, dump, d)],
                                  out_sem.at[sl]).start()

    def scatter_wait(sl):
        pltpu.make_async_copy(ybuf.at[sl], ys_hbm.at[pl.ds(0, BLOCK_CHUNKS)],
                              out_sem.at[sl]).wait()

    active = b < n_active

    @pl.when(b == 0)
    def _():
        gather(0, 0)
        ybuf[...] = jnp.zeros(ybuf.shape, jnp.uint32)

    prev = be_ref[jnp.maximum(b - 1, 0)]
    fresh = (b == 0) | (be_ref[b] != prev)

    def weight_copies(e, ws):
        return [pltpu.make_async_copy(hbm.at[e], buf.at[ws], w_sem.at[ws])
                for hbm, buf in ((w1_hbm, w1f_ref), (w3_hbm, w3f_ref), (w2_hbm, w2f_ref))]

    @pl.when(active & fresh)
    def _():
        ws = seq_ref[b] % 2

        @pl.when(b == 0)
        def _():
            for cp in weight_copies(be_ref[b], ws):
                cp.start(priority=1)

        for cp in weight_copies(be_ref[b], ws):
            cp.wait()

        @pl.when(nxt_ref[b] >= 0)
        def _():
            for cp in weight_copies(nxt_ref[b], 1 - ws):
                cp.start(priority=1)

        w1s_ref[...] = w1f_ref[ws].astype(BF16)
        w3s_ref[...] = w3f_ref[ws].astype(BF16)
        w2s_ref[...] = w2f_ref[ws].astype(BF16)

    @pl.when(active & (b >= 1))
    def _():
        scatter_wait(slot)

    @pl.when(active)
    def _():
        gather_wait(slot)
        gather(b + 1, other)
        scatter(b, other)

        hi, lo = _unpack_words(xbuf[slot].reshape(MOE_BLOCK, D_PACK))
        xb = jnp.concatenate([hi, lo], axis=1).astype(BF16)
        h1 = _dot(xb, w1s_ref[...])
        h3 = _dot(xb, w3s_ref[...])
        hid = (h1 * jax.nn.sigmoid(h1) * h3).astype(BF16)
        y = _dot(hid, w2s_ref[...]).astype(BF16).astype(F32)
        ybuf[slot] = _pack_words(y).reshape(BLOCK_CHUNKS, MOE_CHUNK, D_PACK)

        @pl.when(b == n_active - 1)
        def _():
            scatter(b + 1, slot)
            gather_wait(other)
            scatter_wait(other)
            scatter_wait(slot)


def _experts(src, dst, block_expert, expert_seq, next_expert, n_active, xs, w1, w3, w2, y0):
    nb = block_expert.shape[0]
    d = D_MODEL
    hbm = pl.BlockSpec(memory_space=pl.ANY)
    grid_spec = pltpu.PrefetchScalarGridSpec(
        num_scalar_prefetch=6,
        grid=(nb,),
        in_specs=[hbm, hbm, hbm, hbm, hbm],
        out_specs=hbm,
        scratch_shapes=[
            pltpu.VMEM((2, BLOCK_CHUNKS, MOE_CHUNK, D_PACK), jnp.uint32),
            pltpu.VMEM((2, BLOCK_CHUNKS, MOE_CHUNK, D_PACK), jnp.uint32),
            pltpu.VMEM((2, d, D_EXPERT), F32),
            pltpu.VMEM((2, d, D_EXPERT), F32),
            pltpu.VMEM((2, D_EXPERT, d), F32),
            pltpu.VMEM((d, D_EXPERT), BF16),
            pltpu.VMEM((d, D_EXPERT), BF16),
            pltpu.VMEM((D_EXPERT, d), BF16),
            pltpu.SemaphoreType.DMA((2,)),
            pltpu.SemaphoreType.DMA((2,)),
            pltpu.SemaphoreType.DMA((2,)),
        ],
    )
    return pl.pallas_call(
        _experts_kernel,
        grid_spec=grid_spec,
        out_shape=jax.ShapeDtypeStruct(y0.shape, jnp.uint32),
        input_output_aliases={10: 0},
        compiler_params=pltpu.CompilerParams(
            dimension_semantics=("arbitrary",), vmem_limit_bytes=VMEM_EXPERTS),
        name="experts",
    )(src, dst, block_expert, expert_seq, next_expert, n_active, xs, w1, w3, w2, y0)


def _combine_kernel(h_ref, ri_ref, gs_ref, ys_ref, fw_ref, o_ref):
    tm = h_ref.shape[0]
    hi, lo = _unpack_words(ys_ref[...].reshape(TILE_ROWS, D_PACK))
    g = gs_ref[...]
    gw = jnp.concatenate([g] * (D_PACK // ROUTE_LANES), axis=1)
    yg = jnp.concatenate([hi * gw, lo * gw], axis=1).astype(BF16)
    pos = ri_ref[...].astype(F32)
    rho = lax.broadcasted_iota(jnp.int32, (tm, TILE_ROWS), 1).astype(F32)
    q = jnp.where((rho == pos[:, 0:1]) | (rho == pos[:, 1:2]), 1.0, 0.0).astype(BF16)
    o_ref[...] = _rms(h_ref[...] + _dot(q, yg), fw_ref[...])


def _combine(h2, route_i, gsort, ys, final_w):
    t, d = h2.shape
    nt = t // TM_MIX
    row = lambda i: (i, 0)
    return pl.pallas_call(
        _combine_kernel,
        grid=(nt,),
        in_specs=[
            pl.BlockSpec((TM_MIX, d), row),
            pl.BlockSpec((TM_MIX, ROUTE_LANES), row),
            pl.BlockSpec((TILE_ROWS, ROUTE_LANES), row),
            pl.BlockSpec((TILE_CHUNKS, MOE_CHUNK, D_PACK), lambda i: (i, 0, 0)),
            pl.BlockSpec((1, d), lambda i: (0, 0)),
        ],
        out_specs=pl.BlockSpec((TM_MIX, d), row),
        out_shape=jax.ShapeDtypeStruct((t, d), F32),
        compiler_params=pltpu.CompilerParams(
            dimension_semantics=("arbitrary",), vmem_limit_bytes=VMEM_EXPERTS),
        name="combine",
    )(h2, route_i, gsort, ys, final_w)


def _routing_tables(cnt):
    nt = cnt.shape[0]
    n = cnt[:, 0, EXPERT_LANE0:EXPERT_LANE0 + N_EXPERTS]
    c = (n + MOE_CHUNK - 1) // MOE_CHUNK
    local = jnp.cumsum(c, axis=1) - c
    per_expert = jnp.sum(c, axis=0)
    padded = (per_expert + BLOCK_CHUNKS - 1) // BLOCK_CHUNKS * BLOCK_CHUNKS
    gend = jnp.cumsum(padded)
    gstart = gend - padded
    within = jnp.cumsum(c, axis=0) - c
    g_max = (2 * nt * TM_MIX // MOE_CHUNK + nt * N_EXPERTS
             + N_EXPERTS * (BLOCK_CHUNKS - 1))
    n_blocks = -(-g_max // BLOCK_CHUNKS)
    g = jnp.arange(n_blocks * BLOCK_CHUNKS, dtype=jnp.int32)
    e_of = jnp.minimum(jnp.sum((gend[None, :] <= g[:, None]).astype(jnp.int32), axis=1),
                       N_EXPERTS - 1)
    hot_e = (e_of[:, None] == jnp.arange(N_EXPERTS, dtype=jnp.int32)[None, :]).astype(F32)
    pick = lambda table: jnp.dot(hot_e, table.T.astype(F32), precision=lax.Precision.HIGHEST)
    rel = g.astype(F32) - pick(gstart[None, :])[:, 0]
    w_rows = pick(within)
    tile = jnp.sum((w_rows <= rel[:, None]).astype(jnp.int32), axis=1) - 1
    hot_t = tile[:, None] == jnp.arange(nt, dtype=jnp.int32)[None, :]
    at_tile = lambda rows: jnp.sum(jnp.where(hot_t, rows, 0.0), axis=1)
    k = rel - at_tile(w_rows)
    valid = k < at_tile(pick(c))
    chunk = (tile * TILE_CHUNKS + (at_tile(pick(local)) + k).astype(jnp.int32))
    pad_src = jnp.full((BLOCK_CHUNKS,), ZERO_CHUNK, jnp.int32)
    pad_dst = jnp.full((BLOCK_CHUNKS,), -1, jnp.int32)
    src = jnp.concatenate([jnp.where(valid, chunk, ZERO_CHUNK).astype(jnp.int32), pad_src])
    dst = jnp.concatenate([pad_dst, jnp.where(valid, chunk, -1).astype(jnp.int32)])
    block_start = jnp.arange(n_blocks, dtype=jnp.int32) * BLOCK_CHUNKS
    block_expert = jnp.minimum(
        jnp.sum((block_start[:, None] >= gend[None, :]).astype(jnp.int32), axis=1),
        N_EXPERTS - 1).astype(jnp.int32)
    n_active = (gend[-1:] // BLOCK_CHUNKS).astype(jnp.int32)
    blk = jnp.arange(n_blocks, dtype=jnp.int32)
    is_first = (blk < n_active[0]) & ((blk == 0) | (block_expert != jnp.roll(block_expert, 1)))
    expert_seq = (jnp.cumsum(is_first.astype(jnp.int32)) - 1).astype(jnp.int32)
    later_first = is_first[None, :] & (blk[None, :] > blk[:, None])
    next_blk = jnp.min(jnp.where(later_first, blk[None, :], n_blocks), axis=1)
    next_expert = jnp.where(
        next_blk < n_blocks,
        jnp.sum(jnp.where(blk[None, :] == next_blk[:, None], block_expert[None, :], 0), axis=1),
        -1).astype(jnp.int32)
    return src, dst, block_expert, expert_seq, next_expert, n_active


def _layer(h, mem, p, final_w):
    bsz, seq, d = h.shape
    t = bsz * seq
    assert t // TM_MIX >= DUMP_TILE0 + 2 * BLOCK_CHUNKS // SPARE_PER_TILE
    xt = h.reshape(t, d)

    v, u = _in_proj(xt, p["norm_mix_w"].reshape(1, d), p["w_in"].astype(BF16))
    y_conv = _conv(v, p["conv_w"], p["conv_b"].reshape(1, D_CONV),
                   p["conv_ln_w"].reshape(1, D_CONV), p["conv_ln_b"].reshape(1, D_CONV),
                   bsz, seq)

    a1, pj, aj, bbar = _s5_prep(p["ssm_A_re"], p["ssm_A_im"], p["ssm_log_dt"],
                                      p["ssm_B_re"], p["ssm_B_im"])
    nlb = D_SSM // S5_LANES
    bb = bbar.reshape(2, SSM_GROUP, nlb, S5_GROUPS, SSM_STATE).transpose(0, 2, 3, 1, 4)
    bcat = jnp.concatenate([_block_diag(bb[0]), _block_diag(bb[1])], axis=-1).astype(BF16)
    c_re = p["ssm_C_re"].reshape(nlb, S5_GROUPS, SSM_GROUP, SSM_STATE).transpose(0, 1, 3, 2)
    c_im = p["ssm_C_im"].reshape(nlb, S5_GROUPS, SSM_GROUP, SSM_STATE).transpose(0, 1, 3, 2)
    ccat = jnp.concatenate([_block_diag(c_re), -_block_diag(c_im)], axis=1).astype(BF16)
    glu = p["ssm_glu_w"].reshape(nlb, S5_GROUPS, SSM_GROUP, 2 * SSM_GROUP)
    wab = jnp.concatenate([_block_diag(glu[..., :SSM_GROUP]),
                           _block_diag(glu[..., SSM_GROUP:])], axis=-1).astype(BF16)
    y_ssm = _s5(u, bcat, ccat, p["ssm_D"].reshape(1, D_SSM), wab,
                a1, pj, aj, bsz, seq)

    k, vv = _kv(mem, p["norm_mem_w"].reshape(1, d), p["xk_w"].astype(BF16),
                p["xv_w"].astype(BF16))
    w_out = p["w_out"].astype(BF16)
    wr = jnp.concatenate([p["router_group_w"], p["router_expert_w"]], axis=1)
    wr = jnp.pad(wr, ((0, 0), (0, ROUTE_LANES - wr.shape[1]))).astype(BF16)
    br = jnp.concatenate([p["router_group_b"], p["router_expert_b"].reshape(-1)])
    br = jnp.pad(br, (0, ROUTE_LANES - br.shape[0])).reshape(1, ROUTE_LANES)
    h2, xs, y0, gsort, route_i, cnt = _mix_attn(
        xt, y_conv, y_ssm, w_out[:D_CONV], w_out[D_CONV:], p["norm_x_w"].reshape(1, d),
        (p["xq_w"] * (XHEAD_DIM ** -0.5)).astype(BF16), k, vv, p["xo_w"].astype(BF16),
        p["norm_ffn_w"].reshape(1, d), wr, br, seq)

    src, dst, block_expert, expert_seq, next_expert, n_active = _routing_tables(cnt)
    ys = _experts(src, dst, block_expert, expert_seq, next_expert, n_active, xs,
                  p["moe_w1"], p["moe_w3"], p["moe_w2"], y0)
    out = _combine(h2, route_i, gsort, ys, final_w.reshape(1, d))
    return out.reshape(bsz, seq, d)


def kernel(x, mem, norm_mix_w, w_in, conv_w, conv_b, conv_ln_w, conv_ln_b, ssm_A_re, ssm_A_im, ssm_log_dt, ssm_B_re, ssm_B_im, ssm_C_re, ssm_C_im, ssm_D, ssm_glu_w, w_out, norm_x_w, norm_mem_w, xq_w, xk_w, xv_w, xo_w, norm_ffn_w, router_group_w, router_group_b, router_expert_w, router_expert_b, moe_w1, moe_w3, moe_w2, final_norm_w):
    stacked = dict(
        norm_mix_w=norm_mix_w, w_in=w_in, conv_w=conv_w, conv_b=conv_b,
        conv_ln_w=conv_ln_w, conv_ln_b=conv_ln_b, ssm_A_re=ssm_A_re, ssm_A_im=ssm_A_im,
        ssm_log_dt=ssm_log_dt, ssm_B_re=ssm_B_re, ssm_B_im=ssm_B_im, ssm_C_re=ssm_C_re,
        ssm_C_im=ssm_C_im, ssm_D=ssm_D, ssm_glu_w=ssm_glu_w, w_out=w_out,
        norm_x_w=norm_x_w, norm_mem_w=norm_mem_w, xq_w=xq_w, xk_w=xk_w, xv_w=xv_w,
        xo_w=xo_w, norm_ffn_w=norm_ffn_w, router_group_w=router_group_w,
        router_group_b=router_group_b, router_expert_w=router_expert_w,
        router_expert_b=router_expert_b, moe_w1=moe_w1, moe_w3=moe_w3, moe_w2=moe_w2)
    depth = norm_mix_w.shape[0]
    assert depth == 1, "final norm is fused into the single layer's combine step"
    layer = {name: w[0] for name, w in stacked.items()}
    return _layer(x, mem, layer, final_norm_w)
```

```python
import functools

import jax
import jax.numpy as jnp
from jax import lax
from jax.experimental import pallas as pl
from jax.experimental.pallas import tpu as pltpu

D_MODEL = 1024
D_CONV = 512
CONV_WIDTH = 31
D_SSM = 512
SSM_GROUP = 16
N_SSM_GROUPS = 32
SSM_STATE = 64
N_XHEADS = 4
XHEAD_DIM = 256
N_EXPERT_GROUPS = 4
EXPERTS_PER_GROUP = 8
N_EXPERTS = 32
D_EXPERT = 512
EPS = 1e-6

F32 = jnp.float32
BF16 = jnp.bfloat16

SUBLANES = 8
LANES = 128

TM_PROJ = 512
TM_CONV = 512
CONV_CHUNK = 128
CONV_NORM_ROWS = 128
CONV_HALO = 32
S5_STEPS = 64
S5_TILE = SUBLANES * S5_STEPS
S5_GROUP_ROWS = 256
S5_LANES = 128
S5_GROUPS = S5_LANES // SSM_GROUP
S5_STATE = S5_GROUPS * SSM_STATE
TM_MIX = 512
ROUTE_LANES = 128
EXPERT_LANE0 = N_EXPERT_GROUPS
MOE_CHUNK = SUBLANES
MOE_BLOCK = 256
BLOCK_CHUNKS = MOE_BLOCK // MOE_CHUNK
TILE_USED_CHUNKS = 2 * TM_MIX // MOE_CHUNK + N_EXPERTS * (MOE_CHUNK - 1) // MOE_CHUNK
TILE_CHUNKS = 160
TILE_ROWS = TILE_CHUNKS * MOE_CHUNK
D_PACK = D_MODEL // 2
HI_MASK = 0xFFFF0000

VMEM_SMALL = 40 << 20
VMEM_EXPERTS = 48 << 20
VMEM_MIX = 62 << 20


def _rms(x, w):
    return x * lax.rsqrt(jnp.mean(x * x, axis=-1, keepdims=True) + EPS) * w


def _dot(a, b):
    return jnp.dot(a, b, preferred_element_type=F32)


def _s5_prep_kernel(are_ref, aim_ref, ldt_ref, btre_ref, btim_ref,
                    a1_ref, pj_ref, aj_ref, bbar_ref):
    lam_re = are_ref[...]
    lam_im = aim_ref[...]
    dt = jnp.exp(ldt_ref[...])
    x = lam_re * dt
    y = lam_im * dt

    def power(k):
        mag = jnp.exp(k * x)
        return mag * jnp.cos(k * y), mag * jnp.sin(k * y)

    n = x.shape[-1]
    ones8 = jnp.ones((SUBLANES, n), F32)
    a_re, a_im = power(ones8)
    a1_ref[0] = a_re
    a1_ref[1] = a_im

    row = lax.broadcasted_iota(jnp.int32, (SUBLANES, n), 0)
    for i, d in enumerate((1, 2, 4)):
        p_re, p_im = power(ones8 * float(d * S5_STEPS))
        keep = row >= d
        pj_ref[0, i] = jnp.where(keep, p_re, 0.0)
        pj_ref[1, i] = jnp.where(keep, p_im, 0.0)
    j_re, j_im = power(ones8 * float(S5_STEPS))
    aj_ref[0] = j_re
    aj_ref[1] = j_im

    num_re = a_re[0:1] - 1.0
    num_im = a_im[0:1]
    den = lam_re * lam_re + lam_im * lam_im
    c_re = (num_re * lam_re + num_im * lam_im) / den
    c_im = (num_im * lam_re - num_re * lam_im) / den
    b_re = btre_ref[...]
    b_im = btim_ref[...]
    bbar_ref[0] = c_re * b_re - c_im * b_im
    bbar_ref[1] = c_re * b_im + c_im * b_re


def _s5_prep(a_re, a_im, log_dt, b_re, b_im):
    n = N_SSM_GROUPS * SSM_STATE
    are = a_re.reshape(1, n)
    aim = a_im.reshape(1, n)
    ldt = jnp.repeat(log_dt, SSM_STATE).reshape(1, n)
    btre = jnp.transpose(b_re, (2, 0, 1)).reshape(SSM_GROUP, n)
    btim = jnp.transpose(b_im, (2, 0, 1)).reshape(SSM_GROUP, n)
    return pl.pallas_call(
        _s5_prep_kernel,
        out_shape=(
            jax.ShapeDtypeStruct((2, SUBLANES, n), F32),
            jax.ShapeDtypeStruct((2, 3, SUBLANES, n), F32),
            jax.ShapeDtypeStruct((2, SUBLANES, n), F32),
            jax.ShapeDtypeStruct((2, SSM_GROUP, n), F32),
        ),
        name="s5_prep",
    )(are, aim, ldt, btre, btim)


def _block_diag(w):
    nl, g, r, c = w.shape
    eye = jnp.eye(g, dtype=w.dtype)
    return jnp.einsum("lgrc,gh->lgrhc", w, eye).reshape(nl, g * r, g * c)


def _in_proj_kernel(x_ref, nw_ref, w_ref, v_ref, u_ref):
    xn = _rms(x_ref[...], nw_ref[...]).astype(BF16)
    proj = _dot(xn, w_ref[...])
    a = proj[:, :D_CONV]
    g = proj[:, D_CONV:2 * D_CONV]
    v_ref[...] = a * jax.nn.sigmoid(g)
    for lt in range(D_SSM // LANES):
        c0 = 2 * D_CONV + lt * LANES
        for r in range(SUBLANES):
            u_ref[lt, pl.ds(r, S5_STEPS, stride=SUBLANES), :] = (
                proj[r * S5_STEPS:(r + 1) * S5_STEPS, c0:c0 + LANES])


def _in_proj(xt, norm_w, w_in):
    t = xt.shape[0]
    n_out = 2 * D_CONV + D_SSM
    return pl.pallas_call(
        _in_proj_kernel,
        grid=(t // TM_PROJ,),
        in_specs=[
            pl.BlockSpec((TM_PROJ, D_MODEL), lambda i: (i, 0)),
            pl.BlockSpec((1, D_MODEL), lambda i: (0, 0)),
            pl.BlockSpec((D_MODEL, n_out), lambda i: (0, 0)),
        ],
        out_specs=(
            pl.BlockSpec((TM_PROJ, D_CONV), lambda i: (i, 0)),
            pl.BlockSpec((D_SSM // LANES, TM_PROJ, LANES), lambda i: (0, i, 0)),
        ),
        out_shape=(
            jax.ShapeDtypeStruct((t, D_CONV), F32),
            jax.ShapeDtypeStruct((D_SSM // LANES, t, LANES), F32),
        ),
        compiler_params=pltpu.CompilerParams(
            dimension_semantics=("arbitrary",), vmem_limit_bytes=VMEM_SMALL),
        name="in_proj",
    )(xt, norm_w, w_in)


def _conv_kernel(v_ref, w_ref, b_ref, lnw_ref, lnb_ref, o_ref, ext_ref, sh_ref):
    tt = pl.program_id(1)
    rows = CONV_HALO + TM_CONV

    @pl.when(tt == 0)
    def _():
        ext_ref[pl.ds(0, CONV_HALO), :] = jnp.zeros((CONV_HALO, D_CONV), F32)

    @pl.when(tt > 0)
    def _():
        ext_ref[pl.ds(0, CONV_HALO), :] = ext_ref[pl.ds(TM_CONV, CONV_HALO), :]

    ext_ref[pl.ds(CONV_HALO, TM_CONV), :] = v_ref[...]
    for s in range(1, SUBLANES):
        sh_ref[s - 1, pl.ds(0, rows - SUBLANES), :] = ext_ref[pl.ds(s, rows - SUBLANES), :]
    bias = b_ref[...]
    lnw = lnw_ref[...]
    lnb = lnb_ref[...]
    tap0 = CONV_HALO - (CONV_WIDTH - 1)

    groups = CONV_CHUNK // SUBLANES

    def chunk(ci, carry):
        base = pl.multiple_of(ci * CONV_CHUNK, CONV_CHUNK)
        for lt in range(D_CONV // LANES):
            lanes = pl.ds(lt * LANES, LANES)
            acc = [jnp.broadcast_to(bias[:, lt * LANES:(lt + 1) * LANES], (SUBLANES, LANES))] * groups
            for s in range(SUBLANES):
                taps = [j for j in range(CONV_WIDTH) if (tap0 + j) % SUBLANES == s]
                src = ext_ref if s == 0 else sh_ref.at[s - 1]
                ngroups = (tap0 + taps[-1] - s) // SUBLANES + groups
                win = [src[pl.ds(base + SUBLANES * g, SUBLANES), lanes] for g in range(ngroups)]
                for j in taps:
                    g0 = (tap0 + j - s) // SUBLANES
                    wj = w_ref[j, :, lanes]
                    acc = [acc[r] + wj * win[g0 + r] for r in range(groups)]
            o_ref[pl.ds(base, CONV_CHUNK), lanes] = jnp.concatenate(acc, axis=0)
        return carry

    lax.fori_loop(0, TM_CONV // CONV_CHUNK, chunk, 0)

    for bi in range(TM_CONV // CONV_NORM_ROWS):
        rows_b = pl.ds(bi * CONV_NORM_ROWS, CONV_NORM_ROWS)
        acc = o_ref[rows_b, :]
        mu = jnp.mean(acc, axis=-1, keepdims=True)
        cen = acc - mu
        var = jnp.mean(cen * cen, axis=-1, keepdims=True)
        z = cen * lax.rsqrt(var + EPS) * lnw + lnb
        o_ref[rows_b, :] = z * jax.nn.sigmoid(z)


def _conv(v, conv_w, conv_b, ln_w, ln_b, bsz, seq):
    nt = seq // TM_CONV
    row = lambda b, t: (b * nt + t, 0)
    const = lambda b, t: (0, 0)
    return pl.pallas_call(
        _conv_kernel,
        grid=(bsz, nt),
        in_specs=[
            pl.BlockSpec((TM_CONV, D_CONV), row),
            pl.BlockSpec((CONV_WIDTH, SUBLANES, D_CONV), lambda b, t: (0, 0, 0)),
            pl.BlockSpec((1, D_CONV), const),
            pl.BlockSpec((1, D_CONV), const),
            pl.BlockSpec((1, D_CONV), const),
        ],
        out_specs=pl.BlockSpec((TM_CONV, D_CONV), row),
        out_shape=jax.ShapeDtypeStruct(v.shape, F32),
        scratch_shapes=[
            pltpu.VMEM((CONV_HALO + TM_CONV, D_CONV), F32),
            pltpu.VMEM((SUBLANES - 1, CONV_HALO + TM_CONV, D_CONV), F32),
        ],
        compiler_params=pltpu.CompilerParams(
            dimension_semantics=("arbitrary", "arbitrary")),
        name="conv",
    )(v, jnp.broadcast_to(conv_w[:, None, :], (CONV_WIDTH, SUBLANES, D_CONV)), conv_b, ln_w, ln_b)


def _cmul(a_re, a_im, b_re, b_im):
    return a_re * b_re - a_im * b_im, a_re * b_im + a_im * b_re


def _s5_kernel(u_ref, bcat_ref, ccat_ref, d_ref, wab_ref, a1_ref,
               pj_ref, aj_ref, o_ref, up_ref, bu_ref, st_ref, carry_ref):
    tt = pl.program_id(1)
    ns = S5_STATE
    nseq = u_ref.shape[0]
    steps_per_group = S5_GROUP_ROWS // SUBLANES
    n_groups = S5_TILE // S5_GROUP_ROWS

    @pl.when((pl.program_id(0) == 0) & (tt == 0))
    def _():
        up_ref[...] = jnp.zeros(up_ref.shape, F32)
        st_ref[...] = jnp.zeros(st_ref.shape, BF16)

    @pl.when(tt == 0)
    def _():
        carry_ref[...] = jnp.zeros(carry_ref.shape, F32)

    def scan_and_project(cur, prv):
        up_ref[cur] = u_ref[...]

        a_re = a1_ref[0]
        a_im = a1_ref[1]

        def step(q, j, s):
            rows = pl.ds(j * SUBLANES, SUBLANES)
            m_re, m_im = _cmul(a_re, a_im, s[0], s[1])
            return m_re + bu_ref[q, rows, pl.ds(0, ns)], m_im + bu_ref[q, rows, pl.ds(ns, ns)]

        def project_out(q, g):
            rows = pl.ds(g * S5_GROUP_ROWS, S5_GROUP_ROWS)
            y = _dot(st_ref[prv, q, rows, :], ccat_ref[0]) + d_ref[...] * up_ref[prv, q, rows, :]
            y = jax.nn.gelu(y)
            ab = _dot(y.astype(BF16), wab_ref[0])
            o_ref[q, rows, :] = ab[:, :S5_LANES] * jax.nn.sigmoid(ab[:, S5_LANES:])

        zero = jnp.zeros((SUBLANES, ns), F32)
        state = [(zero, zero)] * nseq
        for g in range(n_groups):
            rows = pl.ds(g * S5_GROUP_ROWS, S5_GROUP_ROWS)
            for q in range(nseq):
                bu_ref[q, rows, :] = _dot(up_ref[cur, q, rows, :].astype(BF16), bcat_ref[0])
            for jj in range(steps_per_group):
                state = [step(q, g * steps_per_group + jj, state[q]) for q in range(nseq)]
            for q in range(nseq):
                project_out(q, g)

        row = lax.broadcasted_iota(jnp.int32, (SUBLANES, ns), 0)
        first = row == 0
        entry = []
        for q in range(nseq):
            e_re, e_im = state[q]
            c_re = jnp.where(first, pltpu.roll(carry_ref[q, 0], 1, 0), pltpu.roll(e_re, 1, 0))
            c_im = jnp.where(first, pltpu.roll(carry_ref[q, 1], 1, 0), pltpu.roll(e_im, 1, 0))
            for i, d in enumerate((1, 2, 4)):
                r_re = pltpu.roll(c_re, d, 0)
                r_im = pltpu.roll(c_im, d, 0)
                m_re, m_im = _cmul(pj_ref[0, i], pj_ref[1, i], r_re, r_im)
                c_re = c_re + m_re
                c_im = c_im + m_im
            f_re, f_im = _cmul(aj_ref[0], aj_ref[1], c_re, c_im)
            carry_ref[q, 0] = f_re + e_re
            carry_ref[q, 1] = f_im + e_im
            entry.append((c_re, c_im))

        state = entry
        pack = 2 * SUBLANES
        for j in range(0, S5_STEPS, 2):
            mid = [step(q, j, state[q]) for q in range(nseq)]
            state = [step(q, j + 1, mid[q]) for q in range(nseq)]
            for q in range(nseq):
                st_ref[cur, q, pl.ds(j * SUBLANES, pack), pl.ds(0, ns)] = (
                    jnp.concatenate([mid[q][0], state[q][0]], axis=0).astype(BF16))
                st_ref[cur, q, pl.ds(j * SUBLANES, pack), pl.ds(ns, ns)] = (
                    jnp.concatenate([mid[q][1], state[q][1]], axis=0).astype(BF16))

    for parity in range(2):
        pl.when(tt % 2 == parity)(functools.partial(scan_and_project, parity, 1 - parity))


def _s5(u, bcat, ccat, d, wab, a1, pj, aj, bsz, seq):
    nt = seq // S5_TILE
    nlb = D_SSM // S5_LANES
    ns = S5_STATE
    assert S5_LANES == LANES and S5_TILE == TM_PROJ == TM_MIX
    u4 = u.reshape(nlb, bsz, seq, S5_LANES)
    row_in = lambda l, t: (l, 0, jnp.minimum(t, nt - 1), 0)
    row_out = lambda l, t: (l, 0, jnp.maximum(t - 1, 0), 0)
    lane3 = lambda l, t: (0, 0, l)
    lane4 = lambda l, t: (0, 0, 0, l)
    out = pl.pallas_call(
        _s5_kernel,
        grid=(nlb, nt + 1),
        in_specs=[
            pl.BlockSpec((None, bsz, S5_TILE, S5_LANES), row_in),
            pl.BlockSpec((1, S5_LANES, 2 * ns), lambda l, t: (l, 0, 0)),
            pl.BlockSpec((1, 2 * ns, S5_LANES), lambda l, t: (l, 0, 0)),
            pl.BlockSpec((1, S5_LANES), lambda l, t: (0, l)),
            pl.BlockSpec((1, S5_LANES, 2 * S5_LANES), lambda l, t: (l, 0, 0)),
            pl.BlockSpec((2, SUBLANES, ns), lane3),
            pl.BlockSpec((2, 3, SUBLANES, ns), lane4),
            pl.BlockSpec((2, SUBLANES, ns), lane3),
        ],
        out_specs=pl.BlockSpec((None, bsz, S5_TILE, S5_LANES), row_out),
        out_shape=jax.ShapeDtypeStruct(u4.shape, F32),
        scratch_shapes=[
            pltpu.VMEM((2, bsz, S5_TILE, S5_LANES), F32),
            pltpu.VMEM((bsz, S5_TILE, 2 * ns), F32),
            pltpu.VMEM((2, bsz, S5_TILE, 2 * ns), BF16),
            pltpu.VMEM((bsz, 2, SUBLANES, ns), F32),
        ],
        compiler_params=pltpu.CompilerParams(
            dimension_semantics=("arbitrary", "arbitrary"),
            vmem_limit_bytes=VMEM_SMALL),
        name="s5",
    )(u4, bcat, ccat, d, wab, a1, pj, aj)
    return out.reshape(nlb, bsz * seq, S5_LANES)


def _kv_kernel(m_ref, nw_ref, wk_ref, wv_ref, k_ref, v_ref):
    mn = _rms(m_ref[0], nw_ref[...]).astype(BF16)
    k_ref[0] = _dot(mn, wk_ref[...]).astype(BF16)
    v_ref[0] = _dot(mn, wv_ref[...]).astype(BF16)


def _kv(mem, norm_w, wk, wv):
    bsz, mlen, d = mem.shape
    blk = pl.BlockSpec((1, mlen, d), lambda b: (b, 0, 0))
    wspec = pl.BlockSpec((d, d), lambda b: (0, 0))
    return pl.pallas_call(
        _kv_kernel,
        grid=(bsz,),
        in_specs=[blk, pl.BlockSpec((1, d), lambda b: (0, 0)), wspec, wspec],
        out_specs=(blk, blk),
        out_shape=(jax.ShapeDtypeStruct(mem.shape, BF16),) * 2,
        compiler_params=pltpu.CompilerParams(
            dimension_semantics=("arbitrary",), vmem_limit_bytes=VMEM_SMALL),
        name="kv",
    )(mem, norm_w, wk, wv)


def _mix_attn_kernel(x_ref, yc_ref, ys_ref, wot_ref, wob_ref, nx_ref, wq_ref,
                     k_ref, v_ref, wo_ref, nf_ref, wr_ref, br_ref,
                     h_ref, xs_ref, y0_ref, gs_ref, ri_ref, cnt_ref,
                     h2s_ref, tri_ref, rho_ref):
    i = pl.program_id(0)
    tm = x_ref.shape[0]
    neg = -jnp.inf
    big = float(ROUTE_LANES)

    @pl.when(i == 0)
    def _():
        h2s_ref[...] = jnp.zeros(h2s_ref.shape, F32)
        r_i = lax.broadcasted_iota(jnp.int32, (tm, tm), 0)
        c_i = lax.broadcasted_iota(jnp.int32, (tm, tm), 1)
        tri_ref[...] = jnp.where(r_i > c_i, 1.0, 0.0).astype(BF16)
        rho_ref[...] = lax.broadcasted_iota(jnp.int32, (tm, TILE_ROWS), 1).astype(F32)

    ys_time = jnp.concatenate(
        [jnp.concatenate([ys_ref[lt, pl.ds(r, S5_STEPS, stride=SUBLANES), :]
                          for r in range(SUBLANES)], axis=0)
         for lt in range(D_SSM // LANES)], axis=1)
    h1 = (x_ref[...] + _dot(yc_ref[...].astype(BF16), wot_ref[...])
          + _dot(ys_time.astype(BF16), wob_ref[...]))
    hn = _rms(h1, nx_ref[...]).astype(BF16)
    q = _dot(hn, wq_ref[...])

    hf = _rms(h2s_ref[...], nf_ref[...]).astype(BF16)
    logits = _dot(hf, wr_ref[...]) + br_ref[...]
    lane = lax.broadcasted_iota(jnp.int32, (tm, ROUTE_LANES), 1)
    lane_f = lane.astype(F32)

    def top1(vals):
        m = jnp.max(vals, axis=-1, keepdims=True)
        idx = jnp.min(jnp.where(vals == m, lane_f, big), axis=-1, keepdims=True)
        return m, idx

    gl = jnp.where(lane < N_EXPERT_GROUPS, logits, neg)
    gmax, gidx = top1(gl)
    g_w = 1.0 / jnp.sum(jnp.exp(gl - gmax), axis=-1, keepdims=True)
    lo = EXPERT_LANE0 + EXPERTS_PER_GROUP * gidx
    el = jnp.where((lane_f >= lo) & (lane_f < lo + EXPERTS_PER_GROUP), logits, neg)
    m1, i1 = top1(el)
    m2, i2 = top1(jnp.where(lane_f == i1, neg, el))
    e21 = jnp.exp(m2 - m1)
    gate1 = g_w / (1.0 + e21)
    gate2 = g_w * e21 / (1.0 + e21)

    heads = []
    for hd in range(N_XHEADS):
        sl = slice(hd * XHEAD_DIM, (hd + 1) * XHEAD_DIM)
        qh = q[:, sl].astype(BF16)
        s = lax.dot_general(qh, k_ref[0, :, sl], (((1,), (1,)), ((), ())),
                            preferred_element_type=F32)
        s = s - jnp.max(s, axis=-1, keepdims=True)
        p = jnp.exp(s)
        p = p / jnp.sum(p, axis=-1, keepdims=True)
        heads.append(_dot(p.astype(BF16), v_ref[0, :, sl]).astype(BF16))
    o = jnp.concatenate(heads, axis=-1)

    hot1 = lane_f == i1
    hot2 = lane_f == i2
    hot = jnp.where(hot1 | hot2, 1.0, 0.0)
    before = _dot(tri_ref[...], hot.astype(BF16))
    count = jnp.sum(hot, axis=0, keepdims=True)
    chunks = jnp.floor((count + (MOE_CHUNK - 1.0)) * (1.0 / MOE_CHUNK))
    l_i = lax.broadcasted_iota(jnp.int32, (ROUTE_LANES, ROUTE_LANES), 0)
    l_j = lax.broadcasted_iota(jnp.int32, (ROUTE_LANES, ROUTE_LANES), 1)
    upper = jnp.where(l_i < l_j, 1.0, 0.0).astype(BF16)
    first_chunk = _dot(jnp.broadcast_to(chunks, (SUBLANES, ROUTE_LANES)).astype(BF16), upper)[0:1]
    start = first_chunk * float(MOE_CHUNK) + before
    pos1 = jnp.sum(jnp.where(hot1, start, 0.0), axis=-1, keepdims=True)
    pos2 = jnp.sum(jnp.where(hot2, start, 0.0), axis=-1, keepdims=True)
    cnt_ref[0] = jnp.broadcast_to(count, (SUBLANES, ROUTE_LANES)).astype(jnp.int32)
    ri_ref[...] = jnp.where(lane == 0, pos1, jnp.where(lane == 1, pos2, 0.0)).astype(jnp.int32)

    h2 = h1 + _dot(o, wo_ref[...])
    h_ref[...] = h2
    h2s_ref[...] = h2

    def pieces(g):
        hi = g.astype(BF16).astype(F32)
        mid = (g - hi).astype(BF16).astype(F32)
        low = (g - hi - mid).astype(BF16).astype(F32)
        return hi, mid, low

    g6 = jnp.where(lane == 6, 1.0, 0.0)
    for li, piece in enumerate(pieces(gate1) + pieces(gate2)):
        g6 = jnp.where(lane == li, piece, g6)
    rho = rho_ref[...]
    ptk = jnp.where(rho == pos1, 1.0, jnp.where(rho == pos2, 2.0, 0.0)).astype(BF16)
    rhs = jnp.concatenate([hf, g6.astype(BF16)], axis=1)
    res = lax.dot_general(ptk, rhs, (((0,), (0,)), ((), ())), preferred_element_type=F32)
    sg = res[:, D_MODEL:]
    which = sg[:, 6:7]
    srt = res[:, :D_MODEL] * jnp.where(which == 2.0, 0.5, 1.0)
    xs_ref[...] = _pack_words(srt).reshape(TILE_CHUNKS, MOE_CHUNK, D_PACK)
    y0_ref[...] = jnp.zeros(y0_ref.shape, jnp.uint32)
    first = sg[:, 0:1] + sg[:, 1:2] + sg[:, 2:3]
    second = 0.5 * (sg[:, 3:4] + sg[:, 4:5] + sg[:, 5:6])
    gsort = jnp.where(which == 1.0, first, jnp.where(which == 2.0, second, 0.0))
    gs_ref[...] = jnp.broadcast_to(gsort, (TILE_ROWS, ROUTE_LANES))


def _pack_words(v):
    hi = lax.bitcast_convert_type(v[:, :D_PACK], jnp.uint32) & jnp.uint32(HI_MASK)
    lo = lax.bitcast_convert_type(v[:, D_PACK:], jnp.uint32) >> 16
    return hi | lo


def _unpack_words(w):
    hi = lax.bitcast_convert_type(w & jnp.uint32(HI_MASK), F32)
    lo = lax.bitcast_convert_type(w << 16, F32)
    return hi, lo


def _mix_attn(xt, yc, ys, wot, wob, nx, wq, k, v, wo, nf, wr, br, seq):
    t, d = xt.shape
    nt = t // TM_MIX
    tiles_per_batch = seq // TM_MIX
    mlen = k.shape[1]
    att = lambda i: jnp.minimum(i, nt - 1)
    rte = lambda i: jnp.maximum(i - 1, 0)
    row_a = lambda i: (att(i), 0)
    row_r = lambda i: (rte(i), 0)
    const = lambda i: (0, 0)
    tile3 = lambda i: (rte(i), 0, 0)
    kvspec = pl.BlockSpec((1, mlen, d), lambda i: (att(i) // tiles_per_batch, 0, 0))
    packed = jax.ShapeDtypeStruct((nt * TILE_CHUNKS, MOE_CHUNK, D_PACK), jnp.uint32)
    return pl.pallas_call(
        _mix_attn_kernel,
        grid=(nt + 1,),
        in_specs=[
            pl.BlockSpec((TM_MIX, d), row_a),
            pl.BlockSpec((TM_MIX, D_CONV), row_a),
            pl.BlockSpec((D_SSM // LANES, TM_MIX, LANES), lambda i: (0, att(i), 0)),
            pl.BlockSpec((D_CONV, d), const),
            pl.BlockSpec((D_SSM, d), const),
            pl.BlockSpec((1, d), const),
            pl.BlockSpec((d, d), const),
            kvspec, kvspec,
            pl.BlockSpec((d, d), const),
            pl.BlockSpec((1, d), const),
            pl.BlockSpec((d, ROUTE_LANES), const),
            pl.BlockSpec((1, ROUTE_LANES), const),
        ],
        out_specs=(
            pl.BlockSpec((TM_MIX, d), row_a),
            pl.BlockSpec((TILE_CHUNKS, MOE_CHUNK, D_PACK), tile3),
            pl.BlockSpec((TILE_CHUNKS, MOE_CHUNK, D_PACK), tile3),
            pl.BlockSpec((TILE_ROWS, ROUTE_LANES), row_r),
            pl.BlockSpec((TM_MIX, ROUTE_LANES), row_r),
            pl.BlockSpec((1, SUBLANES, ROUTE_LANES), tile3),
        ),
        out_shape=(
            jax.ShapeDtypeStruct((t, d), F32),
            packed,
            packed,
            jax.ShapeDtypeStruct((nt * TILE_ROWS, ROUTE_LANES), F32),
            jax.ShapeDtypeStruct((t, ROUTE_LANES), jnp.int32),
            jax.ShapeDtypeStruct((nt, SUBLANES, ROUTE_LANES), jnp.int32),
        ),
        scratch_shapes=[
            pltpu.VMEM((TM_MIX, d), F32),
            pltpu.VMEM((TM_MIX, TM_MIX), BF16),
            pltpu.VMEM((TM_MIX, TILE_ROWS), F32),
        ],
        compiler_params=pltpu.CompilerParams(
            dimension_semantics=("arbitrary",), vmem_limit_bytes=VMEM_MIX),
        name="mix_attn",
    )(xt, yc, ys, wot, wob, nx, wq, k, v, wo, nf, wr, br)


ZERO_CHUNK = TILE_CHUNKS - 1
DUMP_TILE0 = 16
SPARE_PER_TILE = TILE_CHUNKS - TILE_USED_CHUNKS


def _dump_chunk(slot, k):
    idx = slot * BLOCK_CHUNKS + k
    return (DUMP_TILE0 + idx // SPARE_PER_TILE) * TILE_CHUNKS + TILE_USED_CHUNKS + idx % SPARE_PER_TILE


def _experts_kernel(src_ref, dst_ref, be_ref, seq_ref, nxt_ref, na_ref, xs_hbm, w1_hbm, w3_hbm,
                    w2_hbm, y0_hbm, ys_hbm, xbuf, ybuf, w1f_ref, w3f_ref, w2f_ref,
                    w1s_ref, w3s_ref, w2s_ref, in_sem, out_sem, w_sem):
    del y0_hbm
    b = pl.program_id(0)
    n_active = na_ref[0]
    slot = b % 2
    other = 1 - slot

    def gather(blk, sl):
        for k in range(BLOCK_CHUNKS):
            pltpu.make_async_copy(xs_hbm.at[src_ref[blk * BLOCK_CHUNKS + k]],
                                  xbuf.at[sl, k], in_sem.at[sl]).start()

    def gather_wait(sl):
        pltpu.make_async_copy(xs_hbm.at[pl.ds(0, BLOCK_CHUNKS)], xbuf.at[sl],
                              in_sem.at[sl]).wait()

    def scatter(blk, sl):
        for k in range(BLOCK_CHUNKS):
            d = dst_ref[blk * BLOCK_CHUNKS + k]
            dump = jnp.where(sl == 0, _dump_chunk(0, k), _dump_chunk(1, k))
            pltpu.make_async_copy(ybuf.at[sl, k], ys_hbm.at[jnp.where(d < 0, dump, d)],
                                  out_sem.at[sl]).start()

    def scatter_wait(sl):
        pltpu.make_async_copy(ybuf.at[sl], ys_hbm.at[pl.ds(0, BLOCK_CHUNKS)],
                              out_sem.at[sl]).wait()

    active = b < n_active

    @pl.when(b == 0)
    def _():
        gather(0, 0)
        ybuf[...] = jnp.zeros(ybuf.shape, jnp.uint32)

    prev = be_ref[jnp.maximum(b - 1, 0)]
    fresh = (b == 0) | (be_ref[b] != prev)

    def weight_copies(e, ws):
        return [pltpu.make_async_copy(hbm.at[e], buf.at[ws], w_sem.at[ws])
                for hbm, buf in ((w1_hbm, w1f_ref), (w3_hbm, w3f_ref), (w2_hbm, w2f_ref))]

    @pl.when(active & fresh)
    def _():
        ws = seq_ref[b] % 2

        @pl.when(b == 0)
        def _():
            for cp in weight_copies(be_ref[b], ws):
                cp.start(priority=1)

        for cp in weight_copies(be_ref[b], ws):
            cp.wait()

        @pl.when(nxt_ref[b] >= 0)
        def _():
            for cp in weight_copies(nxt_ref[b], 1 - ws):
                cp.start(priority=1)

        w1s_ref[...] = w1f_ref[ws].astype(BF16)
        w3s_ref[...] = w3f_ref[ws].astype(BF16)
        w2s_ref[...] = w2f_ref[ws].astype(BF16)

    @pl.when(active & (b >= 1))
    def _():
        scatter_wait(slot)

    @pl.when(active)
    def _():
        gather_wait(slot)
        gather(b + 1, other)
        scatter(b, other)

        hi, lo = _unpack_words(xbuf[slot].reshape(MOE_BLOCK, D_PACK))
        xb = jnp.concatenate([hi, lo], axis=1).astype(BF16)
        h1 = _dot(xb, w1s_ref[...])
        h3 = _dot(xb, w3s_ref[...])
        hid = (h1 * jax.nn.sigmoid(h1) * h3).astype(BF16)
        y = _dot(hid, w2s_ref[...]).astype(BF16).astype(F32)
        ybuf[slot] = _pack_words(y).reshape(BLOCK_CHUNKS, MOE_CHUNK, D_PACK)

        @pl.when(b == n_active - 1)
        def _():
            scatter(b + 1, slot)
            gather_wait(other)
            scatter_wait(other)
            scatter_wait(slot)


def _experts(src, dst, block_expert, expert_seq, next_expert, n_active, xs, w1, w3, w2, y0):
    nb = block_expert.shape[0]
    d = D_MODEL
    hbm = pl.BlockSpec(memory_space=pl.ANY)
    grid_spec = pltpu.PrefetchScalarGridSpec(
        num_scalar_prefetch=6,
        grid=(nb,),
        in_specs=[hbm, hbm, hbm, hbm, hbm],
        out_specs=hbm,
        scratch_shapes=[
            pltpu.VMEM((2, BLOCK_CHUNKS, MOE_CHUNK, D_PACK), jnp.uint32),
            pltpu.VMEM((2, BLOCK_CHUNKS, MOE_CHUNK, D_PACK), jnp.uint32),
            pltpu.VMEM((2, d, D_EXPERT), F32),
            pltpu.VMEM((2, d, D_EXPERT), F32),
            pltpu.VMEM((2, D_EXPERT, d), F32),
            pltpu.VMEM((d, D_EXPERT), BF16),
            pltpu.VMEM((d, D_EXPERT), BF16),
            pltpu.VMEM((D_EXPERT, d), BF16),
            pltpu.SemaphoreType.DMA((2,)),
            pltpu.SemaphoreType.DMA((2,)),
            pltpu.SemaphoreType.DMA((2,)),
        ],
    )
    return pl.pallas_call(
        _experts_kernel,
        grid_spec=grid_spec,
        out_shape=jax.ShapeDtypeStruct(y0.shape, jnp.uint32),
        input_output_aliases={10: 0},
        compiler_params=pltpu.CompilerParams(
            dimension_semantics=("arbitrary",), vmem_limit_bytes=VMEM_EXPERTS),
        name="experts",
    )(src, dst, block_expert, expert_seq, next_expert, n_active, xs, w1, w3, w2, y0)


def _combine_kernel(h_ref, ri_ref, gs_ref, ys_ref, fw_ref, o_ref):
    tm = h_ref.shape[0]
    hi, lo = _unpack_words(ys_ref[...].reshape(TILE_ROWS, D_PACK))
    g = gs_ref[...]
    gw = jnp.concatenate([g] * (D_PACK // ROUTE_LANES), axis=1)
    yg = jnp.concatenate([hi * gw, lo * gw], axis=1).astype(BF16)
    pos = ri_ref[...].astype(F32)
    rho = lax.broadcasted_iota(jnp.int32, (tm, TILE_ROWS), 1).astype(F32)
    q = jnp.where((rho == pos[:, 0:1]) | (rho == pos[:, 1:2]), 1.0, 0.0).astype(BF16)
    o_ref[...] = _rms(h_ref[...] + _dot(q, yg), fw_ref[...])


def _combine(h2, route_i, gsort, ys, final_w):
    t, d = h2.shape
    nt = t // TM_MIX
    row = lambda i: (i, 0)
    return pl.pallas_call(
        _combine_kernel,
        grid=(nt,),
        in_specs=[
            pl.BlockSpec((TM_MIX, d), row),
            pl.BlockSpec((TM_MIX, ROUTE_LANES), row),
            pl.BlockSpec((TILE_ROWS, ROUTE_LANES), row),
            pl.BlockSpec((TILE_CHUNKS, MOE_CHUNK, D_PACK), lambda i: (i, 0, 0)),
            pl.BlockSpec((1, d), lambda i: (0, 0)),
        ],
        out_specs=pl.BlockSpec((TM_MIX, d), row),
        out_shape=jax.ShapeDtypeStruct((t, d), F32),
        compiler_params=pltpu.CompilerParams(
            dimension_semantics=("arbitrary",), vmem_limit_bytes=VMEM_EXPERTS),
        name="combine",
    )(h2, route_i, gsort, ys, final_w)


def _routing_tables(cnt):
    nt = cnt.shape[0]
    n = cnt[:, 0, EXPERT_LANE0:EXPERT_LANE0 + N_EXPERTS]
    c = (n + MOE_CHUNK - 1) // MOE_CHUNK
    local = jnp.cumsum(c, axis=1) - c
    per_expert = jnp.sum(c, axis=0)
    padded = (per_expert + BLOCK_CHUNKS - 1) // BLOCK_CHUNKS * BLOCK_CHUNKS
    gend = jnp.cumsum(padded)
    gstart = gend - padded
    within = jnp.cumsum(c, axis=0) - c
    g_max = (2 * nt * TM_MIX // MOE_CHUNK + nt * N_EXPERTS
             + N_EXPERTS * (BLOCK_CHUNKS - 1))
    n_blocks = -(-g_max // BLOCK_CHUNKS)
    g = jnp.arange(n_blocks * BLOCK_CHUNKS, dtype=jnp.int32)
    e_of = jnp.minimum(jnp.sum((gend[None, :] <= g[:, None]).astype(jnp.int32), axis=1),
                       N_EXPERTS - 1)
    hot_e = (e_of[:, None] == jnp.arange(N_EXPERTS, dtype=jnp.int32)[None, :]).astype(F32)
    pick = lambda table: jnp.dot(hot_e, table.T.astype(F32), precision=lax.Precision.HIGHEST)
    rel = g.astype(F32) - pick(gstart[None, :])[:, 0]
    w_rows = pick(within)
    tile = jnp.sum((w_rows <= rel[:, None]).astype(jnp.int32), axis=1) - 1
    hot_t = tile[:, None] == jnp.arange(nt, dtype=jnp.int32)[None, :]
    at_tile = lambda rows: jnp.sum(jnp.where(hot_t, rows, 0.0), axis=1)
    k = rel - at_tile(w_rows)
    valid = k < at_tile(pick(c))
    chunk = (tile * TILE_CHUNKS + (at_tile(pick(local)) + k).astype(jnp.int32))
    pad_src = jnp.full((BLOCK_CHUNKS,), ZERO_CHUNK, jnp.int32)
    pad_dst = jnp.full((BLOCK_CHUNKS,), -1, jnp.int32)
    src = jnp.concatenate([jnp.where(valid, chunk, ZERO_CHUNK).astype(jnp.int32), pad_src])
    dst = jnp.concatenate([pad_dst, jnp.where(valid, chunk, -1).astype(jnp.int32)])
    block_start = jnp.arange(n_blocks, dtype=jnp.int32) * BLOCK_CHUNKS
    block_expert = jnp.minimum(
        jnp.sum((block_start[:, None] >= gend[None, :]).astype(jnp.int32), axis=1),
        N_EXPERTS - 1).astype(jnp.int32)
    n_active = (gend[-1:] // BLOCK_CHUNKS).astype(jnp.int32)
    blk = jnp.arange(n_blocks, dtype=jnp.int32)
    is_first = (blk < n_active[0]) & ((blk == 0) | (block_expert != jnp.roll(block_expert, 1)))
    expert_seq = (jnp.cumsum(is_first.astype(jnp.int32)) - 1).astype(jnp.int32)
    later_first = is_first[None, :] & (blk[None, :] > blk[:, None])
    next_blk = jnp.min(jnp.where(later_first, blk[None, :], n_blocks), axis=1)
    next_expert = jnp.where(
        next_blk < n_blocks,
        jnp.sum(jnp.where(blk[None, :] == next_blk[:, None], block_expert[None, :], 0), axis=1),
        -1).astype(jnp.int32)
    return src, dst, block_expert, expert_seq, next_expert, n_active


def _layer(h, mem, p, final_w):
    bsz, seq, d = h.shape
    t = bsz * seq
    assert t // TM_MIX >= DUMP_TILE0 + 2 * BLOCK_CHUNKS // SPARE_PER_TILE
    xt = h.reshape(t, d)

    v, u = _in_proj(xt, p["norm_mix_w"].reshape(1, d), p["w_in"].astype(BF16))
    y_conv = _conv(v, p["conv_w"], p["conv_b"].reshape(1, D_CONV),
                   p["conv_ln_w"].reshape(1, D_CONV), p["conv_ln_b"].reshape(1, D_CONV),
                   bsz, seq)

    a1, pj, aj, bbar = _s5_prep(p["ssm_A_re"], p["ssm_A_im"], p["ssm_log_dt"],
                                      p["ssm_B_re"], p["ssm_B_im"])
    nlb = D_SSM // S5_LANES
    bb = bbar.reshape(2, SSM_GROUP, nlb, S5_GROUPS, SSM_STATE).transpose(0, 2, 3, 1, 4)
    bcat = jnp.concatenate([_block_diag(bb[0]), _block_diag(bb[1])], axis=-1).astype(BF16)
    c_re = p["ssm_C_re"].reshape(nlb, S5_GROUPS, SSM_GROUP, SSM_STATE).transpose(0, 1, 3, 2)
    c_im = p["ssm_C_im"].reshape(nlb, S5_GROUPS, SSM_GROUP, SSM_STATE).transpose(0, 1, 3, 2)
    ccat = jnp.concatenate([_block_diag(c_re), -_block_diag(c_im)], axis=1).astype(BF16)
    glu = p["ssm_glu_w"].reshape(nlb, S5_GROUPS, SSM_GROUP, 2 * SSM_GROUP)
    wab = jnp.concatenate([_block_diag(glu[..., :SSM_GROUP]),
                           _block_diag(glu[..., SSM_GROUP:])], axis=-1).astype(BF16)
    y_ssm = _s5(u, bcat, ccat, p["ssm_D"].reshape(1, D_SSM), wab,
                a1, pj, aj, bsz, seq)

    k, vv = _kv(mem, p["norm_mem_w"].reshape(1, d), p["xk_w"].astype(BF16),
                p["xv_w"].astype(BF16))
    w_out = p["w_out"].astype(BF16)
    wr = jnp.concatenate([p["router_group_w"], p["router_expert_w"]], axis=1)
    wr = jnp.pad(wr, ((0, 0), (0, ROUTE_LANES - wr.shape[1]))).astype(BF16)
    br = jnp.concatenate([p["router_group_b"], p["router_expert_b"].reshape(-1)])
    br = jnp.pad(br, (0, ROUTE_LANES - br.shape[0])).reshape(1, ROUTE_LANES)
    h2, xs, y0, gsort, route_i, cnt = _mix_attn(
        xt, y_conv, y_ssm, w_out[:D_CONV], w_out[D_CONV:], p["norm_x_w"].reshape(1, d),
        (p["xq_w"] * (XHEAD_DIM ** -0.5)).astype(BF16), k, vv, p["xo_w"].astype(BF16),
        p["norm_ffn_w"].reshape(1, d), wr, br, seq)

    src, dst, block_expert, expert_seq, next_expert, n_active = _routing_tables(cnt)
    ys = _experts(src, dst, block_expert, expert_seq, next_expert, n_active, xs,
                  p["moe_w1"], p["moe_w3"], p["moe_w2"], y0)
    out = _combine(h2, route_i, gsort, ys, final_w.reshape(1, d))
    return out.reshape(bsz, seq, d)


def kernel(x, mem, norm_mix_w, w_in, conv_w, conv_b, conv_ln_w, conv_ln_b, ssm_A_re, ssm_A_im, ssm_log_dt, ssm_B_re, ssm_B_im, ssm_C_re, ssm_C_im, ssm_D, ssm_glu_w, w_out, norm_x_w, norm_mem_w, xq_w, xk_w, xv_w, xo_w, norm_ffn_w, router_group_w, router_group_b, router_expert_w, router_expert_b, moe_w1, moe_w3, moe_w2, final_norm_w):
    stacked = dict(
        norm_mix_w=norm_mix_w, w_in=w_in, conv_w=conv_w, conv_b=conv_b,
        conv_ln_w=conv_ln_w, conv_ln_b=conv_ln_b, ssm_A_re=ssm_A_re, ssm_A_im=ssm_A_im,
        ssm_log_dt=ssm_log_dt, ssm_B_re=ssm_B_re, ssm_B_im=ssm_B_im, ssm_C_re=ssm_C_re,
        ssm_C_im=ssm_C_im, ssm_D=ssm_D, ssm_glu_w=ssm_glu_w, w_out=w_out,
        norm_x_w=norm_x_w, norm_mem_w=norm_mem_w, xq_w=xq_w, xk_w=xk_w, xv_w=xv_w,
        xo_w=xo_w, norm_ffn_w=norm_ffn_w, router_group_w=router_group_w,
        router_group_b=router_group_b, router_expert_w=router_expert_w,
        router_expert_b=router_expert_b, moe_w1=moe_w1, moe_w3=moe_w3, moe_w2=moe_w2)
    depth = norm_mix_w.shape[0]
    assert depth == 1, "final norm is fused into the single layer's combine step"
    layer = {name: w[0] for name, w in stacked.items()}
    return _layer(x, mem, layer, final_norm_w)
```

```python
import functools

import jax
import jax.numpy as jnp
from jax import lax
from jax.experimental import pallas as pl
from jax.experimental.pallas import tpu as pltpu

D_MODEL = 1024
D_CONV = 512
CONV_WIDTH = 31
D_SSM = 512
SSM_GROUP = 16
N_SSM_GROUPS = 32
SSM_STATE = 64
N_XHEADS = 4
XHEAD_DIM = 256
N_EXPERT_GROUPS = 4
EXPERTS_PER_GROUP = 8
N_EXPERTS = 32
D_EXPERT = 512
EPS = 1e-6

F32 = jnp.float32
BF16 = jnp.bfloat16

SUBLANES = 8
LANES = 128

TM_PROJ = 1024
TM_CONV = 1024
CONV_CHUNK = 128
CONV_NORM_ROWS = 128
CONV_HALO = 32
S5_STEPS = 64
S5_TILE = SUBLANES * S5_STEPS
S5_GROUP_ROWS = 256
S5_LANES = 128
S5_GROUPS = S5_LANES // SSM_GROUP
S5_STATE = S5_GROUPS * SSM_STATE
TM_MIX = 512
ROUTE_LANES = 128
EXPERT_LANE0 = N_EXPERT_GROUPS
MOE_CHUNK = SUBLANES
MOE_BLOCK = 256
BLOCK_CHUNKS = MOE_BLOCK // MOE_CHUNK
TILE_USED_CHUNKS = 2 * TM_MIX // MOE_CHUNK + N_EXPERTS * (MOE_CHUNK - 1) // MOE_CHUNK
TILE_CHUNKS = 160
TILE_ROWS = TILE_CHUNKS * MOE_CHUNK
D_PACK = D_MODEL // 2
HI_MASK = 0xFFFF0000

VMEM_SMALL = 40 << 20
VMEM_EXPERTS = 48 << 20
VMEM_MIX = 62 << 20


def _rms(x, w):
    return x * lax.rsqrt(jnp.mean(x * x, axis=-1, keepdims=True) + EPS) * w


def _dot(a, b):
    return jnp.dot(a, b, preferred_element_type=F32)


def _s5_prep_kernel(are_ref, aim_ref, ldt_ref, btre_ref, btim_ref,
                    a1_ref, pj_ref, aj_ref, bbar_ref):
    lam_re = are_ref[...]
    lam_im = aim_ref[...]
    dt = jnp.exp(ldt_ref[...])
    x = lam_re * dt
    y = lam_im * dt

    def power(k):
        mag = jnp.exp(k * x)
        return mag * jnp.cos(k * y), mag * jnp.sin(k * y)

    n = x.shape[-1]
    ones8 = jnp.ones((SUBLANES, n), F32)
    a_re, a_im = power(ones8)
    a1_ref[0] = a_re
    a1_ref[1] = a_im

    row = lax.broadcasted_iota(jnp.int32, (SUBLANES, n), 0)
    for i, d in enumerate((1, 2, 4)):
        p_re, p_im = power(ones8 * float(d * S5_STEPS))
        keep = row >= d
        pj_ref[0, i] = jnp.where(keep, p_re, 0.0)
        pj_ref[1, i] = jnp.where(keep, p_im, 0.0)
    j_re, j_im = power(ones8 * float(S5_STEPS))
    aj_ref[0] = j_re
    aj_ref[1] = j_im

    num_re = a_re[0:1] - 1.0
    num_im = a_im[0:1]
    den = lam_re * lam_re + lam_im * lam_im
    c_re = (num_re * lam_re + num_im * lam_im) / den
    c_im = (num_im * lam_re - num_re * lam_im) / den
    b_re = btre_ref[...]
    b_im = btim_ref[...]
    bbar_ref[0] = c_re * b_re - c_im * b_im
    bbar_ref[1] = c_re * b_im + c_im * b_re


def _s5_prep(a_re, a_im, log_dt, b_re, b_im):
    n = N_SSM_GROUPS * SSM_STATE
    are = a_re.reshape(1, n)
    aim = a_im.reshape(1, n)
    ldt = jnp.repeat(log_dt, SSM_STATE).reshape(1, n)
    btre = jnp.transpose(b_re, (2, 0, 1)).reshape(SSM_GROUP, n)
    btim = jnp.transpose(b_im, (2, 0, 1)).reshape(SSM_GROUP, n)
    return pl.pallas_call(
        _s5_prep_kernel,
        out_shape=(
            jax.ShapeDtypeStruct((2, SUBLANES, n), F32),
            jax.ShapeDtypeStruct((2, 3, SUBLANES, n), F32),
            jax.ShapeDtypeStruct((2, SUBLANES, n), F32),
            jax.ShapeDtypeStruct((2, SSM_GROUP, n), F32),
        ),
        name="s5_prep",
    )(are, aim, ldt, btre, btim)


def _block_diag(w):
    nl, g, r, c = w.shape
    eye = jnp.eye(g, dtype=w.dtype)
    return jnp.einsum("lgrc,gh->lgrhc", w, eye).reshape(nl, g * r, g * c)


def _in_proj_kernel(x_ref, nw_ref, w_ref, v_ref, u_ref):
    xn = _rms(x_ref[...], nw_ref[...]).astype(BF16)
    proj = _dot(xn, w_ref[...])
    a = proj[:, :D_CONV]
    g = proj[:, D_CONV:2 * D_CONV]
    v_ref[...] = a * jax.nn.sigmoid(g)
    for lt in range(D_SSM // LANES):
        c0 = 2 * D_CONV + lt * LANES
        for tile in range(TM_PROJ // S5_TILE):
            for r in range(SUBLANES):
                t0 = tile * S5_TILE + r * S5_STEPS
                u_ref[lt, pl.ds(tile * S5_TILE + r, S5_STEPS, stride=SUBLANES), :] = (
                    proj[t0:t0 + S5_STEPS, c0:c0 + LANES])


def _in_proj(xt, norm_w, w_in):
    t = xt.shape[0]
    n_out = 2 * D_CONV + D_SSM
    return pl.pallas_call(
        _in_proj_kernel,
        grid=(t // TM_PROJ,),
        in_specs=[
            pl.BlockSpec((TM_PROJ, D_MODEL), lambda i: (i, 0)),
            pl.BlockSpec((1, D_MODEL), lambda i: (0, 0)),
            pl.BlockSpec((D_MODEL, n_out), lambda i: (0, 0)),
        ],
        out_specs=(
            pl.BlockSpec((TM_PROJ, D_CONV), lambda i: (i, 0)),
            pl.BlockSpec((D_SSM // LANES, TM_PROJ, LANES), lambda i: (0, i, 0)),
        ),
        out_shape=(
            jax.ShapeDtypeStruct((t, D_CONV), F32),
            jax.ShapeDtypeStruct((D_SSM // LANES, t, LANES), F32),
        ),
        compiler_params=pltpu.CompilerParams(
            dimension_semantics=("arbitrary",), vmem_limit_bytes=VMEM_SMALL),
        name="in_proj",
    )(xt, norm_w, w_in)


def _conv_kernel(v_ref, w_ref, b_ref, lnw_ref, lnb_ref, o_ref, ext_ref, sh_ref):
    tt = pl.program_id(1)
    rows = CONV_HALO + TM_CONV

    @pl.when(tt == 0)
    def _():
        ext_ref[pl.ds(0, CONV_HALO), :] = jnp.zeros((CONV_HALO, D_CONV), F32)

    @pl.when(tt > 0)
    def _():
        ext_ref[pl.ds(0, CONV_HALO), :] = ext_ref[pl.ds(TM_CONV, CONV_HALO), :]

    ext_ref[pl.ds(CONV_HALO, TM_CONV), :] = v_ref[...]
    for s in range(1, SUBLANES):
        sh_ref[s - 1, pl.ds(0, rows - SUBLANES), :] = ext_ref[pl.ds(s, rows - SUBLANES), :]
    bias = b_ref[...]
    lnw = lnw_ref[...]
    lnb = lnb_ref[...]
    tap0 = CONV_HALO - (CONV_WIDTH - 1)

    groups = CONV_CHUNK // SUBLANES

    def chunk(ci, carry):
        base = pl.multiple_of(ci * CONV_CHUNK, CONV_CHUNK)
        for lt in range(D_CONV // LANES):
            lanes = pl.ds(lt * LANES, LANES)
            acc = [jnp.broadcast_to(bias[:, lt * LANES:(lt + 1) * LANES], (SUBLANES, LANES))] * groups
            for s in range(SUBLANES):
                taps = [j for j in range(CONV_WIDTH) if (tap0 + j) % SUBLANES == s]
                src = ext_ref if s == 0 else sh_ref.at[s - 1]
                ngroups = (tap0 + taps[-1] - s) // SUBLANES + groups
                win = [src[pl.ds(base + SUBLANES * g, SUBLANES), lanes] for g in range(ngroups)]
                for j in taps:
                    g0 = (tap0 + j - s) // SUBLANES
                    wj = w_ref[j, :, lanes]
                    acc = [acc[r] + wj * win[g0 + r] for r in range(groups)]
            o_ref[pl.ds(base, CONV_CHUNK), lanes] = jnp.concatenate(acc, axis=0)
        return carry

    lax.fori_loop(0, TM_CONV // CONV_CHUNK, chunk, 0)

    for bi in range(TM_CONV // CONV_NORM_ROWS):
        rows_b = pl.ds(bi * CONV_NORM_ROWS, CONV_NORM_ROWS)
        acc = o_ref[rows_b, :]
        mu = jnp.mean(acc, axis=-1, keepdims=True)
        cen = acc - mu
        var = jnp.mean(cen * cen, axis=-1, keepdims=True)
        z = cen * lax.rsqrt(var + EPS) * lnw + lnb
        o_ref[rows_b, :] = z * jax.nn.sigmoid(z)


def _conv(v, conv_w, conv_b, ln_w, ln_b, bsz, seq):
    nt = seq // TM_CONV
    row = lambda b, t: (b * nt + t, 0)
    const = lambda b, t: (0, 0)
    return pl.pallas_call(
        _conv_kernel,
        grid=(bsz, nt),
        in_specs=[
            pl.BlockSpec((TM_CONV, D_CONV), row),
            pl.BlockSpec((CONV_WIDTH, SUBLANES, D_CONV), lambda b, t: (0, 0, 0)),
            pl.BlockSpec((1, D_CONV), const),
            pl.BlockSpec((1, D_CONV), const),
            pl.BlockSpec((1, D_CONV), const),
        ],
        out_specs=pl.BlockSpec((TM_CONV, D_CONV), row),
        out_shape=jax.ShapeDtypeStruct(v.shape, F32),
        scratch_shapes=[
            pltpu.VMEM((CONV_HALO + TM_CONV, D_CONV), F32),
            pltpu.VMEM((SUBLANES - 1, CONV_HALO + TM_CONV, D_CONV), F32),
        ],
        compiler_params=pltpu.CompilerParams(
            dimension_semantics=("arbitrary", "arbitrary")),
        name="conv",
    )(v, jnp.broadcast_to(conv_w[:, None, :], (CONV_WIDTH, SUBLANES, D_CONV)), conv_b, ln_w, ln_b)


def _cmul(a_re, a_im, b_re, b_im):
    return a_re * b_re - a_im * b_im, a_re * b_im + a_im * b_re


def _s5_kernel(u_ref, bcat_ref, ccat_ref, d_ref, wab_ref, a1_ref,
               pj_ref, aj_ref, o_ref, up_ref, bu_ref, st_ref, carry_ref):
    tt = pl.program_id(1)
    ns = S5_STATE
    nseq = u_ref.shape[0]
    steps_per_group = S5_GROUP_ROWS // SUBLANES
    n_groups = S5_TILE // S5_GROUP_ROWS

    @pl.when((pl.program_id(0) == 0) & (tt == 0))
    def _():
        up_ref[...] = jnp.zeros(up_ref.shape, F32)
        st_ref[...] = jnp.zeros(st_ref.shape, BF16)

    @pl.when(tt == 0)
    def _():
        carry_ref[...] = jnp.zeros(carry_ref.shape, F32)

    def scan_and_project(cur, prv):
        up_ref[cur] = u_ref[...]

        a_re = a1_ref[0]
        a_im = a1_ref[1]

        def step(q, j, s):
            rows = pl.ds(j * SUBLANES, SUBLANES)
            m_re, m_im = _cmul(a_re, a_im, s[0], s[1])
            return m_re + bu_ref[q, rows, pl.ds(0, ns)], m_im + bu_ref[q, rows, pl.ds(ns, ns)]

        def project_out(q, g):
            rows = pl.ds(g * S5_GROUP_ROWS, S5_GROUP_ROWS)
            y = _dot(st_ref[prv, q, rows, :], ccat_ref[0]) + d_ref[...] * up_ref[prv, q, rows, :]
            y = jax.nn.gelu(y)
            ab = _dot(y.astype(BF16), wab_ref[0])
            o_ref[q, rows, :] = ab[:, :S5_LANES] * jax.nn.sigmoid(ab[:, S5_LANES:])

        zero = jnp.zeros((SUBLANES, ns), F32)
        state = [(zero, zero)] * nseq
        for g in range(n_groups):
            rows = pl.ds(g * S5_GROUP_ROWS, S5_GROUP_ROWS)
            for q in range(nseq):
                bu_ref[q, rows, :] = _dot(up_ref[cur, q, rows, :].astype(BF16), bcat_ref[0])
            for jj in range(steps_per_group):
                state = [step(q, g * steps_per_group + jj, state[q]) for q in range(nseq)]
            for q in range(nseq):
                project_out(q, g)

        row = lax.broadcasted_iota(jnp.int32, (SUBLANES, ns), 0)
        first = row == 0
        entry = []
        for q in range(nseq):
            e_re, e_im = state[q]
            c_re = jnp.where(first, pltpu.roll(carry_ref[q, 0], 1, 0), pltpu.roll(e_re, 1, 0))
            c_im = jnp.where(first, pltpu.roll(carry_ref[q, 1], 1, 0), pltpu.roll(e_im, 1, 0))
            for i, d in enumerate((1, 2, 4)):
                r_re = pltpu.roll(c_re, d, 0)
                r_im = pltpu.roll(c_im, d, 0)
                m_re, m_im = _cmul(pj_ref[0, i], pj_ref[1, i], r_re, r_im)
                c_re = c_re + m_re
                c_im = c_im + m_im
            f_re, f_im = _cmul(aj_ref[0], aj_ref[1], c_re, c_im)
            carry_ref[q, 0] = f_re + e_re
            carry_ref[q, 1] = f_im + e_im
            entry.append((c_re, c_im))

        state = entry
        pack = 2 * SUBLANES
        for j in range(0, S5_STEPS, 2):
            mid = [step(q, j, state[q]) for q in range(nseq)]
            state = [step(q, j + 1, mid[q]) for q in range(nseq)]
            for q in range(nseq):
                st_ref[cur, q, pl.ds(j * SUBLANES, pack), pl.ds(0, ns)] = (
                    jnp.concatenate([mid[q][0], state[q][0]], axis=0).astype(BF16))
                st_ref[cur, q, pl.ds(j * SUBLANES, pack), pl.ds(ns, ns)] = (
                    jnp.concatenate([mid[q][1], state[q][1]], axis=0).astype(BF16))

    for parity in range(2):
        pl.when(tt % 2 == parity)(functools.partial(scan_and_project, parity, 1 - parity))


def _s5(u, bcat, ccat, d, wab, a1, pj, aj, bsz, seq):
    nt = seq // S5_TILE
    nlb = D_SSM // S5_LANES
    ns = S5_STATE
    assert S5_LANES == LANES and S5_TILE == TM_MIX and TM_PROJ % S5_TILE == 0
    u4 = u.reshape(nlb, bsz, seq, S5_LANES)
    row_in = lambda l, t: (l, 0, jnp.minimum(t, nt - 1), 0)
    row_out = lambda l, t: (l, 0, jnp.maximum(t - 1, 0), 0)
    lane3 = lambda l, t: (0, 0, l)
    lane4 = lambda l, t: (0, 0, 0, l)
    out = pl.pallas_call(
        _s5_kernel,
        grid=(nlb, nt + 1),
        in_specs=[
            pl.BlockSpec((None, bsz, S5_TILE, S5_LANES), row_in),
            pl.BlockSpec((1, S5_LANES, 2 * ns), lambda l, t: (l, 0, 0)),
            pl.BlockSpec((1, 2 * ns, S5_LANES), lambda l, t: (l, 0, 0)),
            pl.BlockSpec((1, S5_LANES), lambda l, t: (0, l)),
            pl.BlockSpec((1, S5_LANES, 2 * S5_LANES), lambda l, t: (l, 0, 0)),
            pl.BlockSpec((2, SUBLANES, ns), lane3),
            pl.BlockSpec((2, 3, SUBLANES, ns), lane4),
            pl.BlockSpec((2, SUBLANES, ns), lane3),
        ],
        out_specs=pl.BlockSpec((None, bsz, S5_TILE, S5_LANES), row_out),
        out_shape=jax.ShapeDtypeStruct(u4.shape, F32),
        scratch_shapes=[
            pltpu.VMEM((2, bsz, S5_TILE, S5_LANES), F32),
            pltpu.VMEM((bsz, S5_TILE, 2 * ns), F32),
            pltpu.VMEM((2, bsz, S5_TILE, 2 * ns), BF16),
            pltpu.VMEM((bsz, 2, SUBLANES, ns), F32),
        ],
        compiler_params=pltpu.CompilerParams(
            dimension_semantics=("arbitrary", "arbitrary"),
            vmem_limit_bytes=VMEM_SMALL),
        name="s5",
    )(u4, bcat, ccat, d, wab, a1, pj, aj)
    return out.reshape(nlb, bsz * seq, S5_LANES)


def _kv_kernel(m_ref, nw_ref, wk_ref, wv_ref, k_ref, v_ref):
    mn = _rms(m_ref[0], nw_ref[...]).astype(BF16)
    k_ref[0] = _dot(mn, wk_ref[...]).astype(BF16)
    v_ref[0] = _dot(mn, wv_ref[...]).astype(BF16)


def _kv(mem, norm_w, wk, wv):
    bsz, mlen, d = mem.shape
    blk = pl.BlockSpec((1, mlen, d), lambda b: (b, 0, 0))
    wspec = pl.BlockSpec((d, d), lambda b: (0, 0))
    return pl.pallas_call(
        _kv_kernel,
        grid=(bsz,),
        in_specs=[blk, pl.BlockSpec((1, d), lambda b: (0, 0)), wspec, wspec],
        out_specs=(blk, blk),
        out_shape=(jax.ShapeDtypeStruct(mem.shape, BF16),) * 2,
        compiler_params=pltpu.CompilerParams(
            dimension_semantics=("arbitrary",), vmem_limit_bytes=VMEM_SMALL),
        name="kv",
    )(mem, norm_w, wk, wv)


def _mix_attn_kernel(x_ref, yc_ref, ys_ref, wot_ref, wob_ref, nx_ref, wq_ref,
                     k_ref, v_ref, wo_ref, nf_ref, wr_ref, br_ref,
                     h_ref, xs_ref, y0_ref, gs_ref, ri_ref, cnt_ref,
                     h2s_ref, tri_ref, rho_ref):
    i = pl.program_id(0)
    tm = x_ref.shape[0]
    neg = -jnp.inf
    big = float(ROUTE_LANES)

    @pl.when(i == 0)
    def _():
        h2s_ref[...] = jnp.zeros(h2s_ref.shape, F32)
        r_i = lax.broadcasted_iota(jnp.int32, (tm, tm), 0)
        c_i = lax.broadcasted_iota(jnp.int32, (tm, tm), 1)
        tri_ref[...] = jnp.where(r_i > c_i, 1.0, 0.0).astype(BF16)
        rho_ref[...] = lax.broadcasted_iota(jnp.int32, (tm, TILE_ROWS), 1).astype(F32)

    ys_time = jnp.concatenate(
        [jnp.concatenate([ys_ref[lt, pl.ds(r, S5_STEPS, stride=SUBLANES), :]
                          for r in range(SUBLANES)], axis=0)
         for lt in range(D_SSM // LANES)], axis=1)
    h1 = (x_ref[...] + _dot(yc_ref[...].astype(BF16), wot_ref[...])
          + _dot(ys_time.astype(BF16), wob_ref[...]))
    hn = _rms(h1, nx_ref[...]).astype(BF16)
    q = _dot(hn, wq_ref[...])

    hf = _rms(h2s_ref[...], nf_ref[...]).astype(BF16)
    logits = _dot(hf, wr_ref[...]) + br_ref[...]
    lane = lax.broadcasted_iota(jnp.int32, (tm, ROUTE_LANES), 1)
    lane_f = lane.astype(F32)

    def top1(vals):
        m = jnp.max(vals, axis=-1, keepdims=True)
        idx = jnp.min(jnp.where(vals == m, lane_f, big), axis=-1, keepdims=True)
        return m, idx

    gl = jnp.where(lane < N_EXPERT_GROUPS, logits, neg)
    gmax, gidx = top1(gl)
    g_w = 1.0 / jnp.sum(jnp.exp(gl - gmax), axis=-1, keepdims=True)
    lo = EXPERT_LANE0 + EXPERTS_PER_GROUP * gidx
    el = jnp.where((lane_f >= lo) & (lane_f < lo + EXPERTS_PER_GROUP), logits, neg)
    m1, i1 = top1(el)
    m2, i2 = top1(jnp.where(lane_f == i1, neg, el))
    e21 = jnp.exp(m2 - m1)
    gate1 = g_w / (1.0 + e21)
    gate2 = g_w * e21 / (1.0 + e21)

    heads = []
    for hd in range(N_XHEADS):
        sl = slice(hd * XHEAD_DIM, (hd + 1) * XHEAD_DIM)
        qh = q[:, sl].astype(BF16)
        s = lax.dot_general(qh, k_ref[0, :, sl], (((1,), (1,)), ((), ())),
                            preferred_element_type=F32)
        s = s - jnp.max(s, axis=-1, keepdims=True)
        p = jnp.exp(s)
        p = p / jnp.sum(p, axis=-1, keepdims=True)
        heads.append(_dot(p.astype(BF16), v_ref[0, :, sl]).astype(BF16))
    o = jnp.concatenate(heads, axis=-1)

    hot1 = lane_f == i1
    hot2 = lane_f == i2
    hot = jnp.where(hot1 | hot2, 1.0, 0.0)
    before = _dot(tri_ref[...], hot.astype(BF16))
    count = jnp.sum(hot, axis=0, keepdims=True)
    chunks = jnp.floor((count + (MOE_CHUNK - 1.0)) * (1.0 / MOE_CHUNK))
    l_i = lax.broadcasted_iota(jnp.int32, (ROUTE_LANES, ROUTE_LANES), 0)
    l_j = lax.broadcasted_iota(jnp.int32, (ROUTE_LANES, ROUTE_LANES), 1)
    upper = jnp.where(l_i < l_j, 1.0, 0.0).astype(BF16)
    first_chunk = _dot(jnp.broadcast_to(chunks, (SUBLANES, ROUTE_LANES)).astype(BF16), upper)[0:1]
    start = first_chunk * float(MOE_CHUNK) + before
    pos1 = jnp.sum(jnp.where(hot1, start, 0.0), axis=-1, keepdims=True)
    pos2 = jnp.sum(jnp.where(hot2, start, 0.0), axis=-1, keepdims=True)
    cnt_ref[0] = jnp.broadcast_to(count, (SUBLANES, ROUTE_LANES)).astype(jnp.int32)
    ri_ref[...] = jnp.where(lane == 0, pos1, jnp.where(lane == 1, pos2, 0.0)).astype(jnp.int32)

    h2 = h1 + _dot(o, wo_ref[...])
    h_ref[...] = h2
    h2s_ref[...] = h2

    def pieces(g):
        hi = g.astype(BF16).astype(F32)
        mid = (g - hi).astype(BF16).astype(F32)
        low = (g - hi - mid).astype(BF16).astype(F32)
        return hi, mid, low

    g6 = jnp.where(lane == 6, 1.0, 0.0)
    for li, piece in enumerate(pieces(gate1) + pieces(gate2)):
        g6 = jnp.where(lane == li, piece, g6)
    rho = rho_ref[...]
    ptk = jnp.where(rho == pos1, 1.0, jnp.where(rho == pos2, 2.0, 0.0)).astype(BF16)
    rhs = jnp.concatenate([hf, g6.astype(BF16)], axis=1)
    res = lax.dot_general(ptk, rhs, (((0,), (0,)), ((), ())), preferred_element_type=F32)
    sg = res[:, D_MODEL:]
    which = sg[:, 6:7]
    srt = res[:, :D_MODEL] * jnp.where(which == 2.0, 0.5, 1.0)
    xs_ref[...] = _pack_words(srt).reshape(TILE_CHUNKS, MOE_CHUNK, D_PACK)
    y0_ref[...] = jnp.zeros(y0_ref.shape, jnp.uint32)
    first = sg[:, 0:1] + sg[:, 1:2] + sg[:, 2:3]
    second = 0.5 * (sg[:, 3:4] + sg[:, 4:5] + sg[:, 5:6])
    gsort = jnp.where(which == 1.0, first, jnp.where(which == 2.0, second, 0.0))
    gs_ref[...] = jnp.broadcast_to(gsort, (TILE_ROWS, ROUTE_LANES))


def _pack_words(v):
    hi = lax.bitcast_convert_type(v[:, :D_PACK], jnp.uint32) & jnp.uint32(HI_MASK)
    lo = lax.bitcast_convert_type(v[:, D_PACK:], jnp.uint32) >> 16
    return hi | lo


def _unpack_words(w):
    hi = lax.bitcast_convert_type(w & jnp.uint32(HI_MASK), F32)
    lo = lax.bitcast_convert_type(w << 16, F32)
    return hi, lo


def _mix_attn(xt, yc, ys, wot, wob, nx, wq, k, v, wo, nf, wr, br, seq):
    t, d = xt.shape
    nt = t // TM_MIX
    tiles_per_batch = seq // TM_MIX
    mlen = k.shape[1]
    att = lambda i: jnp.minimum(i, nt - 1)
    rte = lambda i: jnp.maximum(i - 1, 0)
    row_a = lambda i: (att(i), 0)
    row_r = lambda i: (rte(i), 0)
    const = lambda i: (0, 0)
    tile3 = lambda i: (rte(i), 0, 0)
    kvspec = pl.BlockSpec((1, mlen, d), lambda i: (att(i) // tiles_per_batch, 0, 0))
    packed = jax.ShapeDtypeStruct((nt * TILE_CHUNKS, MOE_CHUNK, D_PACK), jnp.uint32)
    return pl.pallas_call(
        _mix_attn_kernel,
        grid=(nt + 1,),
        in_specs=[
            pl.BlockSpec((TM_MIX, d), row_a),
            pl.BlockSpec((TM_MIX, D_CONV), row_a),
            pl.BlockSpec((D_SSM // LANES, TM_MIX, LANES), lambda i: (0, att(i), 0)),
            pl.BlockSpec((D_CONV, d), const),
            pl.BlockSpec((D_SSM, d), const),
            pl.BlockSpec((1, d), const),
            pl.BlockSpec((d, d), const),
            kvspec, kvspec,
            pl.BlockSpec((d, d), const),
            pl.BlockSpec((1, d), const),
            pl.BlockSpec((d, ROUTE_LANES), const),
            pl.BlockSpec((1, ROUTE_LANES), const),
        ],
        out_specs=(
            pl.BlockSpec((TM_MIX, d), row_a),
            pl.BlockSpec((TILE_CHUNKS, MOE_CHUNK, D_PACK), tile3),
            pl.BlockSpec((TILE_CHUNKS, MOE_CHUNK, D_PACK), tile3),
            pl.BlockSpec((TILE_ROWS, ROUTE_LANES), row_r),
            pl.BlockSpec((TM_MIX, ROUTE_LANES), row_r),
            pl.BlockSpec((1, SUBLANES, ROUTE_LANES), tile3),
        ),
        out_shape=(
            jax.ShapeDtypeStruct((t, d), F32),
            packed,
            packed,
            jax.ShapeDtypeStruct((nt * TILE_ROWS, ROUTE_LANES), F32),
            jax.ShapeDtypeStruct((t, ROUTE_LANES), jnp.int32),
            jax.ShapeDtypeStruct((nt, SUBLANES, ROUTE_LANES), jnp.int32),
        ),
        scratch_shapes=[
            pltpu.VMEM((TM_MIX, d), F32),
            pltpu.VMEM((TM_MIX, TM_MIX), BF16),
            pltpu.VMEM((TM_MIX, TILE_ROWS), F32),
        ],
        compiler_params=pltpu.CompilerParams(
            dimension_semantics=("arbitrary",), vmem_limit_bytes=VMEM_MIX),
        name="mix_attn",
    )(xt, yc, ys, wot, wob, nx, wq, k, v, wo, nf, wr, br)


ZERO_CHUNK = TILE_CHUNKS - 1
DUMP_TILE0 = 16
SPARE_PER_TILE = TILE_CHUNKS - TILE_USED_CHUNKS


def _dump_chunk(slot, k):
    idx = slot * BLOCK_CHUNKS + k
    return (DUMP_TILE0 + idx // SPARE_PER_TILE) * TILE_CHUNKS + TILE_USED_CHUNKS + idx % SPARE_PER_TILE


def _experts_kernel(src_ref, dst_ref, be_ref, seq_ref, nxt_ref, na_ref, xs_hbm, w1_hbm, w3_hbm,
                    w2_hbm, y0_hbm, ys_hbm, xbuf, ybuf, w1f_ref, w3f_ref, w2f_ref,
                    w1s_ref, w3s_ref, w2s_ref, in_sem, out_sem, w_sem):
    del y0_hbm
    b = pl.program_id(0)
    n_active = na_ref[0]
    slot = b % 2
    other = 1 - slot

    def gather(blk, sl):
        for k in range(BLOCK_CHUNKS):
            pltpu.make_async_copy(xs_hbm.at[src_ref[blk * BLOCK_CHUNKS + k]],
                                  xbuf.at[sl, k], in_sem.at[sl]).start()

    def gather_wait(sl):
        pltpu.make_async_copy(xs_hbm.at[pl.ds(0, BLOCK_CHUNKS)], xbuf.at[sl],
                              in_sem.at[sl]).wait()

    def scatter(blk, sl):
        for k in range(BLOCK_CHUNKS):
            d = dst_ref[blk * BLOCK_CHUNKS + k]
            dump = jnp.where(sl == 0, _dump_chunk(0, k), _dump_chunk(1, k))
            pltpu.make_async_copy(ybuf.at[sl, k], ys_hbm.at[jnp.where(d < 0, dump, d)],
                                  out_sem.at[sl]).start()

    def scatter_wait(sl):
        pltpu.make_async_copy(ybuf.at[sl], ys_hbm.at[pl.ds(0, BLOCK_CHUNKS)],
                              out_sem.at[sl]).wait()

    active = b < n_active

    @pl.when(b == 0)
    def _():
        gather(0, 0)
        ybuf[...] = jnp.zeros(ybuf.shape, jnp.uint32)

    prev = be_ref[jnp.maximum(b - 1, 0)]
    fresh = (b == 0) | (be_ref[b] != prev)

    def weight_copies(e, ws):
        return [pltpu.make_async_copy(hbm.at[e], buf.at[ws], w_sem.at[ws])
                for hbm, buf in ((w1_hbm, w1f_ref), (w3_hbm, w3f_ref), (w2_hbm, w2f_ref))]

    @pl.when(active & fresh)
    def _():
        ws = seq_ref[b] % 2

        @pl.when(b == 0)
        def _():
            for cp in weight_copies(be_ref[b], ws):
                cp.start(priority=1)

        for cp in weight_copies(be_ref[b], ws):
            cp.wait()

        @pl.when(nxt_ref[b] >= 0)
        def _():
            for cp in weight_copies(nxt_ref[b], 1 - ws):
                cp.start(priority=1)

        w1s_ref[...] = w1f_ref[ws].astype(BF16)
        w3s_ref[...] = w3f_ref[ws].astype(BF16)
        w2s_ref[...] = w2f_ref[ws].astype(BF16)

    @pl.when(active & (b >= 1))
    def _():
        scatter_wait(slot)

    @pl.when(active)
    def _():
        gather_wait(slot)
        gather(b + 1, other)
        scatter(b, other)

        hi, lo = _unpack_words(xbuf[slot].reshape(MOE_BLOCK, D_PACK))
        xb = jnp.concatenate([hi, lo], axis=1).astype(BF16)
        h1 = _dot(xb, w1s_ref[...])
        h3 = _dot(xb, w3s_ref[...])
        hid = (h1 * jax.nn.sigmoid(h1) * h3).astype(BF16)
        y = _dot(hid, w2s_ref[...]).astype(BF16).astype(F32)
        ybuf[slot] = _pack_words(y).reshape(BLOCK_CHUNKS, MOE_CHUNK, D_PACK)

        @pl.when(b == n_active - 1)
        def _():
            scatter(b + 1, slot)
            gather_wait(other)
            scatter_wait(other)
            scatter_wait(slot)


def _experts(src, dst, block_expert, expert_seq, next_expert, n_active, xs, w1, w3, w2, y0):
    nb = block_expert.shape[0]
    d = D_MODEL
    hbm = pl.BlockSpec(memory_space=pl.ANY)
    grid_spec = pltpu.PrefetchScalarGridSpec(
        num_scalar_prefetch=6,
        grid=(nb,),
        in_specs=[hbm, hbm, hbm, hbm, hbm],
        out_specs=hbm,
        scratch_shapes=[
            pltpu.VMEM((2, BLOCK_CHUNKS, MOE_CHUNK, D_PACK), jnp.uint32),
            pltpu.VMEM((2, BLOCK_CHUNKS, MOE_CHUNK, D_PACK), jnp.uint32),
            pltpu.VMEM((2, d, D_EXPERT), F32),
            pltpu.VMEM((2, d, D_EXPERT), F32),
            pltpu.VMEM((2, D_EXPERT, d), F32),
            pltpu.VMEM((d, D_EXPERT), BF16),
            pltpu.VMEM((d, D_EXPERT), BF16),
            pltpu.VMEM((D_EXPERT, d), BF16),
            pltpu.SemaphoreType.DMA((2,)),
            pltpu.SemaphoreType.DMA((2,)),
            pltpu.SemaphoreType.DMA((2,)),
        ],
    )
    return pl.pallas_call(
        _experts_kernel,
        grid_spec=grid_spec,
        out_shape=jax.ShapeDtypeStruct(y0.shape, jnp.uint32),
        input_output_aliases={10: 0},
        compiler_params=pltpu.CompilerParams(
            dimension_semantics=("arbitrary",), vmem_limit_bytes=VMEM_EXPERTS),
        name="experts",
    )(src, dst, block_expert, expert_seq, next_expert, n_active, xs, w1, w3, w2, y0)


def _combine_kernel(h_ref, ri_ref, gs_ref, ys_ref, fw_ref, o_ref):
    tm = h_ref.shape[0]
    hi, lo = _unpack_words(ys_ref[...].reshape(TILE_ROWS, D_PACK))
    g = gs_ref[...]
    gw = jnp.concatenate([g] * (D_PACK // ROUTE_LANES), axis=1)
    yg = jnp.concatenate([hi * gw, lo * gw], axis=1).astype(BF16)
    pos = ri_ref[...].astype(F32)
    rho = lax.broadcasted_iota(jnp.int32, (tm, TILE_ROWS), 1).astype(F32)
    q = jnp.where((rho == pos[:, 0:1]) | (rho == pos[:, 1:2]), 1.0, 0.0).astype(BF16)
    o_ref[...] = _rms(h_ref[...] + _dot(q, yg), fw_ref[...])


def _combine(h2, route_i, gsort, ys, final_w):
    t, d = h2.shape
    nt = t // TM_MIX
    row = lambda i: (i, 0)
    return pl.pallas_call(
        _combine_kernel,
        grid=(nt,),
        in_specs=[
            pl.BlockSpec((TM_MIX, d), row),
            pl.BlockSpec((TM_MIX, ROUTE_LANES), row),
            pl.BlockSpec((TILE_ROWS, ROUTE_LANES), row),
            pl.BlockSpec((TILE_CHUNKS, MOE_CHUNK, D_PACK), lambda i: (i, 0, 0)),
            pl.BlockSpec((1, d), lambda i: (0, 0)),
        ],
        out_specs=pl.BlockSpec((TM_MIX, d), row),
        out_shape=jax.ShapeDtypeStruct((t, d), F32),
        compiler_params=pltpu.CompilerParams(
            dimension_semantics=("arbitrary",), vmem_limit_bytes=VMEM_EXPERTS),
        name="combine",
    )(h2, route_i, gsort, ys, final_w)


def _routing_tables(cnt):
    nt = cnt.shape[0]
    n = cnt[:, 0, EXPERT_LANE0:EXPERT_LANE0 + N_EXPERTS]
    c = (n + MOE_CHUNK - 1) // MOE_CHUNK
    local = jnp.cumsum(c, axis=1) - c
    per_expert = jnp.sum(c, axis=0)
    padded = (per_expert + BLOCK_CHUNKS - 1) // BLOCK_CHUNKS * BLOCK_CHUNKS
    gend = jnp.cumsum(padded)
    gstart = gend - padded
    within = jnp.cumsum(c, axis=0) - c
    g_max = (2 * nt * TM_MIX // MOE_CHUNK + nt * N_EXPERTS
             + N_EXPERTS * (BLOCK_CHUNKS - 1))
    n_blocks = -(-g_max // BLOCK_CHUNKS)
    g = jnp.arange(n_blocks * BLOCK_CHUNKS, dtype=jnp.int32)
    e_of = jnp.minimum(jnp.sum((gend[None, :] <= g[:, None]).astype(jnp.int32), axis=1),
                       N_EXPERTS - 1)
    hot_e = (e_of[:, None] == jnp.arange(N_EXPERTS, dtype=jnp.int32)[None, :]).astype(F32)
    pick = lambda table: jnp.dot(hot_e, table.T.astype(F32), precision=lax.Precision.HIGHEST)
    rel = g.astype(F32) - pick(gstart[None, :])[:, 0]
    w_rows = pick(within)
    tile = jnp.sum((w_rows <= rel[:, None]).astype(jnp.int32), axis=1) - 1
    hot_t = tile[:, None] == jnp.arange(nt, dtype=jnp.int32)[None, :]
    at_tile = lambda rows: jnp.sum(jnp.where(hot_t, rows, 0.0), axis=1)
    k = rel - at_tile(w_rows)
    valid = k < at_tile(pick(c))
    chunk = (tile * TILE_CHUNKS + (at_tile(pick(local)) + k).astype(jnp.int32))
    pad_src = jnp.full((BLOCK_CHUNKS,), ZERO_CHUNK, jnp.int32)
    pad_dst = jnp.full((BLOCK_CHUNKS,), -1, jnp.int32)
    src = jnp.concatenate([jnp.where(valid, chunk, ZERO_CHUNK).astype(jnp.int32), pad_src])
    dst = jnp.concatenate([pad_dst, jnp.where(valid, chunk, -1).astype(jnp.int32)])
    block_start = jnp.arange(n_blocks, dtype=jnp.int32) * BLOCK_CHUNKS
    block_expert = jnp.minimum(
        jnp.sum((block_start[:, None] >= gend[None, :]).astype(jnp.int32), axis=1),
        N_EXPERTS - 1).astype(jnp.int32)
    n_active = (gend[-1:] // BLOCK_CHUNKS).astype(jnp.int32)
    blk = jnp.arange(n_blocks, dtype=jnp.int32)
    is_first = (blk < n_active[0]) & ((blk == 0) | (block_expert != jnp.roll(block_expert, 1)))
    expert_seq = (jnp.cumsum(is_first.astype(jnp.int32)) - 1).astype(jnp.int32)
    later_first = is_first[None, :] & (blk[None, :] > blk[:, None])
    next_blk = jnp.min(jnp.where(later_first, blk[None, :], n_blocks), axis=1)
    next_expert = jnp.where(
        next_blk < n_blocks,
        jnp.sum(jnp.where(blk[None, :] == next_blk[:, None], block_expert[None, :], 0), axis=1),
        -1).astype(jnp.int32)
    return src, dst, block_expert, expert_seq, next_expert, n_active


def _layer(h, mem, p, final_w):
    bsz, seq, d = h.shape
    t = bsz * seq
    assert t // TM_MIX >= DUMP_TILE0 + 2 * BLOCK_CHUNKS // SPARE_PER_TILE
    xt = h.reshape(t, d)

    v, u = _in_proj(xt, p["norm_mix_w"].reshape(1, d), p["w_in"].astype(BF16))
    y_conv = _conv(v, p["conv_w"], p["conv_b"].reshape(1, D_CONV),
                   p["conv_ln_w"].reshape(1, D_CONV), p["conv_ln_b"].reshape(1, D_CONV),
                   bsz, seq)

    a1, pj, aj, bbar = _s5_prep(p["ssm_A_re"], p["ssm_A_im"], p["ssm_log_dt"],
                                      p["ssm_B_re"], p["ssm_B_im"])
    nlb = D_SSM // S5_LANES
    bb = bbar.reshape(2, SSM_GROUP, nlb, S5_GROUPS, SSM_STATE).transpose(0, 2, 3, 1, 4)
    bcat = jnp.concatenate([_block_diag(bb[0]), _block_diag(bb[1])], axis=-1).astype(BF16)
    c_re = p["ssm_C_re"].reshape(nlb, S5_GROUPS, SSM_GROUP, SSM_STATE).transpose(0, 1, 3, 2)
    c_im = p["ssm_C_im"].reshape(nlb, S5_GROUPS, SSM_GROUP, SSM_STATE).transpose(0, 1, 3, 2)
    ccat = jnp.concatenate([_block_diag(c_re), -_block_diag(c_im)], axis=1).astype(BF16)
    glu = p["ssm_glu_w"].reshape(nlb, S5_GROUPS, SSM_GROUP, 2 * SSM_GROUP)
    wab = jnp.concatenate([_block_diag(glu[..., :SSM_GROUP]),
                           _block_diag(glu[..., SSM_GROUP:])], axis=-1).astype(BF16)
    y_ssm = _s5(u, bcat, ccat, p["ssm_D"].reshape(1, D_SSM), wab,
                a1, pj, aj, bsz, seq)

    k, vv = _kv(mem, p["norm_mem_w"].reshape(1, d), p["xk_w"].astype(BF16),
                p["xv_w"].astype(BF16))
    w_out = p["w_out"].astype(BF16)
    wr = jnp.concatenate([p["router_group_w"], p["router_expert_w"]], axis=1)
    wr = jnp.pad(wr, ((0, 0), (0, ROUTE_LANES - wr.shape[1]))).astype(BF16)
    br = jnp.concatenate([p["router_group_b"], p["router_expert_b"].reshape(-1)])
    br = jnp.pad(br, (0, ROUTE_LANES - br.shape[0])).reshape(1, ROUTE_LANES)
    h2, xs, y0, gsort, route_i, cnt = _mix_attn(
        xt, y_conv, y_ssm, w_out[:D_CONV], w_out[D_CONV:], p["norm_x_w"].reshape(1, d),
        (p["xq_w"] * (XHEAD_DIM ** -0.5)).astype(BF16), k, vv, p["xo_w"].astype(BF16),
        p["norm_ffn_w"].reshape(1, d), wr, br, seq)

    src, dst, block_expert, expert_seq, next_expert, n_active = _routing_tables(cnt)
    ys = _experts(src, dst, block_expert, expert_seq, next_expert, n_active, xs,
                  p["moe_w1"], p["moe_w3"], p["moe_w2"], y0)
    out = _combine(h2, route_i, gsort, ys, final_w.reshape(1, d))
    return out.reshape(bsz, seq, d)


def kernel(x, mem, norm_mix_w, w_in, conv_w, conv_b, conv_ln_w, conv_ln_b, ssm_A_re, ssm_A_im, ssm_log_dt, ssm_B_re, ssm_B_im, ssm_C_re, ssm_C_im, ssm_D, ssm_glu_w, w_out, norm_x_w, norm_mem_w, xq_w, xk_w, xv_w, xo_w, norm_ffn_w, router_group_w, router_group_b, router_expert_w, router_expert_b, moe_w1, moe_w3, moe_w2, final_norm_w):
    stacked = dict(
        norm_mix_w=norm_mix_w, w_in=w_in, conv_w=conv_w, conv_b=conv_b,
        conv_ln_w=conv_ln_w, conv_ln_b=conv_ln_b, ssm_A_re=ssm_A_re, ssm_A_im=ssm_A_im,
        ssm_log_dt=ssm_log_dt, ssm_B_re=ssm_B_re, ssm_B_im=ssm_B_im, ssm_C_re=ssm_C_re,
        ssm_C_im=ssm_C_im, ssm_D=ssm_D, ssm_glu_w=ssm_glu_w, w_out=w_out,
        norm_x_w=norm_x_w, norm_mem_w=norm_mem_w, xq_w=xq_w, xk_w=xk_w, xv_w=xv_w,
        xo_w=xo_w, norm_ffn_w=norm_ffn_w, router_group_w=router_group_w,
        router_group_b=router_group_b, router_expert_w=router_expert_w,
        router_expert_b=router_expert_b, moe_w1=moe_w1, moe_w3=moe_w3, moe_w2=moe_w2)
    depth = norm_mix_w.shape[0]
    assert depth == 1, "final norm is fused into the single layer's combine step"
    layer = {name: w[0] for name, w in stacked.items()}
    return _layer(x, mem, layer, final_norm_w)
```

```python
import functools

import jax
import jax.numpy as jnp
from jax import lax
from jax.experimental import pallas as pl
from jax.experimental.pallas import tpu as pltpu

D_MODEL = 1024
D_CONV = 512
CONV_WIDTH = 31
D_SSM = 512
SSM_GROUP = 16
N_SSM_GROUPS = 32
SSM_STATE = 64
N_XHEADS = 4
XHEAD_DIM = 256
N_EXPERT_GROUPS = 4
EXPERTS_PER_GROUP = 8
N_EXPERTS = 32
D_EXPERT = 512
EPS = 1e-6

F32 = jnp.float32
BF16 = jnp.bfloat16

SUBLANES = 8
LANES = 128

TM_PROJ = 1024
TM_CONV = 1024
CONV_CHUNK = 128
CONV_NORM_ROWS = 128
CONV_HALO = 32
S5_STEPS = 64
S5_TILE = SUBLANES * S5_STEPS
S5_GROUP_ROWS = 256
S5_LANES = 128
S5_GROUPS = S5_LANES // SSM_GROUP
S5_STATE = S5_GROUPS * SSM_STATE
TM_MIX = 512
ROUTE_LANES = 128
EXPERT_LANE0 = N_EXPERT_GROUPS
MOE_CHUNK = SUBLANES
MOE_BLOCK = 256
BLOCK_CHUNKS = MOE_BLOCK // MOE_CHUNK
TILE_USED_CHUNKS = 2 * TM_MIX // MOE_CHUNK + N_EXPERTS * (MOE_CHUNK - 1) // MOE_CHUNK
TILE_CHUNKS = 160
TILE_ROWS = TILE_CHUNKS * MOE_CHUNK
D_PACK = D_MODEL // 2
HI_MASK = 0xFFFF0000

VMEM_SMALL = 40 << 20
VMEM_EXPERTS = 48 << 20
VMEM_MIX = 62 << 20


def _rms(x, w):
    return x * lax.rsqrt(jnp.mean(x * x, axis=-1, keepdims=True) + EPS) * w


def _dot(a, b):
    return jnp.dot(a, b, preferred_element_type=F32)


def _s5_prep_kernel(are_ref, aim_ref, ldt_ref, btre_ref, btim_ref,
                    a1_ref, pj_ref, aj_ref, bbar_ref):
    lam_re = are_ref[...]
    lam_im = aim_ref[...]
    dt = jnp.exp(ldt_ref[...])
    x = lam_re * dt
    y = lam_im * dt

    def power(k):
        mag = jnp.exp(k * x)
        return mag * jnp.cos(k * y), mag * jnp.sin(k * y)

    n = x.shape[-1]
    ones8 = jnp.ones((SUBLANES, n), F32)
    a_re, a_im = power(ones8)
    a1_ref[0] = a_re
    a1_ref[1] = a_im

    row = lax.broadcasted_iota(jnp.int32, (SUBLANES, n), 0)
    for i, d in enumerate((1, 2, 4)):
        p_re, p_im = power(ones8 * float(d * S5_STEPS))
        keep = row >= d
        pj_ref[0, i] = jnp.where(keep, p_re, 0.0)
        pj_ref[1, i] = jnp.where(keep, p_im, 0.0)
    j_re, j_im = power(ones8 * float(S5_STEPS))
    aj_ref[0] = j_re
    aj_ref[1] = j_im

    num_re = a_re[0:1] - 1.0
    num_im = a_im[0:1]
    den = lam_re * lam_re + lam_im * lam_im
    c_re = (num_re * lam_re + num_im * lam_im) / den
    c_im = (num_im * lam_re - num_re * lam_im) / den
    b_re = btre_ref[...]
    b_im = btim_ref[...]
    bbar_ref[0] = c_re * b_re - c_im * b_im
    bbar_ref[1] = c_re * b_im + c_im * b_re


def _s5_prep(a_re, a_im, log_dt, b_re, b_im):
    n = N_SSM_GROUPS * SSM_STATE
    are = a_re.reshape(1, n)
    aim = a_im.reshape(1, n)
    ldt = jnp.repeat(log_dt, SSM_STATE).reshape(1, n)
    btre = jnp.transpose(b_re, (2, 0, 1)).reshape(SSM_GROUP, n)
    btim = jnp.transpose(b_im, (2, 0, 1)).reshape(SSM_GROUP, n)
    return pl.pallas_call(
        _s5_prep_kernel,
        out_shape=(
            jax.ShapeDtypeStruct((2, SUBLANES, n), F32),
            jax.ShapeDtypeStruct((2, 3, SUBLANES, n), F32),
            jax.ShapeDtypeStruct((2, SUBLANES, n), F32),
            jax.ShapeDtypeStruct((2, SSM_GROUP, n), F32),
        ),
        name="s5_prep",
    )(are, aim, ldt, btre, btim)


def _block_diag(w):
    nl, g, r, c = w.shape
    eye = jnp.eye(g, dtype=w.dtype)
    return jnp.einsum("lgrc,gh->lgrhc", w, eye).reshape(nl, g * r, g * c)


def _in_proj_kernel(x_ref, nw_ref, w_ref, v_ref, u_ref):
    xn = _rms(x_ref[...], nw_ref[...]).astype(BF16)
    proj = _dot(xn, w_ref[...])
    a = proj[:, :D_CONV]
    g = proj[:, D_CONV:2 * D_CONV]
    v_ref[...] = a * jax.nn.sigmoid(g)
    for lt in range(D_SSM // LANES):
        c0 = 2 * D_CONV + lt * LANES
        for tile in range(TM_PROJ // S5_TILE):
            for r in range(SUBLANES):
                t0 = tile * S5_TILE + r * S5_STEPS
                u_ref[lt, pl.ds(tile * S5_TILE + r, S5_STEPS, stride=SUBLANES), :] = (
                    proj[t0:t0 + S5_STEPS, c0:c0 + LANES])


def _in_proj(xt, norm_w, w_in):
    t = xt.shape[0]
    n_out = 2 * D_CONV + D_SSM
    return pl.pallas_call(
        _in_proj_kernel,
        grid=(t // TM_PROJ,),
        in_specs=[
            pl.BlockSpec((TM_PROJ, D_MODEL), lambda i: (i, 0)),
            pl.BlockSpec((1, D_MODEL), lambda i: (0, 0)),
            pl.BlockSpec((D_MODEL, n_out), lambda i: (0, 0)),
        ],
        out_specs=(
            pl.BlockSpec((TM_PROJ, D_CONV), lambda i: (i, 0)),
            pl.BlockSpec((D_SSM // LANES, TM_PROJ, LANES), lambda i: (0, i, 0)),
        ),
        out_shape=(
            jax.ShapeDtypeStruct((t, D_CONV), F32),
            jax.ShapeDtypeStruct((D_SSM // LANES, t, LANES), F32),
        ),
        compiler_params=pltpu.CompilerParams(
            dimension_semantics=("arbitrary",), vmem_limit_bytes=VMEM_SMALL),
        name="in_proj",
    )(xt, norm_w, w_in)


def _conv_kernel(v_ref, w_ref, b_ref, lnw_ref, lnb_ref, o_ref, ext_ref, sh_ref):
    tt = pl.program_id(1)
    rows = CONV_HALO + TM_CONV
    n_lt = D_CONV // LANES

    @pl.when(tt == 0)
    def _():
        ext_ref[:, pl.ds(0, CONV_HALO), :] = jnp.zeros((n_lt, CONV_HALO, LANES), F32)

    @pl.when(tt > 0)
    def _():
        ext_ref[:, pl.ds(0, CONV_HALO), :] = ext_ref[:, pl.ds(TM_CONV, CONV_HALO), :]

    for lt in range(n_lt):
        ext_ref[lt, pl.ds(CONV_HALO, TM_CONV), :] = v_ref[:, pl.ds(lt * LANES, LANES)]
    for s in range(1, SUBLANES):
        sh_ref[s - 1, :, pl.ds(0, rows - SUBLANES), :] = ext_ref[:, pl.ds(s, rows - SUBLANES), :]
    bias = b_ref[...]
    lnw = lnw_ref[...]
    lnb = lnb_ref[...]
    tap0 = CONV_HALO - (CONV_WIDTH - 1)

    groups = CONV_CHUNK // SUBLANES

    def chunk(ci, carry):
        base = pl.multiple_of(ci * CONV_CHUNK, CONV_CHUNK)
        for lt in range(n_lt):
            lanes = pl.ds(lt * LANES, LANES)
            acc = [jnp.broadcast_to(bias[:, lt * LANES:(lt + 1) * LANES], (SUBLANES, LANES))] * groups
            for s in range(SUBLANES):
                taps = [j for j in range(CONV_WIDTH) if (tap0 + j) % SUBLANES == s]
                src = ext_ref.at[lt] if s == 0 else sh_ref.at[s - 1, lt]
                ngroups = (tap0 + taps[-1] - s) // SUBLANES + groups
                win = [src[pl.ds(base + SUBLANES * g, SUBLANES), :] for g in range(ngroups)]
                for j in taps:
                    g0 = (tap0 + j - s) // SUBLANES
                    wj = w_ref[j, :, lanes]
                    acc = [acc[r] + wj * win[g0 + r] for r in range(groups)]
            o_ref[pl.ds(base, CONV_CHUNK), lanes] = jnp.concatenate(acc, axis=0)
        return carry

    lax.fori_loop(0, TM_CONV // CONV_CHUNK, chunk, 0)

    for bi in range(TM_CONV // CONV_NORM_ROWS):
        rows_b = pl.ds(bi * CONV_NORM_ROWS, CONV_NORM_ROWS)
        acc = o_ref[rows_b, :]
        mu = jnp.mean(acc, axis=-1, keepdims=True)
        cen = acc - mu
        var = jnp.mean(cen * cen, axis=-1, keepdims=True)
        z = cen * lax.rsqrt(var + EPS) * lnw + lnb
        o_ref[rows_b, :] = z * jax.nn.sigmoid(z)


def _conv(v, conv_w, conv_b, ln_w, ln_b, bsz, seq):
    nt = seq // TM_CONV
    row = lambda b, t: (b * nt + t, 0)
    const = lambda b, t: (0, 0)
    return pl.pallas_call(
        _conv_kernel,
        grid=(bsz, nt),
        in_specs=[
            pl.BlockSpec((TM_CONV, D_CONV), row),
            pl.BlockSpec((CONV_WIDTH, SUBLANES, D_CONV), lambda b, t: (0, 0, 0)),
            pl.BlockSpec((1, D_CONV), const),
            pl.BlockSpec((1, D_CONV), const),
            pl.BlockSpec((1, D_CONV), const),
        ],
        out_specs=pl.BlockSpec((TM_CONV, D_CONV), row),
        out_shape=jax.ShapeDtypeStruct(v.shape, F32),
        scratch_shapes=[
            pltpu.VMEM((D_CONV // LANES, CONV_HALO + TM_CONV, LANES), F32),
            pltpu.VMEM((SUBLANES - 1, D_CONV // LANES, CONV_HALO + TM_CONV, LANES), F32),
        ],
        compiler_params=pltpu.CompilerParams(
            dimension_semantics=("arbitrary", "arbitrary")),
        name="conv",
    )(v, jnp.broadcast_to(conv_w[:, None, :], (CONV_WIDTH, SUBLANES, D_CONV)), conv_b, ln_w, ln_b)


def _cmul(a_re, a_im, b_re, b_im):
    return a_re * b_re - a_im * b_im, a_re * b_im + a_im * b_re


def _s5_kernel(u_ref, bcat_ref, ccat_ref, d_ref, wab_ref, a1_ref,
               pj_ref, aj_ref, o_ref, up_ref, bu_ref, st_ref, carry_ref):
    tt = pl.program_id(1)
    ns = S5_STATE
    nseq = u_ref.shape[0]
    steps_per_group = S5_GROUP_ROWS // SUBLANES
    n_groups = S5_TILE // S5_GROUP_ROWS

    @pl.when((pl.program_id(0) == 0) & (tt == 0))
    def _():
        up_ref[...] = jnp.zeros(up_ref.shape, F32)
        st_ref[...] = jnp.zeros(st_ref.shape, BF16)

    @pl.when(tt == 0)
    def _():
        carry_ref[...] = jnp.zeros(carry_ref.shape, F32)

    def scan_and_project(cur, prv):
        up_ref[cur] = u_ref[...]

        a_re = a1_ref[0]
        a_im = a1_ref[1]

        def step(q, j, s):
            rows = pl.ds(j * SUBLANES, SUBLANES)
            m_re, m_im = _cmul(a_re, a_im, s[0], s[1])
            return m_re + bu_ref[q, rows, pl.ds(0, ns)], m_im + bu_ref[q, rows, pl.ds(ns, ns)]

        def project_out(q, g):
            rows = pl.ds(g * S5_GROUP_ROWS, S5_GROUP_ROWS)
            y = _dot(st_ref[prv, q, rows, :], ccat_ref[0]) + d_ref[...] * up_ref[prv, q, rows, :]
            y = jax.nn.gelu(y)
            ab = _dot(y.astype(BF16), wab_ref[0])
            o_ref[q, rows, :] = ab[:, :S5_LANES] * jax.nn.sigmoid(ab[:, S5_LANES:])

        zero = jnp.zeros((SUBLANES, ns), F32)
        state = [(zero, zero)] * nseq
        for g in range(n_groups):
            rows = pl.ds(g * S5_GROUP_ROWS, S5_GROUP_ROWS)
            for q in range(nseq):
                bu_ref[q, rows, :] = _dot(up_ref[cur, q, rows, :].astype(BF16), bcat_ref[0])
            for jj in range(steps_per_group):
                state = [step(q, g * steps_per_group + jj, state[q]) for q in range(nseq)]
            for q in range(nseq):
                project_out(q, g)

        row = lax.broadcasted_iota(jnp.int32, (SUBLANES, ns), 0)
        first = row == 0
        entry = []
        for q in range(nseq):
            e_re, e_im = state[q]
            c_re = jnp.where(first, pltpu.roll(carry_ref[q, 0], 1, 0), pltpu.roll(e_re, 1, 0))
            c_im = jnp.where(first, pltpu.roll(carry_ref[q, 1], 1, 0), pltpu.roll(e_im, 1, 0))
            for i, d in enumerate((1, 2, 4)):
                r_re = pltpu.roll(c_re, d, 0)
                r_im = pltpu.roll(c_im, d, 0)
                m_re, m_im = _cmul(pj_ref[0, i], pj_ref[1, i], r_re, r_im)
                c_re = c_re + m_re
                c_im = c_im + m_im
            f_re, f_im = _cmul(aj_ref[0], aj_ref[1], c_re, c_im)
            carry_ref[q, 0] = f_re + e_re
            carry_ref[q, 1] = f_im + e_im
            entry.append((c_re, c_im))

        state = entry
        pack = 2 * SUBLANES
        for j in range(0, S5_STEPS, 2):
            mid = [step(q, j, state[q]) for q in range(nseq)]
            state = [step(q, j + 1, mid[q]) for q in range(nseq)]
            for q in range(nseq):
                st_ref[cur, q, pl.ds(j * SUBLANES, pack), pl.ds(0, ns)] = (
                    jnp.concatenate([mid[q][0], state[q][0]], axis=0).astype(BF16))
                st_ref[cur, q, pl.ds(j * SUBLANES, pack), pl.ds(ns, ns)] = (
                    jnp.concatenate([mid[q][1], state[q][1]], axis=0).astype(BF16))

    for parity in range(2):
        pl.when(tt % 2 == parity)(functools.partial(scan_and_project, parity, 1 - parity))


def _s5(u, bcat, ccat, d, wab, a1, pj, aj, bsz, seq):
    nt = seq // S5_TILE
    nlb = D_SSM // S5_LANES
    ns = S5_STATE
    assert S5_LANES == LANES and S5_TILE == TM_MIX and TM_PROJ % S5_TILE == 0
    u4 = u.reshape(nlb, bsz, seq, S5_LANES)
    row_in = lambda l, t: (l, 0, jnp.minimum(t, nt - 1), 0)
    row_out = lambda l, t: (l, 0, jnp.maximum(t - 1, 0), 0)
    lane3 = lambda l, t: (0, 0, l)
    lane4 = lambda l, t: (0, 0, 0, l)
    out = pl.pallas_call(
        _s5_kernel,
        grid=(nlb, nt + 1),
        in_specs=[
            pl.BlockSpec((None, bsz, S5_TILE, S5_LANES), row_in),
            pl.BlockSpec((1, S5_LANES, 2 * ns), lambda l, t: (l, 0, 0)),
            pl.BlockSpec((1, 2 * ns, S5_LANES), lambda l, t: (l, 0, 0)),
            pl.BlockSpec((1, S5_LANES), lambda l, t: (0, l)),
            pl.BlockSpec((1, S5_LANES, 2 * S5_LANES), lambda l, t: (l, 0, 0)),
            pl.BlockSpec((2, SUBLANES, ns), lane3),
            pl.BlockSpec((2, 3, SUBLANES, ns), lane4),
            pl.BlockSpec((2, SUBLANES, ns), lane3),
        ],
        out_specs=pl.BlockSpec((None, bsz, S5_TILE, S5_LANES), row_out),
        out_shape=jax.ShapeDtypeStruct(u4.shape, F32),
        scratch_shapes=[
            pltpu.VMEM((2, bsz, S5_TILE, S5_LANES), F32),
            pltpu.VMEM((bsz, S5_TILE, 2 * ns), F32),
            pltpu.VMEM((2, bsz, S5_TILE, 2 * ns), BF16),
            pltpu.VMEM((bsz, 2, SUBLANES, ns), F32),
        ],
        compiler_params=pltpu.CompilerParams(
            dimension_semantics=("arbitrary", "arbitrary"),
            vmem_limit_bytes=VMEM_SMALL),
        name="s5",
    )(u4, bcat, ccat, d, wab, a1, pj, aj)
    return out.reshape(nlb, bsz * seq, S5_LANES)


def _kv_kernel(m_ref, nw_ref, wk_ref, wv_ref, k_ref, v_ref):
    mn = _rms(m_ref[0], nw_ref[...]).astype(BF16)
    k_ref[0] = _dot(mn, wk_ref[...]).astype(BF16)
    v_ref[0] = _dot(mn, wv_ref[...]).astype(BF16)


def _kv(mem, norm_w, wk, wv):
    bsz, mlen, d = mem.shape
    blk = pl.BlockSpec((1, mlen, d), lambda b: (b, 0, 0))
    wspec = pl.BlockSpec((d, d), lambda b: (0, 0))
    return pl.pallas_call(
        _kv_kernel,
        grid=(bsz,),
        in_specs=[blk, pl.BlockSpec((1, d), lambda b: (0, 0)), wspec, wspec],
        out_specs=(blk, blk),
        out_shape=(jax.ShapeDtypeStruct(mem.shape, BF16),) * 2,
        compiler_params=pltpu.CompilerParams(
            dimension_semantics=("arbitrary",), vmem_limit_bytes=VMEM_SMALL),
        name="kv",
    )(mem, norm_w, wk, wv)


def _mix_attn_kernel(x_ref, yc_ref, ys_ref, wot_ref, wob_ref, nx_ref, wq_ref,
                     k_ref, v_ref, wo_ref, nf_ref, wr_ref, br_ref,
                     h_ref, xs_ref, y0_ref, gs_ref, ri_ref, cnt_ref,
                     h2s_ref, tri_ref, rho_ref):
    i = pl.program_id(0)
    tm = x_ref.shape[0]
    neg = -jnp.inf
    big = float(ROUTE_LANES)

    @pl.when(i == 0)
    def _():
        h2s_ref[...] = jnp.zeros(h2s_ref.shape, F32)
        r_i = lax.broadcasted_iota(jnp.int32, (tm, tm), 0)
        c_i = lax.broadcasted_iota(jnp.int32, (tm, tm), 1)
        tri_ref[...] = jnp.where(r_i > c_i, 1.0, 0.0).astype(BF16)
        rho_ref[...] = lax.broadcasted_iota(jnp.int32, (tm, TILE_ROWS), 1).astype(F32)

    ys_time = jnp.concatenate(
        [jnp.concatenate([ys_ref[lt, pl.ds(r, S5_STEPS, stride=SUBLANES), :]
                          for r in range(SUBLANES)], axis=0)
         for lt in range(D_SSM // LANES)], axis=1)
    h1 = (x_ref[...] + _dot(yc_ref[...].astype(BF16), wot_ref[...])
          + _dot(ys_time.astype(BF16), wob_ref[...]))
    hn = _rms(h1, nx_ref[...]).astype(BF16)
    q = _dot(hn, wq_ref[...])

    hf = _rms(h2s_ref[...], nf_ref[...]).astype(BF16)
    logits = _dot(hf, wr_ref[...]) + br_ref[...]
    lane = lax.broadcasted_iota(jnp.int32, (tm, ROUTE_LANES), 1)
    lane_f = lane.astype(F32)

    def top1(vals):
        m = jnp.max(vals, axis=-1, keepdims=True)
        idx = jnp.min(jnp.where(vals == m, lane_f, big), axis=-1, keepdims=True)
        return m, idx

    gl = jnp.where(lane < N_EXPERT_GROUPS, logits, neg)
    gmax, gidx = top1(gl)
    g_w = 1.0 / jnp.sum(jnp.exp(gl - gmax), axis=-1, keepdims=True)
    lo = EXPERT_LANE0 + EXPERTS_PER_GROUP * gidx
    el = jnp.where((lane_f >= lo) & (lane_f < lo + EXPERTS_PER_GROUP), logits, neg)
    m1, i1 = top1(el)
    m2, i2 = top1(jnp.where(lane_f == i1, neg, el))
    e21 = jnp.exp(m2 - m1)
    gate1 = g_w / (1.0 + e21)
    gate2 = g_w * e21 / (1.0 + e21)

    heads = []
    for hd in range(N_XHEADS):
        sl = slice(hd * XHEAD_DIM, (hd + 1) * XHEAD_DIM)
        qh = q[:, sl].astype(BF16)
        s = lax.dot_general(qh, k_ref[0, :, sl], (((1,), (1,)), ((), ())),
                            preferred_element_type=F32)
        s = s - jnp.max(s, axis=-1, keepdims=True)
        p = jnp.exp(s)
        p = p / jnp.sum(p, axis=-1, keepdims=True)
        heads.append(_dot(p.astype(BF16), v_ref[0, :, sl]).astype(BF16))
    o = jnp.concatenate(heads, axis=-1)

    hot1 = lane_f == i1
    hot2 = lane_f == i2
    hot = jnp.where(hot1 | hot2, 1.0, 0.0)
    before = _dot(tri_ref[...], hot.astype(BF16))
    count = jnp.sum(hot, axis=0, keepdims=True)
    chunks = jnp.floor((count + (MOE_CHUNK - 1.0)) * (1.0 / MOE_CHUNK))
    l_i = lax.broadcasted_iota(jnp.int32, (ROUTE_LANES, ROUTE_LANES), 0)
    l_j = lax.broadcasted_iota(jnp.int32, (ROUTE_LANES, ROUTE_LANES), 1)
    upper = jnp.where(l_i < l_j, 1.0, 0.0).astype(BF16)
    first_chunk = _dot(jnp.broadcast_to(chunks, (SUBLANES, ROUTE_LANES)).astype(BF16), upper)[0:1]
    start = first_chunk * float(MOE_CHUNK) + before
    pos1 = jnp.sum(jnp.where(hot1, start, 0.0), axis=-1, keepdims=True)
    pos2 = jnp.sum(jnp.where(hot2, start, 0.0), axis=-1, keepdims=True)
    cnt_ref[0] = jnp.broadcast_to(count, (SUBLANES, ROUTE_LANES)).astype(jnp.int32)
    ri_ref[...] = jnp.where(lane == 0, pos1, jnp.where(lane == 1, pos2, 0.0)).astype(jnp.int32)

    h2 = h1 + _dot(o, wo_ref[...])
    h_ref[...] = h2
    h2s_ref[...] = h2

    def pieces(g):
        hi = g.astype(BF16).astype(F32)
        mid = (g - hi).astype(BF16).astype(F32)
        low = (g - hi - mid).astype(BF16).astype(F32)
        return hi, mid, low

    g6 = jnp.where(lane == 6, 1.0, 0.0)
    for li, piece in enumerate(pieces(gate1) + pieces(gate2)):
        g6 = jnp.where(lane == li, piece, g6)
    rho = rho_ref[...]
    ptk = jnp.where(rho == pos1, 1.0, jnp.where(rho == pos2, 2.0, 0.0)).astype(BF16)
    rhs = jnp.concatenate([hf, g6.astype(BF16)], axis=1)
    res = lax.dot_general(ptk, rhs, (((0,), (0,)), ((), ())), preferred_element_type=F32)
    sg = res[:, D_MODEL:]
    which = sg[:, 6:7]
    srt = res[:, :D_MODEL] * jnp.where(which == 2.0, 0.5, 1.0)
    xs_ref[...] = _pack_words(srt).reshape(TILE_CHUNKS, MOE_CHUNK, D_PACK)
    y0_ref[...] = jnp.zeros(y0_ref.shape, jnp.uint32)
    first = sg[:, 0:1] + sg[:, 1:2] + sg[:, 2:3]
    second = 0.5 * (sg[:, 3:4] + sg[:, 4:5] + sg[:, 5:6])
    gsort = jnp.where(which == 1.0, first, jnp.where(which == 2.0, second, 0.0))
    gs_ref[...] = jnp.broadcast_to(gsort, (TILE_ROWS, ROUTE_LANES))


def _pack_words(v):
    hi = lax.bitcast_convert_type(v[:, :D_PACK], jnp.uint32) & jnp.uint32(HI_MASK)
    lo = lax.bitcast_convert_type(v[:, D_PACK:], jnp.uint32) >> 16
    return hi | lo


def _unpack_words(w):
    hi = lax.bitcast_convert_type(w & jnp.uint32(HI_MASK), F32)
    lo = lax.bitcast_convert_type(w << 16, F32)
    return hi, lo


def _mix_attn(xt, yc, ys, wot, wob, nx, wq, k, v, wo, nf, wr, br, seq):
    t, d = xt.shape
    nt = t // TM_MIX
    tiles_per_batch = seq // TM_MIX
    mlen = k.shape[1]
    att = lambda i: jnp.minimum(i, nt - 1)
    rte = lambda i: jnp.maximum(i - 1, 0)
    row_a = lambda i: (att(i), 0)
    row_r = lambda i: (rte(i), 0)
    const = lambda i: (0, 0)
    tile3 = lambda i: (rte(i), 0, 0)
    kvspec = pl.BlockSpec((1, mlen, d), lambda i: (att(i) // tiles_per_batch, 0, 0))
    packed = jax.ShapeDtypeStruct((nt * TILE_CHUNKS, MOE_CHUNK, D_PACK), jnp.uint32)
    return pl.pallas_call(
        _mix_attn_kernel,
        grid=(nt + 1,),
        in_specs=[
            pl.BlockSpec((TM_MIX, d), row_a),
            pl.BlockSpec((TM_MIX, D_CONV), row_a),
            pl.BlockSpec((D_SSM // LANES, TM_MIX, LANES), lambda i: (0, att(i), 0)),
            pl.BlockSpec((D_CONV, d), const),
            pl.BlockSpec((D_SSM, d), const),
            pl.BlockSpec((1, d), const),
            pl.BlockSpec((d, d), const),
            kvspec, kvspec,
            pl.BlockSpec((d, d), const),
            pl.BlockSpec((1, d), const),
            pl.BlockSpec((d, ROUTE_LANES), const),
            pl.BlockSpec((1, ROUTE_LANES), const),
        ],
        out_specs=(
            pl.BlockSpec((TM_MIX, d), row_a),
            pl.BlockSpec((TILE_CHUNKS, MOE_CHUNK, D_PACK), tile3),
            pl.BlockSpec((TILE_CHUNKS, MOE_CHUNK, D_PACK), tile3),
            pl.BlockSpec((TILE_ROWS, ROUTE_LANES), row_r),
            pl.BlockSpec((TM_MIX, ROUTE_LANES), row_r),
            pl.BlockSpec((1, SUBLANES, ROUTE_LANES), tile3),
        ),
        out_shape=(
            jax.ShapeDtypeStruct((t, d), F32),
            packed,
            packed,
            jax.ShapeDtypeStruct((nt * TILE_ROWS, ROUTE_LANES), F32),
            jax.ShapeDtypeStruct((t, ROUTE_LANES), jnp.int32),
            jax.ShapeDtypeStruct((nt, SUBLANES, ROUTE_LANES), jnp.int32),
        ),
        scratch_shapes=[
            pltpu.VMEM((TM_MIX, d), F32),
            pltpu.VMEM((TM_MIX, TM_MIX), BF16),
            pltpu.VMEM((TM_MIX, TILE_ROWS), F32),
        ],
        compiler_params=pltpu.CompilerParams(
            dimension_semantics=("arbitrary",), vmem_limit_bytes=VMEM_MIX),
        name="mix_attn",
    )(xt, yc, ys, wot, wob, nx, wq, k, v, wo, nf, wr, br)


ZERO_CHUNK = TILE_CHUNKS - 1
DUMP_TILE0 = 16
SPARE_PER_TILE = TILE_CHUNKS - TILE_USED_CHUNKS


def _dump_chunk(slot, k):
    idx = slot * BLOCK_CHUNKS + k
    return (DUMP_TILE0 + idx // SPARE_PER_TILE) * TILE_CHUNKS + TILE_USED_CHUNKS + idx % SPARE_PER_TILE


def _experts_kernel(src_ref, dst_ref, be_ref, seq_ref, nxt_ref, na_ref, xs_hbm, w1_hbm, w3_hbm,
                    w2_hbm, y0_hbm, ys_hbm, xbuf, ybuf, w1f_ref, w3f_ref, w2f_ref,
                    w1s_ref, w3s_ref, w2s_ref, in_sem, out_sem, w_sem):
    del y0_hbm
    b = pl.program_id(0)
    n_active = na_ref[0]
    slot = b % 2
    other = 1 - slot

    def gather(blk, sl):
        for k in range(BLOCK_CHUNKS):
            pltpu.make_async_copy(xs_hbm.at[src_ref[blk * BLOCK_CHUNKS + k]],
                                  xbuf.at[sl, k], in_sem.at[sl]).start()

    def gather_wait(sl):
        pltpu.make_async_copy(xs_hbm.at[pl.ds(0, BLOCK_CHUNKS)], xbuf.at[sl],
                              in_sem.at[sl]).wait()

    def scatter(blk, sl):
        for k in range(BLOCK_CHUNKS):
            d = dst_ref[blk * BLOCK_CHUNKS + k]
            dump = jnp.where(sl == 0, _dump_chunk(0, k), _dump_chunk(1, k))
            pltpu.make_async_copy(ybuf.at[sl, k], ys_hbm.at[jnp.where(d < 0, dump, d)],
                                  out_sem.at[sl]).start()

    def scatter_wait(sl):
        pltpu.make_async_copy(ybuf.at[sl], ys_hbm.at[pl.ds(0, BLOCK_CHUNKS)],
                              out_sem.at[sl]).wait()

    active = b < n_active

    @pl.when(b == 0)
    def _():
        gather(0, 0)
        ybuf[...] = jnp.zeros(ybuf.shape, jnp.uint32)

    prev = be_ref[jnp.maximum(b - 1, 0)]
    fresh = (b == 0) | (be_ref[b] != prev)

    def weight_copies(e, ws):
        return [pltpu.make_async_copy(hbm.at[e], buf.at[ws], w_sem.at[ws])
                for hbm, buf in ((w1_hbm, w1f_ref), (w3_hbm, w3f_ref), (w2_hbm, w2f_ref))]

    @pl.when(active & fresh)
    def _():
        ws = seq_ref[b] % 2

        @pl.when(b == 0)
        def _():
            for cp in weight_copies(be_ref[b], ws):
                cp.start(priority=1)

        for cp in weight_copies(be_ref[b], ws):
            cp.wait()

        @pl.when(nxt_ref[b] >= 0)
        def _():
            for cp in weight_copies(nxt_ref[b], 1 - ws):
                cp.start(priority=1)

        w1s_ref[...] = w1f_ref[ws].astype(BF16)
        w3s_ref[...] = w3f_ref[ws].astype(BF16)
        w2s_ref[...] = w2f_ref[ws].astype(BF16)

    @pl.when(active & (b >= 1))
    def _():
        scatter_wait(slot)

    @pl.when(active)
    def _():
        gather_wait(slot)
        gather(b + 1, other)
        scatter(b, other)

        hi, lo = _unpack_words(xbuf[slot].reshape(MOE_BLOCK, D_PACK))
        xb = jnp.concatenate([hi, lo], axis=1).astype(BF16)
        h1 = _dot(xb, w1s_ref[...])
        h3 = _dot(xb, w3s_ref[...])
        hid = (h1 * jax.nn.sigmoid(h1) * h3).astype(BF16)
        y = _dot(hid, w2s_ref[...]).astype(BF16).astype(F32)
        ybuf[slot] = _pack_words(y).reshape(BLOCK_CHUNKS, MOE_CHUNK, D_PACK)

        @pl.when(b == n_active - 1)
        def _():
            scatter(b + 1, slot)
            gather_wait(other)
            scatter_wait(other)
            scatter_wait(slot)


def _experts(src, dst, block_expert, expert_seq, next_expert, n_active, xs, w1, w3, w2, y0):
    nb = block_expert.shape[0]
    d = D_MODEL
    hbm = pl.BlockSpec(memory_space=pl.ANY)
    grid_spec = pltpu.PrefetchScalarGridSpec(
        num_scalar_prefetch=6,
        grid=(nb,),
        in_specs=[hbm, hbm, hbm, hbm, hbm],
        out_specs=hbm,
        scratch_shapes=[
            pltpu.VMEM((2, BLOCK_CHUNKS, MOE_CHUNK, D_PACK), jnp.uint32),
            pltpu.VMEM((2, BLOCK_CHUNKS, MOE_CHUNK, D_PACK), jnp.uint32),
            pltpu.VMEM((2, d, D_EXPERT), F32),
            pltpu.VMEM((2, d, D_EXPERT), F32),
            pltpu.VMEM((2, D_EXPERT, d), F32),
            pltpu.VMEM((d, D_EXPERT), BF16),
            pltpu.VMEM((d, D_EXPERT), BF16),
            pltpu.VMEM((D_EXPERT, d), BF16),
            pltpu.SemaphoreType.DMA((2,)),
            pltpu.SemaphoreType.DMA((2,)),
            pltpu.SemaphoreType.DMA((2,)),
        ],
    )
    return pl.pallas_call(
        _experts_kernel,
        grid_spec=grid_spec,
        out_shape=jax.ShapeDtypeStruct(y0.shape, jnp.uint32),
        input_output_aliases={10: 0},
        compiler_params=pltpu.CompilerParams(
            dimension_semantics=("arbitrary",), vmem_limit_bytes=VMEM_EXPERTS),
        name="experts",
    )(src, dst, block_expert, expert_seq, next_expert, n_active, xs, w1, w3, w2, y0)


def _combine_kernel(h_ref, ri_ref, gs_ref, ys_ref, fw_ref, o_ref):
    tm = h_ref.shape[0]
    hi, lo = _unpack_words(ys_ref[...].reshape(TILE_ROWS, D_PACK))
    g = gs_ref[...]
    gw = jnp.concatenate([g] * (D_PACK // ROUTE_LANES), axis=1)
    yg = jnp.concatenate([hi * gw, lo * gw], axis=1).astype(BF16)
    pos = ri_ref[...].astype(F32)
    rho = lax.broadcasted_iota(jnp.int32, (tm, TILE_ROWS), 1).astype(F32)
    q = jnp.where((rho == pos[:, 0:1]) | (rho == pos[:, 1:2]), 1.0, 0.0).astype(BF16)
    o_ref[...] = _rms(h_ref[...] + _dot(q, yg), fw_ref[...])


def _combine(h2, route_i, gsort, ys, final_w):
    t, d = h2.shape
    nt = t // TM_MIX
    row = lambda i: (i, 0)
    return pl.pallas_call(
        _combine_kernel,
        grid=(nt,),
        in_specs=[
            pl.BlockSpec((TM_MIX, d), row),
            pl.BlockSpec((TM_MIX, ROUTE_LANES), row),
            pl.BlockSpec((TILE_ROWS, ROUTE_LANES), row),
            pl.BlockSpec((TILE_CHUNKS, MOE_CHUNK, D_PACK), lambda i: (i, 0, 0)),
            pl.BlockSpec((1, d), lambda i: (0, 0)),
        ],
        out_specs=pl.BlockSpec((TM_MIX, d), row),
        out_shape=jax.ShapeDtypeStruct((t, d), F32),
        compiler_params=pltpu.CompilerParams(
            dimension_semantics=("arbitrary",), vmem_limit_bytes=VMEM_EXPERTS),
        name="combine",
    )(h2, route_i, gsort, ys, final_w)


def _routing_tables(cnt):
    nt = cnt.shape[0]
    n = cnt[:, 0, EXPERT_LANE0:EXPERT_LANE0 + N_EXPERTS]
    c = (n + MOE_CHUNK - 1) // MOE_CHUNK
    local = jnp.cumsum(c, axis=1) - c
    per_expert = jnp.sum(c, axis=0)
    padded = (per_expert + BLOCK_CHUNKS - 1) // BLOCK_CHUNKS * BLOCK_CHUNKS
    gend = jnp.cumsum(padded)
    gstart = gend - padded
    within = jnp.cumsum(c, axis=0) - c
    g_max = (2 * nt * TM_MIX // MOE_CHUNK + nt * N_EXPERTS
             + N_EXPERTS * (BLOCK_CHUNKS - 1))
    n_blocks = -(-g_max // BLOCK_CHUNKS)
    g = jnp.arange(n_blocks * BLOCK_CHUNKS, dtype=jnp.int32)
    e_of = jnp.minimum(jnp.sum((gend[None, :] <= g[:, None]).astype(jnp.int32), axis=1),
                       N_EXPERTS - 1)
    hot_e = (e_of[:, None] == jnp.arange(N_EXPERTS, dtype=jnp.int32)[None, :]).astype(F32)
    pick = lambda table: jnp.dot(hot_e, table.T.astype(F32), precision=lax.Precision.HIGHEST)
    rel = g.astype(F32) - pick(gstart[None, :])[:, 0]
    w_rows = pick(within)
    tile = jnp.sum((w_rows <= rel[:, None]).astype(jnp.int32), axis=1) - 1
    hot_t = tile[:, None] == jnp.arange(nt, dtype=jnp.int32)[None, :]
    at_tile = lambda rows: jnp.sum(jnp.where(hot_t, rows, 0.0), axis=1)
    k = rel - at_tile(w_rows)
    valid = k < at_tile(pick(c))
    chunk = (tile * TILE_CHUNKS + (at_tile(pick(local)) + k).astype(jnp.int32))
    pad_src = jnp.full((BLOCK_CHUNKS,), ZERO_CHUNK, jnp.int32)
    pad_dst = jnp.full((BLOCK_CHUNKS,), -1, jnp.int32)
    src = jnp.concatenate([jnp.where(valid, chunk, ZERO_CHUNK).astype(jnp.int32), pad_src])
    dst = jnp.concatenate([pad_dst, jnp.where(valid, chunk, -1).astype(jnp.int32)])
    block_start = jnp.arange(n_blocks, dtype=jnp.int32) * BLOCK_CHUNKS
    block_expert = jnp.minimum(
        jnp.sum((block_start[:, None] >= gend[None, :]).astype(jnp.int32), axis=1),
        N_EXPERTS - 1).astype(jnp.int32)
    n_active = (gend[-1:] // BLOCK_CHUNKS).astype(jnp.int32)
    blk = jnp.arange(n_blocks, dtype=jnp.int32)
    is_first = (blk < n_active[0]) & ((blk == 0) | (block_expert != jnp.roll(block_expert, 1)))
    expert_seq = (jnp.cumsum(is_first.astype(jnp.int32)) - 1).astype(jnp.int32)
    later_first = is_first[None, :] & (blk[None, :] > blk[:, None])
    next_blk = jnp.min(jnp.where(later_first, blk[None, :], n_blocks), axis=1)
    next_expert = jnp.where(
        next_blk < n_blocks,
        jnp.sum(jnp.where(blk[None, :] == next_blk[:, None], block_expert[None, :], 0), axis=1),
        -1).astype(jnp.int32)
    return src, dst, block_expert, expert_seq, next_expert, n_active


def _layer(h, mem, p, final_w):
    bsz, seq, d = h.shape
    t = bsz * seq
    assert t // TM_MIX >= DUMP_TILE0 + 2 * BLOCK_CHUNKS // SPARE_PER_TILE
    xt = h.reshape(t, d)

    v, u = _in_proj(xt, p["norm_mix_w"].reshape(1, d), p["w_in"].astype(BF16))
    y_conv = _conv(v, p["conv_w"], p["conv_b"].reshape(1, D_CONV),
                   p["conv_ln_w"].reshape(1, D_CONV), p["conv_ln_b"].reshape(1, D_CONV),
                   bsz, seq)

    a1, pj, aj, bbar = _s5_prep(p["ssm_A_re"], p["ssm_A_im"], p["ssm_log_dt"],
                                      p["ssm_B_re"], p["ssm_B_im"])
    nlb = D_SSM // S5_LANES
    bb = bbar.reshape(2, SSM_GROUP, nlb, S5_GROUPS, SSM_STATE).transpose(0, 2, 3, 1, 4)
    bcat = jnp.concatenate([_block_diag(bb[0]), _block_diag(bb[1])], axis=-1).astype(BF16)
    c_re = p["ssm_C_re"].reshape(nlb, S5_GROUPS, SSM_GROUP, SSM_STATE).transpose(0, 1, 3, 2)
    c_im = p["ssm_C_im"].reshape(nlb, S5_GROUPS, SSM_GROUP, SSM_STATE).transpose(0, 1, 3, 2)
    ccat = jnp.concatenate([_block_diag(c_re), -_block_diag(c_im)], axis=1).astype(BF16)
    glu = p["ssm_glu_w"].reshape(nlb, S5_GROUPS, SSM_GROUP, 2 * SSM_GROUP)
    wab = jnp.concatenate([_block_diag(glu[..., :SSM_GROUP]),
                           _block_diag(glu[..., SSM_GROUP:])], axis=-1).astype(BF16)
    y_ssm = _s5(u, bcat, ccat, p["ssm_D"].reshape(1, D_SSM), wab,
                a1, pj, aj, bsz, seq)

    k, vv = _kv(mem, p["norm_mem_w"].reshape(1, d), p["xk_w"].astype(BF16),
                p["xv_w"].astype(BF16))
    w_out = p["w_out"].astype(BF16)
    wr = jnp.concatenate([p["router_group_w"], p["router_expert_w"]], axis=1)
    wr = jnp.pad(wr, ((0, 0), (0, ROUTE_LANES - wr.shape[1]))).astype(BF16)
    br = jnp.concatenate([p["router_group_b"], p["router_expert_b"].reshape(-1)])
    br = jnp.pad(br, (0, ROUTE_LANES - br.shape[0])).reshape(1, ROUTE_LANES)
    h2, xs, y0, gsort, route_i, cnt = _mix_attn(
        xt, y_conv, y_ssm, w_out[:D_CONV], w_out[D_CONV:], p["norm_x_w"].reshape(1, d),
        (p["xq_w"] * (XHEAD_DIM ** -0.5)).astype(BF16), k, vv, p["xo_w"].astype(BF16),
        p["norm_ffn_w"].reshape(1, d), wr, br, seq)

    src, dst, block_expert, expert_seq, next_expert, n_active = _routing_tables(cnt)
    ys = _experts(src, dst, block_expert, expert_seq, next_expert, n_active, xs,
                  p["moe_w1"], p["moe_w3"], p["moe_w2"], y0)
    out = _combine(h2, route_i, gsort, ys, final_w.reshape(1, d))
    return out.reshape(bsz, seq, d)


def kernel(x, mem, norm_mix_w, w_in, conv_w, conv_b, conv_ln_w, conv_ln_b, ssm_A_re, ssm_A_im, ssm_log_dt, ssm_B_re, ssm_B_im, ssm_C_re, ssm_C_im, ssm_D, ssm_glu_w, w_out, norm_x_w, norm_mem_w, xq_w, xk_w, xv_w, xo_w, norm_ffn_w, router_group_w, router_group_b, router_expert_w, router_expert_b, moe_w1, moe_w3, moe_w2, final_norm_w):
    stacked = dict(
        norm_mix_w=norm_mix_w, w_in=w_in, conv_w=conv_w, conv_b=conv_b,
        conv_ln_w=conv_ln_w, conv_ln_b=conv_ln_b, ssm_A_re=ssm_A_re, ssm_A_im=ssm_A_im,
        ssm_log_dt=ssm_log_dt, ssm_B_re=ssm_B_re, ssm_B_im=ssm_B_im, ssm_C_re=ssm_C_re,
        ssm_C_im=ssm_C_im, ssm_D=ssm_D, ssm_glu_w=ssm_glu_w, w_out=w_out,
        norm_x_w=norm_x_w, norm_mem_w=norm_mem_w, xq_w=xq_w, xk_w=xk_w, xv_w=xv_w,
        xo_w=xo_w, norm_ffn_w=norm_ffn_w, router_group_w=router_group_w,
        router_group_b=router_group_b, router_expert_w=router_expert_w,
        router_expert_b=router_expert_b, moe_w1=moe_w1, moe_w3=moe_w3, moe_w2=moe_w2)
    depth = norm_mix_w.shape[0]
    assert depth == 1, "final norm is fused into the single layer's combine step"
    layer = {name: w[0] for name, w in stacked.items()}
    return _layer(x, mem, layer, final_norm_w)
```

```python
import functools

import jax
import jax.numpy as jnp
from jax import lax
from jax.experimental import pallas as pl
from jax.experimental.pallas import tpu as pltpu

D_MODEL = 1024
D_CONV = 512
CONV_WIDTH = 31
D_SSM = 512
SSM_GROUP = 16
N_SSM_GROUPS = 32
SSM_STATE = 64
N_XHEADS = 4
XHEAD_DIM = 256
N_EXPERT_GROUPS = 4
EXPERTS_PER_GROUP = 8
N_EXPERTS = 32
D_EXPERT = 512
EPS = 1e-6

F32 = jnp.float32
BF16 = jnp.bfloat16

SUBLANES = 8
LANES = 128

TM_PROJ = 1024
TM_CONV = 1024
CONV_CHUNK = 128
CONV_NORM_ROWS = 128
CONV_HALO = 32
S5_STEPS = 64
S5_TILE = SUBLANES * S5_STEPS
S5_GROUP_ROWS = 256
S5_LANES = 128
S5_GROUPS = S5_LANES // SSM_GROUP
S5_STATE = S5_GROUPS * SSM_STATE
TM_MIX = 512
ROUTE_LANES = 128
EXPERT_LANE0 = N_EXPERT_GROUPS
MOE_CHUNK = SUBLANES
MOE_BLOCK = 256
BLOCK_CHUNKS = MOE_BLOCK // MOE_CHUNK
TILE_USED_CHUNKS = 2 * TM_MIX // MOE_CHUNK + N_EXPERTS * (MOE_CHUNK - 1) // MOE_CHUNK
TILE_CHUNKS = 160
TILE_ROWS = TILE_CHUNKS * MOE_CHUNK
D_PACK = D_MODEL // 2
HI_MASK = 0xFFFF0000

VMEM_SMALL = 40 << 20
VMEM_EXPERTS = 48 << 20
VMEM_MIX = 62 << 20


def _rms(x, w):
    return x * lax.rsqrt(jnp.mean(x * x, axis=-1, keepdims=True) + EPS) * w


def _dot(a, b):
    return jnp.dot(a, b, preferred_element_type=F32)


def _s5_prep_kernel(are_ref, aim_ref, ldt_ref, btre_ref, btim_ref,
                    a1_ref, pj_ref, aj_ref, bbar_ref):
    lam_re = are_ref[...]
    lam_im = aim_ref[...]
    dt = jnp.exp(ldt_ref[...])
    x = lam_re * dt
    y = lam_im * dt

    def power(k):
        mag = jnp.exp(k * x)
        return mag * jnp.cos(k * y), mag * jnp.sin(k * y)

    n = x.shape[-1]
    ones8 = jnp.ones((SUBLANES, n), F32)
    a_re, a_im = power(ones8)
    a1_ref[0] = a_re
    a1_ref[1] = a_im

    row = lax.broadcasted_iota(jnp.int32, (SUBLANES, n), 0)
    for i, d in enumerate((1, 2, 4)):
        p_re, p_im = power(ones8 * float(d * S5_STEPS))
        keep = row >= d
        pj_ref[0, i] = jnp.where(keep, p_re, 0.0)
        pj_ref[1, i] = jnp.where(keep, p_im, 0.0)
    j_re, j_im = power(ones8 * float(S5_STEPS))
    aj_ref[0] = j_re
    aj_ref[1] = j_im

    num_re = a_re[0:1] - 1.0
    num_im = a_im[0:1]
    den = lam_re * lam_re + lam_im * lam_im
    c_re = (num_re * lam_re + num_im * lam_im) / den
    c_im = (num_im * lam_re - num_re * lam_im) / den
    b_re = btre_ref[...]
    b_im = btim_ref[...]
    bbar_ref[0] = c_re * b_re - c_im * b_im
    bbar_ref[1] = c_re * b_im + c_im * b_re


def _s5_prep(a_re, a_im, log_dt, b_re, b_im):
    n = N_SSM_GROUPS * SSM_STATE
    are = a_re.reshape(1, n)
    aim = a_im.reshape(1, n)
    ldt = jnp.repeat(log_dt, SSM_STATE).reshape(1, n)
    btre = jnp.transpose(b_re, (2, 0, 1)).reshape(SSM_GROUP, n)
    btim = jnp.transpose(b_im, (2, 0, 1)).reshape(SSM_GROUP, n)
    return pl.pallas_call(
        _s5_prep_kernel,
        out_shape=(
            jax.ShapeDtypeStruct((2, SUBLANES, n), F32),
            jax.ShapeDtypeStruct((2, 3, SUBLANES, n), F32),
            jax.ShapeDtypeStruct((2, SUBLANES, n), F32),
            jax.ShapeDtypeStruct((2, SSM_GROUP, n), F32),
        ),
        name="s5_prep",
    )(are, aim, ldt, btre, btim)


def _block_diag(w):
    nl, g, r, c = w.shape
    eye = jnp.eye(g, dtype=w.dtype)
    return jnp.einsum("lgrc,gh->lgrhc", w, eye).reshape(nl, g * r, g * c)


def _in_proj_kernel(x_ref, nw_ref, w_ref, v_ref, u_ref):
    xn = _rms(x_ref[...], nw_ref[...]).astype(BF16)
    proj = _dot(xn, w_ref[...])
    a = proj[:, :D_CONV]
    g = proj[:, D_CONV:2 * D_CONV]
    v_ref[...] = a * jax.nn.sigmoid(g)
    for lt in range(D_SSM // LANES):
        c0 = 2 * D_CONV + lt * LANES
        for tile in range(TM_PROJ // S5_TILE):
            for r in range(SUBLANES):
                t0 = tile * S5_TILE + r * S5_STEPS
                u_ref[lt, pl.ds(tile * S5_TILE + r, S5_STEPS, stride=SUBLANES), :] = (
                    proj[t0:t0 + S5_STEPS, c0:c0 + LANES])


def _in_proj(xt, norm_w, w_in):
    t = xt.shape[0]
    n_out = 2 * D_CONV + D_SSM
    return pl.pallas_call(
        _in_proj_kernel,
        grid=(t // TM_PROJ,),
        in_specs=[
            pl.BlockSpec((TM_PROJ, D_MODEL), lambda i: (i, 0)),
            pl.BlockSpec((1, D_MODEL), lambda i: (0, 0)),
            pl.BlockSpec((D_MODEL, n_out), lambda i: (0, 0)),
        ],
        out_specs=(
            pl.BlockSpec((TM_PROJ, D_CONV), lambda i: (i, 0)),
            pl.BlockSpec((D_SSM // LANES, TM_PROJ, LANES), lambda i: (0, i, 0)),
        ),
        out_shape=(
            jax.ShapeDtypeStruct((t, D_CONV), F32),
            jax.ShapeDtypeStruct((D_SSM // LANES, t, LANES), F32),
        ),
        compiler_params=pltpu.CompilerParams(
            dimension_semantics=("arbitrary",), vmem_limit_bytes=VMEM_SMALL),
        name="in_proj",
    )(xt, norm_w, w_in)


def _conv_kernel(v_ref, w_ref, b_ref, lnw_ref, lnb_ref, o_ref, ext_ref, sh_ref):
    tt = pl.program_id(1)
    rows = CONV_HALO + TM_CONV
    n_lt = D_CONV // LANES

    @pl.when(tt == 0)
    def _():
        ext_ref[:, pl.ds(0, CONV_HALO), :] = jnp.zeros((n_lt, CONV_HALO, LANES), F32)

    @pl.when(tt > 0)
    def _():
        ext_ref[:, pl.ds(0, CONV_HALO), :] = ext_ref[:, pl.ds(TM_CONV, CONV_HALO), :]

    for lt in range(n_lt):
        ext_ref[lt, pl.ds(CONV_HALO, TM_CONV), :] = v_ref[:, pl.ds(lt * LANES, LANES)]
    for s in range(1, SUBLANES):
        sh_ref[s - 1, :, pl.ds(0, rows - SUBLANES), :] = ext_ref[:, pl.ds(s, rows - SUBLANES), :]
    bias = b_ref[...]
    lnw = lnw_ref[...]
    lnb = lnb_ref[...]
    tap0 = CONV_HALO - (CONV_WIDTH - 1)

    groups = CONV_CHUNK // SUBLANES

    def chunk(ci, carry):
        base = pl.multiple_of(ci * CONV_CHUNK, CONV_CHUNK)
        for lt in range(n_lt):
            lanes = pl.ds(lt * LANES, LANES)
            acc = [jnp.broadcast_to(bias[:, lt * LANES:(lt + 1) * LANES], (SUBLANES, LANES))] * groups
            for s in range(SUBLANES):
                taps = [j for j in range(CONV_WIDTH) if (tap0 + j) % SUBLANES == s]
                src = ext_ref.at[lt] if s == 0 else sh_ref.at[s - 1, lt]
                ngroups = (tap0 + taps[-1] - s) // SUBLANES + groups
                win = [src[pl.ds(base + SUBLANES * g, SUBLANES), :] for g in range(ngroups)]
                for j in taps:
                    g0 = (tap0 + j - s) // SUBLANES
                    wj = w_ref[j, :, lanes]
                    acc = [acc[r] + wj * win[g0 + r] for r in range(groups)]
            o_ref[pl.ds(base, CONV_CHUNK), lanes] = jnp.concatenate(acc, axis=0)
        return carry

    lax.fori_loop(0, TM_CONV // CONV_CHUNK, chunk, 0)

    for bi in range(TM_CONV // CONV_NORM_ROWS):
        rows_b = pl.ds(bi * CONV_NORM_ROWS, CONV_NORM_ROWS)
        acc = o_ref[rows_b, :]
        mu = jnp.mean(acc, axis=-1, keepdims=True)
        cen = acc - mu
        var = jnp.mean(cen * cen, axis=-1, keepdims=True)
        z = cen * lax.rsqrt(var + EPS) * lnw + lnb
        o_ref[rows_b, :] = z * jax.nn.sigmoid(z)


def _conv(v, conv_w, conv_b, ln_w, ln_b, bsz, seq):
    nt = seq // TM_CONV
    row = lambda b, t: (b * nt + t, 0)
    const = lambda b, t: (0, 0)
    return pl.pallas_call(
        _conv_kernel,
        grid=(bsz, nt),
        in_specs=[
            pl.BlockSpec((TM_CONV, D_CONV), row),
            pl.BlockSpec((CONV_WIDTH, SUBLANES, D_CONV), lambda b, t: (0, 0, 0)),
            pl.BlockSpec((1, D_CONV), const),
            pl.BlockSpec((1, D_CONV), const),
            pl.BlockSpec((1, D_CONV), const),
        ],
        out_specs=pl.BlockSpec((TM_CONV, D_CONV), row),
        out_shape=jax.ShapeDtypeStruct(v.shape, F32),
        scratch_shapes=[
            pltpu.VMEM((D_CONV // LANES, CONV_HALO + TM_CONV, LANES), F32),
            pltpu.VMEM((SUBLANES - 1, D_CONV // LANES, CONV_HALO + TM_CONV, LANES), F32),
        ],
        compiler_params=pltpu.CompilerParams(
            dimension_semantics=("arbitrary", "arbitrary")),
        name="conv",
    )(v, jnp.broadcast_to(conv_w[:, None, :], (CONV_WIDTH, SUBLANES, D_CONV)), conv_b, ln_w, ln_b)


def _cmul(a_re, a_im, b_re, b_im):
    return a_re * b_re - a_im * b_im, a_re * b_im + a_im * b_re


def _s5_kernel(u_ref, bcat_ref, ccat_ref, d_ref, wab_ref, a1_ref,
               pj_ref, aj_ref, o_ref, up_ref, bu_ref, st_ref, carry_ref):
    tt = pl.program_id(1)
    ns = S5_STATE
    nseq = u_ref.shape[0]
    steps_per_group = S5_GROUP_ROWS // SUBLANES
    n_groups = S5_TILE // S5_GROUP_ROWS

    @pl.when((pl.program_id(0) == 0) & (tt == 0))
    def _():
        up_ref[...] = jnp.zeros(up_ref.shape, F32)
        st_ref[...] = jnp.zeros(st_ref.shape, BF16)

    @pl.when(tt == 0)
    def _():
        carry_ref[...] = jnp.zeros(carry_ref.shape, F32)

    def scan_and_project(cur, prv):
        up_ref[cur] = u_ref[...]

        a_re = a1_ref[0]
        a_im = a1_ref[1]

        def step(q, j, s):
            rows = pl.ds(j * SUBLANES, SUBLANES)
            m_re, m_im = _cmul(a_re, a_im, s[0], s[1])
            return m_re + bu_ref[q, rows, pl.ds(0, ns)], m_im + bu_ref[q, rows, pl.ds(ns, ns)]

        def project_out(q, g):
            rows = pl.ds(g * S5_GROUP_ROWS, S5_GROUP_ROWS)
            y = _dot(st_ref[prv, q, rows, :], ccat_ref[0]) + d_ref[...] * up_ref[prv, q, rows, :]
            y = jax.nn.gelu(y)
            ab = _dot(y.astype(BF16), wab_ref[0])
            o_ref[q, rows, :] = ab[:, :S5_LANES] * jax.nn.sigmoid(ab[:, S5_LANES:])

        zero = jnp.zeros((SUBLANES, ns), F32)
        state = [(zero, zero)] * nseq
        for g in range(n_groups):
            rows = pl.ds(g * S5_GROUP_ROWS, S5_GROUP_ROWS)
            for q in range(nseq):
                bu_ref[q, rows, :] = _dot(up_ref[cur, q, rows, :].astype(BF16), bcat_ref[0])
            for jj in range(steps_per_group):
                state = [step(q, g * steps_per_group + jj, state[q]) for q in range(nseq)]
            for q in range(nseq):
                project_out(q, g)

        row = lax.broadcasted_iota(jnp.int32, (SUBLANES, ns), 0)
        first = row == 0
        entry = []
        for q in range(nseq):
            e_re, e_im = state[q]
            c_re = jnp.where(first, pltpu.roll(carry_ref[q, 0], 1, 0), pltpu.roll(e_re, 1, 0))
            c_im = jnp.where(first, pltpu.roll(carry_ref[q, 1], 1, 0), pltpu.roll(e_im, 1, 0))
            for i, d in enumerate((1, 2, 4)):
                r_re = pltpu.roll(c_re, d, 0)
                r_im = pltpu.roll(c_im, d, 0)
                m_re, m_im = _cmul(pj_ref[0, i], pj_ref[1, i], r_re, r_im)
                c_re = c_re + m_re
                c_im = c_im + m_im
            f_re, f_im = _cmul(aj_ref[0], aj_ref[1], c_re, c_im)
            carry_ref[q, 0] = f_re + e_re
            carry_ref[q, 1] = f_im + e_im
            entry.append((c_re, c_im))

        state = entry
        pack = 2 * SUBLANES
        for j in range(0, S5_STEPS, 2):
            mid = [step(q, j, state[q]) for q in range(nseq)]
            state = [step(q, j + 1, mid[q]) for q in range(nseq)]
            for q in range(nseq):
                st_ref[cur, q, pl.ds(j * SUBLANES, pack), pl.ds(0, ns)] = (
                    jnp.concatenate([mid[q][0], state[q][0]], axis=0).astype(BF16))
                st_ref[cur, q, pl.ds(j * SUBLANES, pack), pl.ds(ns, ns)] = (
                    jnp.concatenate([mid[q][1], state[q][1]], axis=0).astype(BF16))

    for parity in range(2):
        pl.when(tt % 2 == parity)(functools.partial(scan_and_project, parity, 1 - parity))


def _s5(u, bcat, ccat, d, wab, a1, pj, aj, bsz, seq):
    nt = seq // S5_TILE
    nlb = D_SSM // S5_LANES
    ns = S5_STATE
    assert S5_LANES == LANES and S5_TILE == TM_MIX and TM_PROJ % S5_TILE == 0
    u4 = u.reshape(nlb, bsz, seq, S5_LANES)
    row_in = lambda l, t: (l, 0, jnp.minimum(t, nt - 1), 0)
    row_out = lambda l, t: (l, 0, jnp.maximum(t - 1, 0), 0)
    lane3 = lambda l, t: (0, 0, l)
    lane4 = lambda l, t: (0, 0, 0, l)
    out = pl.pallas_call(
        _s5_kernel,
        grid=(nlb, nt + 1),
        in_specs=[
            pl.BlockSpec((None, bsz, S5_TILE, S5_LANES), row_in),
            pl.BlockSpec((1, S5_LANES, 2 * ns), lambda l, t: (l, 0, 0)),
            pl.BlockSpec((1, 2 * ns, S5_LANES), lambda l, t: (l, 0, 0)),
            pl.BlockSpec((1, S5_LANES), lambda l, t: (0, l)),
            pl.BlockSpec((1, S5_LANES, 2 * S5_LANES), lambda l, t: (l, 0, 0)),
            pl.BlockSpec((2, SUBLANES, ns), lane3),
            pl.BlockSpec((2, 3, SUBLANES, ns), lane4),
            pl.BlockSpec((2, SUBLANES, ns), lane3),
        ],
        out_specs=pl.BlockSpec((None, bsz, S5_TILE, S5_LANES), row_out),
        out_shape=jax.ShapeDtypeStruct(u4.shape, F32),
        scratch_shapes=[
            pltpu.VMEM((2, bsz, S5_TILE, S5_LANES), F32),
            pltpu.VMEM((bsz, S5_TILE, 2 * ns), F32),
            pltpu.VMEM((2, bsz, S5_TILE, 2 * ns), BF16),
            pltpu.VMEM((bsz, 2, SUBLANES, ns), F32),
        ],
        compiler_params=pltpu.CompilerParams(
            dimension_semantics=("arbitrary", "arbitrary"),
            vmem_limit_bytes=VMEM_SMALL),
        name="s5",
    )(u4, bcat, ccat, d, wab, a1, pj, aj)
    return out.reshape(nlb, bsz * seq, S5_LANES)


def _kv_kernel(m_ref, nw_ref, wk_ref, wv_ref, k_ref, v_ref):
    mn = _rms(m_ref[0], nw_ref[...]).astype(BF16)
    k_ref[0] = _dot(mn, wk_ref[...]).astype(BF16)
    v_ref[0] = _dot(mn, wv_ref[...]).astype(BF16)


def _kv(mem, norm_w, wk, wv):
    bsz, mlen, d = mem.shape
    blk = pl.BlockSpec((1, mlen, d), lambda b: (b, 0, 0))
    wspec = pl.BlockSpec((d, d), lambda b: (0, 0))
    return pl.pallas_call(
        _kv_kernel,
        grid=(bsz,),
        in_specs=[blk, pl.BlockSpec((1, d), lambda b: (0, 0)), wspec, wspec],
        out_specs=(blk, blk),
        out_shape=(jax.ShapeDtypeStruct(mem.shape, BF16),) * 2,
        compiler_params=pltpu.CompilerParams(
            dimension_semantics=("arbitrary",), vmem_limit_bytes=VMEM_SMALL),
        name="kv",
    )(mem, norm_w, wk, wv)


def _mix_attn_kernel(x_ref, yc_ref, ys_ref, wot_ref, wob_ref, nx_ref, wq_ref,
                     k_ref, v_ref, wo_ref, nf_ref, wr_ref, br_ref,
                     h_ref, xs_ref, gs_ref, ri_ref, cnt_ref,
                     h2s_ref, tri_ref, rho_ref):
    i = pl.program_id(0)
    tm = x_ref.shape[0]
    neg = -jnp.inf
    big = float(ROUTE_LANES)

    @pl.when(i == 0)
    def _():
        h2s_ref[...] = jnp.zeros(h2s_ref.shape, F32)
        r_i = lax.broadcasted_iota(jnp.int32, (tm, tm), 0)
        c_i = lax.broadcasted_iota(jnp.int32, (tm, tm), 1)
        tri_ref[...] = jnp.where(r_i > c_i, 1.0, 0.0).astype(BF16)
        rho_ref[...] = lax.broadcasted_iota(jnp.int32, (tm, TILE_ROWS), 1).astype(F32)

    ys_time = jnp.concatenate(
        [jnp.concatenate([ys_ref[lt, pl.ds(r, S5_STEPS, stride=SUBLANES), :]
                          for r in range(SUBLANES)], axis=0)
         for lt in range(D_SSM // LANES)], axis=1)
    h1 = (x_ref[...] + _dot(yc_ref[...].astype(BF16), wot_ref[...])
          + _dot(ys_time.astype(BF16), wob_ref[...]))
    hn = _rms(h1, nx_ref[...]).astype(BF16)
    q = _dot(hn, wq_ref[...])

    hf = _rms(h2s_ref[...], nf_ref[...]).astype(BF16)
    logits = _dot(hf, wr_ref[...]) + br_ref[...]
    lane = lax.broadcasted_iota(jnp.int32, (tm, ROUTE_LANES), 1)
    lane_f = lane.astype(F32)

    def top1(vals):
        m = jnp.max(vals, axis=-1, keepdims=True)
        idx = jnp.min(jnp.where(vals == m, lane_f, big), axis=-1, keepdims=True)
        return m, idx

    gl = jnp.where(lane < N_EXPERT_GROUPS, logits, neg)
    gmax, gidx = top1(gl)
    g_w = 1.0 / jnp.sum(jnp.exp(gl - gmax), axis=-1, keepdims=True)
    lo = EXPERT_LANE0 + EXPERTS_PER_GROUP * gidx
    el = jnp.where((lane_f >= lo) & (lane_f < lo + EXPERTS_PER_GROUP), logits, neg)
    m1, i1 = top1(el)
    m2, i2 = top1(jnp.where(lane_f == i1, neg, el))
    e21 = jnp.exp(m2 - m1)
    gate1 = g_w / (1.0 + e21)
    gate2 = g_w * e21 / (1.0 + e21)

    heads = []
    for hd in range(N_XHEADS):
        sl = slice(hd * XHEAD_DIM, (hd + 1) * XHEAD_DIM)
        qh = q[:, sl].astype(BF16)
        s = lax.dot_general(qh, k_ref[0, :, sl], (((1,), (1,)), ((), ())),
                            preferred_element_type=F32)
        s = s - jnp.max(s, axis=-1, keepdims=True)
        p = jnp.exp(s)
        p = p / jnp.sum(p, axis=-1, keepdims=True)
        heads.append(_dot(p.astype(BF16), v_ref[0, :, sl]).astype(BF16))
    o = jnp.concatenate(heads, axis=-1)

    hot1 = lane_f == i1
    hot2 = lane_f == i2
    hot = jnp.where(hot1 | hot2, 1.0, 0.0)
    before = _dot(tri_ref[...], hot.astype(BF16))
    count = jnp.sum(hot, axis=0, keepdims=True)
    chunks = jnp.floor((count + (MOE_CHUNK - 1.0)) * (1.0 / MOE_CHUNK))
    l_i = lax.broadcasted_iota(jnp.int32, (ROUTE_LANES, ROUTE_LANES), 0)
    l_j = lax.broadcasted_iota(jnp.int32, (ROUTE_LANES, ROUTE_LANES), 1)
    upper = jnp.where(l_i < l_j, 1.0, 0.0).astype(BF16)
    first_chunk = _dot(jnp.broadcast_to(chunks, (SUBLANES, ROUTE_LANES)).astype(BF16), upper)[0:1]
    start = first_chunk * float(MOE_CHUNK) + before
    pos1 = jnp.sum(jnp.where(hot1, start, 0.0), axis=-1, keepdims=True)
    pos2 = jnp.sum(jnp.where(hot2, start, 0.0), axis=-1, keepdims=True)
    cnt_ref[0] = jnp.broadcast_to(count, (SUBLANES, ROUTE_LANES)).astype(jnp.int32)
    ri_ref[...] = jnp.where(lane == 0, pos1, jnp.where(lane == 1, pos2, 0.0)).astype(jnp.int32)

    h2 = h1 + _dot(o, wo_ref[...])
    h_ref[...] = h2
    h2s_ref[...] = h2

    def pieces(g):
        hi = g.astype(BF16).astype(F32)
        mid = (g - hi).astype(BF16).astype(F32)
        low = (g - hi - mid).astype(BF16).astype(F32)
        return hi, mid, low

    g6 = jnp.where(lane == 6, 1.0, 0.0)
    for li, piece in enumerate(pieces(gate1) + pieces(gate2)):
        g6 = jnp.where(lane == li, piece, g6)
    rho = rho_ref[...]
    ptk = jnp.where(rho == pos1, 1.0, jnp.where(rho == pos2, 2.0, 0.0)).astype(BF16)
    rhs = jnp.concatenate([hf, g6.astype(BF16)], axis=1)
    res = lax.dot_general(ptk, rhs, (((0,), (0,)), ((), ())), preferred_element_type=F32)
    sg = res[:, D_MODEL:]
    which = sg[:, 6:7]
    srt = res[:, :D_MODEL] * jnp.where(which == 2.0, 0.5, 1.0)
    xs_ref[...] = _pack_words(srt).reshape(TILE_CHUNKS, MOE_CHUNK, D_PACK)
    first = sg[:, 0:1] + sg[:, 1:2] + sg[:, 2:3]
    second = 0.5 * (sg[:, 3:4] + sg[:, 4:5] + sg[:, 5:6])
    gsort = jnp.where(which == 1.0, first, jnp.where(which == 2.0, second, 0.0))
    gs_ref[...] = jnp.broadcast_to(gsort, (TILE_ROWS, ROUTE_LANES))


def _pack_words(v):
    hi = lax.bitcast_convert_type(v[:, :D_PACK], jnp.uint32) & jnp.uint32(HI_MASK)
    lo = lax.bitcast_convert_type(v[:, D_PACK:], jnp.uint32) >> 16
    return hi | lo


def _unpack_words(w):
    hi = lax.bitcast_convert_type(w & jnp.uint32(HI_MASK), F32)
    lo = lax.bitcast_convert_type(w << 16, F32)
    return hi, lo


def _mix_attn(xt, yc, ys, wot, wob, nx, wq, k, v, wo, nf, wr, br, seq):
    t, d = xt.shape
    nt = t // TM_MIX
    tiles_per_batch = seq // TM_MIX
    mlen = k.shape[1]
    att = lambda i: jnp.minimum(i, nt - 1)
    rte = lambda i: jnp.maximum(i - 1, 0)
    row_a = lambda i: (att(i), 0)
    row_r = lambda i: (rte(i), 0)
    const = lambda i: (0, 0)
    tile3 = lambda i: (rte(i), 0, 0)
    kvspec = pl.BlockSpec((1, mlen, d), lambda i: (att(i) // tiles_per_batch, 0, 0))
    packed = jax.ShapeDtypeStruct((nt * TILE_CHUNKS, MOE_CHUNK, D_PACK), jnp.uint32)
    return pl.pallas_call(
        _mix_attn_kernel,
        grid=(nt + 1,),
        in_specs=[
            pl.BlockSpec((TM_MIX, d), row_a),
            pl.BlockSpec((TM_MIX, D_CONV), row_a),
            pl.BlockSpec((D_SSM // LANES, TM_MIX, LANES), lambda i: (0, att(i), 0)),
            pl.BlockSpec((D_CONV, d), const),
            pl.BlockSpec((D_SSM, d), const),
            pl.BlockSpec((1, d), const),
            pl.BlockSpec((d, d), const),
            kvspec, kvspec,
            pl.BlockSpec((d, d), const),
            pl.BlockSpec((1, d), const),
            pl.BlockSpec((d, ROUTE_LANES), const),
            pl.BlockSpec((1, ROUTE_LANES), const),
        ],
        out_specs=(
            pl.BlockSpec((TM_MIX, d), row_a),
            pl.BlockSpec((TILE_CHUNKS, MOE_CHUNK, D_PACK), tile3),
            pl.BlockSpec((TILE_ROWS, ROUTE_LANES), row_r),
            pl.BlockSpec((TM_MIX, ROUTE_LANES), row_r),
            pl.BlockSpec((1, SUBLANES, ROUTE_LANES), tile3),
        ),
        out_shape=(
            jax.ShapeDtypeStruct((t, d), F32),
            packed,
            jax.ShapeDtypeStruct((nt * TILE_ROWS, ROUTE_LANES), F32),
            jax.ShapeDtypeStruct((t, ROUTE_LANES), jnp.int32),
            jax.ShapeDtypeStruct((nt, SUBLANES, ROUTE_LANES), jnp.int32),
        ),
        scratch_shapes=[
            pltpu.VMEM((TM_MIX, d), F32),
            pltpu.VMEM((TM_MIX, TM_MIX), BF16),
            pltpu.VMEM((TM_MIX, TILE_ROWS), F32),
        ],
        compiler_params=pltpu.CompilerParams(
            dimension_semantics=("arbitrary",), vmem_limit_bytes=VMEM_MIX),
        name="mix_attn",
    )(xt, yc, ys, wot, wob, nx, wq, k, v, wo, nf, wr, br)


ZERO_CHUNK = TILE_CHUNKS - 1
DUMP_TILE0 = 16
SPARE_PER_TILE = TILE_CHUNKS - TILE_USED_CHUNKS


def _dump_chunk(slot, k):
    idx = slot * BLOCK_CHUNKS + k
    return (DUMP_TILE0 + idx // SPARE_PER_TILE) * TILE_CHUNKS + TILE_USED_CHUNKS + idx % SPARE_PER_TILE


def _experts_kernel(src_ref, dst_ref, be_ref, seq_ref, nxt_ref, na_ref, xs_hbm, w1_hbm, w3_hbm,
                    w2_hbm, ys_hbm, xbuf, ybuf, w1f_ref, w3f_ref, w2f_ref,
                    w1s_ref, w3s_ref, w2s_ref, in_sem, out_sem, w_sem):
    b = pl.program_id(0)
    n_active = na_ref[0]
    slot = b % 2
    other = 1 - slot

    def gather(blk, sl):
        for k in range(BLOCK_CHUNKS):
            pltpu.make_async_copy(xs_hbm.at[src_ref[blk * BLOCK_CHUNKS + k]],
                                  xbuf.at[sl, k], in_sem.at[sl]).start()

    def gather_wait(sl):
        pltpu.make_async_copy(xs_hbm.at[pl.ds(0, BLOCK_CHUNKS)], xbuf.at[sl],
                              in_sem.at[sl]).wait()

    def scatter(blk, sl):
        for k in range(BLOCK_CHUNKS):
            d = dst_ref[blk * BLOCK_CHUNKS + k]
            dump = jnp.where(sl == 0, _dump_chunk(0, k), _dump_chunk(1, k))
            pltpu.make_async_copy(ybuf.at[sl, k], ys_hbm.at[jnp.where(d < 0, dump, d)],
                                  out_sem.at[sl]).start()

    def scatter_wait(sl):
        pltpu.make_async_copy(ybuf.at[sl], ys_hbm.at[pl.ds(0, BLOCK_CHUNKS)],
                              out_sem.at[sl]).wait()

    active = b < n_active

    @pl.when(b == 0)
    def _():
        gather(0, 0)
        ybuf[...] = jnp.zeros(ybuf.shape, jnp.uint32)

    prev = be_ref[jnp.maximum(b - 1, 0)]
    fresh = (b == 0) | (be_ref[b] != prev)

    def weight_copies(e, ws):
        return [pltpu.make_async_copy(hbm.at[e], buf.at[ws], w_sem.at[ws])
                for hbm, buf in ((w1_hbm, w1f_ref), (w3_hbm, w3f_ref), (w2_hbm, w2f_ref))]

    @pl.when(active & fresh)
    def _():
        ws = seq_ref[b] % 2

        @pl.when(b == 0)
        def _():
            for cp in weight_copies(be_ref[b], ws):
                cp.start(priority=1)

        for cp in weight_copies(be_ref[b], ws):
            cp.wait()

        @pl.when(nxt_ref[b] >= 0)
        def _():
            for cp in weight_copies(nxt_ref[b], 1 - ws):
                cp.start(priority=1)

        w1s_ref[...] = w1f_ref[ws].astype(BF16)
        w3s_ref[...] = w3f_ref[ws].astype(BF16)
        w2s_ref[...] = w2f_ref[ws].astype(BF16)

    @pl.when(active & (b >= 1))
    def _():
        scatter_wait(slot)

    @pl.when(active)
    def _():
        gather_wait(slot)
        gather(b + 1, other)
        scatter(b, other)

        hi, lo = _unpack_words(xbuf[slot].reshape(MOE_BLOCK, D_PACK))
        xb = jnp.concatenate([hi, lo], axis=1).astype(BF16)
        h1 = _dot(xb, w1s_ref[...])
        h3 = _dot(xb, w3s_ref[...])
        hid = (h1 * jax.nn.sigmoid(h1) * h3).astype(BF16)
        y = _dot(hid, w2s_ref[...]).astype(BF16).astype(F32)
        ybuf[slot] = _pack_words(y).reshape(BLOCK_CHUNKS, MOE_CHUNK, D_PACK)

        @pl.when(b == n_active - 1)
        def _():
            scatter(b + 1, slot)
            gather_wait(other)
            scatter_wait(other)
            scatter_wait(slot)


def _experts(src, dst, block_expert, expert_seq, next_expert, n_active, xs, w1, w3, w2):
    nb = block_expert.shape[0]
    d = D_MODEL
    hbm = pl.BlockSpec(memory_space=pl.ANY)
    grid_spec = pltpu.PrefetchScalarGridSpec(
        num_scalar_prefetch=6,
        grid=(nb,),
        in_specs=[hbm, hbm, hbm, hbm],
        out_specs=hbm,
        scratch_shapes=[
            pltpu.VMEM((2, BLOCK_CHUNKS, MOE_CHUNK, D_PACK), jnp.uint32),
            pltpu.VMEM((2, BLOCK_CHUNKS, MOE_CHUNK, D_PACK), jnp.uint32),
            pltpu.VMEM((2, d, D_EXPERT), F32),
            pltpu.VMEM((2, d, D_EXPERT), F32),
            pltpu.VMEM((2, D_EXPERT, d), F32),
            pltpu.VMEM((d, D_EXPERT), BF16),
            pltpu.VMEM((d, D_EXPERT), BF16),
            pltpu.VMEM((D_EXPERT, d), BF16),
            pltpu.SemaphoreType.DMA((2,)),
            pltpu.SemaphoreType.DMA((2,)),
            pltpu.SemaphoreType.DMA((2,)),
        ],
    )
    return pl.pallas_call(
        _experts_kernel,
        grid_spec=grid_spec,
        out_shape=jax.ShapeDtypeStruct(xs.shape, jnp.uint32),
        input_output_aliases={6: 0},
        compiler_params=pltpu.CompilerParams(
            dimension_semantics=("arbitrary",), vmem_limit_bytes=VMEM_EXPERTS),
        name="experts",
    )(src, dst, block_expert, expert_seq, next_expert, n_active, xs, w1, w3, w2)


def _combine_kernel(h_ref, ri_ref, gs_ref, ys_ref, fw_ref, o_ref):
    tm = h_ref.shape[0]
    hi, lo = _unpack_words(ys_ref[...].reshape(TILE_ROWS, D_PACK))
    g = gs_ref[...]
    gw = jnp.concatenate([g] * (D_PACK // ROUTE_LANES), axis=1)
    yg = jnp.concatenate([hi * gw, lo * gw], axis=1).astype(BF16)
    pos = ri_ref[...].astype(F32)
    rho = lax.broadcasted_iota(jnp.int32, (tm, TILE_ROWS), 1).astype(F32)
    q = jnp.where((rho == pos[:, 0:1]) | (rho == pos[:, 1:2]), 1.0, 0.0).astype(BF16)
    o_ref[...] = _rms(h_ref[...] + _dot(q, yg), fw_ref[...])


def _combine(h2, route_i, gsort, ys, final_w):
    t, d = h2.shape
    nt = t // TM_MIX
    row = lambda i: (i, 0)
    return pl.pallas_call(
        _combine_kernel,
        grid=(nt,),
        in_specs=[
            pl.BlockSpec((TM_MIX, d), row),
            pl.BlockSpec((TM_MIX, ROUTE_LANES), row),
            pl.BlockSpec((TILE_ROWS, ROUTE_LANES), row),
            pl.BlockSpec((TILE_CHUNKS, MOE_CHUNK, D_PACK), lambda i: (i, 0, 0)),
            pl.BlockSpec((1, d), lambda i: (0, 0)),
        ],
        out_specs=pl.BlockSpec((TM_MIX, d), row),
        out_shape=jax.ShapeDtypeStruct((t, d), F32),
        compiler_params=pltpu.CompilerParams(
            dimension_semantics=("arbitrary",), vmem_limit_bytes=VMEM_EXPERTS),
        name="combine",
    )(h2, route_i, gsort, ys, final_w)


def _routing_tables(cnt):
    nt = cnt.shape[0]
    n = cnt[:, 0, EXPERT_LANE0:EXPERT_LANE0 + N_EXPERTS]
    c = (n + MOE_CHUNK - 1) // MOE_CHUNK
    local = jnp.cumsum(c, axis=1) - c
    per_expert = jnp.sum(c, axis=0)
    padded = (per_expert + BLOCK_CHUNKS - 1) // BLOCK_CHUNKS * BLOCK_CHUNKS
    gend = jnp.cumsum(padded)
    gstart = gend - padded
    within = jnp.cumsum(c, axis=0) - c
    g_max = (2 * nt * TM_MIX // MOE_CHUNK + nt * N_EXPERTS
             + N_EXPERTS * (BLOCK_CHUNKS - 1))
    n_blocks = -(-g_max // BLOCK_CHUNKS)
    g = jnp.arange(n_blocks * BLOCK_CHUNKS, dtype=jnp.int32)
    e_of = jnp.minimum(jnp.sum((gend[None, :] <= g[:, None]).astype(jnp.int32), axis=1),
                       N_EXPERTS - 1)
    hot_e = (e_of[:, None] == jnp.arange(N_EXPERTS, dtype=jnp.int32)[None, :]).astype(F32)
    pick = lambda table: jnp.dot(hot_e, table.T.astype(F32), precision=lax.Precision.HIGHEST)
    rel = g.astype(F32) - pick(gstart[None, :])[:, 0]
    w_rows = pick(within)
    tile = jnp.sum((w_rows <= rel[:, None]).astype(jnp.int32), axis=1) - 1
    hot_t = tile[:, None] == jnp.arange(nt, dtype=jnp.int32)[None, :]
    at_tile = lambda rows: jnp.sum(jnp.where(hot_t, rows, 0.0), axis=1)
    k = rel - at_tile(w_rows)
    valid = k < at_tile(pick(c))
    chunk = (tile * TILE_CHUNKS + (at_tile(pick(local)) + k).astype(jnp.int32))
    pad_src = jnp.full((BLOCK_CHUNKS,), ZERO_CHUNK, jnp.int32)
    pad_dst = jnp.full((BLOCK_CHUNKS,), -1, jnp.int32)
    src = jnp.concatenate([jnp.where(valid, chunk, ZERO_CHUNK).astype(jnp.int32), pad_src])
    dst = jnp.concatenate([pad_dst, jnp.where(valid, chunk, -1).astype(jnp.int32)])
    block_start = jnp.arange(n_blocks, dtype=jnp.int32) * BLOCK_CHUNKS
    block_expert = jnp.minimum(
        jnp.sum((block_start[:, None] >= gend[None, :]).astype(jnp.int32), axis=1),
        N_EXPERTS - 1).astype(jnp.int32)
    n_active = (gend[-1:] // BLOCK_CHUNKS).astype(jnp.int32)
    blk = jnp.arange(n_blocks, dtype=jnp.int32)
    is_first = (blk < n_active[0]) & ((blk == 0) | (block_expert != jnp.roll(block_expert, 1)))
    expert_seq = (jnp.cumsum(is_first.astype(jnp.int32)) - 1).astype(jnp.int32)
    later_first = is_first[None, :] & (blk[None, :] > blk[:, None])
    next_blk = jnp.min(jnp.where(later_first, blk[None, :], n_blocks), axis=1)
    next_expert = jnp.where(
        next_blk < n_blocks,
        jnp.sum(jnp.where(blk[None, :] == next_blk[:, None], block_expert[None, :], 0), axis=1),
        -1).astype(jnp.int32)
    return src, dst, block_expert, expert_seq, next_expert, n_active


def _layer(h, mem, p, final_w):
    bsz, seq, d = h.shape
    t = bsz * seq
    assert t // TM_MIX >= DUMP_TILE0 + 2 * BLOCK_CHUNKS // SPARE_PER_TILE
    xt = h.reshape(t, d)

    v, u = _in_proj(xt, p["norm_mix_w"].reshape(1, d), p["w_in"].astype(BF16))
    y_conv = _conv(v, p["conv_w"], p["conv_b"].reshape(1, D_CONV),
                   p["conv_ln_w"].reshape(1, D_CONV), p["conv_ln_b"].reshape(1, D_CONV),
                   bsz, seq)

    a1, pj, aj, bbar = _s5_prep(p["ssm_A_re"], p["ssm_A_im"], p["ssm_log_dt"],
                                      p["ssm_B_re"], p["ssm_B_im"])
    nlb = D_SSM // S5_LANES
    bb = bbar.reshape(2, SSM_GROUP, nlb, S5_GROUPS, SSM_STATE).transpose(0, 2, 3, 1, 4)
    bcat = jnp.concatenate([_block_diag(bb[0]), _block_diag(bb[1])], axis=-1).astype(BF16)
    c_re = p["ssm_C_re"].reshape(nlb, S5_GROUPS, SSM_GROUP, SSM_STATE).transpose(0, 1, 3, 2)
    c_im = p["ssm_C_im"].reshape(nlb, S5_GROUPS, SSM_GROUP, SSM_STATE).transpose(0, 1, 3, 2)
    ccat = jnp.concatenate([_block_diag(c_re), -_block_diag(c_im)], axis=1).astype(BF16)
    glu = p["ssm_glu_w"].reshape(nlb, S5_GROUPS, SSM_GROUP, 2 * SSM_GROUP)
    wab = jnp.concatenate([_block_diag(glu[..., :SSM_GROUP]),
                           _block_diag(glu[..., SSM_GROUP:])], axis=-1).astype(BF16)
    y_ssm = _s5(u, bcat, ccat, p["ssm_D"].reshape(1, D_SSM), wab,
                a1, pj, aj, bsz, seq)

    k, vv = _kv(mem, p["norm_mem_w"].reshape(1, d), p["xk_w"].astype(BF16),
                p["xv_w"].astype(BF16))
    w_out = p["w_out"].astype(BF16)
    wr = jnp.concatenate([p["router_group_w"], p["router_expert_w"]], axis=1)
    wr = jnp.pad(wr, ((0, 0), (0, ROUTE_LANES - wr.shape[1]))).astype(BF16)
    br = jnp.concatenate([p["router_group_b"], p["router_expert_b"].reshape(-1)])
    br = jnp.pad(br, (0, ROUTE_LANES - br.shape[0])).reshape(1, ROUTE_LANES)
    h2, xs, gsort, route_i, cnt = _mix_attn(
        xt, y_conv, y_ssm, w_out[:D_CONV], w_out[D_CONV:], p["norm_x_w"].reshape(1, d),
        (p["xq_w"] * (XHEAD_DIM ** -0.5)).astype(BF16), k, vv, p["xo_w"].astype(BF16),
        p["norm_ffn_w"].reshape(1, d), wr, br, seq)

    src, dst, block_expert, expert_seq, next_expert, n_active = _routing_tables(cnt)
    ys = _experts(src, dst, block_expert, expert_seq, next_expert, n_active, xs,
                  p["moe_w1"], p["moe_w3"], p["moe_w2"])
    out = _combine(h2, route_i, gsort, ys, final_w.reshape(1, d))
    return out.reshape(bsz, seq, d)


def kernel(x, mem, norm_mix_w, w_in, conv_w, conv_b, conv_ln_w, conv_ln_b, ssm_A_re, ssm_A_im, ssm_log_dt, ssm_B_re, ssm_B_im, ssm_C_re, ssm_C_im, ssm_D, ssm_glu_w, w_out, norm_x_w, norm_mem_w, xq_w, xk_w, xv_w, xo_w, norm_ffn_w, router_group_w, router_group_b, router_expert_w, router_expert_b, moe_w1, moe_w3, moe_w2, final_norm_w):
    stacked = dict(
        norm_mix_w=norm_mix_w, w_in=w_in, conv_w=conv_w, conv_b=conv_b,
        conv_ln_w=conv_ln_w, conv_ln_b=conv_ln_b, ssm_A_re=ssm_A_re, ssm_A_im=ssm_A_im,
        ssm_log_dt=ssm_log_dt, ssm_B_re=ssm_B_re, ssm_B_im=ssm_B_im, ssm_C_re=ssm_C_re,
        ssm_C_im=ssm_C_im, ssm_D=ssm_D, ssm_glu_w=ssm_glu_w, w_out=w_out,
        norm_x_w=norm_x_w, norm_mem_w=norm_mem_w, xq_w=xq_w, xk_w=xk_w, xv_w=xv_w,
        xo_w=xo_w, norm_ffn_w=norm_ffn_w, router_group_w=router_group_w,
        router_group_b=router_group_b, router_expert_w=router_expert_w,
        router_expert_b=router_expert_b, moe_w1=moe_w1, moe_w3=moe_w3, moe_w2=moe_w2)
    depth = norm_mix_w.shape[0]
    assert depth == 1, "final norm is fused into the single layer's combine step"
    layer = {name: w[0] for name, w in stacked.items()}
    return _layer(x, mem, layer, final_norm_w)
```

```python
import functools

import jax
import jax.numpy as jnp
from jax import lax
from jax.experimental import pallas as pl
from jax.experimental.pallas import tpu as pltpu

D_MODEL = 1024
D_CONV = 512
CONV_WIDTH = 31
D_SSM = 512
SSM_GROUP = 16
N_SSM_GROUPS = 32
SSM_STATE = 64
N_XHEADS = 4
XHEAD_DIM = 256
N_EXPERT_GROUPS = 4
EXPERTS_PER_GROUP = 8
N_EXPERTS = 32
D_EXPERT = 512
EPS = 1e-6

F32 = jnp.float32
BF16 = jnp.bfloat16

SUBLANES = 8
LANES = 128

TM_PROJ = 1024
TM_CONV = 1024
CONV_CHUNK = 128
CONV_NORM_ROWS = 128
CONV_HALO = 32
S5_STEPS = 64
S5_TILE = SUBLANES * S5_STEPS
S5_GROUP_ROWS = 256
S5_LANES = 128
S5_GROUPS = S5_LANES // SSM_GROUP
S5_STATE = S5_GROUPS * SSM_STATE
TM_MIX = 512
ROUTE_LANES = 128
EXPERT_LANE0 = N_EXPERT_GROUPS
MOE_CHUNK = SUBLANES
MOE_BLOCK = 256
BLOCK_CHUNKS = MOE_BLOCK // MOE_CHUNK
TILE_USED_CHUNKS = 2 * TM_MIX // MOE_CHUNK + N_EXPERTS * (MOE_CHUNK - 1) // MOE_CHUNK
TILE_CHUNKS = 160
TILE_ROWS = TILE_CHUNKS * MOE_CHUNK
D_PACK = D_MODEL // 2
HI_MASK = 0xFFFF0000

VMEM_SMALL = 40 << 20
VMEM_EXPERTS = 48 << 20
VMEM_MIX = 62 << 20


def _rms(x, w):
    return x * lax.rsqrt(jnp.mean(x * x, axis=-1, keepdims=True) + EPS) * w


def _dot(a, b):
    return jnp.dot(a, b, preferred_element_type=F32)


def _s5_prep_kernel(are_ref, aim_ref, ldt_ref, btre_ref, btim_ref,
                    a1_ref, pj_ref, aj_ref, bbar_ref):
    lam_re = are_ref[...]
    lam_im = aim_ref[...]
    dt = jnp.exp(ldt_ref[...])
    x = lam_re * dt
    y = lam_im * dt

    def power(k):
        mag = jnp.exp(k * x)
        return mag * jnp.cos(k * y), mag * jnp.sin(k * y)

    n = x.shape[-1]
    ones8 = jnp.ones((SUBLANES, n), F32)
    a_re, a_im = power(ones8)
    a1_ref[0] = a_re
    a1_ref[1] = a_im

    row = lax.broadcasted_iota(jnp.int32, (SUBLANES, n), 0)
    for i, d in enumerate((1, 2, 4)):
        p_re, p_im = power(ones8 * float(d * S5_STEPS))
        keep = row >= d
        pj_ref[0, i] = jnp.where(keep, p_re, 0.0)
        pj_ref[1, i] = jnp.where(keep, p_im, 0.0)
    j_re, j_im = power(ones8 * float(S5_STEPS))
    aj_ref[0] = j_re
    aj_ref[1] = j_im

    num_re = a_re[0:1] - 1.0
    num_im = a_im[0:1]
    den = lam_re * lam_re + lam_im * lam_im
    c_re = (num_re * lam_re + num_im * lam_im) / den
    c_im = (num_im * lam_re - num_re * lam_im) / den
    b_re = btre_ref[...]
    b_im = btim_ref[...]
    bbar_ref[0] = c_re * b_re - c_im * b_im
    bbar_ref[1] = c_re * b_im + c_im * b_re


def _s5_prep(a_re, a_im, log_dt, b_re, b_im):
    n = N_SSM_GROUPS * SSM_STATE
    are = a_re.reshape(1, n)
    aim = a_im.reshape(1, n)
    ldt = jnp.repeat(log_dt, SSM_STATE).reshape(1, n)
    btre = jnp.transpose(b_re, (2, 0, 1)).reshape(SSM_GROUP, n)
    btim = jnp.transpose(b_im, (2, 0, 1)).reshape(SSM_GROUP, n)
    return pl.pallas_call(
        _s5_prep_kernel,
        out_shape=(
            jax.ShapeDtypeStruct((2, SUBLANES, n), F32),
            jax.ShapeDtypeStruct((2, 3, SUBLANES, n), F32),
            jax.ShapeDtypeStruct((2, SUBLANES, n), F32),
            jax.ShapeDtypeStruct((2, SSM_GROUP, n), F32),
        ),
        name="s5_prep",
    )(are, aim, ldt, btre, btim)


def _block_diag(w):
    nl, g, r, c = w.shape
    eye = jnp.eye(g, dtype=w.dtype)
    return jnp.einsum("lgrc,gh->lgrhc", w, eye).reshape(nl, g * r, g * c)


def _in_proj_kernel(x_ref, nw_ref, w_ref, v_ref, u_ref):
    xn = _rms(x_ref[...], nw_ref[...]).astype(BF16)
    proj = _dot(xn, w_ref[...])
    a = proj[:, :D_CONV]
    g = proj[:, D_CONV:2 * D_CONV]
    v_ref[...] = a * jax.nn.sigmoid(g)
    for lt in range(D_SSM // LANES):
        c0 = 2 * D_CONV + lt * LANES
        for tile in range(TM_PROJ // S5_TILE):
            for r in range(SUBLANES):
                t0 = tile * S5_TILE + r * S5_STEPS
                u_ref[lt, pl.ds(tile * S5_TILE + r, S5_STEPS, stride=SUBLANES), :] = (
                    proj[t0:t0 + S5_STEPS, c0:c0 + LANES])


def _in_proj(xt, norm_w, w_in):
    t = xt.shape[0]
    n_out = 2 * D_CONV + D_SSM
    return pl.pallas_call(
        _in_proj_kernel,
        grid=(t // TM_PROJ,),
        in_specs=[
            pl.BlockSpec((TM_PROJ, D_MODEL), lambda i: (i, 0)),
            pl.BlockSpec((1, D_MODEL), lambda i: (0, 0)),
            pl.BlockSpec((D_MODEL, n_out), lambda i: (0, 0)),
        ],
        out_specs=(
            pl.BlockSpec((TM_PROJ, D_CONV), lambda i: (i, 0)),
            pl.BlockSpec((D_SSM // LANES, TM_PROJ, LANES), lambda i: (0, i, 0)),
        ),
        out_shape=(
            jax.ShapeDtypeStruct((t, D_CONV), F32),
            jax.ShapeDtypeStruct((D_SSM // LANES, t, LANES), F32),
        ),
        compiler_params=pltpu.CompilerParams(
            dimension_semantics=("arbitrary",), vmem_limit_bytes=VMEM_SMALL),
        name="in_proj",
    )(xt, norm_w, w_in)


def _conv_kernel(v_ref, w_ref, b_ref, lnw_ref, lnb_ref, o_ref, ext_ref, sh_ref):
    tt = pl.program_id(1)
    rows = CONV_HALO + TM_CONV
    n_lt = D_CONV // LANES

    @pl.when(tt == 0)
    def _():
        ext_ref[:, pl.ds(0, CONV_HALO), :] = jnp.zeros((n_lt, CONV_HALO, LANES), F32)

    @pl.when(tt > 0)
    def _():
        ext_ref[:, pl.ds(0, CONV_HALO), :] = ext_ref[:, pl.ds(TM_CONV, CONV_HALO), :]

    for lt in range(n_lt):
        ext_ref[lt, pl.ds(CONV_HALO, TM_CONV), :] = v_ref[:, pl.ds(lt * LANES, LANES)]
    for s in range(1, SUBLANES):
        sh_ref[s - 1, :, pl.ds(0, rows - SUBLANES), :] = ext_ref[:, pl.ds(s, rows - SUBLANES), :]
    bias = b_ref[...]
    lnw = lnw_ref[...]
    lnb = lnb_ref[...]
    tap0 = CONV_HALO - (CONV_WIDTH - 1)

    groups = CONV_CHUNK // SUBLANES

    def chunk(ci, carry):
        base = pl.multiple_of(ci * CONV_CHUNK, CONV_CHUNK)
        for lt in range(n_lt):
            lanes = pl.ds(lt * LANES, LANES)
            acc = [jnp.broadcast_to(bias[:, lt * LANES:(lt + 1) * LANES], (SUBLANES, LANES))] * groups
            for s in range(SUBLANES):
                taps = [j for j in range(CONV_WIDTH) if (tap0 + j) % SUBLANES == s]
                src = ext_ref.at[lt] if s == 0 else sh_ref.at[s - 1, lt]
                ngroups = (tap0 + taps[-1] - s) // SUBLANES + groups
                win = [src[pl.ds(base + SUBLANES * g, SUBLANES), :] for g in range(ngroups)]
                for j in taps:
                    g0 = (tap0 + j - s) // SUBLANES
                    wj = w_ref[j, :, lanes]
                    acc = [acc[r] + wj * win[g0 + r] for r in range(groups)]
            o_ref[pl.ds(base, CONV_CHUNK), lanes] = jnp.concatenate(acc, axis=0)
        return carry

    lax.fori_loop(0, TM_CONV // CONV_CHUNK, chunk, 0)

    for bi in range(TM_CONV // CONV_NORM_ROWS):
        rows_b = pl.ds(bi * CONV_NORM_ROWS, CONV_NORM_ROWS)
        acc = o_ref[rows_b, :]
        mu = jnp.mean(acc, axis=-1, keepdims=True)
        cen = acc - mu
        var = jnp.mean(cen * cen, axis=-1, keepdims=True)
        z = cen * lax.rsqrt(var + EPS) * lnw + lnb
        o_ref[rows_b, :] = z * jax.nn.sigmoid(z)


def _conv(v, conv_w, conv_b, ln_w, ln_b, bsz, seq):
    nt = seq // TM_CONV
    row = lambda b, t: (b * nt + t, 0)
    const = lambda b, t: (0, 0)
    return pl.pallas_call(
        _conv_kernel,
        grid=(bsz, nt),
        in_specs=[
            pl.BlockSpec((TM_CONV, D_CONV), row),
            pl.BlockSpec((CONV_WIDTH, SUBLANES, D_CONV), lambda b, t: (0, 0, 0)),
            pl.BlockSpec((1, D_CONV), const),
            pl.BlockSpec((1, D_CONV), const),
            pl.BlockSpec((1, D_CONV), const),
        ],
        out_specs=pl.BlockSpec((TM_CONV, D_CONV), row),
        out_shape=jax.ShapeDtypeStruct(v.shape, F32),
        scratch_shapes=[
            pltpu.VMEM((D_CONV // LANES, CONV_HALO + TM_CONV, LANES), F32),
            pltpu.VMEM((SUBLANES - 1, D_CONV // LANES, CONV_HALO + TM_CONV, LANES), F32),
        ],
        compiler_params=pltpu.CompilerParams(
            dimension_semantics=("arbitrary", "arbitrary")),
        name="conv",
    )(v, jnp.broadcast_to(conv_w[:, None, :], (CONV_WIDTH, SUBLANES, D_CONV)), conv_b, ln_w, ln_b)


def _cmul(a_re, a_im, b_re, b_im):
    return a_re * b_re - a_im * b_im, a_re * b_im + a_im * b_re


def _s5_kernel(u_ref, bcat_ref, ccat_ref, d_ref, wab_ref, a1_ref,
               pj_ref, aj_ref, o_ref, up_ref, bu_ref, st_ref, carry_ref):
    tt = pl.program_id(1)
    ns = S5_STATE
    nseq = u_ref.shape[0]
    steps_per_group = S5_GROUP_ROWS // SUBLANES
    n_groups = S5_TILE // S5_GROUP_ROWS

    @pl.when((pl.program_id(0) == 0) & (tt == 0))
    def _():
        up_ref[...] = jnp.zeros(up_ref.shape, F32)
        st_ref[...] = jnp.zeros(st_ref.shape, BF16)

    @pl.when(tt == 0)
    def _():
        carry_ref[...] = jnp.zeros(carry_ref.shape, F32)

    def scan_and_project(cur, prv):
        up_ref[cur] = u_ref[...]

        a_re = a1_ref[0]
        a_im = a1_ref[1]

        def step(q, j, s):
            rows = pl.ds(j * SUBLANES, SUBLANES)
            m_re, m_im = _cmul(a_re, a_im, s[0], s[1])
            return m_re + bu_ref[q, rows, pl.ds(0, ns)], m_im + bu_ref[q, rows, pl.ds(ns, ns)]

        def project_out(q, g):
            rows = pl.ds(g * S5_GROUP_ROWS, S5_GROUP_ROWS)
            y = _dot(st_ref[prv, q, rows, :], ccat_ref[0]) + d_ref[...] * up_ref[prv, q, rows, :]
            y = jax.nn.gelu(y)
            ab = _dot(y.astype(BF16), wab_ref[0])
            o_ref[q, rows, :] = ab[:, :S5_LANES] * jax.nn.sigmoid(ab[:, S5_LANES:])

        zero = jnp.zeros((SUBLANES, ns), F32)
        state = [(zero, zero)] * nseq
        for g in range(n_groups):
            rows = pl.ds(g * S5_GROUP_ROWS, S5_GROUP_ROWS)
            for q in range(nseq):
                bu_ref[q, rows, :] = _dot(up_ref[cur, q, rows, :].astype(BF16), bcat_ref[0])
            for jj in range(steps_per_group):
                state = [step(q, g * steps_per_group + jj, state[q]) for q in range(nseq)]
            for q in range(nseq):
                project_out(q, g)

        row = lax.broadcasted_iota(jnp.int32, (SUBLANES, ns), 0)
        first = row == 0
        entry = []
        for q in range(nseq):
            e_re, e_im = state[q]
            c_re = jnp.where(first, pltpu.roll(carry_ref[q, 0], 1, 0), pltpu.roll(e_re, 1, 0))
            c_im = jnp.where(first, pltpu.roll(carry_ref[q, 1], 1, 0), pltpu.roll(e_im, 1, 0))
            for i, d in enumerate((1, 2, 4)):
                r_re = pltpu.roll(c_re, d, 0)
                r_im = pltpu.roll(c_im, d, 0)
                m_re, m_im = _cmul(pj_ref[0, i], pj_ref[1, i], r_re, r_im)
                c_re = c_re + m_re
                c_im = c_im + m_im
            f_re, f_im = _cmul(aj_ref[0], aj_ref[1], c_re, c_im)
            carry_ref[q, 0] = f_re + e_re
            carry_ref[q, 1] = f_im + e_im
            entry.append((c_re, c_im))

        state = entry
        pack = 2 * SUBLANES
        for j in range(0, S5_STEPS, 2):
            mid = [step(q, j, state[q]) for q in range(nseq)]
            state = [step(q, j + 1, mid[q]) for q in range(nseq)]
            for q in range(nseq):
                st_ref[cur, q, pl.ds(j * SUBLANES, pack), pl.ds(0, ns)] = (
                    jnp.concatenate([mid[q][0], state[q][0]], axis=0).astype(BF16))
                st_ref[cur, q, pl.ds(j * SUBLANES, pack), pl.ds(ns, ns)] = (
                    jnp.concatenate([mid[q][1], state[q][1]], axis=0).astype(BF16))

    for parity in range(2):
        pl.when(tt % 2 == parity)(functools.partial(scan_and_project, parity, 1 - parity))


def _s5(u, bcat, ccat, d, wab, a1, pj, aj, bsz, seq):
    nt = seq // S5_TILE
    nlb = D_SSM // S5_LANES
    ns = S5_STATE
    assert S5_LANES == LANES and S5_TILE == TM_MIX and TM_PROJ % S5_TILE == 0
    u4 = u.reshape(nlb, bsz, seq, S5_LANES)
    row_in = lambda l, t: (l, 0, jnp.minimum(t, nt - 1), 0)
    row_out = lambda l, t: (l, 0, jnp.maximum(t - 1, 0), 0)
    lane3 = lambda l, t: (0, 0, l)
    lane4 = lambda l, t: (0, 0, 0, l)
    out = pl.pallas_call(
        _s5_kernel,
        grid=(nlb, nt + 1),
        in_specs=[
            pl.BlockSpec((None, bsz, S5_TILE, S5_LANES), row_in),
            pl.BlockSpec((1, S5_LANES, 2 * ns), lambda l, t: (l, 0, 0)),
            pl.BlockSpec((1, 2 * ns, S5_LANES), lambda l, t: (l, 0, 0)),
            pl.BlockSpec((1, S5_LANES), lambda l, t: (0, l)),
            pl.BlockSpec((1, S5_LANES, 2 * S5_LANES), lambda l, t: (l, 0, 0)),
            pl.BlockSpec((2, SUBLANES, ns), lane3),
            pl.BlockSpec((2, 3, SUBLANES, ns), lane4),
            pl.BlockSpec((2, SUBLANES, ns), lane3),
        ],
        out_specs=pl.BlockSpec((None, bsz, S5_TILE, S5_LANES), row_out),
        out_shape=jax.ShapeDtypeStruct(u4.shape, F32),
        scratch_shapes=[
            pltpu.VMEM((2, bsz, S5_TILE, S5_LANES), F32),
            pltpu.VMEM((bsz, S5_TILE, 2 * ns), F32),
            pltpu.VMEM((2, bsz, S5_TILE, 2 * ns), BF16),
            pltpu.VMEM((bsz, 2, SUBLANES, ns), F32),
        ],
        compiler_params=pltpu.CompilerParams(
            dimension_semantics=("arbitrary", "arbitrary"),
            vmem_limit_bytes=VMEM_SMALL),
        name="s5",
    )(u4, bcat, ccat, d, wab, a1, pj, aj)
    return out.reshape(nlb, bsz * seq, S5_LANES)


def _kv_kernel(m_ref, nw_ref, wk_ref, wv_ref, k_ref, v_ref):
    mn = _rms(m_ref[0], nw_ref[...]).astype(BF16)
    k_ref[0] = _dot(mn, wk_ref[...]).astype(BF16)
    v_ref[0] = _dot(mn, wv_ref[...]).astype(BF16)


def _kv(mem, norm_w, wk, wv):
    bsz, mlen, d = mem.shape
    blk = pl.BlockSpec((1, mlen, d), lambda b: (b, 0, 0))
    wspec = pl.BlockSpec((d, d), lambda b: (0, 0))
    return pl.pallas_call(
        _kv_kernel,
        grid=(bsz,),
        in_specs=[blk, pl.BlockSpec((1, d), lambda b: (0, 0)), wspec, wspec],
        out_specs=(blk, blk),
        out_shape=(jax.ShapeDtypeStruct(mem.shape, BF16),) * 2,
        compiler_params=pltpu.CompilerParams(
            dimension_semantics=("arbitrary",), vmem_limit_bytes=VMEM_SMALL),
        name="kv",
    )(mem, norm_w, wk, wv)


def _mix_attn_kernel(x_ref, yc_ref, ys_ref, wot_ref, wob_ref, nx_ref, wq_ref,
                     k_ref, v_ref, wo_ref, nf_ref, wr_ref, br_ref,
                     h_ref, xs_ref, gs_ref, ri_ref, cnt_ref,
                     h2s_ref, tri_ref, rho_ref):
    i = pl.program_id(0)
    tm = x_ref.shape[0]
    neg = -jnp.inf
    big = float(ROUTE_LANES)

    @pl.when(i == 0)
    def _():
        h2s_ref[...] = jnp.zeros(h2s_ref.shape, F32)
        r_i = lax.broadcasted_iota(jnp.int32, (tm, tm), 0)
        c_i = lax.broadcasted_iota(jnp.int32, (tm, tm), 1)
        tri_ref[...] = jnp.where(r_i > c_i, 1.0, 0.0).astype(BF16)
        rho_ref[...] = lax.broadcasted_iota(jnp.int32, (tm, TILE_ROWS), 1).astype(F32)

    ys_time = jnp.concatenate(
        [jnp.concatenate([ys_ref[lt, pl.ds(r, S5_STEPS, stride=SUBLANES), :]
                          for r in range(SUBLANES)], axis=0)
         for lt in range(D_SSM // LANES)], axis=1)
    h1 = (x_ref[...] + _dot(yc_ref[...].astype(BF16), wot_ref[...])
          + _dot(ys_time.astype(BF16), wob_ref[...]))
    hn = _rms(h1, nx_ref[...]).astype(BF16)
    q = _dot(hn, wq_ref[...])

    hf = _rms(h2s_ref[...], nf_ref[...]).astype(BF16)
    logits = _dot(hf, wr_ref[...]) + br_ref[...]
    lane = lax.broadcasted_iota(jnp.int32, (tm, ROUTE_LANES), 1)
    lane_f = lane.astype(F32)

    def top1(vals):
        m = jnp.max(vals, axis=-1, keepdims=True)
        idx = jnp.min(jnp.where(vals == m, lane_f, big), axis=-1, keepdims=True)
        return m, idx

    gl = jnp.where(lane < N_EXPERT_GROUPS, logits, neg)
    gmax, gidx = top1(gl)
    g_w = 1.0 / jnp.sum(jnp.exp(gl - gmax), axis=-1, keepdims=True)
    lo = EXPERT_LANE0 + EXPERTS_PER_GROUP * gidx
    el = jnp.where((lane_f >= lo) & (lane_f < lo + EXPERTS_PER_GROUP), logits, neg)
    m1, i1 = top1(el)
    m2, i2 = top1(jnp.where(lane_f == i1, neg, el))
    e21 = jnp.exp(m2 - m1)
    gate1 = g_w / (1.0 + e21)
    gate2 = g_w * e21 / (1.0 + e21)

    heads = []
    for hd in range(N_XHEADS):
        sl = slice(hd * XHEAD_DIM, (hd + 1) * XHEAD_DIM)
        qh = q[:, sl].astype(BF16)
        s = lax.dot_general(qh, k_ref[0, :, sl], (((1,), (1,)), ((), ())),
                            preferred_element_type=F32)
        s = s - jnp.max(s, axis=-1, keepdims=True)
        p = jnp.exp(s)
        p = p / jnp.sum(p, axis=-1, keepdims=True)
        heads.append(_dot(p.astype(BF16), v_ref[0, :, sl]).astype(BF16))
    o = jnp.concatenate(heads, axis=-1)

    hot1 = lane_f == i1
    hot2 = lane_f == i2
    hot = jnp.where(hot1 | hot2, 1.0, 0.0)
    before = _dot(tri_ref[...], hot.astype(BF16))
    count = jnp.sum(hot, axis=0, keepdims=True)
    chunks = jnp.floor((count + (MOE_CHUNK - 1.0)) * (1.0 / MOE_CHUNK))
    l_i = lax.broadcasted_iota(jnp.int32, (ROUTE_LANES, ROUTE_LANES), 0)
    l_j = lax.broadcasted_iota(jnp.int32, (ROUTE_LANES, ROUTE_LANES), 1)
    upper = jnp.where(l_i < l_j, 1.0, 0.0).astype(BF16)
    first_chunk = _dot(jnp.broadcast_to(chunks, (SUBLANES, ROUTE_LANES)).astype(BF16), upper)[0:1]
    start = first_chunk * float(MOE_CHUNK) + before
    pos1 = jnp.sum(jnp.where(hot1, start, 0.0), axis=-1, keepdims=True)
    pos2 = jnp.sum(jnp.where(hot2, start, 0.0), axis=-1, keepdims=True)
    cnt_ref[0] = jnp.broadcast_to(count, (SUBLANES, ROUTE_LANES)).astype(jnp.int32)
    ri_ref[...] = jnp.where(lane == 0, pos1, jnp.where(lane == 1, pos2, 0.0)).astype(jnp.int32)

    h2 = h1 + _dot(o, wo_ref[...])
    h_ref[...] = h2
    h2s_ref[...] = h2

    def pieces(g):
        hi = g.astype(BF16).astype(F32)
        mid = (g - hi).astype(BF16).astype(F32)
        low = (g - hi - mid).astype(BF16).astype(F32)
        return hi, mid, low

    g6 = jnp.where(lane == 6, 1.0, 0.0)
    for li, piece in enumerate(pieces(gate1) + pieces(gate2)):
        g6 = jnp.where(lane == li, piece, g6)
    rho = rho_ref[...]
    ptk = jnp.where(rho == pos1, 1.0, jnp.where(rho == pos2, 2.0, 0.0)).astype(BF16)
    rhs = jnp.concatenate([hf, g6.astype(BF16)], axis=1)
    res = lax.dot_general(ptk, rhs, (((0,), (0,)), ((), ())), preferred_element_type=F32)
    sg = res[:, D_MODEL:]
    which = sg[:, 6:7]
    srt = res[:, :D_MODEL] * jnp.where(which == 2.0, 0.5, 1.0)
    xs_ref[...] = _pack_words(srt).reshape(TILE_CHUNKS, MOE_CHUNK, D_PACK)
    first = sg[:, 0:1] + sg[:, 1:2] + sg[:, 2:3]
    second = 0.5 * (sg[:, 3:4] + sg[:, 4:5] + sg[:, 5:6])
    gsort = jnp.where(which == 1.0, first, jnp.where(which == 2.0, second, 0.0))
    gs_ref[...] = jnp.broadcast_to(gsort, (TILE_ROWS, ROUTE_LANES))


def _pack_words(v):
    hi = lax.bitcast_convert_type(v[:, :D_PACK], jnp.uint32) & jnp.uint32(HI_MASK)
    lo = lax.bitcast_convert_type(v[:, D_PACK:], jnp.uint32) >> 16
    return hi | lo


def _unpack_words(w):
    hi = lax.bitcast_convert_type(w & jnp.uint32(HI_MASK), F32)
    lo = lax.bitcast_convert_type(w << 16, F32)
    return hi, lo


def _mix_attn(xt, yc, ys, wot, wob, nx, wq, k, v, wo, nf, wr, br, seq):
    t, d = xt.shape
    nt = t // TM_MIX
    tiles_per_batch = seq // TM_MIX
    mlen = k.shape[1]
    att = lambda i: jnp.minimum(i, nt - 1)
    rte = lambda i: jnp.maximum(i - 1, 0)
    row_a = lambda i: (att(i), 0)
    row_r = lambda i: (rte(i), 0)
    const = lambda i: (0, 0)
    tile3 = lambda i: (rte(i), 0, 0)
    kvspec = pl.BlockSpec((1, mlen, d), lambda i: (att(i) // tiles_per_batch, 0, 0))
    packed = jax.ShapeDtypeStruct((nt * TILE_CHUNKS, MOE_CHUNK, D_PACK), jnp.uint32)
    return pl.pallas_call(
        _mix_attn_kernel,
        grid=(nt + 1,),
        in_specs=[
            pl.BlockSpec((TM_MIX, d), row_a),
            pl.BlockSpec((TM_MIX, D_CONV), row_a),
            pl.BlockSpec((D_SSM // LANES, TM_MIX, LANES), lambda i: (0, att(i), 0)),
            pl.BlockSpec((D_CONV, d), const),
            pl.BlockSpec((D_SSM, d), const),
            pl.BlockSpec((1, d), const),
            pl.BlockSpec((d, d), const),
            kvspec, kvspec,
            pl.BlockSpec((d, d), const),
            pl.BlockSpec((1, d), const),
            pl.BlockSpec((d, ROUTE_LANES), const),
            pl.BlockSpec((1, ROUTE_LANES), const),
        ],
        out_specs=(
            pl.BlockSpec((TM_MIX, d), row_a),
            pl.BlockSpec((TILE_CHUNKS, MOE_CHUNK, D_PACK), tile3),
            pl.BlockSpec((TILE_ROWS, ROUTE_LANES), row_r),
            pl.BlockSpec((TM_MIX, ROUTE_LANES), row_r),
            pl.BlockSpec((1, SUBLANES, ROUTE_LANES), tile3),
        ),
        out_shape=(
            jax.ShapeDtypeStruct((t, d), F32),
            packed,
            jax.ShapeDtypeStruct((nt * TILE_ROWS, ROUTE_LANES), F32),
            jax.ShapeDtypeStruct((t, ROUTE_LANES), jnp.int32),
            jax.ShapeDtypeStruct((nt, SUBLANES, ROUTE_LANES), jnp.int32),
        ),
        scratch_shapes=[
            pltpu.VMEM((TM_MIX, d), F32),
            pltpu.VMEM((TM_MIX, TM_MIX), BF16),
            pltpu.VMEM((TM_MIX, TILE_ROWS), F32),
        ],
        compiler_params=pltpu.CompilerParams(
            dimension_semantics=("arbitrary",), vmem_limit_bytes=VMEM_MIX),
        name="mix_attn",
    )(xt, yc, ys, wot, wob, nx, wq, k, v, wo, nf, wr, br)


ZERO_CHUNK = TILE_CHUNKS - 1
DUMP_TILE0 = 16
SPARE_PER_TILE = TILE_CHUNKS - TILE_USED_CHUNKS


def _dump_chunk(slot, k):
    idx = slot * BLOCK_CHUNKS + k
    return (DUMP_TILE0 + idx // SPARE_PER_TILE) * TILE_CHUNKS + TILE_USED_CHUNKS + idx % SPARE_PER_TILE


def _experts_kernel(src_ref, dst_ref, be_ref, seq_ref, nxt_ref, na_ref, xs_hbm, w1_hbm, w3_hbm,
                    w2_hbm, ys_hbm, xbuf, ybuf, w1f_ref, w3f_ref, w2f_ref,
                    w1s_ref, w3s_ref, w2s_ref, in_sem, out_sem, w_sem):
    b = pl.program_id(0)
    n_active = na_ref[0]
    slot = b % 2
    other = 1 - slot

    def gather(blk, sl):
        for k in range(BLOCK_CHUNKS):
            pltpu.make_async_copy(xs_hbm.at[src_ref[blk * BLOCK_CHUNKS + k]],
                                  xbuf.at[sl, k], in_sem.at[sl]).start()

    def gather_wait(sl):
        pltpu.make_async_copy(xs_hbm.at[pl.ds(0, BLOCK_CHUNKS)], xbuf.at[sl],
                              in_sem.at[sl]).wait()

    def scatter(blk, sl):
        for k in range(BLOCK_CHUNKS):
            d = dst_ref[blk * BLOCK_CHUNKS + k]
            dump = jnp.where(sl == 0, _dump_chunk(0, k), _dump_chunk(1, k))
            pltpu.make_async_copy(ybuf.at[sl, k], ys_hbm.at[jnp.where(d < 0, dump, d)],
                                  out_sem.at[sl]).start()

    def scatter_wait(sl):
        pltpu.make_async_copy(ybuf.at[sl], ys_hbm.at[pl.ds(0, BLOCK_CHUNKS)],
                              out_sem.at[sl]).wait()

    active = b < n_active

    @pl.when(b == 0)
    def _():
        gather(0, 0)
        ybuf[...] = jnp.zeros(ybuf.shape, jnp.uint32)

    prev = be_ref[jnp.maximum(b - 1, 0)]
    fresh = (b == 0) | (be_ref[b] != prev)

    def weight_copies(e, ws):
        return [pltpu.make_async_copy(hbm.at[e], buf.at[ws], w_sem.at[ws])
                for hbm, buf in ((w1_hbm, w1f_ref), (w3_hbm, w3f_ref), (w2_hbm, w2f_ref))]

    @pl.when(active & fresh)
    def _():
        ws = seq_ref[b] % 2

        @pl.when(b == 0)
        def _():
            for cp in weight_copies(be_ref[b], ws):
                cp.start(priority=1)

        for cp in weight_copies(be_ref[b], ws):
            cp.wait()

        @pl.when(nxt_ref[b] >= 0)
        def _():
            for cp in weight_copies(nxt_ref[b], 1 - ws):
                cp.start(priority=1)

        w1s_ref[...] = w1f_ref[ws].astype(BF16)
        w3s_ref[...] = w3f_ref[ws].astype(BF16)
        w2s_ref[...] = w2f_ref[ws].astype(BF16)

    @pl.when(active & (b >= 1))
    def _():
        scatter_wait(slot)

    @pl.when(active)
    def _():
        gather_wait(slot)
        gather(b + 1, other)
        scatter(b, other)

        hi, lo = _unpack_words(xbuf[slot].reshape(MOE_BLOCK, D_PACK))
        xb = jnp.concatenate([hi, lo], axis=1).astype(BF16)
        h1 = _dot(xb, w1s_ref[...])
        h3 = _dot(xb, w3s_ref[...])
        hid = (h1 * jax.nn.sigmoid(h1) * h3).astype(BF16)
        y = _dot(hid, w2s_ref[...]).astype(BF16).astype(F32)
        ybuf[slot] = _pack_words(y).reshape(BLOCK_CHUNKS, MOE_CHUNK, D_PACK)

        @pl.when(b == n_active - 1)
        def _():
            scatter(b + 1, slot)
            gather_wait(other)
            scatter_wait(other)
            scatter_wait(slot)


def _experts(src, dst, block_expert, expert_seq, next_expert, n_active, xs, w1, w3, w2):
    nb = block_expert.shape[0]
    d = D_MODEL
    hbm = pl.BlockSpec(memory_space=pl.ANY)
    grid_spec = pltpu.PrefetchScalarGridSpec(
        num_scalar_prefetch=6,
        grid=(nb,),
        in_specs=[hbm, hbm, hbm, hbm],
        out_specs=hbm,
        scratch_shapes=[
            pltpu.VMEM((2, BLOCK_CHUNKS, MOE_CHUNK, D_PACK), jnp.uint32),
            pltpu.VMEM((2, BLOCK_CHUNKS, MOE_CHUNK, D_PACK), jnp.uint32),
            pltpu.VMEM((2, d, D_EXPERT), F32),
            pltpu.VMEM((2, d, D_EXPERT), F32),
            pltpu.VMEM((2, D_EXPERT, d), F32),
            pltpu.VMEM((d, D_EXPERT), BF16),
            pltpu.VMEM((d, D_EXPERT), BF16),
            pltpu.VMEM((D_EXPERT, d), BF16),
            pltpu.SemaphoreType.DMA((2,)),
            pltpu.SemaphoreType.DMA((2,)),
            pltpu.SemaphoreType.DMA((2,)),
        ],
    )
    return pl.pallas_call(
        _experts_kernel,
        grid_spec=grid_spec,
        out_shape=jax.ShapeDtypeStruct(xs.shape, jnp.uint32),
        input_output_aliases={6: 0},
        compiler_params=pltpu.CompilerParams(
            dimension_semantics=("arbitrary",), vmem_limit_bytes=VMEM_EXPERTS),
        name="experts",
    )(src, dst, block_expert, expert_seq, next_expert, n_active, xs, w1, w3, w2)


COMBINE_RING = 3


def _combine_kernel(h_hbm, ri_ref, gs_ref, ys_hbm, fw_ref, o_ref, hbuf, ybuf, sem):
    i = pl.program_id(0)
    nt = pl.num_programs(0)
    tm = hbuf.shape[1]

    def copies(tile, slot):
        return (pltpu.make_async_copy(h_hbm.at[pl.ds(tile * tm, tm)], hbuf.at[slot], sem.at[0, slot]),
                pltpu.make_async_copy(ys_hbm.at[pl.ds(tile * TILE_CHUNKS, TILE_CHUNKS)], ybuf.at[slot],
                                      sem.at[1, slot]))

    @pl.when(i == 0)
    def _():
        for t0 in range(COMBINE_RING - 1):
            for cp in copies(t0, t0):
                cp.start()

    ahead = i + COMBINE_RING - 1

    @pl.when(ahead < nt)
    def _():
        for cp in copies(ahead, ahead % COMBINE_RING):
            cp.start()

    slot = i % COMBINE_RING
    for cp in copies(i, slot):
        cp.wait()

    hi, lo = _unpack_words(ybuf[slot].reshape(TILE_ROWS, D_PACK))
    g = gs_ref[...]
    gw = jnp.concatenate([g] * (D_PACK // ROUTE_LANES), axis=1)
    yg = jnp.concatenate([hi * gw, lo * gw], axis=1).astype(BF16)
    pos = ri_ref[...].astype(F32)
    rho = lax.broadcasted_iota(jnp.int32, (tm, TILE_ROWS), 1).astype(F32)
    q = jnp.where((rho == pos[:, 0:1]) | (rho == pos[:, 1:2]), 1.0, 0.0).astype(BF16)
    o_ref[...] = _rms(hbuf[slot] + _dot(q, yg), fw_ref[...])


def _combine(h2, route_i, gsort, ys, final_w):
    t, d = h2.shape
    nt = t // TM_MIX
    assert nt >= COMBINE_RING
    row = lambda i: (i, 0)
    hbm = pl.BlockSpec(memory_space=pl.ANY)
    return pl.pallas_call(
        _combine_kernel,
        grid=(nt,),
        in_specs=[
            hbm,
            pl.BlockSpec((TM_MIX, ROUTE_LANES), row),
            pl.BlockSpec((TILE_ROWS, ROUTE_LANES), row),
            hbm,
            pl.BlockSpec((1, d), lambda i: (0, 0)),
        ],
        out_specs=pl.BlockSpec((TM_MIX, d), row),
        out_shape=jax.ShapeDtypeStruct((t, d), F32),
        scratch_shapes=[
            pltpu.VMEM((COMBINE_RING, TM_MIX, d), F32),
            pltpu.VMEM((COMBINE_RING, TILE_CHUNKS, MOE_CHUNK, D_PACK), jnp.uint32),
            pltpu.SemaphoreType.DMA((2, COMBINE_RING)),
        ],
        compiler_params=pltpu.CompilerParams(
            dimension_semantics=("arbitrary",), vmem_limit_bytes=VMEM_EXPERTS),
        name="combine",
    )(h2, route_i, gsort, ys, final_w)


def _routing_tables(cnt):
    nt = cnt.shape[0]
    n = cnt[:, 0, EXPERT_LANE0:EXPERT_LANE0 + N_EXPERTS]
    c = (n + MOE_CHUNK - 1) // MOE_CHUNK
    local = jnp.cumsum(c, axis=1) - c
    per_expert = jnp.sum(c, axis=0)
    padded = (per_expert + BLOCK_CHUNKS - 1) // BLOCK_CHUNKS * BLOCK_CHUNKS
    gend = jnp.cumsum(padded)
    gstart = gend - padded
    within = jnp.cumsum(c, axis=0) - c
    g_max = (2 * nt * TM_MIX // MOE_CHUNK + nt * N_EXPERTS
             + N_EXPERTS * (BLOCK_CHUNKS - 1))
    n_blocks = -(-g_max // BLOCK_CHUNKS)
    g = jnp.arange(n_blocks * BLOCK_CHUNKS, dtype=jnp.int32)
    e_of = jnp.minimum(jnp.sum((gend[None, :] <= g[:, None]).astype(jnp.int32), axis=1),
                       N_EXPERTS - 1)
    hot_e = (e_of[:, None] == jnp.arange(N_EXPERTS, dtype=jnp.int32)[None, :]).astype(F32)
    pick = lambda table: jnp.dot(hot_e, table.T.astype(F32), precision=lax.Precision.HIGHEST)
    rel = g.astype(F32) - pick(gstart[None, :])[:, 0]
    w_rows = pick(within)
    tile = jnp.sum((w_rows <= rel[:, None]).astype(jnp.int32), axis=1) - 1
    hot_t = tile[:, None] == jnp.arange(nt, dtype=jnp.int32)[None, :]
    at_tile = lambda rows: jnp.sum(jnp.where(hot_t, rows, 0.0), axis=1)
    k = rel - at_tile(w_rows)
    valid = k < at_tile(pick(c))
    chunk = (tile * TILE_CHUNKS + (at_tile(pick(local)) + k).astype(jnp.int32))
    pad_src = jnp.full((BLOCK_CHUNKS,), ZERO_CHUNK, jnp.int32)
    pad_dst = jnp.full((BLOCK_CHUNKS,), -1, jnp.int32)
    src = jnp.concatenate([jnp.where(valid, chunk, ZERO_CHUNK).astype(jnp.int32), pad_src])
    dst = jnp.concatenate([pad_dst, jnp.where(valid, chunk, -1).astype(jnp.int32)])
    block_start = jnp.arange(n_blocks, dtype=jnp.int32) * BLOCK_CHUNKS
    block_expert = jnp.minimum(
        jnp.sum((block_start[:, None] >= gend[None, :]).astype(jnp.int32), axis=1),
        N_EXPERTS - 1).astype(jnp.int32)
    n_active = (gend[-1:] // BLOCK_CHUNKS).astype(jnp.int32)
    blk = jnp.arange(n_blocks, dtype=jnp.int32)
    is_first = (blk < n_active[0]) & ((blk == 0) | (block_expert != jnp.roll(block_expert, 1)))
    expert_seq = (jnp.cumsum(is_first.astype(jnp.int32)) - 1).astype(jnp.int32)
    later_first = is_first[None, :] & (blk[None, :] > blk[:, None])
    next_blk = jnp.min(jnp.where(later_first, blk[None, :], n_blocks), axis=1)
    next_expert = jnp.where(
        next_blk < n_blocks,
        jnp.sum(jnp.where(blk[None, :] == next_blk[:, None], block_expert[None, :], 0), axis=1),
        -1).astype(jnp.int32)
    return src, dst, block_expert, expert_seq, next_expert, n_active


def _layer(h, mem, p, final_w):
    bsz, seq, d = h.shape
    t = bsz * seq
    assert t // TM_MIX >= DUMP_TILE0 + 2 * BLOCK_CHUNKS // SPARE_PER_TILE
    xt = h.reshape(t, d)

    v, u = _in_proj(xt, p["norm_mix_w"].reshape(1, d), p["w_in"].astype(BF16))
    y_conv = _conv(v, p["conv_w"], p["conv_b"].reshape(1, D_CONV),
                   p["conv_ln_w"].reshape(1, D_CONV), p["conv_ln_b"].reshape(1, D_CONV),
                   bsz, seq)

    a1, pj, aj, bbar = _s5_prep(p["ssm_A_re"], p["ssm_A_im"], p["ssm_log_dt"],
                                      p["ssm_B_re"], p["ssm_B_im"])
    nlb = D_SSM // S5_LANES
    bb = bbar.reshape(2, SSM_GROUP, nlb, S5_GROUPS, SSM_STATE).transpose(0, 2, 3, 1, 4)
    bcat = jnp.concatenate([_block_diag(bb[0]), _block_diag(bb[1])], axis=-1).astype(BF16)
    c_re = p["ssm_C_re"].reshape(nlb, S5_GROUPS, SSM_GROUP, SSM_STATE).transpose(0, 1, 3, 2)
    c_im = p["ssm_C_im"].reshape(nlb, S5_GROUPS, SSM_GROUP, SSM_STATE).transpose(0, 1, 3, 2)
    ccat = jnp.concatenate([_block_diag(c_re), -_block_diag(c_im)], axis=1).astype(BF16)
    glu = p["ssm_glu_w"].reshape(nlb, S5_GROUPS, SSM_GROUP, 2 * SSM_GROUP)
    wab = jnp.concatenate([_block_diag(glu[..., :SSM_GROUP]),
                           _block_diag(glu[..., SSM_GROUP:])], axis=-1).astype(BF16)
    y_ssm = _s5(u, bcat, ccat, p["ssm_D"].reshape(1, D_SSM), wab,
                a1, pj, aj, bsz, seq)

    k, vv = _kv(mem, p["norm_mem_w"].reshape(1, d), p["xk_w"].astype(BF16),
                p["xv_w"].astype(BF16))
    w_out = p["w_out"].astype(BF16)
    wr = jnp.concatenate([p["router_group_w"], p["router_expert_w"]], axis=1)
    wr = jnp.pad(wr, ((0, 0), (0, ROUTE_LANES - wr.shape[1]))).astype(BF16)
    br = jnp.concatenate([p["router_group_b"], p["router_expert_b"].reshape(-1)])
    br = jnp.pad(br, (0, ROUTE_LANES - br.shape[0])).reshape(1, ROUTE_LANES)
    h2, xs, gsort, route_i, cnt = _mix_attn(
        xt, y_conv, y_ssm, w_out[:D_CONV], w_out[D_CONV:], p["norm_x_w"].reshape(1, d),
        (p["xq_w"] * (XHEAD_DIM ** -0.5)).astype(BF16), k, vv, p["xo_w"].astype(BF16),
        p["norm_ffn_w"].reshape(1, d), wr, br, seq)

    src, dst, block_expert, expert_seq, next_expert, n_active = _routing_tables(cnt)
    ys = _experts(src, dst, block_expert, expert_seq, next_expert, n_active, xs,
                  p["moe_w1"], p["moe_w3"], p["moe_w2"])
    out = _combine(h2, route_i, gsort, ys, final_w.reshape(1, d))
    return out.reshape(bsz, seq, d)


def kernel(x, mem, norm_mix_w, w_in, conv_w, conv_b, conv_ln_w, conv_ln_b, ssm_A_re, ssm_A_im, ssm_log_dt, ssm_B_re, ssm_B_im, ssm_C_re, ssm_C_im, ssm_D, ssm_glu_w, w_out, norm_x_w, norm_mem_w, xq_w, xk_w, xv_w, xo_w, norm_ffn_w, router_group_w, router_group_b, router_expert_w, router_expert_b, moe_w1, moe_w3, moe_w2, final_norm_w):
    stacked = dict(
        norm_mix_w=norm_mix_w, w_in=w_in, conv_w=conv_w, conv_b=conv_b,
        conv_ln_w=conv_ln_w, conv_ln_b=conv_ln_b, ssm_A_re=ssm_A_re, ssm_A_im=ssm_A_im,
        ssm_log_dt=ssm_log_dt, ssm_B_re=ssm_B_re, ssm_B_im=ssm_B_im, ssm_C_re=ssm_C_re,
        ssm_C_im=ssm_C_im, ssm_D=ssm_D, ssm_glu_w=ssm_glu_w, w_out=w_out,
        norm_x_w=norm_x_w, norm_mem_w=norm_mem_w, xq_w=xq_w, xk_w=xk_w, xv_w=xv_w,
        xo_w=xo_w, norm_ffn_w=norm_ffn_w, router_group_w=router_group_w,
        router_group_b=router_group_b, router_expert_w=router_expert_w,
        router_expert_b=router_expert_b, moe_w1=moe_w1, moe_w3=moe_w3, moe_w2=moe_w2)
    depth = norm_mix_w.shape[0]
    assert depth == 1, "final norm is fused into the single layer's combine step"
    layer = {name: w[0] for name, w in stacked.items()}
    return _layer(x, mem, layer, final_norm_w)
```

```python
import functools

import jax
import jax.numpy as jnp
from jax import lax
from jax.experimental import pallas as pl
from jax.experimental.pallas import tpu as pltpu

D_MODEL = 1024
D_CONV = 512
CONV_WIDTH = 31
D_SSM = 512
SSM_GROUP = 16
N_SSM_GROUPS = 32
SSM_STATE = 64
N_XHEADS = 4
XHEAD_DIM = 256
N_EXPERT_GROUPS = 4
EXPERTS_PER_GROUP = 8
N_EXPERTS = 32
D_EXPERT = 512
EPS = 1e-6

F32 = jnp.float32
BF16 = jnp.bfloat16

SUBLANES = 8
LANES = 128

TM_PROJ = 1024
TM_CONV = 1024
CONV_CHUNK = 128
CONV_NORM_ROWS = 128
CONV_HALO = 32
S5_STEPS = 64
S5_TILE = SUBLANES * S5_STEPS
S5_GROUP_ROWS = 256
S5_LANES = 128
S5_GROUPS = S5_LANES // SSM_GROUP
S5_STATE = S5_GROUPS * SSM_STATE
TM_MIX = 512
ROUTE_LANES = 128
EXPERT_LANE0 = N_EXPERT_GROUPS
MOE_CHUNK = SUBLANES
MOE_BLOCK = 256
BLOCK_CHUNKS = MOE_BLOCK // MOE_CHUNK
TILE_USED_CHUNKS = 2 * TM_MIX // MOE_CHUNK + N_EXPERTS * (MOE_CHUNK - 1) // MOE_CHUNK
TILE_CHUNKS = 160
TILE_ROWS = TILE_CHUNKS * MOE_CHUNK
D_PACK = D_MODEL // 2
HI_MASK = 0xFFFF0000

VMEM_SMALL = 40 << 20
VMEM_EXPERTS = 48 << 20
VMEM_MIX = 62 << 20


def _rms(x, w):
    return x * lax.rsqrt(jnp.mean(x * x, axis=-1, keepdims=True) + EPS) * w


def _dot(a, b):
    return jnp.dot(a, b, preferred_element_type=F32)


def _s5_prep_kernel(are_ref, aim_ref, ldt_ref, btre_ref, btim_ref,
                    a1_ref, pj_ref, aj_ref, bbar_ref):
    lam_re = are_ref[...]
    lam_im = aim_ref[...]
    dt = jnp.exp(ldt_ref[...])
    x = lam_re * dt
    y = lam_im * dt

    def power(k):
        mag = jnp.exp(k * x)
        return mag * jnp.cos(k * y), mag * jnp.sin(k * y)

    n = x.shape[-1]
    ones8 = jnp.ones((SUBLANES, n), F32)
    a_re, a_im = power(ones8)
    a1_ref[0] = a_re
    a1_ref[1] = a_im

    row = lax.broadcasted_iota(jnp.int32, (SUBLANES, n), 0)
    for i, d in enumerate((1, 2, 4)):
        p_re, p_im = power(ones8 * float(d * S5_STEPS))
        keep = row >= d
        pj_ref[0, i] = jnp.where(keep, p_re, 0.0)
        pj_ref[1, i] = jnp.where(keep, p_im, 0.0)
    j_re, j_im = power(ones8 * float(S5_STEPS))
    aj_ref[0] = j_re
    aj_ref[1] = j_im

    num_re = a_re[0:1] - 1.0
    num_im = a_im[0:1]
    den = lam_re * lam_re + lam_im * lam_im
    c_re = (num_re * lam_re + num_im * lam_im) / den
    c_im = (num_im * lam_re - num_re * lam_im) / den
    b_re = btre_ref[...]
    b_im = btim_ref[...]
    bbar_ref[0] = c_re * b_re - c_im * b_im
    bbar_ref[1] = c_re * b_im + c_im * b_re


def _s5_prep(a_re, a_im, log_dt, b_re, b_im):
    n = N_SSM_GROUPS * SSM_STATE
    are = a_re.reshape(1, n)
    aim = a_im.reshape(1, n)
    ldt = jnp.repeat(log_dt, SSM_STATE).reshape(1, n)
    btre = jnp.transpose(b_re, (2, 0, 1)).reshape(SSM_GROUP, n)
    btim = jnp.transpose(b_im, (2, 0, 1)).reshape(SSM_GROUP, n)
    return pl.pallas_call(
        _s5_prep_kernel,
        out_shape=(
            jax.ShapeDtypeStruct((2, SUBLANES, n), F32),
            jax.ShapeDtypeStruct((2, 3, SUBLANES, n), F32),
            jax.ShapeDtypeStruct((2, SUBLANES, n), F32),
            jax.ShapeDtypeStruct((2, SSM_GROUP, n), F32),
        ),
        name="s5_prep",
    )(are, aim, ldt, btre, btim)


def _block_diag(w):
    nl, g, r, c = w.shape
    eye = jnp.eye(g, dtype=w.dtype)
    return jnp.einsum("lgrc,gh->lgrhc", w, eye).reshape(nl, g * r, g * c)


PROJ_RING = 3


def _in_proj_kernel(x_hbm, nw_ref, w_ref, v_ref, u_ref, xbuf, sem):
    i = pl.program_id(0)
    nt = pl.num_programs(0)

    def copy(tile, slot):
        return pltpu.make_async_copy(x_hbm.at[pl.ds(tile * TM_PROJ, TM_PROJ)], xbuf.at[slot],
                                     sem.at[slot])

    @pl.when(i == 0)
    def _():
        for t0 in range(PROJ_RING - 1):
            copy(t0, t0).start()

    ahead = i + PROJ_RING - 1

    @pl.when(ahead < nt)
    def _():
        copy(ahead, ahead % PROJ_RING).start()

    slot = i % PROJ_RING
    copy(i, slot).wait()
    xn = _rms(xbuf[slot], nw_ref[...]).astype(BF16)
    proj = _dot(xn, w_ref[...])
    a = proj[:, :D_CONV]
    g = proj[:, D_CONV:2 * D_CONV]
    v_ref[...] = a * jax.nn.sigmoid(g)
    for lt in range(D_SSM // LANES):
        c0 = 2 * D_CONV + lt * LANES
        for tile in range(TM_PROJ // S5_TILE):
            for r in range(SUBLANES):
                t0 = tile * S5_TILE + r * S5_STEPS
                u_ref[lt, pl.ds(tile * S5_TILE + r, S5_STEPS, stride=SUBLANES), :] = (
                    proj[t0:t0 + S5_STEPS, c0:c0 + LANES])


def _in_proj(xt, norm_w, w_in):
    t = xt.shape[0]
    assert t // TM_PROJ >= PROJ_RING
    n_out = 2 * D_CONV + D_SSM
    return pl.pallas_call(
        _in_proj_kernel,
        grid=(t // TM_PROJ,),
        in_specs=[
            pl.BlockSpec(memory_space=pl.ANY),
            pl.BlockSpec((1, D_MODEL), lambda i: (0, 0)),
            pl.BlockSpec((D_MODEL, n_out), lambda i: (0, 0)),
        ],
        scratch_shapes=[
            pltpu.VMEM((PROJ_RING, TM_PROJ, D_MODEL), F32),
            pltpu.SemaphoreType.DMA((PROJ_RING,)),
        ],
        out_specs=(
            pl.BlockSpec((TM_PROJ, D_CONV), lambda i: (i, 0)),
            pl.BlockSpec((D_SSM // LANES, TM_PROJ, LANES), lambda i: (0, i, 0)),
        ),
        out_shape=(
            jax.ShapeDtypeStruct((t, D_CONV), F32),
            jax.ShapeDtypeStruct((D_SSM // LANES, t, LANES), F32),
        ),
        compiler_params=pltpu.CompilerParams(
            dimension_semantics=("arbitrary",), vmem_limit_bytes=VMEM_SMALL),
        name="in_proj",
    )(xt, norm_w, w_in)


def _conv_kernel(v_ref, w_ref, b_ref, lnw_ref, lnb_ref, o_ref, ext_ref, sh_ref):
    tt = pl.program_id(1)
    rows = CONV_HALO + TM_CONV
    n_lt = D_CONV // LANES

    @pl.when(tt == 0)
    def _():
        ext_ref[:, pl.ds(0, CONV_HALO), :] = jnp.zeros((n_lt, CONV_HALO, LANES), F32)

    @pl.when(tt > 0)
    def _():
        ext_ref[:, pl.ds(0, CONV_HALO), :] = ext_ref[:, pl.ds(TM_CONV, CONV_HALO), :]

    for lt in range(n_lt):
        ext_ref[lt, pl.ds(CONV_HALO, TM_CONV), :] = v_ref[:, pl.ds(lt * LANES, LANES)]
    for s in range(1, SUBLANES):
        sh_ref[s - 1, :, pl.ds(0, rows - SUBLANES), :] = ext_ref[:, pl.ds(s, rows - SUBLANES), :]
    bias = b_ref[...]
    lnw = lnw_ref[...]
    lnb = lnb_ref[...]
    tap0 = CONV_HALO - (CONV_WIDTH - 1)

    groups = CONV_CHUNK // SUBLANES

    def chunk(ci, carry):
        base = pl.multiple_of(ci * CONV_CHUNK, CONV_CHUNK)
        for lt in range(n_lt):
            lanes = pl.ds(lt * LANES, LANES)
            acc = [jnp.broadcast_to(bias[:, lt * LANES:(lt + 1) * LANES], (SUBLANES, LANES))] * groups
            for s in range(SUBLANES):
                taps = [j for j in range(CONV_WIDTH) if (tap0 + j) % SUBLANES == s]
                src = ext_ref.at[lt] if s == 0 else sh_ref.at[s - 1, lt]
                ngroups = (tap0 + taps[-1] - s) // SUBLANES + groups
                win = [src[pl.ds(base + SUBLANES * g, SUBLANES), :] for g in range(ngroups)]
                for j in taps:
                    g0 = (tap0 + j - s) // SUBLANES
                    wj = w_ref[j, :, lanes]
                    acc = [acc[r] + wj * win[g0 + r] for r in range(groups)]
            o_ref[pl.ds(base, CONV_CHUNK), lanes] = jnp.concatenate(acc, axis=0)
        return carry

    lax.fori_loop(0, TM_CONV // CONV_CHUNK, chunk, 0)

    for bi in range(TM_CONV // CONV_NORM_ROWS):
        rows_b = pl.ds(bi * CONV_NORM_ROWS, CONV_NORM_ROWS)
        acc = o_ref[rows_b, :]
        mu = jnp.mean(acc, axis=-1, keepdims=True)
        cen = acc - mu
        var = jnp.mean(cen * cen, axis=-1, keepdims=True)
        z = cen * lax.rsqrt(var + EPS) * lnw + lnb
        o_ref[rows_b, :] = z * jax.nn.sigmoid(z)


def _conv(v, conv_w, conv_b, ln_w, ln_b, bsz, seq):
    nt = seq // TM_CONV
    row = lambda b, t: (b * nt + t, 0)
    const = lambda b, t: (0, 0)
    return pl.pallas_call(
        _conv_kernel,
        grid=(bsz, nt),
        in_specs=[
            pl.BlockSpec((TM_CONV, D_CONV), row),
            pl.BlockSpec((CONV_WIDTH, SUBLANES, D_CONV), lambda b, t: (0, 0, 0)),
            pl.BlockSpec((1, D_CONV), const),
            pl.BlockSpec((1, D_CONV), const),
            pl.BlockSpec((1, D_CONV), const),
        ],
        out_specs=pl.BlockSpec((TM_CONV, D_CONV), row),
        out_shape=jax.ShapeDtypeStruct(v.shape, F32),
        scratch_shapes=[
            pltpu.VMEM((D_CONV // LANES, CONV_HALO + TM_CONV, LANES), F32),
            pltpu.VMEM((SUBLANES - 1, D_CONV // LANES, CONV_HALO + TM_CONV, LANES), F32),
        ],
        compiler_params=pltpu.CompilerParams(
            dimension_semantics=("arbitrary", "arbitrary")),
        name="conv",
    )(v, jnp.broadcast_to(conv_w[:, None, :], (CONV_WIDTH, SUBLANES, D_CONV)), conv_b, ln_w, ln_b)


def _cmul(a_re, a_im, b_re, b_im):
    return a_re * b_re - a_im * b_im, a_re * b_im + a_im * b_re


def _s5_kernel(u_ref, bcat_ref, ccat_ref, d_ref, wab_ref, a1_ref,
               pj_ref, aj_ref, o_ref, up_ref, bu_ref, st_ref, carry_ref):
    tt = pl.program_id(1)
    ns = S5_STATE
    nseq = u_ref.shape[0]
    steps_per_group = S5_GROUP_ROWS // SUBLANES
    n_groups = S5_TILE // S5_GROUP_ROWS

    @pl.when((pl.program_id(0) == 0) & (tt == 0))
    def _():
        up_ref[...] = jnp.zeros(up_ref.shape, F32)
        st_ref[...] = jnp.zeros(st_ref.shape, BF16)

    @pl.when(tt == 0)
    def _():
        carry_ref[...] = jnp.zeros(carry_ref.shape, F32)

    def scan_and_project(cur, prv):
        up_ref[cur] = u_ref[...]

        a_re = a1_ref[0]
        a_im = a1_ref[1]

        def step(q, j, s):
            rows = pl.ds(j * SUBLANES, SUBLANES)
            m_re, m_im = _cmul(a_re, a_im, s[0], s[1])
            return m_re + bu_ref[q, rows, pl.ds(0, ns)], m_im + bu_ref[q, rows, pl.ds(ns, ns)]

        def project_out(q, g):
            rows = pl.ds(g * S5_GROUP_ROWS, S5_GROUP_ROWS)
            y = _dot(st_ref[prv, q, rows, :], ccat_ref[0]) + d_ref[...] * up_ref[prv, q, rows, :]
            y = jax.nn.gelu(y)
            ab = _dot(y.astype(BF16), wab_ref[0])
            o_ref[q, rows, :] = ab[:, :S5_LANES] * jax.nn.sigmoid(ab[:, S5_LANES:])

        zero = jnp.zeros((SUBLANES, ns), F32)
        state = [(zero, zero)] * nseq
        for g in range(n_groups):
            rows = pl.ds(g * S5_GROUP_ROWS, S5_GROUP_ROWS)
            for q in range(nseq):
                bu_ref[q, rows, :] = _dot(up_ref[cur, q, rows, :].astype(BF16), bcat_ref[0])
            for jj in range(steps_per_group):
                state = [step(q, g * steps_per_group + jj, state[q]) for q in range(nseq)]
            for q in range(nseq):
                project_out(q, g)

        row = lax.broadcasted_iota(jnp.int32, (SUBLANES, ns), 0)
        first = row == 0
        entry = []
        for q in range(nseq):
            e_re, e_im = state[q]
            c_re = jnp.where(first, pltpu.roll(carry_ref[q, 0], 1, 0), pltpu.roll(e_re, 1, 0))
            c_im = jnp.where(first, pltpu.roll(carry_ref[q, 1], 1, 0), pltpu.roll(e_im, 1, 0))
            for i, d in enumerate((1, 2, 4)):
                r_re = pltpu.roll(c_re, d, 0)
                r_im = pltpu.roll(c_im, d, 0)
                m_re, m_im = _cmul(pj_ref[0, i], pj_ref[1, i], r_re, r_im)
                c_re = c_re + m_re
                c_im = c_im + m_im
            f_re, f_im = _cmul(aj_ref[0], aj_ref[1], c_re, c_im)
            carry_ref[q, 0] = f_re + e_re
            carry_ref[q, 1] = f_im + e_im
            entry.append((c_re, c_im))

        state = entry
        pack = 2 * SUBLANES
        for j in range(0, S5_STEPS, 2):
            mid = [step(q, j, state[q]) for q in range(nseq)]
            state = [step(q, j + 1, mid[q]) for q in range(nseq)]
            for q in range(nseq):
                st_ref[cur, q, pl.ds(j * SUBLANES, pack), pl.ds(0, ns)] = (
                    jnp.concatenate([mid[q][0], state[q][0]], axis=0).astype(BF16))
                st_ref[cur, q, pl.ds(j * SUBLANES, pack), pl.ds(ns, ns)] = (
                    jnp.concatenate([mid[q][1], state[q][1]], axis=0).astype(BF16))

    for parity in range(2):
        pl.when(tt % 2 == parity)(functools.partial(scan_and_project, parity, 1 - parity))


def _s5(u, bcat, ccat, d, wab, a1, pj, aj, bsz, seq):
    nt = seq // S5_TILE
    nlb = D_SSM // S5_LANES
    ns = S5_STATE
    assert S5_LANES == LANES and S5_TILE == TM_MIX and TM_PROJ % S5_TILE == 0
    u4 = u.reshape(nlb, bsz, seq, S5_LANES)
    row_in = lambda l, t: (l, 0, jnp.minimum(t, nt - 1), 0)
    row_out = lambda l, t: (l, 0, jnp.maximum(t - 1, 0), 0)
    lane3 = lambda l, t: (0, 0, l)
    lane4 = lambda l, t: (0, 0, 0, l)
    out = pl.pallas_call(
        _s5_kernel,
        grid=(nlb, nt + 1),
        in_specs=[
            pl.BlockSpec((None, bsz, S5_TILE, S5_LANES), row_in),
            pl.BlockSpec((1, S5_LANES, 2 * ns), lambda l, t: (l, 0, 0)),
            pl.BlockSpec((1, 2 * ns, S5_LANES), lambda l, t: (l, 0, 0)),
            pl.BlockSpec((1, S5_LANES), lambda l, t: (0, l)),
            pl.BlockSpec((1, S5_LANES, 2 * S5_LANES), lambda l, t: (l, 0, 0)),
            pl.BlockSpec((2, SUBLANES, ns), lane3),
            pl.BlockSpec((2, 3, SUBLANES, ns), lane4),
            pl.BlockSpec((2, SUBLANES, ns), lane3),
        ],
        out_specs=pl.BlockSpec((None, bsz, S5_TILE, S5_LANES), row_out),
        out_shape=jax.ShapeDtypeStruct(u4.shape, F32),
        scratch_shapes=[
            pltpu.VMEM((2, bsz, S5_TILE, S5_LANES), F32),
            pltpu.VMEM((bsz, S5_TILE, 2 * ns), F32),
            pltpu.VMEM((2, bsz, S5_TILE, 2 * ns), BF16),
            pltpu.VMEM((bsz, 2, SUBLANES, ns), F32),
        ],
        compiler_params=pltpu.CompilerParams(
            dimension_semantics=("arbitrary", "arbitrary"),
            vmem_limit_bytes=VMEM_SMALL),
        name="s5",
    )(u4, bcat, ccat, d, wab, a1, pj, aj)
    return out.reshape(nlb, bsz * seq, S5_LANES)


def _kv_kernel(m_ref, nw_ref, wk_ref, wv_ref, k_ref, v_ref):
    mn = _rms(m_ref[0], nw_ref[...]).astype(BF16)
    k_ref[0] = _dot(mn, wk_ref[...]).astype(BF16)
    v_ref[0] = _dot(mn, wv_ref[...]).astype(BF16)


def _kv(mem, norm_w, wk, wv):
    bsz, mlen, d = mem.shape
    blk = pl.BlockSpec((1, mlen, d), lambda b: (b, 0, 0))
    wspec = pl.BlockSpec((d, d), lambda b: (0, 0))
    return pl.pallas_call(
        _kv_kernel,
        grid=(bsz,),
        in_specs=[blk, pl.BlockSpec((1, d), lambda b: (0, 0)), wspec, wspec],
        out_specs=(blk, blk),
        out_shape=(jax.ShapeDtypeStruct(mem.shape, BF16),) * 2,
        compiler_params=pltpu.CompilerParams(
            dimension_semantics=("arbitrary",), vmem_limit_bytes=VMEM_SMALL),
        name="kv",
    )(mem, norm_w, wk, wv)


def _mix_attn_kernel(x_ref, yc_ref, ys_ref, wot_ref, wob_ref, nx_ref, wq_ref,
                     k_ref, v_ref, wo_ref, nf_ref, wr_ref, br_ref,
                     h_ref, xs_ref, gs_ref, ri_ref, cnt_ref,
                     h2s_ref, tri_ref, rho_ref):
    i = pl.program_id(0)
    tm = x_ref.shape[0]
    neg = -jnp.inf
    big = float(ROUTE_LANES)

    @pl.when(i == 0)
    def _():
        h2s_ref[...] = jnp.zeros(h2s_ref.shape, F32)
        r_i = lax.broadcasted_iota(jnp.int32, (tm, tm), 0)
        c_i = lax.broadcasted_iota(jnp.int32, (tm, tm), 1)
        tri_ref[...] = jnp.where(r_i > c_i, 1.0, 0.0).astype(BF16)
        rho_ref[...] = lax.broadcasted_iota(jnp.int32, (tm, TILE_ROWS), 1).astype(F32)

    ys_time = jnp.concatenate(
        [jnp.concatenate([ys_ref[lt, pl.ds(r, S5_STEPS, stride=SUBLANES), :]
                          for r in range(SUBLANES)], axis=0)
         for lt in range(D_SSM // LANES)], axis=1)
    h1 = (x_ref[...] + _dot(yc_ref[...].astype(BF16), wot_ref[...])
          + _dot(ys_time.astype(BF16), wob_ref[...]))
    hn = _rms(h1, nx_ref[...]).astype(BF16)
    q = _dot(hn, wq_ref[...])

    hf = _rms(h2s_ref[...], nf_ref[...]).astype(BF16)
    logits = _dot(hf, wr_ref[...]) + br_ref[...]
    lane = lax.broadcasted_iota(jnp.int32, (tm, ROUTE_LANES), 1)
    lane_f = lane.astype(F32)

    def top1(vals):
        m = jnp.max(vals, axis=-1, keepdims=True)
        idx = jnp.min(jnp.where(vals == m, lane_f, big), axis=-1, keepdims=True)
        return m, idx

    gl = jnp.where(lane < N_EXPERT_GROUPS, logits, neg)
    gmax, gidx = top1(gl)
    g_w = 1.0 / jnp.sum(jnp.exp(gl - gmax), axis=-1, keepdims=True)
    lo = EXPERT_LANE0 + EXPERTS_PER_GROUP * gidx
    el = jnp.where((lane_f >= lo) & (lane_f < lo + EXPERTS_PER_GROUP), logits, neg)
    m1, i1 = top1(el)
    m2, i2 = top1(jnp.where(lane_f == i1, neg, el))
    e21 = jnp.exp(m2 - m1)
    gate1 = g_w / (1.0 + e21)
    gate2 = g_w * e21 / (1.0 + e21)

    heads = []
    for hd in range(N_XHEADS):
        sl = slice(hd * XHEAD_DIM, (hd + 1) * XHEAD_DIM)
        qh = q[:, sl].astype(BF16)
        s = lax.dot_general(qh, k_ref[0, :, sl], (((1,), (1,)), ((), ())),
                            preferred_element_type=F32)
        s = s - jnp.max(s, axis=-1, keepdims=True)
        p = jnp.exp(s)
        p = p / jnp.sum(p, axis=-1, keepdims=True)
        heads.append(_dot(p.astype(BF16), v_ref[0, :, sl]).astype(BF16))
    o = jnp.concatenate(heads, axis=-1)

    hot1 = lane_f == i1
    hot2 = lane_f == i2
    hot = jnp.where(hot1 | hot2, 1.0, 0.0)
    before = _dot(tri_ref[...], hot.astype(BF16))
    count = jnp.sum(hot, axis=0, keepdims=True)
    chunks = jnp.floor((count + (MOE_CHUNK - 1.0)) * (1.0 / MOE_CHUNK))
    l_i = lax.broadcasted_iota(jnp.int32, (ROUTE_LANES, ROUTE_LANES), 0)
    l_j = lax.broadcasted_iota(jnp.int32, (ROUTE_LANES, ROUTE_LANES), 1)
    upper = jnp.where(l_i < l_j, 1.0, 0.0).astype(BF16)
    first_chunk = _dot(jnp.broadcast_to(chunks, (SUBLANES, ROUTE_LANES)).astype(BF16), upper)[0:1]
    start = first_chunk * float(MOE_CHUNK) + before
    pos1 = jnp.sum(jnp.where(hot1, start, 0.0), axis=-1, keepdims=True)
    pos2 = jnp.sum(jnp.where(hot2, start, 0.0), axis=-1, keepdims=True)
    cnt_ref[0] = jnp.broadcast_to(count, (SUBLANES, ROUTE_LANES)).astype(jnp.int32)
    ri_ref[...] = jnp.where(lane == 0, pos1, jnp.where(lane == 1, pos2, 0.0)).astype(jnp.int32)

    h2 = h1 + _dot(o, wo_ref[...])
    h_ref[...] = h2
    h2s_ref[...] = h2

    def pieces(g):
        hi = g.astype(BF16).astype(F32)
        mid = (g - hi).astype(BF16).astype(F32)
        low = (g - hi - mid).astype(BF16).astype(F32)
        return hi, mid, low

    g6 = jnp.where(lane == 6, 1.0, 0.0)
    for li, piece in enumerate(pieces(gate1) + pieces(gate2)):
        g6 = jnp.where(lane == li, piece, g6)
    rho = rho_ref[...]
    ptk = jnp.where(rho == pos1, 1.0, jnp.where(rho == pos2, 2.0, 0.0)).astype(BF16)
    rhs = jnp.concatenate([hf, g6.astype(BF16)], axis=1)
    res = lax.dot_general(ptk, rhs, (((0,), (0,)), ((), ())), preferred_element_type=F32)
    sg = res[:, D_MODEL:]
    which = sg[:, 6:7]
    srt = res[:, :D_MODEL] * jnp.where(which == 2.0, 0.5, 1.0)
    xs_ref[...] = _pack_words(srt).reshape(TILE_CHUNKS, MOE_CHUNK, D_PACK)
    first = sg[:, 0:1] + sg[:, 1:2] + sg[:, 2:3]
    second = 0.5 * (sg[:, 3:4] + sg[:, 4:5] + sg[:, 5:6])
    gsort = jnp.where(which == 1.0, first, jnp.where(which == 2.0, second, 0.0))
    gs_ref[...] = jnp.broadcast_to(gsort, (TILE_ROWS, ROUTE_LANES))


def _pack_words(v):
    hi = lax.bitcast_convert_type(v[:, :D_PACK], jnp.uint32) & jnp.uint32(HI_MASK)
    lo = lax.bitcast_convert_type(v[:, D_PACK:], jnp.uint32) >> 16
    return hi | lo


def _unpack_words(w):
    hi = lax.bitcast_convert_type(w & jnp.uint32(HI_MASK), F32)
    lo = lax.bitcast_convert_type(w << 16, F32)
    return hi, lo


def _mix_attn(xt, yc, ys, wot, wob, nx, wq, k, v, wo, nf, wr, br, seq):
    t, d = xt.shape
    nt = t // TM_MIX
    tiles_per_batch = seq // TM_MIX
    mlen = k.shape[1]
    att = lambda i: jnp.minimum(i, nt - 1)
    rte = lambda i: jnp.maximum(i - 1, 0)
    row_a = lambda i: (att(i), 0)
    row_r = lambda i: (rte(i), 0)
    const = lambda i: (0, 0)
    tile3 = lambda i: (rte(i), 0, 0)
    kvspec = pl.BlockSpec((1, mlen, d), lambda i: (att(i) // tiles_per_batch, 0, 0))
    packed = jax.ShapeDtypeStruct((nt * TILE_CHUNKS, MOE_CHUNK, D_PACK), jnp.uint32)
    return pl.pallas_call(
        _mix_attn_kernel,
        grid=(nt + 1,),
        in_specs=[
            pl.BlockSpec((TM_MIX, d), row_a),
            pl.BlockSpec((TM_MIX, D_CONV), row_a),
            pl.BlockSpec((D_SSM // LANES, TM_MIX, LANES), lambda i: (0, att(i), 0)),
            pl.BlockSpec((D_CONV, d), const),
            pl.BlockSpec((D_SSM, d), const),
            pl.BlockSpec((1, d), const),
            pl.BlockSpec((d, d), const),
            kvspec, kvspec,
            pl.BlockSpec((d, d), const),
            pl.BlockSpec((1, d), const),
            pl.BlockSpec((d, ROUTE_LANES), const),
            pl.BlockSpec((1, ROUTE_LANES), const),
        ],
        out_specs=(
            pl.BlockSpec((TM_MIX, d), row_a),
            pl.BlockSpec((TILE_CHUNKS, MOE_CHUNK, D_PACK), tile3),
            pl.BlockSpec((TILE_ROWS, ROUTE_LANES), row_r),
            pl.BlockSpec((TM_MIX, ROUTE_LANES), row_r),
            pl.BlockSpec((1, SUBLANES, ROUTE_LANES), tile3),
        ),
        out_shape=(
            jax.ShapeDtypeStruct((t, d), F32),
            packed,
            jax.ShapeDtypeStruct((nt * TILE_ROWS, ROUTE_LANES), F32),
            jax.ShapeDtypeStruct((t, ROUTE_LANES), jnp.int32),
            jax.ShapeDtypeStruct((nt, SUBLANES, ROUTE_LANES), jnp.int32),
        ),
        scratch_shapes=[
            pltpu.VMEM((TM_MIX, d), F32),
            pltpu.VMEM((TM_MIX, TM_MIX), BF16),
            pltpu.VMEM((TM_MIX, TILE_ROWS), F32),
        ],
        compiler_params=pltpu.CompilerParams(
            dimension_semantics=("arbitrary",), vmem_limit_bytes=VMEM_MIX),
        name="mix_attn",
    )(xt, yc, ys, wot, wob, nx, wq, k, v, wo, nf, wr, br)


ZERO_CHUNK = TILE_CHUNKS - 1
DUMP_TILE0 = 16
SPARE_PER_TILE = TILE_CHUNKS - TILE_USED_CHUNKS


def _dump_chunk(slot, k):
    idx = slot * BLOCK_CHUNKS + k
    return (DUMP_TILE0 + idx // SPARE_PER_TILE) * TILE_CHUNKS + TILE_USED_CHUNKS + idx % SPARE_PER_TILE


def _experts_kernel(src_ref, dst_ref, be_ref, seq_ref, nxt_ref, na_ref, xs_hbm, w1_hbm, w3_hbm,
                    w2_hbm, ys_hbm, xbuf, ybuf, w1f_ref, w3f_ref, w2f_ref,
                    w1s_ref, w3s_ref, w2s_ref, in_sem, out_sem, w_sem):
    b = pl.program_id(0)
    n_active = na_ref[0]
    slot = b % 2
    other = 1 - slot

    def gather(blk, sl):
        for k in range(BLOCK_CHUNKS):
            pltpu.make_async_copy(xs_hbm.at[src_ref[blk * BLOCK_CHUNKS + k]],
                                  xbuf.at[sl, k], in_sem.at[sl]).start()

    def gather_wait(sl):
        pltpu.make_async_copy(xs_hbm.at[pl.ds(0, BLOCK_CHUNKS)], xbuf.at[sl],
                              in_sem.at[sl]).wait()

    def scatter(blk, sl):
        for k in range(BLOCK_CHUNKS):
            d = dst_ref[blk * BLOCK_CHUNKS + k]
            dump = jnp.where(sl == 0, _dump_chunk(0, k), _dump_chunk(1, k))
            pltpu.make_async_copy(ybuf.at[sl, k], ys_hbm.at[jnp.where(d < 0, dump, d)],
                                  out_sem.at[sl]).start()

    def scatter_wait(sl):
        pltpu.make_async_copy(ybuf.at[sl], ys_hbm.at[pl.ds(0, BLOCK_CHUNKS)],
                              out_sem.at[sl]).wait()

    active = b < n_active

    @pl.when(b == 0)
    def _():
        gather(0, 0)
        ybuf[...] = jnp.zeros(ybuf.shape, jnp.uint32)

    prev = be_ref[jnp.maximum(b - 1, 0)]
    fresh = (b == 0) | (be_ref[b] != prev)

    def weight_copies(e, ws):
        return [pltpu.make_async_copy(hbm.at[e], buf.at[ws], w_sem.at[ws])
                for hbm, buf in ((w1_hbm, w1f_ref), (w3_hbm, w3f_ref), (w2_hbm, w2f_ref))]

    @pl.when(active & fresh)
    def _():
        ws = seq_ref[b] % 2

        @pl.when(b == 0)
        def _():
            for cp in weight_copies(be_ref[b], ws):
                cp.start(priority=1)

        for cp in weight_copies(be_ref[b], ws):
            cp.wait()

        @pl.when(nxt_ref[b] >= 0)
        def _():
            for cp in weight_copies(nxt_ref[b], 1 - ws):
                cp.start(priority=1)

        w1s_ref[...] = w1f_ref[ws].astype(BF16)
        w3s_ref[...] = w3f_ref[ws].astype(BF16)
        w2s_ref[...] = w2f_ref[ws].astype(BF16)

    @pl.when(active & (b >= 1))
    def _():
        scatter_wait(slot)

    @pl.when(active)
    def _():
        gather_wait(slot)
        gather(b + 1, other)
        scatter(b, other)

        hi, lo = _unpack_words(xbuf[slot].reshape(MOE_BLOCK, D_PACK))
        xb = jnp.concatenate([hi, lo], axis=1).astype(BF16)
        h1 = _dot(xb, w1s_ref[...])
        h3 = _dot(xb, w3s_ref[...])
        hid = (h1 * jax.nn.sigmoid(h1) * h3).astype(BF16)
        y = _dot(hid, w2s_ref[...]).astype(BF16).astype(F32)
        ybuf[slot] = _pack_words(y).reshape(BLOCK_CHUNKS, MOE_CHUNK, D_PACK)

        @pl.when(b == n_active - 1)
        def _():
            scatter(b + 1, slot)
            gather_wait(other)
            scatter_wait(other)
            scatter_wait(slot)


def _experts(src, dst, block_expert, expert_seq, next_expert, n_active, xs, w1, w3, w2):
    nb = block_expert.shape[0]
    d = D_MODEL
    hbm = pl.BlockSpec(memory_space=pl.ANY)
    grid_spec = pltpu.PrefetchScalarGridSpec(
        num_scalar_prefetch=6,
        grid=(nb,),
        in_specs=[hbm, hbm, hbm, hbm],
        out_specs=hbm,
        scratch_shapes=[
            pltpu.VMEM((2, BLOCK_CHUNKS, MOE_CHUNK, D_PACK), jnp.uint32),
            pltpu.VMEM((2, BLOCK_CHUNKS, MOE_CHUNK, D_PACK), jnp.uint32),
            pltpu.VMEM((2, d, D_EXPERT), F32),
            pltpu.VMEM((2, d, D_EXPERT), F32),
            pltpu.VMEM((2, D_EXPERT, d), F32),
            pltpu.VMEM((d, D_EXPERT), BF16),
            pltpu.VMEM((d, D_EXPERT), BF16),
            pltpu.VMEM((D_EXPERT, d), BF16),
            pltpu.SemaphoreType.DMA((2,)),
            pltpu.SemaphoreType.DMA((2,)),
            pltpu.SemaphoreType.DMA((2,)),
        ],
    )
    return pl.pallas_call(
        _experts_kernel,
        grid_spec=grid_spec,
        out_shape=jax.ShapeDtypeStruct(xs.shape, jnp.uint32),
        input_output_aliases={6: 0},
        compiler_params=pltpu.CompilerParams(
            dimension_semantics=("arbitrary",), vmem_limit_bytes=VMEM_EXPERTS),
        name="experts",
    )(src, dst, block_expert, expert_seq, next_expert, n_active, xs, w1, w3, w2)


COMBINE_RING = 3


def _combine_kernel(h_hbm, ri_ref, gs_ref, ys_hbm, fw_ref, o_ref, hbuf, ybuf, sem):
    i = pl.program_id(0)
    nt = pl.num_programs(0)
    tm = hbuf.shape[1]

    def copies(tile, slot):
        return (pltpu.make_async_copy(h_hbm.at[pl.ds(tile * tm, tm)], hbuf.at[slot], sem.at[0, slot]),
                pltpu.make_async_copy(ys_hbm.at[pl.ds(tile * TILE_CHUNKS, TILE_CHUNKS)], ybuf.at[slot],
                                      sem.at[1, slot]))

    @pl.when(i == 0)
    def _():
        for t0 in range(COMBINE_RING - 1):
            for cp in copies(t0, t0):
                cp.start()

    ahead = i + COMBINE_RING - 1

    @pl.when(ahead < nt)
    def _():
        for cp in copies(ahead, ahead % COMBINE_RING):
            cp.start()

    slot = i % COMBINE_RING
    for cp in copies(i, slot):
        cp.wait()

    hi, lo = _unpack_words(ybuf[slot].reshape(TILE_ROWS, D_PACK))
    g = gs_ref[...]
    gw = jnp.concatenate([g] * (D_PACK // ROUTE_LANES), axis=1)
    yg = jnp.concatenate([hi * gw, lo * gw], axis=1).astype(BF16)
    pos = ri_ref[...].astype(F32)
    rho = lax.broadcasted_iota(jnp.int32, (tm, TILE_ROWS), 1).astype(F32)
    q = jnp.where((rho == pos[:, 0:1]) | (rho == pos[:, 1:2]), 1.0, 0.0).astype(BF16)
    o_ref[...] = _rms(hbuf[slot] + _dot(q, yg), fw_ref[...])


def _combine(h2, route_i, gsort, ys, final_w):
    t, d = h2.shape
    nt = t // TM_MIX
    assert nt >= COMBINE_RING
    row = lambda i: (i, 0)
    hbm = pl.BlockSpec(memory_space=pl.ANY)
    return pl.pallas_call(
        _combine_kernel,
        grid=(nt,),
        in_specs=[
            hbm,
            pl.BlockSpec((TM_MIX, ROUTE_LANES), row),
            pl.BlockSpec((TILE_ROWS, ROUTE_LANES), row),
            hbm,
            pl.BlockSpec((1, d), lambda i: (0, 0)),
        ],
        out_specs=pl.BlockSpec((TM_MIX, d), row),
        out_shape=jax.ShapeDtypeStruct((t, d), F32),
        scratch_shapes=[
            pltpu.VMEM((COMBINE_RING, TM_MIX, d), F32),
            pltpu.VMEM((COMBINE_RING, TILE_CHUNKS, MOE_CHUNK, D_PACK), jnp.uint32),
            pltpu.SemaphoreType.DMA((2, COMBINE_RING)),
        ],
        compiler_params=pltpu.CompilerParams(
            dimension_semantics=("arbitrary",), vmem_limit_bytes=VMEM_EXPERTS),
        name="combine",
    )(h2, route_i, gsort, ys, final_w)


def _routing_tables(cnt):
    nt = cnt.shape[0]
    n = cnt[:, 0, EXPERT_LANE0:EXPERT_LANE0 + N_EXPERTS]
    c = (n + MOE_CHUNK - 1) // MOE_CHUNK
    local = jnp.cumsum(c, axis=1) - c
    per_expert = jnp.sum(c, axis=0)
    padded = (per_expert + BLOCK_CHUNKS - 1) // BLOCK_CHUNKS * BLOCK_CHUNKS
    gend = jnp.cumsum(padded)
    gstart = gend - padded
    within = jnp.cumsum(c, axis=0) - c
    g_max = (2 * nt * TM_MIX // MOE_CHUNK + nt * N_EXPERTS
             + N_EXPERTS * (BLOCK_CHUNKS - 1))
    n_blocks = -(-g_max // BLOCK_CHUNKS)
    g = jnp.arange(n_blocks * BLOCK_CHUNKS, dtype=jnp.int32)
    e_of = jnp.minimum(jnp.sum((gend[None, :] <= g[:, None]).astype(jnp.int32), axis=1),
                       N_EXPERTS - 1)
    hot_e = (e_of[:, None] == jnp.arange(N_EXPERTS, dtype=jnp.int32)[None, :]).astype(F32)
    pick = lambda table: jnp.dot(hot_e, table.T.astype(F32), precision=lax.Precision.HIGHEST)
    rel = g.astype(F32) - pick(gstart[None, :])[:, 0]
    w_rows = pick(within)
    tile = jnp.sum((w_rows <= rel[:, None]).astype(jnp.int32), axis=1) - 1
    hot_t = tile[:, None] == jnp.arange(nt, dtype=jnp.int32)[None, :]
    at_tile = lambda rows: jnp.sum(jnp.where(hot_t, rows, 0.0), axis=1)
    k = rel - at_tile(w_rows)
    valid = k < at_tile(pick(c))
    chunk = (tile * TILE_CHUNKS + (at_tile(pick(local)) + k).astype(jnp.int32))
    pad_src = jnp.full((BLOCK_CHUNKS,), ZERO_CHUNK, jnp.int32)
    pad_dst = jnp.full((BLOCK_CHUNKS,), -1, jnp.int32)
    src = jnp.concatenate([jnp.where(valid, chunk, ZERO_CHUNK).astype(jnp.int32), pad_src])
    dst = jnp.concatenate([pad_dst, jnp.where(valid, chunk, -1).astype(jnp.int32)])
    block_start = jnp.arange(n_blocks, dtype=jnp.int32) * BLOCK_CHUNKS
    block_expert = jnp.minimum(
        jnp.sum((block_start[:, None] >= gend[None, :]).astype(jnp.int32), axis=1),
        N_EXPERTS - 1).astype(jnp.int32)
    n_active = (gend[-1:] // BLOCK_CHUNKS).astype(jnp.int32)
    blk = jnp.arange(n_blocks, dtype=jnp.int32)
    is_first = (blk < n_active[0]) & ((blk == 0) | (block_expert != jnp.roll(block_expert, 1)))
    expert_seq = (jnp.cumsum(is_first.astype(jnp.int32)) - 1).astype(jnp.int32)
    later_first = is_first[None, :] & (blk[None, :] > blk[:, None])
    next_blk = jnp.min(jnp.where(later_first, blk[None, :], n_blocks), axis=1)
    next_expert = jnp.where(
        next_blk < n_blocks,
        jnp.sum(jnp.where(blk[None, :] == next_blk[:, None], block_expert[None, :], 0), axis=1),
        -1).astype(jnp.int32)
    return src, dst, block_expert, expert_seq, next_expert, n_active


def _layer(h, mem, p, final_w):
    bsz, seq, d = h.shape
    t = bsz * seq
    assert t // TM_MIX >= DUMP_TILE0 + 2 * BLOCK_CHUNKS // SPARE_PER_TILE
    xt = h.reshape(t, d)

    v, u = _in_proj(xt, p["norm_mix_w"].reshape(1, d), p["w_in"].astype(BF16))
    y_conv = _conv(v, p["conv_w"], p["conv_b"].reshape(1, D_CONV),
                   p["conv_ln_w"].reshape(1, D_CONV), p["conv_ln_b"].reshape(1, D_CONV),
                   bsz, seq)

    a1, pj, aj, bbar = _s5_prep(p["ssm_A_re"], p["ssm_A_im"], p["ssm_log_dt"],
                                      p["ssm_B_re"], p["ssm_B_im"])
    nlb = D_SSM // S5_LANES
    bb = bbar.reshape(2, SSM_GROUP, nlb, S5_GROUPS, SSM_STATE).transpose(0, 2, 3, 1, 4)
    bcat = jnp.concatenate([_block_diag(bb[0]), _block_diag(bb[1])], axis=-1).astype(BF16)
    c_re = p["ssm_C_re"].reshape(nlb, S5_GROUPS, SSM_GROUP, SSM_STATE).transpose(0, 1, 3, 2)
    c_im = p["ssm_C_im"].reshape(nlb, S5_GROUPS, SSM_GROUP, SSM_STATE).transpose(0, 1, 3, 2)
    ccat = jnp.concatenate([_block_diag(c_re), -_block_diag(c_im)], axis=1).astype(BF16)
    glu = p["ssm_glu_w"].reshape(nlb, S5_GROUPS, SSM_GROUP, 2 * SSM_GROUP)
    wab = jnp.concatenate([_block_diag(glu[..., :SSM_GROUP]),
                           _block_diag(glu[..., SSM_GROUP:])], axis=-1).astype(BF16)
    y_ssm = _s5(u, bcat, ccat, p["ssm_D"].reshape(1, D_SSM), wab,
                a1, pj, aj, bsz, seq)

    k, vv = _kv(mem, p["norm_mem_w"].reshape(1, d), p["xk_w"].astype(BF16),
                p["xv_w"].astype(BF16))
    w_out = p["w_out"].astype(BF16)
    wr = jnp.concatenate([p["router_group_w"], p["router_expert_w"]], axis=1)
    wr = jnp.pad(wr, ((0, 0), (0, ROUTE_LANES - wr.shape[1]))).astype(BF16)
    br = jnp.concatenate([p["router_group_b"], p["router_expert_b"].reshape(-1)])
    br = jnp.pad(br, (0, ROUTE_LANES - br.shape[0])).reshape(1, ROUTE_LANES)
    h2, xs, gsort, route_i, cnt = _mix_attn(
        xt, y_conv, y_ssm, w_out[:D_CONV], w_out[D_CONV:], p["norm_x_w"].reshape(1, d),
        (p["xq_w"] * (XHEAD_DIM ** -0.5)).astype(BF16), k, vv, p["xo_w"].astype(BF16),
        p["norm_ffn_w"].reshape(1, d), wr, br, seq)

    src, dst, block_expert, expert_seq, next_expert, n_active = _routing_tables(cnt)
    ys = _experts(src, dst, block_expert, expert_seq, next_expert, n_active, xs,
                  p["moe_w1"], p["moe_w3"], p["moe_w2"])
    out = _combine(h2, route_i, gsort, ys, final_w.reshape(1, d))
    return out.reshape(bsz, seq, d)


def kernel(x, mem, norm_mix_w, w_in, conv_w, conv_b, conv_ln_w, conv_ln_b, ssm_A_re, ssm_A_im, ssm_log_dt, ssm_B_re, ssm_B_im, ssm_C_re, ssm_C_im, ssm_D, ssm_glu_w, w_out, norm_x_w, norm_mem_w, xq_w, xk_w, xv_w, xo_w, norm_ffn_w, router_group_w, router_group_b, router_expert_w, router_expert_b, moe_w1, moe_w3, moe_w2, final_norm_w):
    stacked = dict(
        norm_mix_w=norm_mix_w, w_in=w_in, conv_w=conv_w, conv_b=conv_b,
        conv_ln_w=conv_ln_w, conv_ln_b=conv_ln_b, ssm_A_re=ssm_A_re, ssm_A_im=ssm_A_im,
        ssm_log_dt=ssm_log_dt, ssm_B_re=ssm_B_re, ssm_B_im=ssm_B_im, ssm_C_re=ssm_C_re,
        ssm_C_im=ssm_C_im, ssm_D=ssm_D, ssm_glu_w=ssm_glu_w, w_out=w_out,
        norm_x_w=norm_x_w, norm_mem_w=norm_mem_w, xq_w=xq_w, xk_w=xk_w, xv_w=xv_w,
        xo_w=xo_w, norm_ffn_w=norm_ffn_w, router_group_w=router_group_w,
        router_group_b=router_group_b, router_expert_w=router_expert_w,
        router_expert_b=router_expert_b, moe_w1=moe_w1, moe_w3=moe_w3, moe_w2=moe_w2)
    depth = norm_mix_w.shape[0]
    assert depth == 1, "final norm is fused into the single layer's combine step"
    layer = {name: w[0] for name, w in stacked.items()}
    return _layer(x, mem, layer, final_norm_w)
```
